```python
import jax, jax.numpy as jnp
from jax import lax
import numpy as np

D_MODEL = 2048
BATCH = 2
SEQ = 8192
DEPTH = 1
DEC_BATCH = 32
DEC_SEQ = 32
PAST_LEN = 2048

CHUNK = 64
EPS = 1e-6
A_HEADS = 16
A_HEAD_DIM = 64
A_WIDTH = A_HEADS * A_HEAD_DIM
MLP_CHUNK = 128
B_HEADS = 16
B_KV_HEADS = 2
B_HEAD_DIM = 64
B_WIDTH = B_HEADS * B_HEAD_DIM
WINDOW = 128
WINDOW_CHUNKS = WINDOW // CHUNK
ROPE_THETA = 10000.0
ATTN_SCALE = B_HEAD_DIM ** -0.5
MIX_WIDTH = A_WIDTH + B_WIDTH
Q_OFF = 2 * A_WIDTH
K_OFF = Q_OFF + B_WIDTH
V_OFF = K_OFF + B_KV_HEADS * B_HEAD_DIM
IN_WIDTH = V_OFF + B_KV_HEADS * B_HEAD_DIM
MEM_TOKENS = 256
MEM_HEADS = 4
MEM_HEAD_DIM = 128
MEM_WIDTH = MEM_HEADS * MEM_HEAD_DIM
MEM_SCALE = MEM_HEAD_DIM ** -0.5
N_EXPERTS = 32
TOP_K = 4
D_FF = D_MODEL
SWIGLU_LIMIT = 7.0
SWIGLU_ALPHA = 1.702
MOE_BLOCK = 256
NEG_BIG = -1e30

kernel_name = 'hymba_style_streaming_sgu_swa_moe_step'


def rmsnorm(x, g):
    xf = x.astype(jnp.float32)
    inv = lax.rsqrt(jnp.mean(xf * xf, axis=-1, keepdims=True) + EPS)
    return (xf * inv).astype(x.dtype) * g


def rope(x, pos):
    half = x.shape[-1] // 2
    inv_freq = ROPE_THETA ** (-jnp.arange(half, dtype=jnp.float32) / half)
    ang = pos.astype(jnp.float32)[:, None] * inv_freq[None, :]
    cos = jnp.cos(ang)[:, None, :]
    sin = jnp.sin(ang)[:, None, :]
    xf = x.astype(jnp.float32)
    x1, x2 = xf[..., :half], xf[..., half:]
    return jnp.concatenate([x1 * cos - x2 * sin, x2 * cos + x1 * sin], axis=-1).astype(x.dtype)


def sink_softmax(s, sink):
    m = jnp.maximum(jnp.max(s, axis=-1, keepdims=True), sink)
    p = jnp.exp(s - m)
    return p / (jnp.sum(p, axis=-1, keepdims=True) + jnp.exp(sink - m))


def split_in(xn, pos, w_in, g_sgu, g_q, g_k):
    B, S = xn.shape[:2]
    z = xn @ w_in
    za = jax.nn.gelu(z[..., :Q_OFF], approximate=False)
    u = za[..., :A_WIDTH].reshape(B, S, A_HEADS, A_HEAD_DIM)
    va = rmsnorm(za[..., A_WIDTH:], g_sgu).reshape(B, S, A_HEADS, A_HEAD_DIM)
    q = rmsnorm(z[..., Q_OFF:K_OFF].reshape(B, S, B_HEADS, B_HEAD_DIM), g_q)
    k = rmsnorm(z[..., K_OFF:V_OFF].reshape(B, S, B_KV_HEADS, B_HEAD_DIM), g_k)
    vb = z[..., V_OFF:].reshape(B, S, B_KV_HEADS, B_HEAD_DIM)
    return u, va, rope(q, pos), rope(k, pos), vb


def chunk_sgu(u, v, w_s, b_s):
    B, S, H, Dh = v.shape
    L = min(S, MLP_CHUNK)
    vb = v.reshape(B, S // L, L, H, Dh)
    w = jnp.where(jnp.tril(jnp.ones((L, L), dtype=bool))[None], w_s[:, :L, :L], 0.0)
    mixed = jnp.einsum('hts,bnshd->bnthd', w, vb) + b_s[:, :L].T[None, None, :, :, None]
    return (u * mixed.reshape(B, S, H, Dh)).reshape(B, S, A_WIDTH)


def swa_prompt(q, k, v, sinks):
    B, S = q.shape[:2]
    n = S // CHUNK
    G = B_HEADS // B_KV_HEADS
    qb = q.reshape(B, n, CHUNK, B_KV_HEADS, G, B_HEAD_DIM)
    pad = ((0, 0), (WINDOW, 0), (0, 0), (0, 0))
    kp = jnp.pad(k, pad).reshape(B, n + WINDOW_CHUNKS, CHUNK, B_KV_HEADS, B_HEAD_DIM)
    vp = jnp.pad(v, pad).reshape(B, n + WINDOW_CHUNKS, CHUNK, B_KV_HEADS, B_HEAD_DIM)
    kb = jnp.concatenate([kp[:, j:j + n] for j in range(WINDOW_CHUNKS + 1)], axis=2)
    vb = jnp.concatenate([vp[:, j:j + n] for j in range(WINDOW_CHUNKS + 1)], axis=2)
    key_chunk = (jnp.arange(n)[:, None] - WINDOW_CHUNKS
                 + jnp.arange((WINDOW_CHUNKS + 1) * CHUNK)[None, :] // CHUNK)
    valid = key_chunk >= 0
    s = jnp.einsum('bnqhgd,bnkhd->bnhgqk', qb, kb).astype(jnp.float32) * ATTN_SCALE
    s = jnp.where(valid[None, :, None, None, None, :], s, NEG_BIG)
    p = sink_softmax(s, sinks.astype(jnp.float32).reshape(1, 1, B_KV_HEADS, G, 1, 1))
    o = jnp.einsum('bnhgqk,bnkhd->bnqhgd', p.astype(vb.dtype), vb)
    return o.reshape(B, S, B_WIDTH)


def swa_sample(q, k, v, cache_k, cache_v, sinks):
    B, T = q.shape[:2]
    G = B_HEADS // B_KV_HEADS
    kk = jnp.concatenate([cache_k, k], axis=1)
    vv = jnp.concatenate([cache_v, v], axis=1)
    qg = q.reshape(B, T, B_KV_HEADS, G, B_HEAD_DIM)
    s = jnp.einsum('bqhgd,bkhd->bhgqk', qg, kk).astype(jnp.float32) * ATTN_SCALE
    p = sink_softmax(s, sinks.astype(jnp.float32).reshape(1, B_KV_HEADS, G, 1, 1))
    o = jnp.einsum('bhgqk,bkhd->bqhgd', p.astype(vv.dtype), vv)
    return o.reshape(B, T, B_WIDTH)


def merge_out(oa, ob, g_out_a, g_out_b, w_out):
    return jnp.concatenate([rmsnorm(oa, g_out_a), rmsnorm(ob, g_out_b)], axis=-1) @ w_out


def mem_kv(mem, g_mem_src, w_mk, w_mv, g_mk):
    B, M = mem.shape[:2]
    m = rmsnorm(mem, g_mem_src)
    k = rmsnorm((m @ w_mk).reshape(B, M, MEM_HEADS, MEM_HEAD_DIM), g_mk)
    v = (m @ w_mv).reshape(B, M, MEM_HEADS, MEM_HEAD_DIM)
    return k, v


def mem_attend(hn, mk, mv, w_mq, g_mq, w_mo):
    B, S = hn.shape[:2]
    q = rmsnorm((hn @ w_mq).reshape(B, S, MEM_HEADS, MEM_HEAD_DIM), g_mq)
    s = jnp.einsum('bshd,bmhd->bhsm', q, mk).astype(jnp.float32) * MEM_SCALE
    p = jax.nn.softmax(s, axis=-1)
    o = jnp.einsum('bhsm,bmhd->bshd', p.astype(mv.dtype), mv).reshape(B, S, MEM_WIDTH)
    return o @ w_mo


def moe(h, w_router, b_router, w_gate_up, b_gate_up, w_down, b_down):
    lead = h.shape[:-1]
    xt = h.reshape(-1, D_MODEL)
    T = xt.shape[0]
    A = T * TOP_K
    logits = (xt @ w_router).astype(jnp.float32) + b_router.astype(jnp.float32)
    top_val, top_idx = lax.top_k(logits, TOP_K)
    gates = jax.nn.softmax(top_val, axis=-1)
    flat_e = top_idx.reshape(-1).astype(jnp.int32)
    order = jnp.argsort(flat_e)
    sorted_e = flat_e[order]
    sorted_tok = (order // TOP_K).astype(jnp.int32)
    counts = jnp.bincount(flat_e, length=N_EXPERTS).astype(jnp.int32)
    padded = (counts + MOE_BLOCK - 1) // MOE_BLOCK * MOE_BLOCK
    start = jnp.cumsum(counts) - counts
    pend = jnp.cumsum(padded)
    pstart = pend - padded
    dest = (pstart[sorted_e] + jnp.arange(A, dtype=jnp.int32) - start[sorted_e]).astype(jnp.int32)
    n_blocks = -(-(A + N_EXPERTS * (MOE_BLOCK - 1)) // MOE_BLOCK)
    rows = jnp.zeros((n_blocks * MOE_BLOCK,), jnp.int32).at[dest].set(sorted_tok)
    block_start = jnp.arange(n_blocks, dtype=jnp.int32) * MOE_BLOCK
    block_e = jnp.minimum(jnp.searchsorted(pend, block_start, side='right'), N_EXPERTS - 1)
    xb = xt[rows].reshape(n_blocks, MOE_BLOCK, D_MODEL)

    def expert_block(args):
        xblk, e = args
        gu = xblk @ w_gate_up[e] + b_gate_up[e]
        gate = jnp.minimum(gu[:, :D_FF], SWIGLU_LIMIT)
        up = jnp.clip(gu[:, D_FF:], -SWIGLU_LIMIT, SWIGLU_LIMIT)
        act = (up + 1.0) * (gate * jax.nn.sigmoid(SWIGLU_ALPHA * gate))
        return act @ w_down[e] + b_down[e]

    yb = lax.map(expert_block, (xb, block_e)).reshape(-1, D_MODEL)
    slot = jnp.zeros((A,), jnp.int32).at[order].set(dest).reshape(T, TOP_K)
    y = jnp.einsum('tk,tkd->td', gates.astype(yb.dtype), yb[slot])
    return y.reshape(*lead, D_MODEL)


def setup_inputs(seed: int = 0) -> dict:
    key = jax.random.key(seed)
    ks = jax.random.split(key, 40)
    f32 = jnp.float32

    def nrm(k, shape, scale=1.0):
        return jax.random.normal(k, shape, f32) * scale

    def gain(k, shape):
        return 1.0 + 0.02 * jax.random.normal(k, shape, f32)

    W = min(WINDOW, PAST_LEN)
    return {
        'x_prompt': nrm(ks[0], (BATCH, SEQ, D_MODEL)),
        'x_sample': nrm(ks[1], (DEC_BATCH, DEC_SEQ, D_MODEL)),
        'cache_swa_k': nrm(ks[2], (DEPTH, DEC_BATCH, W, B_KV_HEADS, B_HEAD_DIM)),
        'cache_swa_v': nrm(ks[3], (DEPTH, DEC_BATCH, W, B_KV_HEADS, B_HEAD_DIM)),
        'cache_mem_k': nrm(ks[4], (DEPTH, DEC_BATCH, MEM_TOKENS, MEM_HEADS, MEM_HEAD_DIM)),
        'cache_mem_v': nrm(ks[5], (DEPTH, DEC_BATCH, MEM_TOKENS, MEM_HEADS, MEM_HEAD_DIM)),
        'mem_prompt': nrm(ks[6], (BATCH, MEM_TOKENS, D_MODEL)),
        'g_mix': gain(ks[7], (DEPTH, D_MODEL)),
        'w_in': nrm(ks[8], (DEPTH, D_MODEL, IN_WIDTH), D_MODEL ** -0.5),
        'g_sgu': gain(ks[9], (DEPTH, A_WIDTH)),
        'w_s': nrm(ks[10], (DEPTH, A_HEADS, MLP_CHUNK, MLP_CHUNK), MLP_CHUNK ** -0.5),
        'b_s': 1.0 + 0.02 * jax.random.normal(ks[11], (DEPTH, A_HEADS, MLP_CHUNK), f32),
        'g_q': gain(ks[12], (DEPTH, B_HEAD_DIM)),
        'g_k': gain(ks[13], (DEPTH, B_HEAD_DIM)),
        'sinks': nrm(ks[14], (DEPTH, B_HEADS), 0.5),
        'g_out_a': gain(ks[15], (DEPTH, A_WIDTH)),
        'g_out_b': gain(ks[16], (DEPTH, B_WIDTH)),
        'w_out': nrm(ks[17], (DEPTH, MIX_WIDTH, D_MODEL), MIX_WIDTH ** -0.5),
        'g_mem_in': gain(ks[18], (DEPTH, D_MODEL)),
        'g_mem_src': gain(ks[19], (DEPTH, D_MODEL)),
        'w_mq': nrm(ks[20], (DEPTH, D_MODEL, MEM_WIDTH), D_MODEL ** -0.5),
        'w_mk': nrm(ks[21], (DEPTH, D_MODEL, MEM_WIDTH), D_MODEL ** -0.5),
        'w_mv': nrm(ks[22], (DEPTH, D_MODEL, MEM_WIDTH), D_MODEL ** -0.5),
        'g_mq': gain(ks[23], (DEPTH, MEM_HEAD_DIM)),
        'g_mk': gain(ks[24], (DEPTH, MEM_HEAD_DIM)),
        'w_mo': nrm(ks[25], (DEPTH, MEM_WIDTH, D_MODEL), MEM_WIDTH ** -0.5),
        'g_moe': gain(ks[26], (DEPTH, D_MODEL)),
        'w_router': nrm(ks[27], (DEPTH, D_MODEL, N_EXPERTS), D_MODEL ** -0.5),
        'b_router': nrm(ks[28], (DEPTH, N_EXPERTS), 0.01),
        'w_gate_up': nrm(ks[29], (DEPTH, N_EXPERTS, D_MODEL, 2 * D_FF), D_MODEL ** -0.5),
        'b_gate_up': nrm(ks[30], (DEPTH, N_EXPERTS, 2 * D_FF), 0.01),
        'w_down': nrm(ks[31], (DEPTH, N_EXPERTS, D_FF, D_MODEL), D_FF ** -0.5),
        'b_down': nrm(ks[32], (DEPTH, N_EXPERTS, D_MODEL), 0.01),
    }


def reference(x_prompt, x_sample, cache_swa_k, cache_swa_v, cache_mem_k, cache_mem_v, mem_prompt,
              g_mix, w_in, g_sgu, w_s, b_s, g_q, g_k, sinks, g_out_a, g_out_b, w_out,
              g_mem_in, g_mem_src, w_mq, w_mk, w_mv, g_mq, g_mk, w_mo,
              g_moe, w_router, b_router, w_gate_up, b_gate_up, w_down, b_down):
    S = x_prompt.shape[1]
    T = x_sample.shape[1]
    pos_p = jnp.arange(S, dtype=jnp.int32)
    pos_s = PAST_LEN + jnp.arange(T, dtype=jnp.int32)
    xp, xs = x_prompt, x_sample
    swa_k_p, swa_v_p, mem_k_p, mem_v_p = [], [], [], []
    swa_k_s, swa_v_s, sgu_v_s = [], [], []
    for l in range(DEPTH):
        u, va, q, k, vb = split_in(rmsnorm(xp, g_mix[l]), pos_p, w_in[l], g_sgu[l], g_q[l], g_k[l])
        oa = chunk_sgu(u, va, w_s[l], b_s[l])
        ob = swa_prompt(q, k, vb, sinks[l])
        xp = xp + merge_out(oa, ob, g_out_a[l], g_out_b[l], w_out[l])
        mk, mv = mem_kv(mem_prompt, g_mem_src[l], w_mk[l], w_mv[l], g_mk[l])
        xp = xp + mem_attend(rmsnorm(xp, g_mem_in[l]), mk, mv, w_mq[l], g_mq[l], w_mo[l])
        xp = xp + moe(rmsnorm(xp, g_moe[l]), w_router[l], b_router[l], w_gate_up[l], b_gate_up[l], w_down[l], b_down[l])
        swa_k_p.append(k[:, S - WINDOW:])
        swa_v_p.append(vb[:, S - WINDOW:])
        mem_k_p.append(mk)
        mem_v_p.append(mv)
        u2, va2, q2, k2, vb2 = split_in(rmsnorm(xs, g_mix[l]), pos_s, w_in[l], g_sgu[l], g_q[l], g_k[l])
        oa2 = chunk_sgu(u2, va2, w_s[l], b_s[l])
        ob2 = swa_sample(q2, k2, vb2, cache_swa_k[l], cache_swa_v[l], sinks[l])
        xs = xs + merge_out(oa2, ob2, g_out_a[l], g_out_b[l], w_out[l])
        xs = xs + mem_attend(rmsnorm(xs, g_mem_in[l]), cache_mem_k[l], cache_mem_v[l], w_mq[l], g_mq[l], w_mo[l])
        xs = xs + moe(rmsnorm(xs, g_moe[l]), w_router[l], b_router[l], w_gate_up[l], b_gate_up[l], w_down[l], b_down[l])
        swa_k_s.append(k2)
        swa_v_s.append(vb2)
        sgu_v_s.append(va2)
    new_swa_k_prompt = jnp.stack(swa_k_p, 0)
    new_swa_v_prompt = jnp.stack(swa_v_p, 0)
    new_mem_k_prompt = jnp.stack(mem_k_p, 0)
    new_mem_v_prompt = jnp.stack(mem_v_p, 0)
    new_swa_k_sample = jnp.stack(swa_k_s, 0)
    new_swa_v_sample = jnp.stack(swa_v_s, 0)
    new_sgu_v_sample = jnp.stack(sgu_v_s, 0)
    return (xp, xs, new_swa_k_prompt, new_swa_v_prompt, new_mem_k_prompt, new_mem_v_prompt,
            new_swa_k_sample, new_swa_v_sample, new_sgu_v_sample)
```

```python
import functools

import jax
import jax.numpy as jnp
from jax import lax
from jax.experimental import pallas as pl
from jax.experimental.pallas import tpu as pltpu

F32 = jnp.float32
BF16 = jnp.bfloat16
I32 = jnp.int32

D_MODEL = 2048
PAST_LEN = 2048
CHUNK = 64
EPS = 1e-6
A_HEADS = 16
A_HEAD_DIM = 64
A_WIDTH = A_HEADS * A_HEAD_DIM
MLP_CHUNK = 128
B_HEADS = 16
B_KV_HEADS = 2
B_HEAD_DIM = 64
B_WIDTH = B_HEADS * B_HEAD_DIM
KV_WIDTH = B_KV_HEADS * B_HEAD_DIM
WINDOW = 128
ROPE_THETA = 10000.0
ATTN_SCALE = B_HEAD_DIM ** -0.5
Q_OFF = 2 * A_WIDTH
K_OFF = Q_OFF + B_WIDTH
V_OFF = K_OFF + KV_WIDTH
IN_WIDTH = V_OFF + KV_WIDTH
QK_WIDTH = B_WIDTH + KV_WIDTH
MEM_HEADS = 4
MEM_HEAD_DIM = 128
MEM_WIDTH = MEM_HEADS * MEM_HEAD_DIM
MEM_SCALE = MEM_HEAD_DIM ** -0.5
N_EXPERTS = 32
TOP_K = 4
D_FF = D_MODEL
SWIGLU_LIMIT = 7.0
SWIGLU_ALPHA = 1.702
NEG_BIG = -1e30

LANES = 128
VMEM_LIMIT = 56 * 1024 * 1024

TOKEN_TILE = 512
SAMPLE_TILE = 128
EXPERT_ROWS = 1024
EXPERT_SUB = 256
FF_TILE = 256
OUT_TILE = 512
ROW_TILE = 128


def _dot(a, b):
    return jnp.dot(a, b, preferred_element_type=F32)


def _dot_nt(a, b):
    return lax.dot_general(a, b, (((1,), (1,)), ((), ())), preferred_element_type=F32)


def _rms(x):
    return x * lax.rsqrt(jnp.mean(x * x, axis=-1, keepdims=True) + EPS)


def _gelu(x):
    return 0.5 * x * (1.0 + lax.erf(x * 0.7071067811865476))


def _params(n_axes=1):
    return pltpu.CompilerParams(dimension_semantics=("arbitrary",) * n_axes,
                                vmem_limit_bytes=VMEM_LIMIT)


def _resident(shape):
    nd = len(shape)
    return pl.BlockSpec(shape, lambda *_: (0,) * nd, pipeline_mode=pl.Buffered(1))


def _mem_kv_kernel(mem_ref, gsrc_ref, wk_ref, wv_ref, gk_ref, k_ref, v_ref):
    m = (_rms(mem_ref[...]) * gsrc_ref[...]).astype(BF16)
    kz = _dot(m, wk_ref[...])
    gk = gk_ref[...]
    for h in range(MEM_HEADS):
        sl = slice(h * MEM_HEAD_DIM, (h + 1) * MEM_HEAD_DIM)
        k_ref[:, sl] = _rms(kz[:, sl]) * gk
    v_ref[...] = _dot(m, wv_ref[...])


def _mem_kv(mem, g_src, w_mk, w_mv, g_mk):
    n = mem.shape[0]
    out = jax.ShapeDtypeStruct((n, MEM_WIDTH), F32)
    return pl.pallas_call(
        _mem_kv_kernel, out_shape=(out, out), name="mem_kv",
        compiler_params=pltpu.CompilerParams(vmem_limit_bytes=VMEM_LIMIT),
    )(mem, g_src, w_mk, w_mv, g_mk)


def _in_proj_kernel(x_ref, gmix_ref, w_ref, gsgu_ref, gqk_ref, seg_ref, segt_ref, cos_ref, sin_ref,
                    u_ref, va_ref, q_ref, k_ref, v_ref):
    xn = (_rms(x_ref[...]) * gmix_ref[...]).astype(BF16)
    u_ref[...] = _gelu(_dot(xn, w_ref[:, 0:A_WIDTH])).astype(BF16)
    va_ref[...] = _rms(_gelu(_dot(xn, w_ref[:, A_WIDTH:Q_OFF]))) * gsgu_ref[...]
    v_ref[...] = _dot(xn, w_ref[:, V_OFF:IN_WIDTH])

    qk = _dot(xn, w_ref[:, Q_OFF:V_OFF])
    ssq = _dot((qk * qk).astype(BF16), seg_ref[...])
    inv = lax.rsqrt(ssq * (1.0 / B_HEAD_DIM) + EPS)
    inv_hi = inv.astype(BF16)
    inv_lo = (inv - inv_hi.astype(F32)).astype(BF16)
    inv_b = _dot(inv_hi, segt_ref[...]) + _dot(inv_lo, segt_ref[...])
    qkn = (qk * inv_b) * gqk_ref[...]

    cos = cos_ref[...]
    sin = sin_ref[...]
    lane = lax.broadcasted_iota(I32, cos.shape, 1)
    first_half = (lane % B_HEAD_DIM) < (B_HEAD_DIM // 2)
    for g in range(QK_WIDTH // LANES):
        xg = qkn[:, g * LANES:(g + 1) * LANES]
        rot = jnp.where(first_half, pltpu.roll(xg, LANES - B_HEAD_DIM // 2, 1),
                        pltpu.roll(xg, B_HEAD_DIM // 2, 1))
        r = xg * cos + rot * sin
        if g < B_WIDTH // LANES:
            q_ref[:, g * LANES:(g + 1) * LANES] = (r * ATTN_SCALE).astype(BF16)
        else:
            k_ref[...] = r


def _in_proj(x, g_mix, w_in, g_sgu, g_qk, seg, segt, cos_tab, sin_tab, tile, name):
    t = x.shape[0]
    n = t // tile
    tab_tiles = cos_tab.shape[0] // tile
    row = lambda w: pl.BlockSpec((tile, w), lambda i: (i, 0))
    tab = pl.BlockSpec((tile, LANES), lambda i: (i % tab_tiles, 0))
    return pl.pallas_call(
        _in_proj_kernel, grid=(n,), name=name,
        in_specs=[row(D_MODEL), _resident(g_mix.shape), _resident(w_in.shape), _resident(g_sgu.shape),
                  _resident(g_qk.shape), _resident(seg.shape), _resident(segt.shape), tab, tab],
        out_specs=[row(A_WIDTH), row(A_WIDTH), row(B_WIDTH), row(KV_WIDTH), row(KV_WIDTH)],
        out_shape=(jax.ShapeDtypeStruct((t, A_WIDTH), BF16), jax.ShapeDtypeStruct((t, A_WIDTH), F32),
                   jax.ShapeDtypeStruct((t, B_WIDTH), BF16), jax.ShapeDtypeStruct((t, KV_WIDTH), F32),
                   jax.ShapeDtypeStruct((t, KV_WIDTH), F32)),
        compiler_params=_params(),
    )(x, g_mix, w_in, g_sgu, g_qk, seg, segt, cos_tab, sin_tab)


def _split_heads(ref_dst_lo, ref_dst_hi, rows, x, swap):
    lane = lax.broadcasted_iota(I32, x.shape, 1)
    low = lane < B_HEAD_DIM
    zero = jnp.zeros_like(x)
    ref_dst_lo[0, rows, :] = jnp.where(low, x, zero).astype(BF16)
    ref_dst_hi[0, rows, :] = jnp.where(low, zero, swap).astype(BF16)
    ref_dst_lo[1, rows, :] = jnp.where(low, swap, zero).astype(BF16)
    ref_dst_hi[1, rows, :] = jnp.where(low, zero, x).astype(BF16)


def _sgu_chunk(u, va, ws_ref, bias):
    lane = lax.broadcasted_iota(I32, (MLP_CHUNK, LANES), 1)
    low = lane < A_HEAD_DIM
    outs = []
    for p in range(A_WIDTH // LANES):
        sl = slice(p * LANES, (p + 1) * LANES)
        v2 = va[:, sl]
        zero = jnp.zeros_like(v2)
        mixed = (_dot(ws_ref[2 * p], jnp.where(low, v2, zero).astype(BF16))
                 + _dot(ws_ref[2 * p + 1], jnp.where(low, zero, v2).astype(BF16)))
        outs.append(u[:, sl].astype(F32) * (mixed + bias[:, sl]))
    return jnp.concatenate(outs, axis=1)


def _attend(q2, kl, kh, vl, vh, mask, sink_even, sink_odd):
    outs = []
    for kk, vv, sink in ((kl, vl, sink_even), (kh, vh, sink_odd)):
        s = _dot_nt(q2, kk)
        if mask is not None:
            s = jnp.where(mask, s, NEG_BIG)
        m = jnp.maximum(jnp.max(s, axis=-1, keepdims=True), sink)
        p = jnp.exp(s - m)
        den = jnp.sum(p, axis=-1, keepdims=True) + jnp.exp(sink - m)
        outs.append(_dot(p.astype(BF16), vv) * (1.0 / den))
    return outs[0] + outs[1]


def _out_proj(oa, ob, ga_ref, gb_ref, wout_ref, x):
    a = (_rms(oa) * ga_ref[...]).astype(BF16)
    b = (_rms(ob) * gb_ref[...]).astype(BF16)
    return x + _dot(a, wout_ref[0:A_WIDTH, :]) + _dot(b, wout_ref[A_WIDTH:, :])


def _mixer_prompt_kernel(tiles_per_seq, sinks_ref, x_ref, u_ref, va_ref, q_ref, kc_ref, vc_ref, kp_ref, vp_ref,
                         ws_ref, bias_ref, ga_ref, gb_ref, wout_ref, o_ref,
                         kl_s, kh_s, vl_s, vh_s, oa_s, ob_s):
    tile = x_ref.shape[0]
    n_sub = tile // MLP_CHUNK
    seq_start = (pl.program_id(0) % tiles_per_seq) == 0

    for src_p, src_c, dl, dh in ((kp_ref, kc_ref, kl_s, kh_s), (vp_ref, vc_ref, vl_s, vh_s)):
        prev = src_p[...]
        cur = src_c[...]
        _split_heads(dl, dh, slice(0, WINDOW), prev, pltpu.roll(prev, B_HEAD_DIM, 1))
        _split_heads(dl, dh, slice(WINDOW, WINDOW + tile), cur, pltpu.roll(cur, B_HEAD_DIM, 1))

    qc = lax.broadcasted_iota(I32, (MLP_CHUNK, 2 * MLP_CHUNK), 0) // CHUNK
    kc = lax.broadcasted_iota(I32, (MLP_CHUNK, 2 * MLP_CHUNK), 1) // CHUNK
    window_mask = (kc >= qc) & (kc <= qc + 2)
    bias = bias_ref[...]

    def sub(j, carry):
        r0 = pl.multiple_of(j * MLP_CHUNK, MLP_CHUNK)
        rows = pl.ds(r0, MLP_CHUNK)
        keys = pl.ds(r0, 2 * MLP_CHUNK)
        first_key_chunk = jnp.where(jnp.logical_and(seq_start, j == 0), 2, 0)
        mask = window_mask & (kc >= first_key_chunk)
        oa_s[rows, :] = _sgu_chunk(u_ref[rows, :], va_ref[rows, :], ws_ref, bias)
        for p in range(B_WIDTH // LANES):
            h = p // (B_WIDTH // LANES // B_KV_HEADS)
            sl = slice(p * LANES, (p + 1) * LANES)
            ob_s[rows, sl] = _attend(q_ref[rows, sl], kl_s[h, keys, :], kh_s[h, keys, :],
                                     vl_s[h, keys, :], vh_s[h, keys, :], mask,
                                     sinks_ref[2 * p], sinks_ref[2 * p + 1])
        return carry

    lax.fori_loop(0, n_sub, sub, 0)
    o_ref[...] = _out_proj(oa_s[...], ob_s[...], ga_ref, gb_ref, wout_ref, x_ref[...])


def _mixer_prompt(x, u, va, q, k, v, sinks, ws, bias, g_a, g_b, w_out, seq, tile):
    t = x.shape[0]
    n = t // tile
    tiles_per_seq = seq // tile
    per = tile // WINDOW
    row = lambda w: pl.BlockSpec((tile, w), lambda i, s: (i, 0))
    prev = pl.BlockSpec((WINDOW, KV_WIDTH), lambda i, s: (jnp.maximum(i * per - 1, 0), 0))
    res = lambda a: pl.BlockSpec(a.shape, lambda i, s: (0,) * a.ndim, pipeline_mode=pl.Buffered(1))
    grid_spec = pltpu.PrefetchScalarGridSpec(
        num_scalar_prefetch=1, grid=(n,),
        in_specs=[row(D_MODEL), row(A_WIDTH), row(A_WIDTH), row(B_WIDTH), row(KV_WIDTH), row(KV_WIDTH), prev, prev,
                  res(ws), res(bias), res(g_a), res(g_b), res(w_out)],
        out_specs=row(D_MODEL),
        scratch_shapes=[pltpu.VMEM((B_KV_HEADS, WINDOW + tile, LANES), BF16) for _ in range(4)]
        + [pltpu.VMEM((tile, A_WIDTH), F32), pltpu.VMEM((tile, B_WIDTH), F32)])
    return pl.pallas_call(
        functools.partial(_mixer_prompt_kernel, tiles_per_seq), grid_spec=grid_spec, name="mixer_prompt",
        out_shape=jax.ShapeDtypeStruct((t, D_MODEL), F32), compiler_params=_params(),
    )(sinks, x, u, va, q, k, v, k, v, ws, bias, g_a, g_b, w_out)


def _mixer_sample_kernel(dec_seq, sinks_ref, x_ref, u_ref, va_ref, q_ref, kc_ref, vc_ref, ck_ref, cv_ref,
                         ws_ref, bias_ref, ga_ref, gb_ref, wout_ref, o_ref,
                         kl_s, kh_s, vl_s, vh_s, ob_s):
    tile = x_ref.shape[0]
    n_seq = tile // dec_seq
    n_cache = ck_ref.shape[1]
    n_keys = n_cache + dec_seq
    oa = _sgu_chunk(u_ref[...], va_ref[...], ws_ref, bias_ref[...])
    for b in range(n_seq):
        rows = slice(b * dec_seq, (b + 1) * dec_seq)
        for src_c, src_n, dl, dh in ((ck_ref, kc_ref, kl_s, kh_s), (cv_ref, vc_ref, vl_s, vh_s)):
            old = src_c[b]
            new = src_n[rows, :]
            _split_heads(dl, dh, slice(0, n_cache), old, pltpu.roll(old, B_HEAD_DIM, 1))
            _split_heads(dl, dh, slice(n_cache, n_keys), new, pltpu.roll(new, B_HEAD_DIM, 1))
        for p in range(B_WIDTH // LANES):
            h = p // (B_WIDTH // LANES // B_KV_HEADS)
            sl = slice(p * LANES, (p + 1) * LANES)
            ob_s[rows, sl] = _attend(q_ref[rows, sl], kl_s[h], kh_s[h], vl_s[h], vh_s[h], None,
                                     sinks_ref[2 * p], sinks_ref[2 * p + 1])
    o_ref[...] = _out_proj(oa, ob_s[...], ga_ref, gb_ref, wout_ref, x_ref[...])


def _mixer_sample(x, u, va, q, k, v, cache_k, cache_v, sinks, ws, bias, g_a, g_b, w_out, dec_seq, tile):
    t = x.shape[0]
    n = t // tile
    n_seq = tile // dec_seq
    n_cache = cache_k.shape[1]
    row = lambda w: pl.BlockSpec((tile, w), lambda i, s: (i, 0))
    cache = pl.BlockSpec((n_seq, n_cache, KV_WIDTH), lambda i, s: (i, 0, 0))
    res = lambda a: pl.BlockSpec(a.shape, lambda i, s: (0,) * a.ndim, pipeline_mode=pl.Buffered(1))
    grid_spec = pltpu.PrefetchScalarGridSpec(
        num_scalar_prefetch=1, grid=(n,),
        in_specs=[row(D_MODEL), row(A_WIDTH), row(A_WIDTH), row(B_WIDTH), row(KV_WIDTH), row(KV_WIDTH), cache, cache,
                  res(ws), res(bias), res(g_a), res(g_b), res(w_out)],
        out_specs=row(D_MODEL),
        scratch_shapes=[pltpu.VMEM((B_KV_HEADS, n_cache + dec_seq, LANES), BF16) for _ in range(4)]
        + [pltpu.VMEM((tile, B_WIDTH), F32)])
    return pl.pallas_call(
        functools.partial(_mixer_sample_kernel, dec_seq), grid_spec=grid_spec, name="mixer_sample",
        out_shape=jax.ShapeDtypeStruct((t, D_MODEL), F32), compiler_params=_params(),
    )(sinks, x, u, va, q, k, v, cache_k, cache_v, ws, bias, g_a, g_b, w_out)


def _memory_router_kernel(rows_per_mem, x_ref, mk_ref, mv_ref, gin_ref, wq_ref, gq_ref, wo_ref,
                          gmoe_ref, wr_ref, br_ref,
                          x2_ref, hn_ref, idx_ref, gate_ref, rank_ref, cnt_ref, o_s, base_s):
    tile = x_ref.shape[0]
    x = x_ref[...]
    qz = _dot((_rms(x) * gin_ref[...]).astype(BF16), wq_ref[...])
    gq = gq_ref[...]
    for h in range(MEM_HEADS):
        sl = slice(h * MEM_HEAD_DIM, (h + 1) * MEM_HEAD_DIM)
        qh = (_rms(qz[:, sl]) * gq).astype(BF16)
        for r in range(tile // rows_per_mem):
            rows = slice(r * rows_per_mem, (r + 1) * rows_per_mem)
            s = _dot_nt(qh[rows], mk_ref[r, :, sl].astype(BF16)) * MEM_SCALE
            p = jnp.exp(s - jnp.max(s, axis=-1, keepdims=True))
            den = jnp.sum(p, axis=-1, keepdims=True)
            o_s[rows, sl] = _dot(p.astype(BF16), mv_ref[r, :, sl].astype(BF16)) * (1.0 / den)
    x2 = x + _dot(o_s[...].astype(BF16), wo_ref[...])
    x2_ref[...] = x2
    hn = _rms(x2) * gmoe_ref[...]
    hn_ref[...] = hn

    logits = jnp.dot(hn, wr_ref[...], preferred_element_type=F32, precision=lax.Precision.HIGHEST) + br_ref[...]
    lane = lax.broadcasted_iota(I32, (tile, LANES), 1)
    lane_f = lane.astype(F32)
    work = jnp.where(lane < N_EXPERTS, logits, -jnp.inf)
    idx_out = jnp.zeros((tile, LANES), F32)
    val_out = jnp.zeros((tile, LANES), F32)
    hot = jnp.zeros((tile, LANES), F32)
    top = None
    picks = []
    for k in range(TOP_K):
        m = jnp.max(work, axis=-1, keepdims=True)
        pick = jnp.min(jnp.where(work == m, lane_f, float(LANES)), axis=-1, keepdims=True)
        chosen = lane_f == pick
        if top is None:
            top = m
        idx_out = jnp.where(lane == k, pick, idx_out)
        val_out = jnp.where(lane == k, jnp.exp(m - top), val_out)
        hot = jnp.where(chosen, 1.0, hot)
        work = jnp.where(chosen, -jnp.inf, work)
        picks.append(chosen)
    idx_ref[...] = idx_out.astype(I32)
    gate_ref[...] = val_out * (1.0 / jnp.sum(val_out, axis=-1, keepdims=True))

    @pl.when(pl.program_id(0) == 0)
    def _():
        base_s[...] = jnp.zeros_like(base_s)

    earlier = (lax.broadcasted_iota(I32, (tile, tile), 1) < lax.broadcasted_iota(I32, (tile, tile), 0))
    pos = _dot(jnp.where(earlier, 1.0, 0.0).astype(BF16), hot.astype(BF16)) + base_s[0:1, :]
    rank_out = jnp.zeros((tile, LANES), F32)
    for k in range(TOP_K):
        rank_out = jnp.where(lane == k, jnp.sum(jnp.where(picks[k], pos, 0.0), axis=-1, keepdims=True), rank_out)
    rank_ref[...] = rank_out.astype(I32)
    total = base_s[0:1, :] + jnp.sum(hot, axis=0, keepdims=True)
    base_s[...] = jnp.broadcast_to(total, base_s.shape)
    cnt_ref[...] = jnp.broadcast_to(total, cnt_ref.shape).astype(I32)


def _memory_router(x, mk, mv, rows_per_mem, g_in, w_mq, g_mq, w_mo, g_moe, w_r, b_r, tile, name):
    t = x.shape[0]
    n = t // tile
    mems = tile // rows_per_mem if rows_per_mem <= tile else 1
    per_mem_tiles = max(rows_per_mem // tile, 1)
    rpm = min(rows_per_mem, tile)
    row = lambda w: pl.BlockSpec((tile, w), lambda i: (i, 0))
    mem = pl.BlockSpec((mems, mk.shape[1], MEM_WIDTH), lambda i: (i // per_mem_tiles, 0, 0))
    res = lambda a: pl.BlockSpec(a.shape, lambda i: (0,) * a.ndim, pipeline_mode=pl.Buffered(1))
    small = jax.ShapeDtypeStruct((t, LANES), I32)
    return pl.pallas_call(
        functools.partial(_memory_router_kernel, rpm), grid=(n,), name=name,
        in_specs=[row(D_MODEL), mem, mem, res(g_in), res(w_mq), res(g_mq), res(w_mo), res(g_moe), res(w_r), res(b_r)],
        out_specs=[row(D_MODEL), row(D_MODEL), row(LANES), row(LANES), row(LANES),
                   pl.BlockSpec((8, LANES), lambda i: (0, 0))],
        out_shape=(jax.ShapeDtypeStruct((t, D_MODEL), F32), jax.ShapeDtypeStruct((t, D_MODEL), F32),
                   small, jax.ShapeDtypeStruct((t, LANES), F32), small, jax.ShapeDtypeStruct((8, LANES), I32)),
        scratch_shapes=[pltpu.VMEM((tile, MEM_WIDTH), F32), pltpu.VMEM((8, LANES), F32)],
        compiler_params=_params(),
    )(x, mk, mv, g_in, w_mq, g_mq, w_mo, g_moe, w_r, b_r)


def _dispatch_kernel(dest_ref, hn_ref, *rest):
    xb_ref, sem = rest[-2:]
    tile = hn_ref.shape[0]

    def copy(t, k):
        return pltpu.make_async_copy(hn_ref.at[pl.ds(t, 1), :],
                                     xb_ref.at[pl.ds(dest_ref[0, t * TOP_K + k], 1), :], sem)

    def start(t, carry):
        for k in range(TOP_K):
            copy(t, k).start()
        return carry

    def wait(t, carry):
        for k in range(TOP_K):
            copy(t, k).wait()
        return carry

    lax.fori_loop(0, tile, start, 0)
    lax.fori_loop(0, tile, wait, 0)


def _dispatch(hn, dest, xb, n_rows, tile):
    t = hn.shape[0]
    n = t // tile
    dest = dest.reshape(n, 1, tile * TOP_K)
    in_specs = [pl.BlockSpec((None, 1, tile * TOP_K), lambda i: (i, 0, 0), memory_space=pltpu.SMEM),
                pl.BlockSpec((tile, D_MODEL), lambda i: (i, 0))]
    args = [dest, hn]
    if xb is not None:
        in_specs.append(pl.BlockSpec(memory_space=pl.ANY))
        args.append(xb)
    return pl.pallas_call(
        _dispatch_kernel, grid=(n,), name="dispatch",
        in_specs=in_specs,
        out_specs=pl.BlockSpec(memory_space=pl.ANY),
        out_shape=jax.ShapeDtypeStruct((n_rows, D_MODEL), F32),
        scratch_shapes=[pltpu.SemaphoreType.DMA(())],
        input_output_aliases={} if xb is None else {2: 0},
        compiler_params=_params(),
    )(*args)


def _experts_kernel(be_ref, nv_ref, xi_ref, x_ref, wg_ref, wu_ref, bg_ref, bu_ref, wd_ref, bd_ref, y_ref, xb_s, h_s):
    del be_ref, xi_ref
    b = pl.program_id(0)
    s = pl.program_id(1)
    n_ff = D_FF // FF_TILE
    valid = nv_ref[b]
    n_sub = (valid + EXPERT_SUB - 1) // EXPERT_SUB

    @pl.when(s == 0)
    def _():
        def cast(i, carry):
            rows = pl.ds(pl.multiple_of(i * EXPERT_SUB, EXPERT_SUB), EXPERT_SUB)
            rid = i * EXPERT_SUB + lax.broadcasted_iota(I32, (EXPERT_SUB, 1), 0)
            xb_s[rows, :] = jnp.where(rid < valid, x_ref[rows, :], 0.0).astype(BF16)
            return carry
        lax.fori_loop(0, n_sub, cast, 0)

    @pl.when(s < n_ff)
    def _():
        wg = wg_ref[0].astype(BF16)
        wu = wu_ref[0].astype(BF16)
        bg = bg_ref[0]
        bu = bu_ref[0]

        def up(i, carry):
            rows = pl.ds(pl.multiple_of(i * EXPERT_SUB, EXPERT_SUB), EXPERT_SUB)
            xs = xb_s[rows, :]
            gate = jnp.minimum(_dot(xs, wg) + bg, SWIGLU_LIMIT)
            lin = jnp.clip(_dot(xs, wu) + bu, -SWIGLU_LIMIT, SWIGLU_LIMIT)
            act = (lin + 1.0) * (gate * (1.0 / (1.0 + jnp.exp(-SWIGLU_ALPHA * gate))))
            h_s[s, rows, :] = act.astype(BF16)
            return carry
        lax.fori_loop(0, n_sub, up, 0)

    @pl.when(s >= n_ff)
    def _():
        wd = wd_ref[0].astype(BF16)
        bd = bd_ref[0]

        def down(i, carry):
            rows = pl.ds(pl.multiple_of(i * EXPERT_SUB, EXPERT_SUB), EXPERT_SUB)
            h = jnp.concatenate([h_s[f, rows, :] for f in range(n_ff)], axis=1)
            y_ref[rows, :] = _dot(h, wd) + bd
            return carry
        lax.fori_loop(0, n_sub, down, 0)


def _experts(xb, block_expert, block_valid, block_index, w_gu, b_gu, w_d, b_d):
    n_blocks = xb.shape[0] // EXPERT_ROWS
    n_ff = D_FF // FF_TILE
    n_out = D_MODEL // OUT_TILE

    def ff(b, s, nv):
        return jnp.where(nv[b] > 0, jnp.minimum(s, n_ff - 1), n_ff - 1)

    def oc(b, s, nv):
        return jnp.where(nv[b] > 0, jnp.maximum(s - n_ff, 0), n_out - 1)

    b_gu3 = b_gu.reshape(N_EXPERTS, 1, 2 * D_FF)
    b_d3 = b_d.reshape(N_EXPERTS, 1, D_MODEL)
    grid_spec = pltpu.PrefetchScalarGridSpec(
        num_scalar_prefetch=3, grid=(n_blocks, n_ff + n_out),
        in_specs=[
            pl.BlockSpec((EXPERT_ROWS, D_MODEL), lambda b, s, be, nv, xi: (xi[b], 0)),
            pl.BlockSpec((1, D_MODEL, FF_TILE), lambda b, s, be, nv, xi: (be[b], 0, ff(b, s, nv))),
            pl.BlockSpec((1, D_MODEL, FF_TILE), lambda b, s, be, nv, xi: (be[b], 0, ff(b, s, nv) + n_ff)),
            pl.BlockSpec((1, 1, FF_TILE), lambda b, s, be, nv, xi: (be[b], 0, ff(b, s, nv))),
            pl.BlockSpec((1, 1, FF_TILE), lambda b, s, be, nv, xi: (be[b], 0, ff(b, s, nv) + n_ff)),
            pl.BlockSpec((1, D_FF, OUT_TILE), lambda b, s, be, nv, xi: (be[b], 0, oc(b, s, nv))),
            pl.BlockSpec((1, 1, OUT_TILE), lambda b, s, be, nv, xi: (be[b], 0, oc(b, s, nv))),
        ],
        out_specs=pl.BlockSpec((EXPERT_ROWS, OUT_TILE), lambda b, s, be, nv, xi: (xi[b], oc(b, s, nv))),
        scratch_shapes=[pltpu.VMEM((EXPERT_ROWS, D_MODEL), BF16), pltpu.VMEM((n_ff, EXPERT_ROWS, FF_TILE), BF16)])
    return pl.pallas_call(
        _experts_kernel, grid_spec=grid_spec, name="experts",
        out_shape=jax.ShapeDtypeStruct((n_blocks * EXPERT_ROWS, D_MODEL), F32),
        compiler_params=_params(2),
    )(block_expert, block_valid, block_index, xb, w_gu, w_gu, b_gu3, b_gu3, w_d, b_d3)


def _combine_kernel(dest_ref, x_ref, gate_ref, yb_ref, o_ref, buf, sem):
    tile = x_ref.shape[0]

    def copy(t, k):
        return pltpu.make_async_copy(yb_ref.at[pl.ds(dest_ref[0, t * TOP_K + k], 1), :],
                                     buf.at[k, pl.ds(t, 1), :], sem)

    def start(t, carry):
        for k in range(TOP_K):
            copy(t, k).start()
        return carry

    def wait(t, carry):
        for k in range(TOP_K):
            copy(t, k).wait()
        return carry

    lax.fori_loop(0, tile, start, 0)
    lax.fori_loop(0, tile, wait, 0)
    gate = gate_ref[...]
    acc = x_ref[...]
    for k in range(TOP_K):
        acc = acc + gate[:, k:k + 1] * buf[k]
    o_ref[...] = acc


def _combine(x, gate, dest, yb, tile):
    t = x.shape[0]
    n = t // tile
    dest = dest.reshape(n, 1, tile * TOP_K)
    return pl.pallas_call(
        _combine_kernel, grid=(n,), name="combine",
        in_specs=[pl.BlockSpec((None, 1, tile * TOP_K), lambda i: (i, 0, 0), memory_space=pltpu.SMEM),
                  pl.BlockSpec((tile, D_MODEL), lambda i: (i, 0)),
                  pl.BlockSpec((tile, LANES), lambda i: (i, 0)),
                  pl.BlockSpec(memory_space=pl.ANY)],
        out_specs=pl.BlockSpec((tile, D_MODEL), lambda i: (i, 0)),
        out_shape=jax.ShapeDtypeStruct((t, D_MODEL), F32),
        scratch_shapes=[pltpu.VMEM((TOP_K, tile, D_MODEL), F32), pltpu.SemaphoreType.DMA(())],
        compiler_params=_params(),
    )(dest, x, gate, yb)


def _rope_tables(pos):
    half = B_HEAD_DIM // 2
    inv_freq = ROPE_THETA ** (-jnp.arange(half, dtype=F32) / half)
    ang = pos.astype(F32)[:, None] * inv_freq[None, :]
    cos = jnp.cos(ang)
    sin = jnp.sin(ang)
    return jnp.tile(cos, (1, LANES // half)), jnp.tile(jnp.concatenate([-sin, sin], axis=1), (1, LANES // B_HEAD_DIM))


def _layer(l, xp, xs, cache_swa_k, cache_swa_v, cache_mem_k, cache_mem_v, mem_prompt,
           g_mix, w_in, g_sgu, w_s, b_s, g_q, g_k, sinks, g_out_a, g_out_b, w_out,
           g_mem_in, g_mem_src, w_mq, w_mk, w_mv, g_mq, g_mk, w_mo,
           g_moe, w_router, b_router, w_gate_up, b_gate_up, w_down, b_down):
    n_b, seq, _ = xp.shape
    n_db, dec_seq, _ = xs.shape
    tp = n_b * seq
    ts = n_db * dec_seq
    tile = min(TOKEN_TILE, seq)
    row = lambda a: a[l].reshape(1, -1)

    xp2 = xp.reshape(tp, D_MODEL)
    xs2 = xs.reshape(ts, D_MODEL)
    head = jnp.arange(QK_WIDTH, dtype=I32) // B_HEAD_DIM
    seg = (head[:, None] == jnp.arange(LANES, dtype=I32)[None, :]).astype(BF16)
    g_qk = jnp.concatenate([jnp.tile(g_q[l], B_HEADS), jnp.tile(g_k[l], B_KV_HEADS)]).reshape(1, -1)
    w_in_b = w_in[l].astype(BF16)
    cos_p, sin_p = _rope_tables(jnp.arange(seq, dtype=I32))
    cos_s, sin_s = _rope_tables(PAST_LEN + jnp.arange(SAMPLE_TILE, dtype=I32) % dec_seq)
    up, vap, qp, kp, vp = _in_proj(xp2, row(g_mix), w_in_b, row(g_sgu), g_qk, seg, seg.T, cos_p, sin_p,
                                   tile, "in_proj_prompt")
    us, vas, qs, ks, vs = _in_proj(xs2, row(g_mix), w_in_b, row(g_sgu), g_qk, seg, seg.T, cos_s, sin_s,
                                   SAMPLE_TILE, "in_proj_sample")

    w_out_b = w_out[l].astype(BF16)
    tri = jnp.tril(jnp.ones((MLP_CHUNK, MLP_CHUNK), bool))
    ws_p = jnp.where(tri[None], w_s[l], 0.0).astype(BF16)
    bias_p = jnp.repeat(b_s[l].T, A_HEAD_DIM, axis=1)
    x1p = _mixer_prompt(xp2, up, vap, qp, kp, vp, sinks[l], ws_p, bias_p, row(g_out_a), row(g_out_b), w_out_b, seq, tile)

    reps = SAMPLE_TILE // dec_seq
    tri_s = jnp.tril(jnp.ones((dec_seq, dec_seq), bool))
    ws_small = jnp.where(tri_s[None], w_s[l][:, :dec_seq, :dec_seq], 0.0)
    ws_s = jnp.einsum("ab,hts->hatbs", jnp.eye(reps, dtype=F32), ws_small).reshape(A_HEADS, SAMPLE_TILE, SAMPLE_TILE).astype(BF16)
    bias_s = jnp.tile(jnp.repeat(b_s[l][:, :dec_seq].T, A_HEAD_DIM, axis=1), (reps, 1))
    ck = cache_swa_k[l].reshape(n_db, -1, KV_WIDTH)
    cv = cache_swa_v[l].reshape(n_db, -1, KV_WIDTH)
    x1s = _mixer_sample(xs2, us, vas, qs, ks, vs, ck, cv, sinks[l], ws_s, bias_s,
                        row(g_out_a), row(g_out_b), w_out_b, dec_seq, SAMPLE_TILE)

    mk_p, mv_p = _mem_kv(mem_prompt.reshape(-1, D_MODEL), row(g_mem_src), w_mk[l].astype(BF16), w_mv[l].astype(BF16), row(g_mk))
    n_mem = mem_prompt.shape[1]
    w_r = jnp.pad(w_router[l], ((0, 0), (0, LANES - N_EXPERTS)))
    b_r = jnp.pad(b_router[l], (0, LANES - N_EXPERTS)).reshape(1, -1)
    mem_args = (row(g_mem_in), w_mq[l].astype(BF16), row(g_mq), w_mo[l].astype(BF16), row(g_moe), w_r, b_r)
    x2p, hnp, idxp, gatep, rankp, cntp = _memory_router(
        x1p, mk_p.reshape(n_b, n_mem, MEM_WIDTH), mv_p.reshape(n_b, n_mem, MEM_WIDTH),
        seq, *mem_args, tile, "memory_router_prompt")
    x2s, hns, idxs, gates, ranks, cnts = _memory_router(
        x1s, cache_mem_k[l].reshape(n_db, -1, MEM_WIDTH), cache_mem_v[l].reshape(n_db, -1, MEM_WIDTH),
        dec_seq, *mem_args, SAMPLE_TILE, "memory_router_sample")

    cnt_p = cntp[0, :N_EXPERTS]
    cnt_s = cnts[0, :N_EXPERTS]
    total = cnt_p + cnt_s
    nblk = (total + EXPERT_ROWS - 1) // EXPERT_ROWS
    blk_end = jnp.cumsum(nblk)
    row_start = (blk_end - nblk) * EXPERT_ROWS
    n_blocks = (tp + ts) * TOP_K // EXPERT_ROWS + N_EXPERTS
    bidx = jnp.arange(n_blocks, dtype=I32)
    used = bidx < blk_end[-1]
    last = jnp.maximum(blk_end[-1] - 1, 0)
    bsafe = jnp.minimum(bidx, last)
    block_expert = jnp.minimum(jnp.searchsorted(blk_end, bsafe, side="right"), N_EXPERTS - 1).astype(I32)
    within = bsafe - (blk_end - nblk)[block_expert]
    block_valid = jnp.where(used, jnp.clip(total[block_expert] - within * EXPERT_ROWS, 0, EXPERT_ROWS), 0).astype(I32)
    dest_p = row_start[idxp[:, :TOP_K]] + rankp[:, :TOP_K]
    dest_s = (row_start + cnt_p)[idxs[:, :TOP_K]] + ranks[:, :TOP_K]

    xb = _dispatch(hnp, dest_p, None, n_blocks * EXPERT_ROWS, min(ROW_TILE, tp))
    xb = _dispatch(hns, dest_s, xb, n_blocks * EXPERT_ROWS, min(ROW_TILE, ts))
    yb = _experts(xb, block_expert, block_valid, bsafe.astype(I32), w_gate_up[l], b_gate_up[l], w_down[l], b_down[l])
    yp = _combine(x2p, gatep, dest_p, yb, min(ROW_TILE, tp))
    ys = _combine(x2s, gates, dest_s, yb, min(ROW_TILE, ts))

    new = dict(
        swa_k_p=kp.reshape(n_b, seq, B_KV_HEADS, B_HEAD_DIM)[:, seq - WINDOW:],
        swa_v_p=vp.reshape(n_b, seq, B_KV_HEADS, B_HEAD_DIM)[:, seq - WINDOW:],
        mem_k_p=mk_p.reshape(n_b, n_mem, MEM_HEADS, MEM_HEAD_DIM),
        mem_v_p=mv_p.reshape(n_b, n_mem, MEM_HEADS, MEM_HEAD_DIM),
        swa_k_s=ks.reshape(n_db, dec_seq, B_KV_HEADS, B_HEAD_DIM),
        swa_v_s=vs.reshape(n_db, dec_seq, B_KV_HEADS, B_HEAD_DIM),
        sgu_v_s=vas.reshape(n_db, dec_seq, A_HEADS, A_HEAD_DIM))
    return yp.reshape(n_b, seq, D_MODEL), ys.reshape(n_db, dec_seq, D_MODEL), new


def kernel(x_prompt, x_sample, cache_swa_k, cache_swa_v, cache_mem_k, cache_mem_v, mem_prompt, g_mix, w_in, g_sgu, w_s, b_s, g_q, g_k, sinks, g_out_a, g_out_b, w_out, g_mem_in, g_mem_src, w_mq, w_mk, w_mv, g_mq, g_mk, w_mo, g_moe, w_router, b_router, w_gate_up, b_gate_up, w_down, b_down):
    xp, xs = x_prompt, x_sample
    news = []
    for l in range(g_mix.shape[0]):
        xp, xs, new = _layer(l, xp, xs, cache_swa_k, cache_swa_v, cache_mem_k, cache_mem_v, mem_prompt,
                             g_mix, w_in, g_sgu, w_s, b_s, g_q, g_k, sinks, g_out_a, g_out_b, w_out,
                             g_mem_in, g_mem_src, w_mq, w_mk, w_mv, g_mq, g_mk, w_mo,
                             g_moe, w_router, b_router, w_gate_up, b_gate_up, w_down, b_down)
        news.append(new)
    stack = lambda name: jnp.stack([n[name] for n in news], 0)
    return (xp, xs, stack("swa_k_p"), stack("swa_v_p"), stack("mem_k_p"), stack("mem_v_p"),
            stack("swa_k_s"), stack("swa_v_s"), stack("sgu_v_s"))
```

```python
import functools

import jax
import jax.numpy as jnp
from jax import lax
from jax.experimental import pallas as pl
from jax.experimental.pallas import tpu as pltpu

F32 = jnp.float32
BF16 = jnp.bfloat16
I32 = jnp.int32
U32 = jnp.uint32

D_MODEL = 2048
PAST_LEN = 2048
CHUNK = 64
EPS = 1e-6
A_HEADS = 16
A_HEAD_DIM = 64
A_WIDTH = A_HEADS * A_HEAD_DIM
MLP_CHUNK = 128
B_HEADS = 16
B_KV_HEADS = 2
B_HEAD_DIM = 64
B_WIDTH = B_HEADS * B_HEAD_DIM
KV_WIDTH = B_KV_HEADS * B_HEAD_DIM
WINDOW = 128
ROPE_THETA = 10000.0
ATTN_SCALE = B_HEAD_DIM ** -0.5
Q_OFF = 2 * A_WIDTH
K_OFF = Q_OFF + B_WIDTH
V_OFF = K_OFF + KV_WIDTH
IN_WIDTH = V_OFF + KV_WIDTH
QK_WIDTH = B_WIDTH + KV_WIDTH
MEM_HEADS = 4
MEM_HEAD_DIM = 128
MEM_WIDTH = MEM_HEADS * MEM_HEAD_DIM
MEM_SCALE = MEM_HEAD_DIM ** -0.5
N_EXPERTS = 32
TOP_K = 4
D_FF = D_MODEL
SWIGLU_LIMIT = 7.0
SWIGLU_ALPHA = 1.702
NEG_BIG = -1e30

LANES = 128
VMEM_LIMIT = 56 * 1024 * 1024

TOKEN_TILE = 512
SAMPLE_TILE = 128
EXPERT_ROWS = 1024
EXPERT_SUB = 256
FF_TILE = 512
OUT_TILE = 512
ROW_TILE = 128


def _dot(a, b):
    return jnp.dot(a, b, preferred_element_type=F32)


def _dot_nt(a, b):
    return lax.dot_general(a, b, (((1,), (1,)), ((), ())), preferred_element_type=F32)


def _rms(x):
    return x * lax.rsqrt(jnp.mean(x * x, axis=-1, keepdims=True) + EPS)


def _gelu(x):
    return 0.5 * x * (1.0 + lax.erf(x * 0.7071067811865476))


def _pack_rows(x):
    c = x.shape[1] // 2
    bits = pltpu.bitcast(x.astype(BF16).astype(F32), U32)
    return (bits[:, :c] >> 16) | (bits[:, c:] & jnp.uint32(0xFFFF0000))


def _unpack_rows(w):
    return pltpu.bitcast(w << 16, F32), pltpu.bitcast(w & jnp.uint32(0xFFFF0000), F32)


def _params(n_axes=1):
    return pltpu.CompilerParams(dimension_semantics=("arbitrary",) * n_axes,
                                vmem_limit_bytes=VMEM_LIMIT)


def _resident(shape):
    nd = len(shape)
    return pl.BlockSpec(shape, lambda *_: (0,) * nd, pipeline_mode=pl.Buffered(1))


def _mem_kv_kernel(mem_ref, gsrc_ref, wk_ref, wv_ref, gk_ref, k_ref, v_ref):
    m = (_rms(mem_ref[...]) * gsrc_ref[...]).astype(BF16)
    kz = _dot(m, wk_ref[...])
    gk = gk_ref[...]
    for h in range(MEM_HEADS):
        sl = slice(h * MEM_HEAD_DIM, (h + 1) * MEM_HEAD_DIM)
        k_ref[:, sl] = _rms(kz[:, sl]) * gk
    v_ref[...] = _dot(m, wv_ref[...])


def _mem_kv(mem, g_src, w_mk, w_mv, g_mk):
    n = mem.shape[0]
    out = jax.ShapeDtypeStruct((n, MEM_WIDTH), F32)
    return pl.pallas_call(
        _mem_kv_kernel, out_shape=(out, out), name="mem_kv",
        compiler_params=pltpu.CompilerParams(vmem_limit_bytes=VMEM_LIMIT),
    )(mem, g_src, w_mk, w_mv, g_mk)


def _in_proj_kernel(x_ref, gmix_ref, w_ref, gsgu_ref, gqk_ref, seg_ref, segt_ref, cos_ref, sin_ref,
                    u_ref, va_ref, q_ref, k_ref, v_ref):
    xn = (_rms(x_ref[...]) * gmix_ref[...]).astype(BF16)
    u_ref[...] = _gelu(_dot(xn, w_ref[:, 0:A_WIDTH])).astype(BF16)
    va_ref[...] = _rms(_gelu(_dot(xn, w_ref[:, A_WIDTH:Q_OFF]))) * gsgu_ref[...]
    v_ref[...] = _dot(xn, w_ref[:, V_OFF:IN_WIDTH])

    qk = _dot(xn, w_ref[:, Q_OFF:V_OFF])
    ssq = _dot((qk * qk).astype(BF16), seg_ref[...])
    inv = lax.rsqrt(ssq * (1.0 / B_HEAD_DIM) + EPS)
    inv_hi = inv.astype(BF16)
    inv_lo = (inv - inv_hi.astype(F32)).astype(BF16)
    inv_b = _dot(inv_hi, segt_ref[...]) + _dot(inv_lo, segt_ref[...])
    qkn = (qk * inv_b) * gqk_ref[...]

    cos = cos_ref[...]
    sin = sin_ref[...]
    lane = lax.broadcasted_iota(I32, cos.shape, 1)
    first_half = (lane % B_HEAD_DIM) < (B_HEAD_DIM // 2)
    for g in range(QK_WIDTH // LANES):
        xg = qkn[:, g * LANES:(g + 1) * LANES]
        rot = jnp.where(first_half, pltpu.roll(xg, LANES - B_HEAD_DIM // 2, 1),
                        pltpu.roll(xg, B_HEAD_DIM // 2, 1))
        r = xg * cos + rot * sin
        if g < B_WIDTH // LANES:
            q_ref[:, g * LANES:(g + 1) * LANES] = (r * ATTN_SCALE).astype(BF16)
        else:
            k_ref[...] = r


def _in_proj(x, g_mix, w_in, g_sgu, g_qk, seg, segt, cos_tab, sin_tab, tile, name):
    t = x.shape[0]
    n = t // tile
    tab_tiles = cos_tab.shape[0] // tile
    row = lambda w: pl.BlockSpec((tile, w), lambda i: (i, 0))
    tab = pl.BlockSpec((tile, LANES), lambda i: (i % tab_tiles, 0))
    return pl.pallas_call(
        _in_proj_kernel, grid=(n,), name=name,
        in_specs=[row(D_MODEL), _resident(g_mix.shape), _resident(w_in.shape), _resident(g_sgu.shape),
                  _resident(g_qk.shape), _resident(seg.shape), _resident(segt.shape), tab, tab],
        out_specs=[row(A_WIDTH), row(A_WIDTH), row(B_WIDTH), row(KV_WIDTH), row(KV_WIDTH)],
        out_shape=(jax.ShapeDtypeStruct((t, A_WIDTH), BF16), jax.ShapeDtypeStruct((t, A_WIDTH), F32),
                   jax.ShapeDtypeStruct((t, B_WIDTH), BF16), jax.ShapeDtypeStruct((t, KV_WIDTH), F32),
                   jax.ShapeDtypeStruct((t, KV_WIDTH), F32)),
        compiler_params=_params(),
    )(x, g_mix, w_in, g_sgu, g_qk, seg, segt, cos_tab, sin_tab)


def _split_heads(ref_dst_lo, ref_dst_hi, rows, x, swap):
    lane = lax.broadcasted_iota(I32, x.shape, 1)
    low = lane < B_HEAD_DIM
    zero = jnp.zeros_like(x)
    ref_dst_lo[0, rows, :] = jnp.where(low, x, zero).astype(BF16)
    ref_dst_hi[0, rows, :] = jnp.where(low, zero, swap).astype(BF16)
    ref_dst_lo[1, rows, :] = jnp.where(low, swap, zero).astype(BF16)
    ref_dst_hi[1, rows, :] = jnp.where(low, zero, x).astype(BF16)


def _sgu_chunk(u, va, ws_ref, bias):
    lane = lax.broadcasted_iota(I32, (MLP_CHUNK, LANES), 1)
    low = lane < A_HEAD_DIM
    outs = []
    for p in range(A_WIDTH // LANES):
        sl = slice(p * LANES, (p + 1) * LANES)
        v2 = va[:, sl]
        zero = jnp.zeros_like(v2)
        mixed = (_dot(ws_ref[2 * p], jnp.where(low, v2, zero).astype(BF16))
                 + _dot(ws_ref[2 * p + 1], jnp.where(low, zero, v2).astype(BF16)))
        outs.append(u[:, sl].astype(F32) * (mixed + bias[:, sl]))
    return jnp.concatenate(outs, axis=1)


def _attend(q2, kl, kh, vl, vh, mask, sink_even, sink_odd):
    outs = []
    for kk, vv, sink in ((kl, vl, sink_even), (kh, vh, sink_odd)):
        s = _dot_nt(q2, kk)
        if mask is not None:
            s = jnp.where(mask, s, NEG_BIG)
        m = jnp.maximum(jnp.max(s, axis=-1, keepdims=True), sink)
        p = jnp.exp(s - m)
        den = jnp.sum(p, axis=-1, keepdims=True) + jnp.exp(sink - m)
        outs.append(_dot(p.astype(BF16), vv) * (1.0 / den))
    return outs[0] + outs[1]


def _out_proj(oa, ob, ga_ref, gb_ref, wout_ref, x):
    a = (_rms(oa) * ga_ref[...]).astype(BF16)
    b = (_rms(ob) * gb_ref[...]).astype(BF16)
    return x + _dot(a, wout_ref[0:A_WIDTH, :]) + _dot(b, wout_ref[A_WIDTH:, :])


def _mixer_prompt_kernel(tiles_per_seq, sinks_ref, x_ref, u_ref, va_ref, q_ref, kc_ref, vc_ref, kp_ref, vp_ref,
                         ws_ref, bias_ref, ga_ref, gb_ref, wout_ref, o_ref,
                         kl_s, kh_s, vl_s, vh_s, oa_s, ob_s):
    tile = x_ref.shape[0]
    n_sub = tile // MLP_CHUNK
    seq_start = (pl.program_id(0) % tiles_per_seq) == 0

    for src_p, src_c, dl, dh in ((kp_ref, kc_ref, kl_s, kh_s), (vp_ref, vc_ref, vl_s, vh_s)):
        prev = src_p[...]
        cur = src_c[...]
        _split_heads(dl, dh, slice(0, WINDOW), prev, pltpu.roll(prev, B_HEAD_DIM, 1))
        _split_heads(dl, dh, slice(WINDOW, WINDOW + tile), cur, pltpu.roll(cur, B_HEAD_DIM, 1))

    qc = lax.broadcasted_iota(I32, (MLP_CHUNK, 2 * MLP_CHUNK), 0) // CHUNK
    kc = lax.broadcasted_iota(I32, (MLP_CHUNK, 2 * MLP_CHUNK), 1) // CHUNK
    window_mask = (kc >= qc) & (kc <= qc + 2)
    bias = bias_ref[...]

    def sub(j, carry):
        r0 = pl.multiple_of(j * MLP_CHUNK, MLP_CHUNK)
        rows = pl.ds(r0, MLP_CHUNK)
        keys = pl.ds(r0, 2 * MLP_CHUNK)
        first_key_chunk = jnp.where(jnp.logical_and(seq_start, j == 0), 2, 0)
        mask = window_mask & (kc >= first_key_chunk)
        oa_s[rows, :] = _sgu_chunk(u_ref[rows, :], va_ref[rows, :], ws_ref, bias)
        for p in range(B_WIDTH // LANES):
            h = p // (B_WIDTH // LANES // B_KV_HEADS)
            sl = slice(p * LANES, (p + 1) * LANES)
            ob_s[rows, sl] = _attend(q_ref[rows, sl], kl_s[h, keys, :], kh_s[h, keys, :],
                                     vl_s[h, keys, :], vh_s[h, keys, :], mask,
                                     sinks_ref[2 * p], sinks_ref[2 * p + 1])
        return carry

    lax.fori_loop(0, n_sub, sub, 0)
    o_ref[...] = _out_proj(oa_s[...], ob_s[...], ga_ref, gb_ref, wout_ref, x_ref[...])


def _mixer_prompt(x, u, va, q, k, v, sinks, ws, bias, g_a, g_b, w_out, seq, tile):
    t = x.shape[0]
    n = t // tile
    tiles_per_seq = seq // tile
    per = tile // WINDOW
    row = lambda w: pl.BlockSpec((tile, w), lambda i, s: (i, 0))
    prev = pl.BlockSpec((WINDOW, KV_WIDTH), lambda i, s: (jnp.maximum(i * per - 1, 0), 0))
    res = lambda a: pl.BlockSpec(a.shape, lambda i, s: (0,) * a.ndim, pipeline_mode=pl.Buffered(1))
    grid_spec = pltpu.PrefetchScalarGridSpec(
        num_scalar_prefetch=1, grid=(n,),
        in_specs=[row(D_MODEL), row(A_WIDTH), row(A_WIDTH), row(B_WIDTH), row(KV_WIDTH), row(KV_WIDTH), prev, prev,
                  res(ws), res(bias), res(g_a), res(g_b), res(w_out)],
        out_specs=row(D_MODEL),
        scratch_shapes=[pltpu.VMEM((B_KV_HEADS, WINDOW + tile, LANES), BF16) for _ in range(4)]
        + [pltpu.VMEM((tile, A_WIDTH), F32), pltpu.VMEM((tile, B_WIDTH), F32)])
    return pl.pallas_call(
        functools.partial(_mixer_prompt_kernel, tiles_per_seq), grid_spec=grid_spec, name="mixer_prompt",
        out_shape=jax.ShapeDtypeStruct((t, D_MODEL), F32), compiler_params=_params(),
    )(sinks, x, u, va, q, k, v, k, v, ws, bias, g_a, g_b, w_out)


def _mixer_sample_kernel(dec_seq, sinks_ref, x_ref, u_ref, va_ref, q_ref, kc_ref, vc_ref, ck_ref, cv_ref,
                         ws_ref, bias_ref, ga_ref, gb_ref, wout_ref, o_ref,
                         kl_s, kh_s, vl_s, vh_s, ob_s):
    tile = x_ref.shape[0]
    n_seq = tile // dec_seq
    n_cache = ck_ref.shape[1]
    n_keys = n_cache + dec_seq
    oa = _sgu_chunk(u_ref[...], va_ref[...], ws_ref, bias_ref[...])
    for b in range(n_seq):
        rows = slice(b * dec_seq, (b + 1) * dec_seq)
        for src_c, src_n, dl, dh in ((ck_ref, kc_ref, kl_s, kh_s), (cv_ref, vc_ref, vl_s, vh_s)):
            old = src_c[b]
            new = src_n[rows, :]
            _split_heads(dl, dh, slice(0, n_cache), old, pltpu.roll(old, B_HEAD_DIM, 1))
            _split_heads(dl, dh, slice(n_cache, n_keys), new, pltpu.roll(new, B_HEAD_DIM, 1))
        for p in range(B_WIDTH // LANES):
            h = p // (B_WIDTH // LANES // B_KV_HEADS)
            sl = slice(p * LANES, (p + 1) * LANES)
            ob_s[rows, sl] = _attend(q_ref[rows, sl], kl_s[h], kh_s[h], vl_s[h], vh_s[h], None,
                                     sinks_ref[2 * p], sinks_ref[2 * p + 1])
    o_ref[...] = _out_proj(oa, ob_s[...], ga_ref, gb_ref, wout_ref, x_ref[...])


def _mixer_sample(x, u, va, q, k, v, cache_k, cache_v, sinks, ws, bias, g_a, g_b, w_out, dec_seq, tile):
    t = x.shape[0]
    n = t // tile
    n_seq = tile // dec_seq
    n_cache = cache_k.shape[1]
    row = lambda w: pl.BlockSpec((tile, w), lambda i, s: (i, 0))
    cache = pl.BlockSpec((n_seq, n_cache, KV_WIDTH), lambda i, s: (i, 0, 0))
    res = lambda a: pl.BlockSpec(a.shape, lambda i, s: (0,) * a.ndim, pipeline_mode=pl.Buffered(1))
    grid_spec = pltpu.PrefetchScalarGridSpec(
        num_scalar_prefetch=1, grid=(n,),
        in_specs=[row(D_MODEL), row(A_WIDTH), row(A_WIDTH), row(B_WIDTH), row(KV_WIDTH), row(KV_WIDTH), cache, cache,
                  res(ws), res(bias), res(g_a), res(g_b), res(w_out)],
        out_specs=row(D_MODEL),
        scratch_shapes=[pltpu.VMEM((B_KV_HEADS, n_cache + dec_seq, LANES), BF16) for _ in range(4)]
        + [pltpu.VMEM((tile, B_WIDTH), F32)])
    return pl.pallas_call(
        functools.partial(_mixer_sample_kernel, dec_seq), grid_spec=grid_spec, name="mixer_sample",
        out_shape=jax.ShapeDtypeStruct((t, D_MODEL), F32), compiler_params=_params(),
    )(sinks, x, u, va, q, k, v, cache_k, cache_v, ws, bias, g_a, g_b, w_out)


def _memory_router_kernel(rows_per_mem, x_ref, mk_ref, mv_ref, gin_ref, wq_ref, gq_ref, wo_ref,
                          gmoe_ref, wr_ref, br_ref,
                          x2_ref, hn_ref, idx_ref, gate_ref, rank_ref, cnt_ref, o_s, base_s):
    tile = x_ref.shape[0]
    x = x_ref[...]
    qz = _dot((_rms(x) * gin_ref[...]).astype(BF16), wq_ref[...])
    gq = gq_ref[...]
    for h in range(MEM_HEADS):
        sl = slice(h * MEM_HEAD_DIM, (h + 1) * MEM_HEAD_DIM)
        qh = (_rms(qz[:, sl]) * gq).astype(BF16)
        for r in range(tile // rows_per_mem):
            rows = slice(r * rows_per_mem, (r + 1) * rows_per_mem)
            s = _dot_nt(qh[rows], mk_ref[r, :, sl].astype(BF16)) * MEM_SCALE
            p = jnp.exp(s - jnp.max(s, axis=-1, keepdims=True))
            den = jnp.sum(p, axis=-1, keepdims=True)
            o_s[rows, sl] = _dot(p.astype(BF16), mv_ref[r, :, sl].astype(BF16)) * (1.0 / den)
    x2 = x + _dot(o_s[...].astype(BF16), wo_ref[...])
    x2_ref[...] = x2
    hn = _rms(x2) * gmoe_ref[...]
    hn_ref[...] = _pack_rows(hn)

    logits = jnp.dot(hn, wr_ref[...], preferred_element_type=F32, precision=lax.Precision.HIGHEST) + br_ref[...]
    lane = lax.broadcasted_iota(I32, (tile, LANES), 1)
    lane_f = lane.astype(F32)
    work = jnp.where(lane < N_EXPERTS, logits, -jnp.inf)
    idx_out = jnp.zeros((tile, LANES), F32)
    val_out = jnp.zeros((tile, LANES), F32)
    hot = jnp.zeros((tile, LANES), F32)
    top = None
    picks = []
    for k in range(TOP_K):
        m = jnp.max(work, axis=-1, keepdims=True)
        pick = jnp.min(jnp.where(work == m, lane_f, float(LANES)), axis=-1, keepdims=True)
        chosen = lane_f == pick
        if top is None:
            top = m
        idx_out = jnp.where(lane == k, pick, idx_out)
        val_out = jnp.where(lane == k, jnp.exp(m - top), val_out)
        hot = jnp.where(chosen, 1.0, hot)
        work = jnp.where(chosen, -jnp.inf, work)
        picks.append(chosen)
    idx_ref[...] = idx_out.astype(I32)
    gate_ref[...] = val_out * (1.0 / jnp.sum(val_out, axis=-1, keepdims=True))

    @pl.when(pl.program_id(0) == 0)
    def _():
        base_s[...] = jnp.zeros_like(base_s)

    earlier = (lax.broadcasted_iota(I32, (tile, tile), 1) < lax.broadcasted_iota(I32, (tile, tile), 0))
    pos = _dot(jnp.where(earlier, 1.0, 0.0).astype(BF16), hot.astype(BF16)) + base_s[0:1, :]
    rank_out = jnp.zeros((tile, LANES), F32)
    for k in range(TOP_K):
        rank_out = jnp.where(lane == k, jnp.sum(jnp.where(picks[k], pos, 0.0), axis=-1, keepdims=True), rank_out)
    rank_ref[...] = rank_out.astype(I32)
    total = base_s[0:1, :] + jnp.sum(hot, axis=0, keepdims=True)
    base_s[...] = jnp.broadcast_to(total, base_s.shape)
    cnt_ref[...] = jnp.broadcast_to(total, cnt_ref.shape).astype(I32)


def _memory_router(x, mk, mv, rows_per_mem, g_in, w_mq, g_mq, w_mo, g_moe, w_r, b_r, tile, name):
    t = x.shape[0]
    n = t // tile
    mems = tile // rows_per_mem if rows_per_mem <= tile else 1
    per_mem_tiles = max(rows_per_mem // tile, 1)
    rpm = min(rows_per_mem, tile)
    row = lambda w: pl.BlockSpec((tile, w), lambda i: (i, 0))
    mem = pl.BlockSpec((mems, mk.shape[1], MEM_WIDTH), lambda i: (i // per_mem_tiles, 0, 0))
    res = lambda a: pl.BlockSpec(a.shape, lambda i: (0,) * a.ndim, pipeline_mode=pl.Buffered(1))
    small = jax.ShapeDtypeStruct((t, LANES), I32)
    return pl.pallas_call(
        functools.partial(_memory_router_kernel, rpm), grid=(n,), name=name,
        in_specs=[row(D_MODEL), mem, mem, res(g_in), res(w_mq), res(g_mq), res(w_mo), res(g_moe), res(w_r), res(b_r)],
        out_specs=[row(D_MODEL), row(D_MODEL // 2), row(LANES), row(LANES), row(LANES),
                   pl.BlockSpec((8, LANES), lambda i: (0, 0))],
        out_shape=(jax.ShapeDtypeStruct((t, D_MODEL), F32), jax.ShapeDtypeStruct((t, D_MODEL // 2), U32),
                   small, jax.ShapeDtypeStruct((t, LANES), F32), small, jax.ShapeDtypeStruct((8, LANES), I32)),
        scratch_shapes=[pltpu.VMEM((tile, MEM_WIDTH), F32), pltpu.VMEM((8, LANES), F32)],
        compiler_params=_params(),
    )(x, mk, mv, g_in, w_mq, g_mq, w_mo, g_moe, w_r, b_r)


def _dispatch_kernel(dest_ref, hn_ref, *rest):
    xb_ref, sem = rest[-2:]
    tile = hn_ref.shape[0]

    def copy(t, k):
        return pltpu.make_async_copy(hn_ref.at[pl.ds(t, 1), :],
                                     xb_ref.at[pl.ds(dest_ref[0, t * TOP_K + k], 1), :], sem)

    def start(t, carry):
        for k in range(TOP_K):
            copy(t, k).start()
        return carry

    def wait(t, carry):
        for k in range(TOP_K):
            copy(t, k).wait()
        return carry

    lax.fori_loop(0, tile, start, 0)
    lax.fori_loop(0, tile, wait, 0)


def _dispatch(hn, dest, xb, n_rows, tile):
    t = hn.shape[0]
    n = t // tile
    dest = dest.reshape(n, 1, tile * TOP_K)
    in_specs = [pl.BlockSpec((None, 1, tile * TOP_K), lambda i: (i, 0, 0), memory_space=pltpu.SMEM),
                pl.BlockSpec((tile, hn.shape[1]), lambda i: (i, 0))]
    args = [dest, hn]
    if xb is not None:
        in_specs.append(pl.BlockSpec(memory_space=pl.ANY))
        args.append(xb)
    return pl.pallas_call(
        _dispatch_kernel, grid=(n,), name="dispatch",
        in_specs=in_specs,
        out_specs=pl.BlockSpec(memory_space=pl.ANY),
        out_shape=jax.ShapeDtypeStruct((n_rows, hn.shape[1]), hn.dtype),
        scratch_shapes=[pltpu.SemaphoreType.DMA(())],
        input_output_aliases={} if xb is None else {2: 0},
        compiler_params=_params(),
    )(*args)


def _experts_kernel(be_ref, nv_ref, xi_ref, x_ref, wg_ref, wu_ref, bg_ref, bu_ref, wd_ref, bd_ref, y_ref, xb_s, h_s):
    del be_ref, xi_ref
    b = pl.program_id(0)
    s = pl.program_id(1)
    n_ff = D_FF // FF_TILE
    valid = nv_ref[b]
    n_sub = (valid + EXPERT_SUB - 1) // EXPERT_SUB

    @pl.when(s == 0)
    def _():
        def cast(i, carry):
            rows = pl.ds(pl.multiple_of(i * EXPERT_SUB, EXPERT_SUB), EXPERT_SUB)
            keep = (i * EXPERT_SUB + lax.broadcasted_iota(I32, (EXPERT_SUB, 1), 0)) < valid
            lo, hi = _unpack_rows(x_ref[rows, :])
            xb_s[rows, 0:D_MODEL // 2] = jnp.where(keep, lo, 0.0).astype(BF16)
            xb_s[rows, D_MODEL // 2:] = jnp.where(keep, hi, 0.0).astype(BF16)
            return carry
        lax.fori_loop(0, n_sub, cast, 0)

    @pl.when(s < n_ff)
    def _():
        wg = wg_ref[0].astype(BF16)
        wu = wu_ref[0].astype(BF16)
        bg = bg_ref[0]
        bu = bu_ref[0]

        def up(i, carry):
            rows = pl.ds(pl.multiple_of(i * EXPERT_SUB, EXPERT_SUB), EXPERT_SUB)
            xs = xb_s[rows, :]
            gate = jnp.minimum(_dot(xs, wg) + bg, SWIGLU_LIMIT)
            lin = jnp.clip(_dot(xs, wu) + bu, -SWIGLU_LIMIT, SWIGLU_LIMIT)
            act = (lin + 1.0) * (gate * (1.0 / (1.0 + jnp.exp(-SWIGLU_ALPHA * gate))))
            h_s[s, rows, :] = act.astype(BF16)
            return carry
        lax.fori_loop(0, n_sub, up, 0)

    @pl.when(s >= n_ff)
    def _():
        wd = wd_ref[0].astype(BF16)
        bd = bd_ref[0]

        def down(i, carry):
            rows = pl.ds(pl.multiple_of(i * EXPERT_SUB, EXPERT_SUB), EXPERT_SUB)
            h = jnp.concatenate([h_s[f, rows, :] for f in range(n_ff)], axis=1)
            y_ref[rows, :] = _dot(h, wd) + bd
            return carry
        lax.fori_loop(0, n_sub, down, 0)


def _experts(xb, block_expert, block_valid, block_index, w_gu, b_gu, w_d, b_d):
    n_blocks = xb.shape[0] // EXPERT_ROWS
    n_ff = D_FF // FF_TILE
    n_out = D_MODEL // OUT_TILE

    def ff(b, s, nv):
        return jnp.where(nv[b] > 0, jnp.minimum(s, n_ff - 1), n_ff - 1)

    def oc(b, s, nv):
        return jnp.where(nv[b] > 0, jnp.maximum(s - n_ff, 0), n_out - 1)

    b_gu3 = b_gu.reshape(N_EXPERTS, 1, 2 * D_FF)
    b_d3 = b_d.reshape(N_EXPERTS, 1, D_MODEL)
    grid_spec = pltpu.PrefetchScalarGridSpec(
        num_scalar_prefetch=3, grid=(n_blocks, n_ff + n_out),
        in_specs=[
            pl.BlockSpec((EXPERT_ROWS, D_MODEL // 2), lambda b, s, be, nv, xi: (xi[b], 0)),
            pl.BlockSpec((1, D_MODEL, FF_TILE), lambda b, s, be, nv, xi: (be[b], 0, ff(b, s, nv))),
            pl.BlockSpec((1, D_MODEL, FF_TILE), lambda b, s, be, nv, xi: (be[b], 0, ff(b, s, nv) + n_ff)),
            pl.BlockSpec((1, 1, FF_TILE), lambda b, s, be, nv, xi: (be[b], 0, ff(b, s, nv))),
            pl.BlockSpec((1, 1, FF_TILE), lambda b, s, be, nv, xi: (be[b], 0, ff(b, s, nv) + n_ff)),
            pl.BlockSpec((1, D_FF, OUT_TILE), lambda b, s, be, nv, xi: (be[b], 0, oc(b, s, nv))),
            pl.BlockSpec((1, 1, OUT_TILE), lambda b, s, be, nv, xi: (be[b], 0, oc(b, s, nv))),
        ],
        out_specs=pl.BlockSpec((EXPERT_ROWS, OUT_TILE), lambda b, s, be, nv, xi: (xi[b], oc(b, s, nv))),
        scratch_shapes=[pltpu.VMEM((EXPERT_ROWS, D_MODEL), BF16), pltpu.VMEM((n_ff, EXPERT_ROWS, FF_TILE), BF16)])
    return pl.pallas_call(
        _experts_kernel, grid_spec=grid_spec, name="experts",
        out_shape=jax.ShapeDtypeStruct((n_blocks * EXPERT_ROWS, D_MODEL), F32),
        compiler_params=_params(2),
    )(block_expert, block_valid, block_index, xb, w_gu, w_gu, b_gu3, b_gu3, w_d, b_d3)


def _combine_kernel(dest_ref, x_ref, gate_ref, yb_ref, o_ref, buf, sem):
    tile = x_ref.shape[0]

    def copy(t, k):
        return pltpu.make_async_copy(yb_ref.at[pl.ds(dest_ref[0, t * TOP_K + k], 1), :],
                                     buf.at[k, pl.ds(t, 1), :], sem)

    def start(t, carry):
        for k in range(TOP_K):
            copy(t, k).start()
        return carry

    def wait(t, carry):
        for k in range(TOP_K):
            copy(t, k).wait()
        return carry

    lax.fori_loop(0, tile, start, 0)
    lax.fori_loop(0, tile, wait, 0)
    gate = gate_ref[...]
    acc = x_ref[...]
    for k in range(TOP_K):
        acc = acc + gate[:, k:k + 1] * buf[k]
    o_ref[...] = acc


def _combine(x, gate, dest, yb, tile):
    t = x.shape[0]
    n = t // tile
    dest = dest.reshape(n, 1, tile * TOP_K)
    return pl.pallas_call(
        _combine_kernel, grid=(n,), name="combine",
        in_specs=[pl.BlockSpec((None, 1, tile * TOP_K), lambda i: (i, 0, 0), memory_space=pltpu.SMEM),
                  pl.BlockSpec((tile, D_MODEL), lambda i: (i, 0)),
                  pl.BlockSpec((tile, LANES), lambda i: (i, 0)),
                  pl.BlockSpec(memory_space=pl.ANY)],
        out_specs=pl.BlockSpec((tile, D_MODEL), lambda i: (i, 0)),
        out_shape=jax.ShapeDtypeStruct((t, D_MODEL), F32),
        scratch_shapes=[pltpu.VMEM((TOP_K, tile, D_MODEL), F32), pltpu.SemaphoreType.DMA(())],
        compiler_params=_params(),
    )(dest, x, gate, yb)


def _rope_tables(pos):
    half = B_HEAD_DIM // 2
    inv_freq = ROPE_THETA ** (-jnp.arange(half, dtype=F32) / half)
    ang = pos.astype(F32)[:, None] * inv_freq[None, :]
    cos = jnp.cos(ang)
    sin = jnp.sin(ang)
    return jnp.tile(cos, (1, LANES // half)), jnp.tile(jnp.concatenate([-sin, sin], axis=1), (1, LANES // B_HEAD_DIM))


def _layer(l, xp, xs, cache_swa_k, cache_swa_v, cache_mem_k, cache_mem_v, mem_prompt,
           g_mix, w_in, g_sgu, w_s, b_s, g_q, g_k, sinks, g_out_a, g_out_b, w_out,
           g_mem_in, g_mem_src, w_mq, w_mk, w_mv, g_mq, g_mk, w_mo,
           g_moe, w_router, b_router, w_gate_up, b_gate_up, w_down, b_down):
    n_b, seq, _ = xp.shape
    n_db, dec_seq, _ = xs.shape
    tp = n_b * seq
    ts = n_db * dec_seq
    tile = min(TOKEN_TILE, seq)
    row = lambda a: a[l].reshape(1, -1)

    xp2 = xp.reshape(tp, D_MODEL)
    xs2 = xs.reshape(ts, D_MODEL)
    head = jnp.arange(QK_WIDTH, dtype=I32) // B_HEAD_DIM
    seg = (head[:, None] == jnp.arange(LANES, dtype=I32)[None, :]).astype(BF16)
    g_qk = jnp.concatenate([jnp.tile(g_q[l], B_HEADS), jnp.tile(g_k[l], B_KV_HEADS)]).reshape(1, -1)
    w_in_b = w_in[l].astype(BF16)
    cos_p, sin_p = _rope_tables(jnp.arange(seq, dtype=I32))
    cos_s, sin_s = _rope_tables(PAST_LEN + jnp.arange(SAMPLE_TILE, dtype=I32) % dec_seq)
    up, vap, qp, kp, vp = _in_proj(xp2, row(g_mix), w_in_b, row(g_sgu), g_qk, seg, seg.T, cos_p, sin_p,
                                   tile, "in_proj_prompt")
    us, vas, qs, ks, vs = _in_proj(xs2, row(g_mix), w_in_b, row(g_sgu), g_qk, seg, seg.T, cos_s, sin_s,
                                   SAMPLE_TILE, "in_proj_sample")

    w_out_b = w_out[l].astype(BF16)
    tri = jnp.tril(jnp.ones((MLP_CHUNK, MLP_CHUNK), bool))
    ws_p = jnp.where(tri[None], w_s[l], 0.0).astype(BF16)
    bias_p = jnp.repeat(b_s[l].T, A_HEAD_DIM, axis=1)
    x1p = _mixer_prompt(xp2, up, vap, qp, kp, vp, sinks[l], ws_p, bias_p, row(g_out_a), row(g_out_b), w_out_b, seq, tile)

    reps = SAMPLE_TILE // dec_seq
    tri_s = jnp.tril(jnp.ones((dec_seq, dec_seq), bool))
    ws_small = jnp.where(tri_s[None], w_s[l][:, :dec_seq, :dec_seq], 0.0)
    ws_s = jnp.einsum("ab,hts->hatbs", jnp.eye(reps, dtype=F32), ws_small).reshape(A_HEADS, SAMPLE_TILE, SAMPLE_TILE).astype(BF16)
    bias_s = jnp.tile(jnp.repeat(b_s[l][:, :dec_seq].T, A_HEAD_DIM, axis=1), (reps, 1))
    ck = cache_swa_k[l].reshape(n_db, -1, KV_WIDTH)
    cv = cache_swa_v[l].reshape(n_db, -1, KV_WIDTH)
    x1s = _mixer_sample(xs2, us, vas, qs, ks, vs, ck, cv, sinks[l], ws_s, bias_s,
                        row(g_out_a), row(g_out_b), w_out_b, dec_seq, SAMPLE_TILE)

    mk_p, mv_p = _mem_kv(mem_prompt.reshape(-1, D_MODEL), row(g_mem_src), w_mk[l].astype(BF16), w_mv[l].astype(BF16), row(g_mk))
    n_mem = mem_prompt.shape[1]
    w_r = jnp.pad(w_router[l], ((0, 0), (0, LANES - N_EXPERTS)))
    b_r = jnp.pad(b_router[l], (0, LANES - N_EXPERTS)).reshape(1, -1)
    mem_args = (row(g_mem_in), w_mq[l].astype(BF16), row(g_mq), w_mo[l].astype(BF16), row(g_moe), w_r, b_r)
    x2p, hnp, idxp, gatep, rankp, cntp = _memory_router(
        x1p, mk_p.reshape(n_b, n_mem, MEM_WIDTH), mv_p.reshape(n_b, n_mem, MEM_WIDTH),
        seq, *mem_args, tile, "memory_router_prompt")
    x2s, hns, idxs, gates, ranks, cnts = _memory_router(
        x1s, cache_mem_k[l].reshape(n_db, -1, MEM_WIDTH), cache_mem_v[l].reshape(n_db, -1, MEM_WIDTH),
        dec_seq, *mem_args, SAMPLE_TILE, "memory_router_sample")

    cnt_p = cntp[0, :N_EXPERTS]
    cnt_s = cnts[0, :N_EXPERTS]
    total = cnt_p + cnt_s
    nblk = (total + EXPERT_ROWS - 1) // EXPERT_ROWS
    blk_end = jnp.cumsum(nblk)
    row_start = (blk_end - nblk) * EXPERT_ROWS
    n_blocks = (tp + ts) * TOP_K // EXPERT_ROWS + N_EXPERTS
    bidx = jnp.arange(n_blocks, dtype=I32)
    used = bidx < blk_end[-1]
    last = jnp.maximum(blk_end[-1] - 1, 0)
    bsafe = jnp.minimum(bidx, last)
    block_expert = jnp.minimum(jnp.searchsorted(blk_end, bsafe, side="right"), N_EXPERTS - 1).astype(I32)
    within = bsafe - (blk_end - nblk)[block_expert]
    block_valid = jnp.where(used, jnp.clip(total[block_expert] - within * EXPERT_ROWS, 0, EXPERT_ROWS), 0).astype(I32)
    dest_p = row_start[idxp[:, :TOP_K]] + rankp[:, :TOP_K]
    dest_s = (row_start + cnt_p)[idxs[:, :TOP_K]] + ranks[:, :TOP_K]

    xb = _dispatch(hnp, dest_p, None, n_blocks * EXPERT_ROWS, min(ROW_TILE, tp))
    xb = _dispatch(hns, dest_s, xb, n_blocks * EXPERT_ROWS, min(ROW_TILE, ts))
    yb = _experts(xb, block_expert, block_valid, bsafe.astype(I32), w_gate_up[l], b_gate_up[l], w_down[l], b_down[l])
    yp = _combine(x2p, gatep, dest_p, yb, min(ROW_TILE, tp))
    ys = _combine(x2s, gates, dest_s, yb, min(ROW_TILE, ts))

    new = dict(
        swa_k_p=kp.reshape(n_b, seq, B_KV_HEADS, B_HEAD_DIM)[:, seq - WINDOW:],
        swa_v_p=vp.reshape(n_b, seq, B_KV_HEADS, B_HEAD_DIM)[:, seq - WINDOW:],
        mem_k_p=mk_p.reshape(n_b, n_mem, MEM_HEADS, MEM_HEAD_DIM),
        mem_v_p=mv_p.reshape(n_b, n_mem, MEM_HEADS, MEM_HEAD_DIM),
        swa_k_s=ks.reshape(n_db, dec_seq, B_KV_HEADS, B_HEAD_DIM),
        swa_v_s=vs.reshape(n_db, dec_seq, B_KV_HEADS, B_HEAD_DIM),
        sgu_v_s=vas.reshape(n_db, dec_seq, A_HEADS, A_HEAD_DIM))
    return yp.reshape(n_b, seq, D_MODEL), ys.reshape(n_db, dec_seq, D_MODEL), new


def kernel(x_prompt, x_sample, cache_swa_k, cache_swa_v, cache_mem_k, cache_mem_v, mem_prompt, g_mix, w_in, g_sgu, w_s, b_s, g_q, g_k, sinks, g_out_a, g_out_b, w_out, g_mem_in, g_mem_src, w_mq, w_mk, w_mv, g_mq, g_mk, w_mo, g_moe, w_router, b_router, w_gate_up, b_gate_up, w_down, b_down):
    xp, xs = x_prompt, x_sample
    news = []
    for l in range(g_mix.shape[0]):
        xp, xs, new = _layer(l, xp, xs, cache_swa_k, cache_swa_v, cache_mem_k, cache_mem_v, mem_prompt,
                             g_mix, w_in, g_sgu, w_s, b_s, g_q, g_k, sinks, g_out_a, g_out_b, w_out,
                             g_mem_in, g_mem_src, w_mq, w_mk, w_mv, g_mq, g_mk, w_mo,
                             g_moe, w_router, b_router, w_gate_up, b_gate_up, w_down, b_down)
        news.append(new)
    stack = lambda name: jnp.stack([n[name] for n in news], 0)
    return (xp, xs, stack("swa_k_p"), stack("swa_v_p"), stack("mem_k_p"), stack("mem_v_p"),
            stack("swa_k_s"), stack("swa_v_s"), stack("sgu_v_s"))
```

```python
import functools

import jax
import jax.numpy as jnp
from jax import lax
from jax.experimental import pallas as pl
from jax.experimental.pallas import tpu as pltpu

F32 = jnp.float32
BF16 = jnp.bfloat16
I32 = jnp.int32
U32 = jnp.uint32

D_MODEL = 2048
PAST_LEN = 2048
CHUNK = 64
EPS = 1e-6
A_HEADS = 16
A_HEAD_DIM = 64
A_WIDTH = A_HEADS * A_HEAD_DIM
MLP_CHUNK = 128
B_HEADS = 16
B_KV_HEADS = 2
B_HEAD_DIM = 64
B_WIDTH = B_HEADS * B_HEAD_DIM
KV_WIDTH = B_KV_HEADS * B_HEAD_DIM
WINDOW = 128
ROPE_THETA = 10000.0
ATTN_SCALE = B_HEAD_DIM ** -0.5
Q_OFF = 2 * A_WIDTH
K_OFF = Q_OFF + B_WIDTH
V_OFF = K_OFF + KV_WIDTH
IN_WIDTH = V_OFF + KV_WIDTH
QK_WIDTH = B_WIDTH + KV_WIDTH
MEM_HEADS = 4
MEM_HEAD_DIM = 128
MEM_WIDTH = MEM_HEADS * MEM_HEAD_DIM
MEM_SCALE = MEM_HEAD_DIM ** -0.5
N_EXPERTS = 32
TOP_K = 4
D_FF = D_MODEL
SWIGLU_LIMIT = 7.0
SWIGLU_ALPHA = 1.702
NEG_BIG = -1e30

LANES = 128
VMEM_LIMIT = 56 * 1024 * 1024

TOKEN_TILE = 512
SAMPLE_TILE = 128
EXPERT_ROWS = 1024
EXPERT_SUB = 256
FF_TILE = 512
OUT_TILE = 512
W_SPLIT = 2
ROW_TILE = 128


def _dot(a, b):
    return jnp.dot(a, b, preferred_element_type=F32)


def _dot_nt(a, b):
    return lax.dot_general(a, b, (((1,), (1,)), ((), ())), preferred_element_type=F32)


def _rms(x):
    return x * lax.rsqrt(jnp.mean(x * x, axis=-1, keepdims=True) + EPS)


def _gelu(x):
    return 0.5 * x * (1.0 + lax.erf(x * 0.7071067811865476))


def _pack_rows(x):
    c = x.shape[1] // 2
    bits = pltpu.bitcast(x.astype(BF16).astype(F32), U32)
    return (bits[:, :c] >> 16) | (bits[:, c:] & jnp.uint32(0xFFFF0000))


def _unpack_rows(w):
    return pltpu.bitcast(w << 16, F32), pltpu.bitcast(w & jnp.uint32(0xFFFF0000), F32)


def _params(n_axes=1):
    return pltpu.CompilerParams(dimension_semantics=("arbitrary",) * n_axes,
                                vmem_limit_bytes=VMEM_LIMIT)


def _resident(shape):
    nd = len(shape)
    return pl.BlockSpec(shape, lambda *_: (0,) * nd, pipeline_mode=pl.Buffered(1))


def _mem_kv_kernel(mem_ref, gsrc_ref, wk_ref, wv_ref, gk_ref, k_ref, v_ref):
    m = (_rms(mem_ref[...]) * gsrc_ref[...]).astype(BF16)
    kz = _dot(m, wk_ref[...])
    gk = gk_ref[...]
    for h in range(MEM_HEADS):
        sl = slice(h * MEM_HEAD_DIM, (h + 1) * MEM_HEAD_DIM)
        k_ref[:, sl] = _rms(kz[:, sl]) * gk
    v_ref[...] = _dot(m, wv_ref[...])


def _mem_kv(mem, g_src, w_mk, w_mv, g_mk):
    n = mem.shape[0]
    out = jax.ShapeDtypeStruct((n, MEM_WIDTH), F32)
    return pl.pallas_call(
        _mem_kv_kernel, out_shape=(out, out), name="mem_kv",
        compiler_params=pltpu.CompilerParams(vmem_limit_bytes=VMEM_LIMIT),
    )(mem, g_src, w_mk, w_mv, g_mk)


def _in_proj_kernel(x_ref, gmix_ref, w_ref, gsgu_ref, gqk_ref, seg_ref, segt_ref, cos_ref, sin_ref,
                    u_ref, va_ref, q_ref, k_ref, v_ref):
    xn = (_rms(x_ref[...]) * gmix_ref[...]).astype(BF16)
    u_ref[...] = _gelu(_dot(xn, w_ref[:, 0:A_WIDTH])).astype(BF16)
    va_ref[...] = _rms(_gelu(_dot(xn, w_ref[:, A_WIDTH:Q_OFF]))) * gsgu_ref[...]
    v_ref[...] = _dot(xn, w_ref[:, V_OFF:IN_WIDTH])

    qk = _dot(xn, w_ref[:, Q_OFF:V_OFF])
    ssq = _dot((qk * qk).astype(BF16), seg_ref[...])
    inv = lax.rsqrt(ssq * (1.0 / B_HEAD_DIM) + EPS)
    inv_hi = inv.astype(BF16)
    inv_lo = (inv - inv_hi.astype(F32)).astype(BF16)
    inv_b = _dot(inv_hi, segt_ref[...]) + _dot(inv_lo, segt_ref[...])
    qkn = (qk * inv_b) * gqk_ref[...]

    cos = cos_ref[...]
    sin = sin_ref[...]
    lane = lax.broadcasted_iota(I32, cos.shape, 1)
    first_half = (lane % B_HEAD_DIM) < (B_HEAD_DIM // 2)
    for g in range(QK_WIDTH // LANES):
        xg = qkn[:, g * LANES:(g + 1) * LANES]
        rot = jnp.where(first_half, pltpu.roll(xg, LANES - B_HEAD_DIM // 2, 1),
                        pltpu.roll(xg, B_HEAD_DIM // 2, 1))
        r = xg * cos + rot * sin
        if g < B_WIDTH // LANES:
            q_ref[:, g * LANES:(g + 1) * LANES] = (r * ATTN_SCALE).astype(BF16)
        else:
            k_ref[...] = r


def _in_proj(x, g_mix, w_in, g_sgu, g_qk, seg, segt, cos_tab, sin_tab, tile, name):
    t = x.shape[0]
    n = t // tile
    tab_tiles = cos_tab.shape[0] // tile
    row = lambda w: pl.BlockSpec((tile, w), lambda i: (i, 0))
    tab = pl.BlockSpec((tile, LANES), lambda i: (i % tab_tiles, 0))
    return pl.pallas_call(
        _in_proj_kernel, grid=(n,), name=name,
        in_specs=[row(D_MODEL), _resident(g_mix.shape), _resident(w_in.shape), _resident(g_sgu.shape),
                  _resident(g_qk.shape), _resident(seg.shape), _resident(segt.shape), tab, tab],
        out_specs=[row(A_WIDTH), row(A_WIDTH), row(B_WIDTH), row(KV_WIDTH), row(KV_WIDTH)],
        out_shape=(jax.ShapeDtypeStruct((t, A_WIDTH), BF16), jax.ShapeDtypeStruct((t, A_WIDTH), F32),
                   jax.ShapeDtypeStruct((t, B_WIDTH), BF16), jax.ShapeDtypeStruct((t, KV_WIDTH), F32),
                   jax.ShapeDtypeStruct((t, KV_WIDTH), F32)),
        compiler_params=_params(),
    )(x, g_mix, w_in, g_sgu, g_qk, seg, segt, cos_tab, sin_tab)


def _split_heads(ref_dst_lo, ref_dst_hi, rows, x, swap):
    lane = lax.broadcasted_iota(I32, x.shape, 1)
    low = lane < B_HEAD_DIM
    zero = jnp.zeros_like(x)
    ref_dst_lo[0, rows, :] = jnp.where(low, x, zero).astype(BF16)
    ref_dst_hi[0, rows, :] = jnp.where(low, zero, swap).astype(BF16)
    ref_dst_lo[1, rows, :] = jnp.where(low, swap, zero).astype(BF16)
    ref_dst_hi[1, rows, :] = jnp.where(low, zero, x).astype(BF16)


def _sgu_chunk(u, va, ws_ref, bias):
    lane = lax.broadcasted_iota(I32, (MLP_CHUNK, LANES), 1)
    low = lane < A_HEAD_DIM
    outs = []
    for p in range(A_WIDTH // LANES):
        sl = slice(p * LANES, (p + 1) * LANES)
        v2 = va[:, sl]
        zero = jnp.zeros_like(v2)
        mixed = (_dot(ws_ref[2 * p], jnp.where(low, v2, zero).astype(BF16))
                 + _dot(ws_ref[2 * p + 1], jnp.where(low, zero, v2).astype(BF16)))
        outs.append(u[:, sl].astype(F32) * (mixed + bias[:, sl]))
    return jnp.concatenate(outs, axis=1)


def _attend(q2, kl, kh, vl, vh, mask, sink_even, sink_odd):
    outs = []
    for kk, vv, sink in ((kl, vl, sink_even), (kh, vh, sink_odd)):
        s = _dot_nt(q2, kk)
        if mask is not None:
            s = jnp.where(mask, s, NEG_BIG)
        m = jnp.maximum(jnp.max(s, axis=-1, keepdims=True), sink)
        p = jnp.exp(s - m)
        den = jnp.sum(p, axis=-1, keepdims=True) + jnp.exp(sink - m)
        outs.append(_dot(p.astype(BF16), vv) * (1.0 / den))
    return outs[0] + outs[1]


def _out_proj(oa, ob, ga_ref, gb_ref, wout_ref, x):
    a = (_rms(oa) * ga_ref[...]).astype(BF16)
    b = (_rms(ob) * gb_ref[...]).astype(BF16)
    return x + _dot(a, wout_ref[0:A_WIDTH, :]) + _dot(b, wout_ref[A_WIDTH:, :])


def _mixer_prompt_kernel(tiles_per_seq, sinks_ref, x_ref, u_ref, va_ref, q_ref, kc_ref, vc_ref, kp_ref, vp_ref,
                         ws_ref, bias_ref, ga_ref, gb_ref, wout_ref, o_ref,
                         kl_s, kh_s, vl_s, vh_s, oa_s, ob_s):
    tile = x_ref.shape[0]
    n_sub = tile // MLP_CHUNK
    seq_start = (pl.program_id(0) % tiles_per_seq) == 0

    for src_p, src_c, dl, dh in ((kp_ref, kc_ref, kl_s, kh_s), (vp_ref, vc_ref, vl_s, vh_s)):
        prev = src_p[...]
        cur = src_c[...]
        _split_heads(dl, dh, slice(0, WINDOW), prev, pltpu.roll(prev, B_HEAD_DIM, 1))
        _split_heads(dl, dh, slice(WINDOW, WINDOW + tile), cur, pltpu.roll(cur, B_HEAD_DIM, 1))

    qc = lax.broadcasted_iota(I32, (MLP_CHUNK, 2 * MLP_CHUNK), 0) // CHUNK
    kc = lax.broadcasted_iota(I32, (MLP_CHUNK, 2 * MLP_CHUNK), 1) // CHUNK
    window_mask = (kc >= qc) & (kc <= qc + 2)
    bias = bias_ref[...]

    def sub(j, carry):
        r0 = pl.multiple_of(j * MLP_CHUNK, MLP_CHUNK)
        rows = pl.ds(r0, MLP_CHUNK)
        keys = pl.ds(r0, 2 * MLP_CHUNK)
        first_key_chunk = jnp.where(jnp.logical_and(seq_start, j == 0), 2, 0)
        mask = window_mask & (kc >= first_key_chunk)
        oa_s[rows, :] = _sgu_chunk(u_ref[rows, :], va_ref[rows, :], ws_ref, bias)
        for p in range(B_WIDTH // LANES):
            h = p // (B_WIDTH // LANES // B_KV_HEADS)
            sl = slice(p * LANES, (p + 1) * LANES)
            ob_s[rows, sl] = _attend(q_ref[rows, sl], kl_s[h, keys, :], kh_s[h, keys, :],
                                     vl_s[h, keys, :], vh_s[h, keys, :], mask,
                                     sinks_ref[2 * p], sinks_ref[2 * p + 1])
        return carry

    lax.fori_loop(0, n_sub, sub, 0)
    o_ref[...] = _out_proj(oa_s[...], ob_s[...], ga_ref, gb_ref, wout_ref, x_ref[...])


def _mixer_prompt(x, u, va, q, k, v, sinks, ws, bias, g_a, g_b, w_out, seq, tile):
    t = x.shape[0]
    n = t // tile
    tiles_per_seq = seq // tile
    per = tile // WINDOW
    row = lambda w: pl.BlockSpec((tile, w), lambda i, s: (i, 0))
    prev = pl.BlockSpec((WINDOW, KV_WIDTH), lambda i, s: (jnp.maximum(i * per - 1, 0), 0))
    res = lambda a: pl.BlockSpec(a.shape, lambda i, s: (0,) * a.ndim, pipeline_mode=pl.Buffered(1))
    grid_spec = pltpu.PrefetchScalarGridSpec(
        num_scalar_prefetch=1, grid=(n,),
        in_specs=[row(D_MODEL), row(A_WIDTH), row(A_WIDTH), row(B_WIDTH), row(KV_WIDTH), row(KV_WIDTH), prev, prev,
                  res(ws), res(bias), res(g_a), res(g_b), res(w_out)],
        out_specs=row(D_MODEL),
        scratch_shapes=[pltpu.VMEM((B_KV_HEADS, WINDOW + tile, LANES), BF16) for _ in range(4)]
        + [pltpu.VMEM((tile, A_WIDTH), F32), pltpu.VMEM((tile, B_WIDTH), F32)])
    return pl.pallas_call(
        functools.partial(_mixer_prompt_kernel, tiles_per_seq), grid_spec=grid_spec, name="mixer_prompt",
        out_shape=jax.ShapeDtypeStruct((t, D_MODEL), F32), compiler_params=_params(),
    )(sinks, x, u, va, q, k, v, k, v, ws, bias, g_a, g_b, w_out)


def _mixer_sample_kernel(dec_seq, sinks_ref, x_ref, u_ref, va_ref, q_ref, kc_ref, vc_ref, ck_ref, cv_ref,
                         ws_ref, bias_ref, ga_ref, gb_ref, wout_ref, o_ref,
                         kl_s, kh_s, vl_s, vh_s, ob_s):
    tile = x_ref.shape[0]
    n_seq = tile // dec_seq
    n_cache = ck_ref.shape[1]
    n_keys = n_cache + dec_seq
    oa = _sgu_chunk(u_ref[...], va_ref[...], ws_ref, bias_ref[...])
    for b in range(n_seq):
        rows = slice(b * dec_seq, (b + 1) * dec_seq)
        for src_c, src_n, dl, dh in ((ck_ref, kc_ref, kl_s, kh_s), (cv_ref, vc_ref, vl_s, vh_s)):
            old = src_c[b]
            new = src_n[rows, :]
            _split_heads(dl, dh, slice(0, n_cache), old, pltpu.roll(old, B_HEAD_DIM, 1))
            _split_heads(dl, dh, slice(n_cache, n_keys), new, pltpu.roll(new, B_HEAD_DIM, 1))
        for p in range(B_WIDTH // LANES):
            h = p // (B_WIDTH // LANES // B_KV_HEADS)
            sl = slice(p * LANES, (p + 1) * LANES)
            ob_s[rows, sl] = _attend(q_ref[rows, sl], kl_s[h], kh_s[h], vl_s[h], vh_s[h], None,
                                     sinks_ref[2 * p], sinks_ref[2 * p + 1])
    o_ref[...] = _out_proj(oa, ob_s[...], ga_ref, gb_ref, wout_ref, x_ref[...])


def _mixer_sample(x, u, va, q, k, v, cache_k, cache_v, sinks, ws, bias, g_a, g_b, w_out, dec_seq, tile):
    t = x.shape[0]
    n = t // tile
    n_seq = tile // dec_seq
    n_cache = cache_k.shape[1]
    row = lambda w: pl.BlockSpec((tile, w), lambda i, s: (i, 0))
    cache = pl.BlockSpec((n_seq, n_cache, KV_WIDTH), lambda i, s: (i, 0, 0))
    res = lambda a: pl.BlockSpec(a.shape, lambda i, s: (0,) * a.ndim, pipeline_mode=pl.Buffered(1))
    grid_spec = pltpu.PrefetchScalarGridSpec(
        num_scalar_prefetch=1, grid=(n,),
        in_specs=[row(D_MODEL), row(A_WIDTH), row(A_WIDTH), row(B_WIDTH), row(KV_WIDTH), row(KV_WIDTH), cache, cache,
                  res(ws), res(bias), res(g_a), res(g_b), res(w_out)],
        out_specs=row(D_MODEL),
        scratch_shapes=[pltpu.VMEM((B_KV_HEADS, n_cache + dec_seq, LANES), BF16) for _ in range(4)]
        + [pltpu.VMEM((tile, B_WIDTH), F32)])
    return pl.pallas_call(
        functools.partial(_mixer_sample_kernel, dec_seq), grid_spec=grid_spec, name="mixer_sample",
        out_shape=jax.ShapeDtypeStruct((t, D_MODEL), F32), compiler_params=_params(),
    )(sinks, x, u, va, q, k, v, cache_k, cache_v, ws, bias, g_a, g_b, w_out)


def _memory_router_kernel(rows_per_mem, x_ref, mk_ref, mv_ref, gin_ref, wq_ref, gq_ref, wo_ref,
                          gmoe_ref, wrh_ref, wrl_ref, br_ref,
                          x2_ref, hn_ref, idx_ref, gate_ref, rank_ref, cnt_ref, o_s, base_s):
    tile = x_ref.shape[0]
    x = x_ref[...]
    qz = _dot((_rms(x) * gin_ref[...]).astype(BF16), wq_ref[...])
    gq = gq_ref[...]
    for h in range(MEM_HEADS):
        sl = slice(h * MEM_HEAD_DIM, (h + 1) * MEM_HEAD_DIM)
        qh = (_rms(qz[:, sl]) * gq).astype(BF16)
        for r in range(tile // rows_per_mem):
            rows = slice(r * rows_per_mem, (r + 1) * rows_per_mem)
            s = _dot_nt(qh[rows], mk_ref[r, :, sl].astype(BF16)) * MEM_SCALE
            p = jnp.exp(s - jnp.max(s, axis=-1, keepdims=True))
            den = jnp.sum(p, axis=-1, keepdims=True)
            o_s[rows, sl] = _dot(p.astype(BF16), mv_ref[r, :, sl].astype(BF16)) * (1.0 / den)
    x2 = x + _dot(o_s[...].astype(BF16), wo_ref[...])
    x2_ref[...] = x2
    hn = _rms(x2) * gmoe_ref[...]
    hn_ref[...] = _pack_rows(hn)

    hn_hi = hn.astype(BF16)
    hn_lo = (hn - hn_hi.astype(F32)).astype(BF16)
    logits = ((_dot(hn_lo, wrh_ref[...]) + _dot(hn_hi, wrl_ref[...])) + _dot(hn_hi, wrh_ref[...])) + br_ref[...]
    lane = lax.broadcasted_iota(I32, (tile, LANES), 1)
    lane_f = lane.astype(F32)
    work = jnp.where(lane < N_EXPERTS, logits, -jnp.inf)
    idx_out = jnp.zeros((tile, LANES), F32)
    val_out = jnp.zeros((tile, LANES), F32)
    hot = jnp.zeros((tile, LANES), F32)
    top = None
    picks = []
    for k in range(TOP_K):
        m = jnp.max(work, axis=-1, keepdims=True)
        pick = jnp.min(jnp.where(work == m, lane_f, float(LANES)), axis=-1, keepdims=True)
        chosen = lane_f == pick
        if top is None:
            top = m
        idx_out = jnp.where(lane == k, pick, idx_out)
        val_out = jnp.where(lane == k, jnp.exp(m - top), val_out)
        hot = jnp.where(chosen, 1.0, hot)
        work = jnp.where(chosen, -jnp.inf, work)
        picks.append(chosen)
    idx_ref[...] = idx_out.astype(I32)
    gate_ref[...] = val_out * (1.0 / jnp.sum(val_out, axis=-1, keepdims=True))

    @pl.when(pl.program_id(0) == 0)
    def _():
        base_s[...] = jnp.zeros_like(base_s)

    earlier = (lax.broadcasted_iota(I32, (tile, tile), 1) < lax.broadcasted_iota(I32, (tile, tile), 0))
    pos = _dot(jnp.where(earlier, 1.0, 0.0).astype(BF16), hot.astype(BF16)) + base_s[0:1, :]
    rank_out = jnp.zeros((tile, LANES), F32)
    for k in range(TOP_K):
        rank_out = jnp.where(lane == k, jnp.sum(jnp.where(picks[k], pos, 0.0), axis=-1, keepdims=True), rank_out)
    rank_ref[...] = rank_out.astype(I32)
    total = base_s[0:1, :] + jnp.sum(hot, axis=0, keepdims=True)
    base_s[...] = jnp.broadcast_to(total, base_s.shape)
    cnt_ref[...] = jnp.broadcast_to(total, cnt_ref.shape).astype(I32)


def _memory_router(x, mk, mv, rows_per_mem, g_in, w_mq, g_mq, w_mo, g_moe, w_r_hi, w_r_lo, b_r, tile, name):
    t = x.shape[0]
    n = t // tile
    mems = tile // rows_per_mem if rows_per_mem <= tile else 1
    per_mem_tiles = max(rows_per_mem // tile, 1)
    rpm = min(rows_per_mem, tile)
    row = lambda w: pl.BlockSpec((tile, w), lambda i: (i, 0))
    mem = pl.BlockSpec((mems, mk.shape[1], MEM_WIDTH), lambda i: (i // per_mem_tiles, 0, 0))
    res = lambda a: pl.BlockSpec(a.shape, lambda i: (0,) * a.ndim, pipeline_mode=pl.Buffered(1))
    small = jax.ShapeDtypeStruct((t, LANES), I32)
    return pl.pallas_call(
        functools.partial(_memory_router_kernel, rpm), grid=(n,), name=name,
        in_specs=[row(D_MODEL), mem, mem, res(g_in), res(w_mq), res(g_mq), res(w_mo), res(g_moe), res(w_r_hi), res(w_r_lo), res(b_r)],
        out_specs=[row(D_MODEL), row(D_MODEL // 2), row(LANES), row(LANES), row(LANES),
                   pl.BlockSpec((8, LANES), lambda i: (0, 0))],
        out_shape=(jax.ShapeDtypeStruct((t, D_MODEL), F32), jax.ShapeDtypeStruct((t, D_MODEL // 2), U32),
                   small, jax.ShapeDtypeStruct((t, LANES), F32), small, jax.ShapeDtypeStruct((8, LANES), I32)),
        scratch_shapes=[pltpu.VMEM((tile, MEM_WIDTH), F32), pltpu.VMEM((8, LANES), F32)],
        compiler_params=_params(),
    )(x, mk, mv, g_in, w_mq, g_mq, w_mo, g_moe, w_r_hi, w_r_lo, b_r)


def _dispatch_kernel(dest_ref, hn_ref, *rest):
    xb_ref, sem = rest[-2:]
    tile = hn_ref.shape[0]

    def copy(t, k):
        return pltpu.make_async_copy(hn_ref.at[pl.ds(t, 1), :],
                                     xb_ref.at[pl.ds(dest_ref[0, t * TOP_K + k], 1), :], sem)

    def start(t, carry):
        for k in range(TOP_K):
            copy(t, k).start()
        return carry

    def wait(t, carry):
        for k in range(TOP_K):
            copy(t, k).wait()
        return carry

    lax.fori_loop(0, tile, start, 0)
    lax.fori_loop(0, tile, wait, 0)


def _dispatch(hn, dest, xb, n_rows, tile):
    t = hn.shape[0]
    n = t // tile
    dest = dest.reshape(n, 1, tile * TOP_K)
    in_specs = [pl.BlockSpec((None, 1, tile * TOP_K), lambda i: (i, 0, 0), memory_space=pltpu.SMEM),
                pl.BlockSpec((tile, hn.shape[1]), lambda i: (i, 0))]
    args = [dest, hn]
    if xb is not None:
        in_specs.append(pl.BlockSpec(memory_space=pl.ANY))
        args.append(xb)
    return pl.pallas_call(
        _dispatch_kernel, grid=(n,), name="dispatch",
        in_specs=in_specs,
        out_specs=pl.BlockSpec(memory_space=pl.ANY),
        out_shape=jax.ShapeDtypeStruct((n_rows, hn.shape[1]), hn.dtype),
        scratch_shapes=[pltpu.SemaphoreType.DMA(())],
        input_output_aliases={} if xb is None else {2: 0},
        compiler_params=_params(),
    )(*args)


def _experts_kernel(be_ref, nv_ref, xi_ref, x_ref, *rest):
    del be_ref, xi_ref
    wg_refs, wu_refs = rest[0:W_SPLIT], rest[W_SPLIT:2 * W_SPLIT]
    bg_ref, bu_ref = rest[2 * W_SPLIT:2 * W_SPLIT + 2]
    wd_refs = rest[2 * W_SPLIT + 2:3 * W_SPLIT + 2]
    bd_ref, y_ref, xb_s, h_s = rest[3 * W_SPLIT + 2:]
    slab = lambda refs: jnp.concatenate([r[0].astype(BF16) for r in refs], axis=0)
    b = pl.program_id(0)
    s = pl.program_id(1)
    n_ff = D_FF // FF_TILE
    valid = nv_ref[b]
    n_sub = (valid + EXPERT_SUB - 1) // EXPERT_SUB

    @pl.when(s == 0)
    def _():
        def cast(i, carry):
            rows = pl.ds(pl.multiple_of(i * EXPERT_SUB, EXPERT_SUB), EXPERT_SUB)
            keep = (i * EXPERT_SUB + lax.broadcasted_iota(I32, (EXPERT_SUB, 1), 0)) < valid
            lo, hi = _unpack_rows(x_ref[rows, :])
            xb_s[rows, 0:D_MODEL // 2] = jnp.where(keep, lo, 0.0).astype(BF16)
            xb_s[rows, D_MODEL // 2:] = jnp.where(keep, hi, 0.0).astype(BF16)
            return carry
        lax.fori_loop(0, n_sub, cast, 0)

    @pl.when(s < n_ff)
    def _():
        wg = slab(wg_refs)
        wu = slab(wu_refs)
        bg = bg_ref[0]
        bu = bu_ref[0]

        def up(i, carry):
            rows = pl.ds(pl.multiple_of(i * EXPERT_SUB, EXPERT_SUB), EXPERT_SUB)
            xs = xb_s[rows, :]
            gate = jnp.minimum(_dot(xs, wg) + bg, SWIGLU_LIMIT)
            lin = jnp.clip(_dot(xs, wu) + bu, -SWIGLU_LIMIT, SWIGLU_LIMIT)
            act = (lin + 1.0) * (gate * (1.0 / (1.0 + jnp.exp(-SWIGLU_ALPHA * gate))))
            h_s[s, rows, :] = act.astype(BF16)
            return carry
        lax.fori_loop(0, n_sub, up, 0)

    @pl.when(s >= n_ff)
    def _():
        wd = slab(wd_refs)
        bd = bd_ref[0]

        def down(i, carry):
            rows = pl.ds(pl.multiple_of(i * EXPERT_SUB, EXPERT_SUB), EXPERT_SUB)
            h = jnp.concatenate([h_s[f, rows, :] for f in range(n_ff)], axis=1)
            y_ref[rows, :] = _dot(h, wd) + bd
            return carry
        lax.fori_loop(0, n_sub, down, 0)


def _experts(xb, block_expert, block_valid, block_index, w_gu, b_gu, w_d, b_d):
    n_blocks = xb.shape[0] // EXPERT_ROWS
    n_ff = D_FF // FF_TILE
    n_out = D_MODEL // OUT_TILE

    def ff(b, s, nv):
        return jnp.where(nv[b] > 0, jnp.minimum(s, n_ff - 1), n_ff - 1)

    def oc(b, s, nv):
        return jnp.where(nv[b] > 0, jnp.maximum(s - n_ff, 0), n_out - 1)

    b_gu3 = b_gu.reshape(N_EXPERTS, 1, 2 * D_FF)
    b_d3 = b_d.reshape(N_EXPERTS, 1, D_MODEL)
    def gate_slab(j):
        return pl.BlockSpec((1, D_MODEL // W_SPLIT, FF_TILE), lambda b, s, be, nv, xi: (be[b], j, ff(b, s, nv)))

    def up_slab(j):
        return pl.BlockSpec((1, D_MODEL // W_SPLIT, FF_TILE), lambda b, s, be, nv, xi: (be[b], j, ff(b, s, nv) + n_ff))

    def down_slab(j):
        return pl.BlockSpec((1, D_FF // W_SPLIT, OUT_TILE), lambda b, s, be, nv, xi: (be[b], j, oc(b, s, nv)))

    slabs = range(W_SPLIT)
    grid_spec = pltpu.PrefetchScalarGridSpec(
        num_scalar_prefetch=3, grid=(n_blocks, n_ff + n_out),
        in_specs=[pl.BlockSpec((EXPERT_ROWS, D_MODEL // 2), lambda b, s, be, nv, xi: (xi[b], 0))]
        + [gate_slab(j) for j in slabs] + [up_slab(j) for j in slabs]
        + [pl.BlockSpec((1, 1, FF_TILE), lambda b, s, be, nv, xi: (be[b], 0, ff(b, s, nv))),
           pl.BlockSpec((1, 1, FF_TILE), lambda b, s, be, nv, xi: (be[b], 0, ff(b, s, nv) + n_ff))]
        + [down_slab(j) for j in slabs]
        + [pl.BlockSpec((1, 1, OUT_TILE), lambda b, s, be, nv, xi: (be[b], 0, oc(b, s, nv)))],
        out_specs=pl.BlockSpec((EXPERT_ROWS, OUT_TILE), lambda b, s, be, nv, xi: (xi[b], oc(b, s, nv))),
        scratch_shapes=[pltpu.VMEM((EXPERT_ROWS, D_MODEL), BF16), pltpu.VMEM((n_ff, EXPERT_ROWS, FF_TILE), BF16)])
    return pl.pallas_call(
        _experts_kernel, grid_spec=grid_spec, name="experts",
        out_shape=jax.ShapeDtypeStruct((n_blocks * EXPERT_ROWS, D_MODEL), F32),
        compiler_params=_params(2),
    )(block_expert, block_valid, block_index, xb, *([w_gu] * (2 * W_SPLIT)), b_gu3, b_gu3, *([w_d] * W_SPLIT), b_d3)


def _combine_kernel(n_tiles, dest_ref, next_ref, x_ref, gate_ref, yb_ref, o_ref, buf, sem):
    tile = x_ref.shape[0]
    i = pl.program_id(0)
    slot = i % 2

    def copy(idx_ref, s, t, k):
        return pltpu.make_async_copy(yb_ref.at[pl.ds(idx_ref[0, t * TOP_K + k], 1), :],
                                     buf.at[s, k, pl.ds(t, 1), :], sem.at[s])

    def start_tile(idx_ref, s):
        def start(t, carry):
            for k in range(TOP_K):
                copy(idx_ref, s, t, k).start()
            return carry
        lax.fori_loop(0, tile, start, 0)

    pl.when(i == 0)(lambda: start_tile(dest_ref, 0))
    pl.when(i + 1 < n_tiles)(lambda: start_tile(next_ref, 1 - slot))

    def wait(t, carry):
        for k in range(TOP_K):
            copy(dest_ref, slot, t, k).wait()
        return carry
    lax.fori_loop(0, tile, wait, 0)

    gate = gate_ref[...]
    acc = x_ref[...]
    for k in range(TOP_K):
        acc = acc + gate[:, k:k + 1] * buf[slot, k]
    o_ref[...] = acc


def _combine(x, gate, dest, yb, tile):
    t = x.shape[0]
    n = t // tile
    dest = dest.reshape(n, 1, tile * TOP_K)
    return pl.pallas_call(
        functools.partial(_combine_kernel, n), grid=(n,), name="combine",
        in_specs=[pl.BlockSpec((None, 1, tile * TOP_K), lambda i: (i, 0, 0), memory_space=pltpu.SMEM),
                  pl.BlockSpec((None, 1, tile * TOP_K), lambda i: (jnp.minimum(i + 1, n - 1), 0, 0),
                               memory_space=pltpu.SMEM),
                  pl.BlockSpec((tile, D_MODEL), lambda i: (i, 0)),
                  pl.BlockSpec((tile, LANES), lambda i: (i, 0)),
                  pl.BlockSpec(memory_space=pl.ANY)],
        out_specs=pl.BlockSpec((tile, D_MODEL), lambda i: (i, 0)),
        out_shape=jax.ShapeDtypeStruct((t, D_MODEL), F32),
        scratch_shapes=[pltpu.VMEM((2, TOP_K, tile, D_MODEL), F32), pltpu.SemaphoreType.DMA((2,))],
        compiler_params=_params(),
    )(dest, dest, x, gate, yb)


def _rope_tables(pos):
    half = B_HEAD_DIM // 2
    inv_freq = ROPE_THETA ** (-jnp.arange(half, dtype=F32) / half)
    ang = pos.astype(F32)[:, None] * inv_freq[None, :]
    cos = jnp.cos(ang)
    sin = jnp.sin(ang)
    return jnp.tile(cos, (1, LANES // half)), jnp.tile(jnp.concatenate([-sin, sin], axis=1), (1, LANES // B_HEAD_DIM))


def _layer(l, xp, xs, cache_swa_k, cache_swa_v, cache_mem_k, cache_mem_v, mem_prompt,
           g_mix, w_in, g_sgu, w_s, b_s, g_q, g_k, sinks, g_out_a, g_out_b, w_out,
           g_mem_in, g_mem_src, w_mq, w_mk, w_mv, g_mq, g_mk, w_mo,
           g_moe, w_router, b_router, w_gate_up, b_gate_up, w_down, b_down):
    n_b, seq, _ = xp.shape
    n_db, dec_seq, _ = xs.shape
    tp = n_b * seq
    ts = n_db * dec_seq
    tile = min(TOKEN_TILE, seq)
    row = lambda a: a[l].reshape(1, -1)

    xp2 = xp.reshape(tp, D_MODEL)
    xs2 = xs.reshape(ts, D_MODEL)
    head = jnp.arange(QK_WIDTH, dtype=I32) // B_HEAD_DIM
    seg = (head[:, None] == jnp.arange(LANES, dtype=I32)[None, :]).astype(BF16)
    g_qk = jnp.concatenate([jnp.tile(g_q[l], B_HEADS), jnp.tile(g_k[l], B_KV_HEADS)]).reshape(1, -1)
    w_in_b = w_in[l].astype(BF16)
    cos_p, sin_p = _rope_tables(jnp.arange(seq, dtype=I32))
    cos_s, sin_s = _rope_tables(PAST_LEN + jnp.arange(SAMPLE_TILE, dtype=I32) % dec_seq)
    up, vap, qp, kp, vp = _in_proj(xp2, row(g_mix), w_in_b, row(g_sgu), g_qk, seg, seg.T, cos_p, sin_p,
                                   tile, "in_proj_prompt")
    us, vas, qs, ks, vs = _in_proj(xs2, row(g_mix), w_in_b, row(g_sgu), g_qk, seg, seg.T, cos_s, sin_s,
                                   SAMPLE_TILE, "in_proj_sample")

    w_out_b = w_out[l].astype(BF16)
    tri = jnp.tril(jnp.ones((MLP_CHUNK, MLP_CHUNK), bool))
    ws_p = jnp.where(tri[None], w_s[l], 0.0).astype(BF16)
    bias_p = jnp.repeat(b_s[l].T, A_HEAD_DIM, axis=1)
    x1p = _mixer_prompt(xp2, up, vap, qp, kp, vp, sinks[l], ws_p, bias_p, row(g_out_a), row(g_out_b), w_out_b, seq, tile)

    reps = SAMPLE_TILE // dec_seq
    tri_s = jnp.tril(jnp.ones((dec_seq, dec_seq), bool))
    ws_small = jnp.where(tri_s[None], w_s[l][:, :dec_seq, :dec_seq], 0.0)
    ws_s = jnp.einsum("ab,hts->hatbs", jnp.eye(reps, dtype=F32), ws_small).reshape(A_HEADS, SAMPLE_TILE, SAMPLE_TILE).astype(BF16)
    bias_s = jnp.tile(jnp.repeat(b_s[l][:, :dec_seq].T, A_HEAD_DIM, axis=1), (reps, 1))
    ck = cache_swa_k[l].reshape(n_db, -1, KV_WIDTH)
    cv = cache_swa_v[l].reshape(n_db, -1, KV_WIDTH)
    x1s = _mixer_sample(xs2, us, vas, qs, ks, vs, ck, cv, sinks[l], ws_s, bias_s,
                        row(g_out_a), row(g_out_b), w_out_b, dec_seq, SAMPLE_TILE)

    mk_p, mv_p = _mem_kv(mem_prompt.reshape(-1, D_MODEL), row(g_mem_src), w_mk[l].astype(BF16), w_mv[l].astype(BF16), row(g_mk))
    n_mem = mem_prompt.shape[1]
    w_r = jnp.pad(w_router[l], ((0, 0), (0, LANES - N_EXPERTS)))
    b_r = jnp.pad(b_router[l], (0, LANES - N_EXPERTS)).reshape(1, -1)
    w_r_hi = w_r.astype(BF16)
    w_r_lo = (w_r - w_r_hi.astype(F32)).astype(BF16)
    mem_args = (row(g_mem_in), w_mq[l].astype(BF16), row(g_mq), w_mo[l].astype(BF16), row(g_moe), w_r_hi, w_r_lo, b_r)
    x2p, hnp, idxp, gatep, rankp, cntp = _memory_router(
        x1p, mk_p.reshape(n_b, n_mem, MEM_WIDTH), mv_p.reshape(n_b, n_mem, MEM_WIDTH),
        seq, *mem_args, tile, "memory_router_prompt")
    x2s, hns, idxs, gates, ranks, cnts = _memory_router(
        x1s, cache_mem_k[l].reshape(n_db, -1, MEM_WIDTH), cache_mem_v[l].reshape(n_db, -1, MEM_WIDTH),
        dec_seq, *mem_args, SAMPLE_TILE, "memory_router_sample")

    cnt_p = cntp[0, :N_EXPERTS]
    cnt_s = cnts[0, :N_EXPERTS]
    total = cnt_p + cnt_s
    nblk = (total + EXPERT_ROWS - 1) // EXPERT_ROWS
    blk_end = jnp.cumsum(nblk)
    row_start = (blk_end - nblk) * EXPERT_ROWS
    n_blocks = (tp + ts) * TOP_K // EXPERT_ROWS + N_EXPERTS
    bidx = jnp.arange(n_blocks, dtype=I32)
    used = bidx < blk_end[-1]
    last = jnp.maximum(blk_end[-1] - 1, 0)
    bsafe = jnp.minimum(bidx, last)
    block_expert = jnp.minimum(jnp.searchsorted(blk_end, bsafe, side="right"), N_EXPERTS - 1).astype(I32)
    within = bsafe - (blk_end - nblk)[block_expert]
    block_valid = jnp.where(used, jnp.clip(total[block_expert] - within * EXPERT_ROWS, 0, EXPERT_ROWS), 0).astype(I32)
    dest_p = row_start[idxp[:, :TOP_K]] + rankp[:, :TOP_K]
    dest_s = (row_start + cnt_p)[idxs[:, :TOP_K]] + ranks[:, :TOP_K]

    xb = _dispatch(hnp, dest_p, None, n_blocks * EXPERT_ROWS, min(ROW_TILE, tp))
    xb = _dispatch(hns, dest_s, xb, n_blocks * EXPERT_ROWS, min(ROW_TILE, ts))
    yb = _experts(xb, block_expert, block_valid, bsafe.astype(I32), w_gate_up[l], b_gate_up[l], w_down[l], b_down[l])
    yp = _combine(x2p, gatep, dest_p, yb, min(ROW_TILE, tp))
    ys = _combine(x2s, gates, dest_s, yb, min(ROW_TILE, ts))

    new = dict(
        swa_k_p=kp.reshape(n_b, seq, B_KV_HEADS, B_HEAD_DIM)[:, seq - WINDOW:],
        swa_v_p=vp.reshape(n_b, seq, B_KV_HEADS, B_HEAD_DIM)[:, seq - WINDOW:],
        mem_k_p=mk_p.reshape(n_b, n_mem, MEM_HEADS, MEM_HEAD_DIM),
        mem_v_p=mv_p.reshape(n_b, n_mem, MEM_HEADS, MEM_HEAD_DIM),
        swa_k_s=ks.reshape(n_db, dec_seq, B_KV_HEADS, B_HEAD_DIM),
        swa_v_s=vs.reshape(n_db, dec_seq, B_KV_HEADS, B_HEAD_DIM),
        sgu_v_s=vas.reshape(n_db, dec_seq, A_HEADS, A_HEAD_DIM))
    return yp.reshape(n_b, seq, D_MODEL), ys.reshape(n_db, dec_seq, D_MODEL), new


def kernel(x_prompt, x_sample, cache_swa_k, cache_swa_v, cache_mem_k, cache_mem_v, mem_prompt, g_mix, w_in, g_sgu, w_s, b_s, g_q, g_k, sinks, g_out_a, g_out_b, w_out, g_mem_in, g_mem_src, w_mq, w_mk, w_mv, g_mq, g_mk, w_mo, g_moe, w_router, b_router, w_gate_up, b_gate_up, w_down, b_down):
    xp, xs = x_prompt, x_sample
    news = []
    for l in range(g_mix.shape[0]):
        xp, xs, new = _layer(l, xp, xs, cache_swa_k, cache_swa_v, cache_mem_k, cache_mem_v, mem_prompt,
                             g_mix, w_in, g_sgu, w_s, b_s, g_q, g_k, sinks, g_out_a, g_out_b, w_out,
                             g_mem_in, g_mem_src, w_mq, w_mk, w_mv, g_mq, g_mk, w_mo,
                             g_moe, w_router, b_router, w_gate_up, b_gate_up, w_down, b_down)
        news.append(new)
    stack = lambda name: jnp.stack([n[name] for n in news], 0)
    return (xp, xs, stack("swa_k_p"), stack("swa_v_p"), stack("mem_k_p"), stack("mem_v_p"),
            stack("swa_k_s"), stack("swa_v_s"), stack("sgu_v_s"))
```

```python
import functools

import jax
import jax.numpy as jnp
from jax import lax
from jax.experimental import pallas as pl
from jax.experimental.pallas import tpu as pltpu

F32 = jnp.float32
BF16 = jnp.bfloat16
I32 = jnp.int32
U32 = jnp.uint32

D_MODEL = 2048
PAST_LEN = 2048
CHUNK = 64
EPS = 1e-6
A_HEADS = 16
A_HEAD_DIM = 64
A_WIDTH = A_HEADS * A_HEAD_DIM
MLP_CHUNK = 128
B_HEADS = 16
B_KV_HEADS = 2
B_HEAD_DIM = 64
B_WIDTH = B_HEADS * B_HEAD_DIM
KV_WIDTH = B_KV_HEADS * B_HEAD_DIM
WINDOW = 128
ROPE_THETA = 10000.0
ATTN_SCALE = B_HEAD_DIM ** -0.5
Q_OFF = 2 * A_WIDTH
K_OFF = Q_OFF + B_WIDTH
V_OFF = K_OFF + KV_WIDTH
IN_WIDTH = V_OFF + KV_WIDTH
QK_WIDTH = B_WIDTH + KV_WIDTH
MEM_HEADS = 4
MEM_HEAD_DIM = 128
MEM_WIDTH = MEM_HEADS * MEM_HEAD_DIM
MEM_SCALE = MEM_HEAD_DIM ** -0.5
N_EXPERTS = 32
TOP_K = 4
D_FF = D_MODEL
SWIGLU_LIMIT = 7.0
SWIGLU_ALPHA = 1.702
NEG_BIG = -1e30

LANES = 128
VMEM_LIMIT = 56 * 1024 * 1024

TOKEN_TILE = 512
SAMPLE_TILE = 128
EXPERT_ROWS = 1024
EXPERT_SUB = 256
FF_TILE = 512
OUT_TILE = 512
W_SPLIT = 2
ROW_TILE = 128


def _dot(a, b):
    return jnp.dot(a, b, preferred_element_type=F32)


def _dot_nt(a, b):
    return lax.dot_general(a, b, (((1,), (1,)), ((), ())), preferred_element_type=F32)


def _rms(x):
    return x * lax.rsqrt(jnp.mean(x * x, axis=-1, keepdims=True) + EPS)


def _gelu(x):
    return 0.5 * x * (1.0 + lax.erf(x * 0.7071067811865476))


def _pack_rows(x):
    c = x.shape[1] // 2
    bits = pltpu.bitcast(x.astype(BF16).astype(F32), U32)
    return (bits[:, :c] >> 16) | (bits[:, c:] & jnp.uint32(0xFFFF0000))


def _unpack_rows(w):
    return pltpu.bitcast(w << 16, F32), pltpu.bitcast(w & jnp.uint32(0xFFFF0000), F32)


def _params(n_axes=1):
    return pltpu.CompilerParams(dimension_semantics=("arbitrary",) * n_axes,
                                vmem_limit_bytes=VMEM_LIMIT)


def _resident(shape):
    nd = len(shape)
    return pl.BlockSpec(shape, lambda *_: (0,) * nd, pipeline_mode=pl.Buffered(1))


def _mem_kv_kernel(mem_ref, gsrc_ref, wk_ref, wv_ref, gk_ref, k_ref, v_ref):
    m = (_rms(mem_ref[...]) * gsrc_ref[...]).astype(BF16)
    kz = _dot(m, wk_ref[...])
    gk = gk_ref[...]
    for h in range(MEM_HEADS):
        sl = slice(h * MEM_HEAD_DIM, (h + 1) * MEM_HEAD_DIM)
        k_ref[:, sl] = _rms(kz[:, sl]) * gk
    v_ref[...] = _dot(m, wv_ref[...])


def _mem_kv(mem, g_src, w_mk, w_mv, g_mk):
    n = mem.shape[0]
    out = jax.ShapeDtypeStruct((n, MEM_WIDTH), F32)
    return pl.pallas_call(
        _mem_kv_kernel, out_shape=(out, out), name="mem_kv",
        compiler_params=pltpu.CompilerParams(vmem_limit_bytes=VMEM_LIMIT),
    )(mem, g_src, w_mk, w_mv, g_mk)


def _in_proj_kernel(x_ref, gmix_ref, w_ref, gsgu_ref, gqk_ref, seg_ref, segt_ref, cos_ref, sin_ref,
                    u_ref, va_ref, q_ref, k_ref, v_ref):
    xn = (_rms(x_ref[...]) * gmix_ref[...]).astype(BF16)
    u_ref[...] = _gelu(_dot(xn, w_ref[:, 0:A_WIDTH])).astype(BF16)
    va_ref[...] = _rms(_gelu(_dot(xn, w_ref[:, A_WIDTH:Q_OFF]))) * gsgu_ref[...]
    v_ref[...] = _dot(xn, w_ref[:, V_OFF:IN_WIDTH])

    qk = _dot(xn, w_ref[:, Q_OFF:V_OFF])
    ssq = _dot((qk * qk).astype(BF16), seg_ref[...])
    inv = lax.rsqrt(ssq * (1.0 / B_HEAD_DIM) + EPS)
    inv_hi = inv.astype(BF16)
    inv_lo = (inv - inv_hi.astype(F32)).astype(BF16)
    inv_b = _dot(inv_hi, segt_ref[...]) + _dot(inv_lo, segt_ref[...])
    qkn = (qk * inv_b) * gqk_ref[...]

    cos = cos_ref[...]
    sin = sin_ref[...]
    lane = lax.broadcasted_iota(I32, cos.shape, 1)
    first_half = (lane % B_HEAD_DIM) < (B_HEAD_DIM // 2)
    for g in range(QK_WIDTH // LANES):
        xg = qkn[:, g * LANES:(g + 1) * LANES]
        rot = jnp.where(first_half, pltpu.roll(xg, LANES - B_HEAD_DIM // 2, 1),
                        pltpu.roll(xg, B_HEAD_DIM // 2, 1))
        r = xg * cos + rot * sin
        if g < B_WIDTH // LANES:
            q_ref[:, g * LANES:(g + 1) * LANES] = (r * ATTN_SCALE).astype(BF16)
        else:
            k_ref[...] = r


def _in_proj(x, g_mix, w_in, g_sgu, g_qk, seg, segt, cos_tab, sin_tab, tile, name):
    t = x.shape[0]
    n = t // tile
    tab_tiles = cos_tab.shape[0] // tile
    row = lambda w: pl.BlockSpec((tile, w), lambda i: (i, 0))
    tab = pl.BlockSpec((tile, LANES), lambda i: (i % tab_tiles, 0))
    return pl.pallas_call(
        _in_proj_kernel, grid=(n,), name=name,
        in_specs=[row(D_MODEL), _resident(g_mix.shape), _resident(w_in.shape), _resident(g_sgu.shape),
                  _resident(g_qk.shape), _resident(seg.shape), _resident(segt.shape), tab, tab],
        out_specs=[row(A_WIDTH), row(A_WIDTH), row(B_WIDTH), row(KV_WIDTH), row(KV_WIDTH)],
        out_shape=(jax.ShapeDtypeStruct((t, A_WIDTH), BF16), jax.ShapeDtypeStruct((t, A_WIDTH), F32),
                   jax.ShapeDtypeStruct((t, B_WIDTH), BF16), jax.ShapeDtypeStruct((t, KV_WIDTH), F32),
                   jax.ShapeDtypeStruct((t, KV_WIDTH), F32)),
        compiler_params=_params(),
    )(x, g_mix, w_in, g_sgu, g_qk, seg, segt, cos_tab, sin_tab)


def _split_heads(ref_dst_lo, ref_dst_hi, rows, x, swap):
    lane = lax.broadcasted_iota(I32, x.shape, 1)
    low = lane < B_HEAD_DIM
    zero = jnp.zeros_like(x)
    ref_dst_lo[0, rows, :] = jnp.where(low, x, zero).astype(BF16)
    ref_dst_hi[0, rows, :] = jnp.where(low, zero, swap).astype(BF16)
    ref_dst_lo[1, rows, :] = jnp.where(low, swap, zero).astype(BF16)
    ref_dst_hi[1, rows, :] = jnp.where(low, zero, x).astype(BF16)


def _sgu_chunk(u, va, ws_ref, bias):
    lane = lax.broadcasted_iota(I32, (MLP_CHUNK, LANES), 1)
    low = lane < A_HEAD_DIM
    outs = []
    for p in range(A_WIDTH // LANES):
        sl = slice(p * LANES, (p + 1) * LANES)
        v2 = va[:, sl]
        zero = jnp.zeros_like(v2)
        mixed = (_dot(ws_ref[2 * p], jnp.where(low, v2, zero).astype(BF16))
                 + _dot(ws_ref[2 * p + 1], jnp.where(low, zero, v2).astype(BF16)))
        outs.append(u[:, sl].astype(F32) * (mixed + bias[:, sl]))
    return jnp.concatenate(outs, axis=1)


def _attend(q2, kl, kh, vl, vh, mask, sink_even, sink_odd):
    outs = []
    for kk, vv, sink in ((kl, vl, sink_even), (kh, vh, sink_odd)):
        s = _dot_nt(q2, kk)
        if mask is not None:
            s = jnp.where(mask, s, NEG_BIG)
        m = jnp.maximum(jnp.max(s, axis=-1, keepdims=True), sink)
        p = jnp.exp(s - m)
        den = jnp.sum(p, axis=-1, keepdims=True) + jnp.exp(sink - m)
        outs.append(_dot(p.astype(BF16), vv) * (1.0 / den))
    return outs[0] + outs[1]


def _out_proj(oa, ob, ga_ref, gb_ref, wout_ref, x):
    a = (_rms(oa) * ga_ref[...]).astype(BF16)
    b = (_rms(ob) * gb_ref[...]).astype(BF16)
    return x + _dot(a, wout_ref[0:A_WIDTH, :]) + _dot(b, wout_ref[A_WIDTH:, :])


def _mixer_prompt_kernel(tiles_per_seq, sinks_ref, x_ref, u_ref, va_ref, q_ref, kc_ref, vc_ref, kp_ref, vp_ref,
                         ws_ref, bias_ref, ga_ref, gb_ref, wout_ref, o_ref,
                         kl_s, kh_s, vl_s, vh_s, oa_s, ob_s):
    tile = x_ref.shape[0]
    n_sub = tile // MLP_CHUNK
    seq_start = (pl.program_id(0) % tiles_per_seq) == 0

    for src_p, src_c, dl, dh in ((kp_ref, kc_ref, kl_s, kh_s), (vp_ref, vc_ref, vl_s, vh_s)):
        prev = src_p[...]
        cur = src_c[...]
        _split_heads(dl, dh, slice(0, WINDOW), prev, pltpu.roll(prev, B_HEAD_DIM, 1))
        _split_heads(dl, dh, slice(WINDOW, WINDOW + tile), cur, pltpu.roll(cur, B_HEAD_DIM, 1))

    qc = lax.broadcasted_iota(I32, (MLP_CHUNK, 2 * MLP_CHUNK), 0) // CHUNK
    kc = lax.broadcasted_iota(I32, (MLP_CHUNK, 2 * MLP_CHUNK), 1) // CHUNK
    window_mask = (kc >= qc) & (kc <= qc + 2)
    bias = bias_ref[...]

    def sub(j, carry):
        r0 = pl.multiple_of(j * MLP_CHUNK, MLP_CHUNK)
        rows = pl.ds(r0, MLP_CHUNK)
        keys = pl.ds(r0, 2 * MLP_CHUNK)
        first_key_chunk = jnp.where(jnp.logical_and(seq_start, j == 0), 2, 0)
        mask = window_mask & (kc >= first_key_chunk)
        oa_s[rows, :] = _sgu_chunk(u_ref[rows, :], va_ref[rows, :], ws_ref, bias)
        for p in range(B_WIDTH // LANES):
            h = p // (B_WIDTH // LANES // B_KV_HEADS)
            sl = slice(p * LANES, (p + 1) * LANES)
            ob_s[rows, sl] = _attend(q_ref[rows, sl], kl_s[h, keys, :], kh_s[h, keys, :],
                                     vl_s[h, keys, :], vh_s[h, keys, :], mask,
                                     sinks_ref[2 * p], sinks_ref[2 * p + 1])
        return carry

    lax.fori_loop(0, n_sub, sub, 0)
    o_ref[...] = _out_proj(oa_s[...], ob_s[...], ga_ref, gb_ref, wout_ref, x_ref[...])


def _mixer_prompt(x, u, va, q, k, v, sinks, ws, bias, g_a, g_b, w_out, seq, tile):
    t = x.shape[0]
    n = t // tile
    tiles_per_seq = seq // tile
    per = tile // WINDOW
    row = lambda w: pl.BlockSpec((tile, w), lambda i, s: (i, 0))
    prev = pl.BlockSpec((WINDOW, KV_WIDTH), lambda i, s: (jnp.maximum(i * per - 1, 0), 0))
    res = lambda a: pl.BlockSpec(a.shape, lambda i, s: (0,) * a.ndim, pipeline_mode=pl.Buffered(1))
    grid_spec = pltpu.PrefetchScalarGridSpec(
        num_scalar_prefetch=1, grid=(n,),
        in_specs=[row(D_MODEL), row(A_WIDTH), row(A_WIDTH), row(B_WIDTH), row(KV_WIDTH), row(KV_WIDTH), prev, prev,
                  res(ws), res(bias), res(g_a), res(g_b), res(w_out)],
        out_specs=row(D_MODEL),
        scratch_shapes=[pltpu.VMEM((B_KV_HEADS, WINDOW + tile, LANES), BF16) for _ in range(4)]
        + [pltpu.VMEM((tile, A_WIDTH), F32), pltpu.VMEM((tile, B_WIDTH), F32)])
    return pl.pallas_call(
        functools.partial(_mixer_prompt_kernel, tiles_per_seq), grid_spec=grid_spec, name="mixer_prompt",
        out_shape=jax.ShapeDtypeStruct((t, D_MODEL), F32), compiler_params=_params(),
    )(sinks, x, u, va, q, k, v, k, v, ws, bias, g_a, g_b, w_out)


def _mixer_sample_kernel(dec_seq, sinks_ref, x_ref, u_ref, va_ref, q_ref, kc_ref, vc_ref, ck_ref, cv_ref,
                         ws_ref, bias_ref, ga_ref, gb_ref, wout_ref, o_ref,
                         kl_s, kh_s, vl_s, vh_s, ob_s):
    tile = x_ref.shape[0]
    n_seq = tile // dec_seq
    n_cache = ck_ref.shape[1]
    n_keys = n_cache + dec_seq
    oa = _sgu_chunk(u_ref[...], va_ref[...], ws_ref, bias_ref[...])
    for b in range(n_seq):
        rows = slice(b * dec_seq, (b + 1) * dec_seq)
        for src_c, src_n, dl, dh in ((ck_ref, kc_ref, kl_s, kh_s), (cv_ref, vc_ref, vl_s, vh_s)):
            old = src_c[b]
            new = src_n[rows, :]
            _split_heads(dl, dh, slice(0, n_cache), old, pltpu.roll(old, B_HEAD_DIM, 1))
            _split_heads(dl, dh, slice(n_cache, n_keys), new, pltpu.roll(new, B_HEAD_DIM, 1))
        for p in range(B_WIDTH // LANES):
            h = p // (B_WIDTH // LANES // B_KV_HEADS)
            sl = slice(p * LANES, (p + 1) * LANES)
            ob_s[rows, sl] = _attend(q_ref[rows, sl], kl_s[h], kh_s[h], vl_s[h], vh_s[h], None,
                                     sinks_ref[2 * p], sinks_ref[2 * p + 1])
    o_ref[...] = _out_proj(oa, ob_s[...], ga_ref, gb_ref, wout_ref, x_ref[...])


def _mixer_sample(x, u, va, q, k, v, cache_k, cache_v, sinks, ws, bias, g_a, g_b, w_out, dec_seq, tile):
    t = x.shape[0]
    n = t // tile
    n_seq = tile // dec_seq
    n_cache = cache_k.shape[1]
    row = lambda w: pl.BlockSpec((tile, w), lambda i, s: (i, 0))
    cache = pl.BlockSpec((n_seq, n_cache, KV_WIDTH), lambda i, s: (i, 0, 0))
    res = lambda a: pl.BlockSpec(a.shape, lambda i, s: (0,) * a.ndim, pipeline_mode=pl.Buffered(1))
    grid_spec = pltpu.PrefetchScalarGridSpec(
        num_scalar_prefetch=1, grid=(n,),
        in_specs=[row(D_MODEL), row(A_WIDTH), row(A_WIDTH), row(B_WIDTH), row(KV_WIDTH), row(KV_WIDTH), cache, cache,
                  res(ws), res(bias), res(g_a), res(g_b), res(w_out)],
        out_specs=row(D_MODEL),
        scratch_shapes=[pltpu.VMEM((B_KV_HEADS, n_cache + dec_seq, LANES), BF16) for _ in range(4)]
        + [pltpu.VMEM((tile, B_WIDTH), F32)])
    return pl.pallas_call(
        functools.partial(_mixer_sample_kernel, dec_seq), grid_spec=grid_spec, name="mixer_sample",
        out_shape=jax.ShapeDtypeStruct((t, D_MODEL), F32), compiler_params=_params(),
    )(sinks, x, u, va, q, k, v, cache_k, cache_v, ws, bias, g_a, g_b, w_out)


def _memory_router_kernel(rows_per_mem, x_ref, mk_ref, mv_ref, gin_ref, wq_ref, gq_ref, wo_ref,
                          gmoe_ref, wrh_ref, wrl_ref, br_ref,
                          x2_ref, hn_ref, idx_ref, gate_ref, rank_ref, cnt_ref, o_s, base_s):
    tile = x_ref.shape[0]
    x = x_ref[...]
    qz = _dot((_rms(x) * gin_ref[...]).astype(BF16), wq_ref[...])
    gq = gq_ref[...]
    for h in range(MEM_HEADS):
        sl = slice(h * MEM_HEAD_DIM, (h + 1) * MEM_HEAD_DIM)
        qh = (_rms(qz[:, sl]) * gq).astype(BF16)
        for r in range(tile // rows_per_mem):
            rows = slice(r * rows_per_mem, (r + 1) * rows_per_mem)
            s = _dot_nt(qh[rows], mk_ref[r, :, sl].astype(BF16)) * MEM_SCALE
            p = jnp.exp(s - jnp.max(s, axis=-1, keepdims=True))
            den = jnp.sum(p, axis=-1, keepdims=True)
            o_s[rows, sl] = _dot(p.astype(BF16), mv_ref[r, :, sl].astype(BF16)) * (1.0 / den)
    x2 = x + _dot(o_s[...].astype(BF16), wo_ref[...])
    x2_ref[...] = x2
    hn = _rms(x2) * gmoe_ref[...]
    hn_ref[...] = _pack_rows(hn)

    hn_hi = hn.astype(BF16)
    hn_lo = (hn - hn_hi.astype(F32)).astype(BF16)
    logits = ((_dot(hn_lo, wrh_ref[...]) + _dot(hn_hi, wrl_ref[...])) + _dot(hn_hi, wrh_ref[...])) + br_ref[...]
    lane = lax.broadcasted_iota(I32, (tile, LANES), 1)
    lane_f = lane.astype(F32)
    work = jnp.where(lane < N_EXPERTS, logits, -jnp.inf)
    idx_out = jnp.zeros((tile, LANES), F32)
    val_out = jnp.zeros((tile, LANES), F32)
    hot = jnp.zeros((tile, LANES), F32)
    top = None
    picks = []
    for k in range(TOP_K):
        m = jnp.max(work, axis=-1, keepdims=True)
        pick = jnp.min(jnp.where(work == m, lane_f, float(LANES)), axis=-1, keepdims=True)
        chosen = lane_f == pick
        if top is None:
            top = m
        idx_out = jnp.where(lane == k, pick, idx_out)
        val_out = jnp.where(lane == k, jnp.exp(m - top), val_out)
        hot = jnp.where(chosen, 1.0, hot)
        work = jnp.where(chosen, -jnp.inf, work)
        picks.append(chosen)
    idx_ref[...] = idx_out.astype(I32)
    gate_ref[...] = val_out * (1.0 / jnp.sum(val_out, axis=-1, keepdims=True))

    @pl.when(pl.program_id(0) == 0)
    def _():
        base_s[...] = jnp.zeros_like(base_s)

    earlier = (lax.broadcasted_iota(I32, (tile, tile), 1) < lax.broadcasted_iota(I32, (tile, tile), 0))
    pos = _dot(jnp.where(earlier, 1.0, 0.0).astype(BF16), hot.astype(BF16)) + base_s[0:1, :]
    rank_out = jnp.zeros((tile, LANES), F32)
    for k in range(TOP_K):
        rank_out = jnp.where(lane == k, jnp.sum(jnp.where(picks[k], pos, 0.0), axis=-1, keepdims=True), rank_out)
    rank_ref[...] = rank_out.astype(I32)
    total = base_s[0:1, :] + jnp.sum(hot, axis=0, keepdims=True)
    base_s[...] = jnp.broadcast_to(total, base_s.shape)
    cnt_ref[...] = jnp.broadcast_to(total, cnt_ref.shape).astype(I32)


def _memory_router(x, mk, mv, rows_per_mem, g_in, w_mq, g_mq, w_mo, g_moe, w_r_hi, w_r_lo, b_r, tile, name):
    t = x.shape[0]
    n = t // tile
    mems = tile // rows_per_mem if rows_per_mem <= tile else 1
    per_mem_tiles = max(rows_per_mem // tile, 1)
    rpm = min(rows_per_mem, tile)
    row = lambda w: pl.BlockSpec((tile, w), lambda i: (i, 0))
    mem = pl.BlockSpec((mems, mk.shape[1], MEM_WIDTH), lambda i: (i // per_mem_tiles, 0, 0))
    res = lambda a: pl.BlockSpec(a.shape, lambda i: (0,) * a.ndim, pipeline_mode=pl.Buffered(1))
    small = jax.ShapeDtypeStruct((t, LANES), I32)
    return pl.pallas_call(
        functools.partial(_memory_router_kernel, rpm), grid=(n,), name=name,
        in_specs=[row(D_MODEL), mem, mem, res(g_in), res(w_mq), res(g_mq), res(w_mo), res(g_moe), res(w_r_hi), res(w_r_lo), res(b_r)],
        out_specs=[row(D_MODEL), row(D_MODEL // 2), row(LANES), row(LANES), row(LANES),
                   pl.BlockSpec((8, LANES), lambda i: (0, 0))],
        out_shape=(jax.ShapeDtypeStruct((t, D_MODEL), F32), jax.ShapeDtypeStruct((t, D_MODEL // 2), U32),
                   small, jax.ShapeDtypeStruct((t, LANES), F32), small, jax.ShapeDtypeStruct((8, LANES), I32)),
        scratch_shapes=[pltpu.VMEM((tile, MEM_WIDTH), F32), pltpu.VMEM((8, LANES), F32)],
        compiler_params=_params(),
    )(x, mk, mv, g_in, w_mq, g_mq, w_mo, g_moe, w_r_hi, w_r_lo, b_r)


def _dispatch_kernel(dest_ref, hn_ref, *rest):
    xb_ref, sem = rest[-2:]
    tile = hn_ref.shape[0]

    def copy(t, k):
        return pltpu.make_async_copy(hn_ref.at[pl.ds(t, 1), :],
                                     xb_ref.at[pl.ds(dest_ref[0, t * TOP_K + k], 1), :], sem)

    def start(t, carry):
        for k in range(TOP_K):
            copy(t, k).start()
        return carry

    def wait(t, carry):
        for k in range(TOP_K):
            copy(t, k).wait()
        return carry

    lax.fori_loop(0, tile, start, 0)
    lax.fori_loop(0, tile, wait, 0)


def _dispatch(hn, dest, xb, n_rows, tile):
    t = hn.shape[0]
    n = t // tile
    dest = dest.reshape(n, 1, tile * TOP_K)
    in_specs = [pl.BlockSpec((None, 1, tile * TOP_K), lambda i: (i, 0, 0), memory_space=pltpu.SMEM),
                pl.BlockSpec((tile, hn.shape[1]), lambda i: (i, 0))]
    args = [dest, hn]
    if xb is not None:
        in_specs.append(pl.BlockSpec(memory_space=pl.ANY))
        args.append(xb)
    return pl.pallas_call(
        _dispatch_kernel, grid=(n,), name="dispatch",
        in_specs=in_specs,
        out_specs=pl.BlockSpec(memory_space=pl.ANY),
        out_shape=jax.ShapeDtypeStruct((n_rows, hn.shape[1]), hn.dtype),
        scratch_shapes=[pltpu.SemaphoreType.DMA(())],
        input_output_aliases={} if xb is None else {2: 0},
        compiler_params=_params(),
    )(*args)


def _experts_kernel(be_ref, nv_ref, xi_ref, x_ref, *rest):
    del be_ref, xi_ref
    wg_refs, wu_refs = rest[0:W_SPLIT], rest[W_SPLIT:2 * W_SPLIT]
    bg_ref, bu_ref = rest[2 * W_SPLIT:2 * W_SPLIT + 2]
    wd_refs = rest[2 * W_SPLIT + 2:3 * W_SPLIT + 2]
    bd_ref, y_ref, xb_s, h_s = rest[3 * W_SPLIT + 2:]
    slab = lambda refs: jnp.concatenate([r[0].astype(BF16) for r in refs], axis=0)
    b = pl.program_id(0)
    s = pl.program_id(1)
    n_ff = D_FF // FF_TILE
    valid = nv_ref[b]
    n_sub = (valid + EXPERT_SUB - 1) // EXPERT_SUB

    @pl.when(s == 0)
    def _():
        def cast(i, carry):
            rows = pl.ds(pl.multiple_of(i * EXPERT_SUB, EXPERT_SUB), EXPERT_SUB)
            keep = (i * EXPERT_SUB + lax.broadcasted_iota(I32, (EXPERT_SUB, 1), 0)) < valid
            lo, hi = _unpack_rows(x_ref[rows, :])
            xb_s[rows, 0:D_MODEL // 2] = jnp.where(keep, lo, 0.0).astype(BF16)
            xb_s[rows, D_MODEL // 2:] = jnp.where(keep, hi, 0.0).astype(BF16)
            return carry
        lax.fori_loop(0, n_sub, cast, 0)

    @pl.when(s < n_ff)
    def _():
        wg = slab(wg_refs)
        wu = slab(wu_refs)
        bg = bg_ref[0]
        bu = bu_ref[0]

        def up(i, carry):
            rows = pl.ds(pl.multiple_of(i * EXPERT_SUB, EXPERT_SUB), EXPERT_SUB)
            xs = xb_s[rows, :]
            gate = jnp.minimum(_dot(xs, wg) + bg, SWIGLU_LIMIT)
            lin = jnp.clip(_dot(xs, wu) + bu, -SWIGLU_LIMIT, SWIGLU_LIMIT)
            act = (lin + 1.0) * (gate * (1.0 / (1.0 + jnp.exp(-SWIGLU_ALPHA * gate))))
            h_s[s, rows, :] = act.astype(BF16)
            return carry
        lax.fori_loop(0, n_sub, up, 0)

    @pl.when(s >= n_ff)
    def _():
        wd = slab(wd_refs)
        bd = bd_ref[0]

        def down(i, carry):
            rows = pl.ds(pl.multiple_of(i * EXPERT_SUB, EXPERT_SUB), EXPERT_SUB)
            h = jnp.concatenate([h_s[f, rows, :] for f in range(n_ff)], axis=1)
            y_ref[rows, :] = _dot(h, wd) + bd
            return carry
        lax.fori_loop(0, n_sub, down, 0)


def _experts(xb, block_expert, block_valid, block_index, w_gu, b_gu, w_d, b_d):
    n_blocks = xb.shape[0] // EXPERT_ROWS
    n_ff = D_FF // FF_TILE
    n_out = D_MODEL // OUT_TILE

    def ff(b, s, nv):
        return jnp.where(nv[b] > 0, jnp.minimum(s, n_ff - 1), n_ff - 1)

    def oc(b, s, nv):
        return jnp.where(nv[b] > 0, jnp.maximum(s - n_ff, 0), n_out - 1)

    b_gu3 = b_gu.reshape(N_EXPERTS, 1, 2 * D_FF)
    b_d3 = b_d.reshape(N_EXPERTS, 1, D_MODEL)
    def gate_slab(j):
        return pl.BlockSpec((1, D_MODEL // W_SPLIT, FF_TILE), lambda b, s, be, nv, xi: (be[b], j, ff(b, s, nv)))

    def up_slab(j):
        return pl.BlockSpec((1, D_MODEL // W_SPLIT, FF_TILE), lambda b, s, be, nv, xi: (be[b], j, ff(b, s, nv) + n_ff))

    def down_slab(j):
        return pl.BlockSpec((1, D_FF // W_SPLIT, OUT_TILE), lambda b, s, be, nv, xi: (be[b], j, oc(b, s, nv)))

    slabs = range(W_SPLIT)
    grid_spec = pltpu.PrefetchScalarGridSpec(
        num_scalar_prefetch=3, grid=(n_blocks, n_ff + n_out),
        in_specs=[pl.BlockSpec((EXPERT_ROWS, D_MODEL // 2), lambda b, s, be, nv, xi: (xi[b], 0))]
        + [gate_slab(j) for j in slabs] + [up_slab(j) for j in slabs]
        + [pl.BlockSpec((1, 1, FF_TILE), lambda b, s, be, nv, xi: (be[b], 0, ff(b, s, nv))),
           pl.BlockSpec((1, 1, FF_TILE), lambda b, s, be, nv, xi: (be[b], 0, ff(b, s, nv) + n_ff))]
        + [down_slab(j) for j in slabs]
        + [pl.BlockSpec((1, 1, OUT_TILE), lambda b, s, be, nv, xi: (be[b], 0, oc(b, s, nv)))],
        out_specs=pl.BlockSpec((EXPERT_ROWS, OUT_TILE), lambda b, s, be, nv, xi: (xi[b], oc(b, s, nv))),
        scratch_shapes=[pltpu.VMEM((EXPERT_ROWS, D_MODEL), BF16), pltpu.VMEM((n_ff, EXPERT_ROWS, FF_TILE), BF16)])
    return pl.pallas_call(
        _experts_kernel, grid_spec=grid_spec, name="experts",
        out_shape=jax.ShapeDtypeStruct((n_blocks * EXPERT_ROWS, D_MODEL), F32),
        compiler_params=_params(2),
    )(block_expert, block_valid, block_index, xb, *([w_gu] * (2 * W_SPLIT)), b_gu3, b_gu3, *([w_d] * W_SPLIT), b_d3)


def _combine_kernel(n_tiles, dest_ref, next_ref, x_ref, gate_ref, yb_ref, o_ref, buf, sem):
    tile = x_ref.shape[0]
    i = pl.program_id(0)
    slot = i % 2

    def copy(idx_ref, s, t, k):
        return pltpu.make_async_copy(yb_ref.at[pl.ds(idx_ref[0, t * TOP_K + k], 1), :],
                                     buf.at[s, k, pl.ds(t, 1), :], sem.at[s])

    def start_tile(idx_ref, s):
        def start(t, carry):
            for k in range(TOP_K):
                copy(idx_ref, s, t, k).start()
            return carry
        lax.fori_loop(0, tile, start, 0)

    pl.when(i == 0)(lambda: start_tile(dest_ref, 0))
    pl.when(i + 1 < n_tiles)(lambda: start_tile(next_ref, 1 - slot))

    def wait(t, carry):
        for k in range(TOP_K):
            copy(dest_ref, slot, t, k).wait()
        return carry
    lax.fori_loop(0, tile, wait, 0)

    gate = gate_ref[...]
    acc = x_ref[...]
    for k in range(TOP_K):
        acc = acc + gate[:, k:k + 1] * buf[slot, k]
    o_ref[...] = acc


def _combine(x, gate, dest, yb, tile):
    t = x.shape[0]
    n = t // tile
    dest = dest.reshape(n, 1, tile * TOP_K)
    return pl.pallas_call(
        functools.partial(_combine_kernel, n), grid=(n,), name="combine",
        in_specs=[pl.BlockSpec((None, 1, tile * TOP_K), lambda i: (i, 0, 0), memory_space=pltpu.SMEM),
                  pl.BlockSpec((None, 1, tile * TOP_K), lambda i: (jnp.minimum(i + 1, n - 1), 0, 0),
                               memory_space=pltpu.SMEM),
                  pl.BlockSpec((tile, D_MODEL), lambda i: (i, 0)),
                  pl.BlockSpec((tile, LANES), lambda i: (i, 0)),
                  pl.BlockSpec(memory_space=pl.ANY)],
        out_specs=pl.BlockSpec((tile, D_MODEL), lambda i: (i, 0)),
        out_shape=jax.ShapeDtypeStruct((t, D_MODEL), F32),
        scratch_shapes=[pltpu.VMEM((2, TOP_K, tile, D_MODEL), F32), pltpu.SemaphoreType.DMA((2,))],
        compiler_params=_params(),
    )(dest, dest, x, gate, yb)


def _rope_tables(pos):
    half = B_HEAD_DIM // 2
    inv_freq = ROPE_THETA ** (-jnp.arange(half, dtype=F32) / half)
    ang = pos.astype(F32)[:, None] * inv_freq[None, :]
    cos = jnp.cos(ang)
    sin = jnp.sin(ang)
    return jnp.tile(cos, (1, LANES // half)), jnp.tile(jnp.concatenate([-sin, sin], axis=1), (1, LANES // B_HEAD_DIM))


def _layer(l, xp, xs, cache_swa_k, cache_swa_v, cache_mem_k, cache_mem_v, mem_prompt,
           g_mix, w_in, g_sgu, w_s, b_s, g_q, g_k, sinks, g_out_a, g_out_b, w_out,
           g_mem_in, g_mem_src, w_mq, w_mk, w_mv, g_mq, g_mk, w_mo,
           g_moe, w_router, b_router, w_gate_up, b_gate_up, w_down, b_down):
    n_b, seq, _ = xp.shape
    n_db, dec_seq, _ = xs.shape
    tp = n_b * seq
    ts = n_db * dec_seq
    tile = min(TOKEN_TILE, seq)
    row = lambda a: a[l].reshape(1, -1)

    xp2 = xp.reshape(tp, D_MODEL)
    xs2 = xs.reshape(ts, D_MODEL)
    head = jnp.arange(QK_WIDTH, dtype=I32) // B_HEAD_DIM
    seg = (head[:, None] == jnp.arange(LANES, dtype=I32)[None, :]).astype(BF16)
    g_qk = jnp.concatenate([jnp.tile(g_q[l], B_HEADS), jnp.tile(g_k[l], B_KV_HEADS)]).reshape(1, -1)
    w_in_b = w_in[l].astype(BF16)
    cos_p, sin_p = _rope_tables(jnp.arange(seq, dtype=I32))
    cos_s, sin_s = _rope_tables(PAST_LEN + jnp.arange(SAMPLE_TILE, dtype=I32) % dec_seq)
    up, vap, qp, kp, vp = _in_proj(xp2, row(g_mix), w_in_b, row(g_sgu), g_qk, seg, seg.T, cos_p, sin_p,
                                   tile, "in_proj_prompt")
    us, vas, qs, ks, vs = _in_proj(xs2, row(g_mix), w_in_b, row(g_sgu), g_qk, seg, seg.T, cos_s, sin_s,
                                   SAMPLE_TILE, "in_proj_sample")

    w_out_b = w_out[l].astype(BF16)
    tri = jnp.tril(jnp.ones((MLP_CHUNK, MLP_CHUNK), bool))
    ws_p = jnp.where(tri[None], w_s[l], 0.0).astype(BF16)
    bias_p = jnp.repeat(b_s[l].T, A_HEAD_DIM, axis=1)
    x1p = _mixer_prompt(xp2, up, vap, qp, kp, vp, sinks[l], ws_p, bias_p, row(g_out_a), row(g_out_b), w_out_b, seq, tile)

    reps = SAMPLE_TILE // dec_seq
    tri_s = jnp.tril(jnp.ones((dec_seq, dec_seq), bool))
    ws_small = jnp.where(tri_s[None], w_s[l][:, :dec_seq, :dec_seq], 0.0)
    ws_s = jnp.einsum("ab,hts->hatbs", jnp.eye(reps, dtype=F32), ws_small).reshape(A_HEADS, SAMPLE_TILE, SAMPLE_TILE).astype(BF16)
    bias_s = jnp.tile(jnp.repeat(b_s[l][:, :dec_seq].T, A_HEAD_DIM, axis=1), (reps, 1))
    ck = cache_swa_k[l].reshape(n_db, -1, KV_WIDTH)
    cv = cache_swa_v[l].reshape(n_db, -1, KV_WIDTH)
    x1s = _mixer_sample(xs2, us, vas, qs, ks, vs, ck, cv, sinks[l], ws_s, bias_s,
                        row(g_out_a), row(g_out_b), w_out_b, dec_seq, SAMPLE_TILE)

    mk_p, mv_p = _mem_kv(mem_prompt.reshape(-1, D_MODEL), row(g_mem_src), w_mk[l].astype(BF16), w_mv[l].astype(BF16), row(g_mk))
    n_mem = mem_prompt.shape[1]
    w_r = jnp.pad(w_router[l], ((0, 0), (0, LANES - N_EXPERTS)))
    b_r = jnp.pad(b_router[l], (0, LANES - N_EXPERTS)).reshape(1, -1)
    w_r_hi = w_r.astype(BF16)
    w_r_lo = (w_r - w_r_hi.astype(F32)).astype(BF16)
    mem_args = (row(g_mem_in), w_mq[l].astype(BF16), row(g_mq), w_mo[l].astype(BF16), row(g_moe), w_r_hi, w_r_lo, b_r)
    x2p, hnp, idxp, gatep, rankp, cntp = _memory_router(
        x1p, mk_p.reshape(n_b, n_mem, MEM_WIDTH), mv_p.reshape(n_b, n_mem, MEM_WIDTH),
        seq, *mem_args, tile, "memory_router_prompt")
    x2s, hns, idxs, gates, ranks, cnts = _memory_router(
        x1s, cache_mem_k[l].reshape(n_db, -1, MEM_WIDTH), cache_mem_v[l].reshape(n_db, -1, MEM_WIDTH),
        dec_seq, *mem_args, SAMPLE_TILE, "memory_router_sample")

    cnt_p = cntp[0, :N_EXPERTS]
    cnt_s = cnts[0, :N_EXPERTS]
    total = cnt_p + cnt_s
    nblk = (total + EXPERT_ROWS - 1) // EXPERT_ROWS
    blk_end = jnp.cumsum(nblk)
    row_start = (blk_end - nblk) * EXPERT_ROWS
    n_blocks = (tp + ts) * TOP_K // EXPERT_ROWS + N_EXPERTS
    bidx = jnp.arange(n_blocks, dtype=I32)
    used = bidx < blk_end[-1]
    last = jnp.maximum(blk_end[-1] - 1, 0)
    bsafe = jnp.minimum(bidx, last)
    block_expert = jnp.minimum(jnp.searchsorted(blk_end, bsafe, side="right"), N_EXPERTS - 1).astype(I32)
    within = bsafe - (blk_end - nblk)[block_expert]
    block_valid = jnp.where(used, jnp.clip(total[block_expert] - within * EXPERT_ROWS, 0, EXPERT_ROWS), 0).astype(I32)
    dest_p = row_start[idxp[:, :TOP_K]] + rankp[:, :TOP_K]
    dest_s = (row_start + cnt_p)[idxs[:, :TOP_K]] + ranks[:, :TOP_K]

    xb = _dispatch(hnp, dest_p, None, n_blocks * EXPERT_ROWS, min(ROW_TILE, tp))
    xb = _dispatch(hns, dest_s, xb, n_blocks * EXPERT_ROWS, min(ROW_TILE, ts))
    yb = _experts(xb, block_expert, block_valid, bsafe.astype(I32), w_gate_up[l].astype(BF16), b_gate_up[l],
                  w_down[l].astype(BF16), b_down[l])
    yp = _combine(x2p, gatep, dest_p, yb, min(ROW_TILE, tp))
    ys = _combine(x2s, gates, dest_s, yb, min(ROW_TILE, ts))

    new = dict(
        swa_k_p=kp.reshape(n_b, seq, B_KV_HEADS, B_HEAD_DIM)[:, seq - WINDOW:],
        swa_v_p=vp.reshape(n_b, seq, B_KV_HEADS, B_HEAD_DIM)[:, seq - WINDOW:],
        mem_k_p=mk_p.reshape(n_b, n_mem, MEM_HEADS, MEM_HEAD_DIM),
        mem_v_p=mv_p.reshape(n_b, n_mem, MEM_HEADS, MEM_HEAD_DIM),
        swa_k_s=ks.reshape(n_db, dec_seq, B_KV_HEADS, B_HEAD_DIM),
        swa_v_s=vs.reshape(n_db, dec_seq, B_KV_HEADS, B_HEAD_DIM),
        sgu_v_s=vas.reshape(n_db, dec_seq, A_HEADS, A_HEAD_DIM))
    return yp.reshape(n_b, seq, D_MODEL), ys.reshape(n_db, dec_seq, D_MODEL), new


def kernel(x_prompt, x_sample, cache_swa_k, cache_swa_v, cache_mem_k, cache_mem_v, mem_prompt, g_mix, w_in, g_sgu, w_s, b_s, g_q, g_k, sinks, g_out_a, g_out_b, w_out, g_mem_in, g_mem_src, w_mq, w_mk, w_mv, g_mq, g_mk, w_mo, g_moe, w_router, b_router, w_gate_up, b_gate_up, w_down, b_down):
    xp, xs = x_prompt, x_sample
    news = []
    for l in range(g_mix.shape[0]):
        xp, xs, new = _layer(l, xp, xs, cache_swa_k, cache_swa_v, cache_mem_k, cache_mem_v, mem_prompt,
                             g_mix, w_in, g_sgu, w_s, b_s, g_q, g_k, sinks, g_out_a, g_out_b, w_out,
                             g_mem_in, g_mem_src, w_mq, w_mk, w_mv, g_mq, g_mk, w_mo,
                             g_moe, w_router, b_router, w_gate_up, b_gate_up, w_down, b_down)
        news.append(new)
    stack = lambda name: jnp.stack([n[name] for n in news], 0)
    return (xp, xs, stack("swa_k_p"), stack("swa_v_p"), stack("mem_k_p"), stack("mem_v_p"),
            stack("swa_k_s"), stack("swa_v_s"), stack("sgu_v_s"))
```

```python
import functools

import jax
import jax.numpy as jnp
from jax import lax
from jax.experimental import pallas as pl
from jax.experimental.pallas import tpu as pltpu

F32 = jnp.float32
BF16 = jnp.bfloat16
I32 = jnp.int32
U32 = jnp.uint32

D_MODEL = 2048
PAST_LEN = 2048
CHUNK = 64
EPS = 1e-6
A_HEADS = 16
A_HEAD_DIM = 64
A_WIDTH = A_HEADS * A_HEAD_DIM
MLP_CHUNK = 128
B_HEADS = 16
B_KV_HEADS = 2
B_HEAD_DIM = 64
B_WIDTH = B_HEADS * B_HEAD_DIM
KV_WIDTH = B_KV_HEADS * B_HEAD_DIM
WINDOW = 128
ROPE_THETA = 10000.0
ATTN_SCALE = B_HEAD_DIM ** -0.5
Q_OFF = 2 * A_WIDTH
K_OFF = Q_OFF + B_WIDTH
V_OFF = K_OFF + KV_WIDTH
IN_WIDTH = V_OFF + KV_WIDTH
QK_WIDTH = B_WIDTH + KV_WIDTH
MEM_HEADS = 4
MEM_HEAD_DIM = 128
MEM_WIDTH = MEM_HEADS * MEM_HEAD_DIM
MEM_SCALE = MEM_HEAD_DIM ** -0.5
N_EXPERTS = 32
TOP_K = 4
D_FF = D_MODEL
SWIGLU_LIMIT = 7.0
SWIGLU_ALPHA = 1.702
NEG_BIG = -1e30

LANES = 128
VMEM_LIMIT = 56 * 1024 * 1024

TOKEN_TILE = 512
SAMPLE_TILE = 128
EXPERT_ROWS = 1280
EXPERT_SUB = 256
FF_TILE = 512
OUT_TILE = 512
ROW_TILE = 128


def _dot(a, b):
    return jnp.dot(a, b, preferred_element_type=F32)


def _dot_nt(a, b):
    return lax.dot_general(a, b, (((1,), (1,)), ((), ())), preferred_element_type=F32)


def _rms(x):
    return x * lax.rsqrt(jnp.mean(x * x, axis=-1, keepdims=True) + EPS)


def _gelu(x):
    return 0.5 * x * (1.0 + lax.erf(x * 0.7071067811865476))


def _pack_rows(x):
    c = x.shape[1] // 2
    bits = pltpu.bitcast(x.astype(BF16).astype(F32), U32)
    return (bits[:, :c] >> 16) | (bits[:, c:] & jnp.uint32(0xFFFF0000))


def _unpack_rows(w):
    return pltpu.bitcast(w << 16, F32), pltpu.bitcast(w & jnp.uint32(0xFFFF0000), F32)


def _params(n_axes=1):
    return pltpu.CompilerParams(dimension_semantics=("arbitrary",) * n_axes,
                                vmem_limit_bytes=VMEM_LIMIT)


def _resident(shape):
    nd = len(shape)
    return pl.BlockSpec(shape, lambda *_: (0,) * nd, pipeline_mode=pl.Buffered(1))


def _mem_kv_kernel(mem_ref, gsrc_ref, wk_ref, wv_ref, gk_ref, k_ref, v_ref):
    m = (_rms(mem_ref[...]) * gsrc_ref[...]).astype(BF16)
    kz = _dot(m, wk_ref[...])
    gk = gk_ref[...]
    for h in range(MEM_HEADS):
        sl = slice(h * MEM_HEAD_DIM, (h + 1) * MEM_HEAD_DIM)
        k_ref[:, sl] = _rms(kz[:, sl]) * gk
    v_ref[...] = _dot(m, wv_ref[...])


def _mem_kv(mem, g_src, w_mk, w_mv, g_mk):
    n = mem.shape[0]
    out = jax.ShapeDtypeStruct((n, MEM_WIDTH), F32)
    return pl.pallas_call(
        _mem_kv_kernel, out_shape=(out, out), name="mem_kv",
        compiler_params=pltpu.CompilerParams(vmem_limit_bytes=VMEM_LIMIT),
    )(mem, g_src, w_mk, w_mv, g_mk)


def _in_proj_kernel(x_ref, gmix_ref, w_ref, gsgu_ref, gqk_ref, seg_ref, segt_ref, cos_ref, sin_ref,
                    u_ref, va_ref, q_ref, k_ref, v_ref):
    xn = (_rms(x_ref[...]) * gmix_ref[...]).astype(BF16)
    u_ref[...] = _gelu(_dot(xn, w_ref[:, 0:A_WIDTH])).astype(BF16)
    va_ref[...] = _rms(_gelu(_dot(xn, w_ref[:, A_WIDTH:Q_OFF]))) * gsgu_ref[...]
    v_ref[...] = _dot(xn, w_ref[:, V_OFF:IN_WIDTH])

    qk = _dot(xn, w_ref[:, Q_OFF:V_OFF])
    ssq = _dot((qk * qk).astype(BF16), seg_ref[...])
    inv = lax.rsqrt(ssq * (1.0 / B_HEAD_DIM) + EPS)
    inv_hi = inv.astype(BF16)
    inv_lo = (inv - inv_hi.astype(F32)).astype(BF16)
    inv_b = _dot(inv_hi, segt_ref[...]) + _dot(inv_lo, segt_ref[...])
    qkn = (qk * inv_b) * gqk_ref[...]

    cos = cos_ref[...]
    sin = sin_ref[...]
    lane = lax.broadcasted_iota(I32, cos.shape, 1)
    first_half = (lane % B_HEAD_DIM) < (B_HEAD_DIM // 2)
    for g in range(QK_WIDTH // LANES):
        xg = qkn[:, g * LANES:(g + 1) * LANES]
        rot = jnp.where(first_half, pltpu.roll(xg, LANES - B_HEAD_DIM // 2, 1),
                        pltpu.roll(xg, B_HEAD_DIM // 2, 1))
        r = xg * cos + rot * sin
        if g < B_WIDTH // LANES:
            q_ref[:, g * LANES:(g + 1) * LANES] = (r * ATTN_SCALE).astype(BF16)
        else:
            k_ref[...] = r


def _in_proj(x, g_mix, w_in, g_sgu, g_qk, seg, segt, cos_tab, sin_tab, tile, name):
    t = x.shape[0]
    n = t // tile
    tab_tiles = cos_tab.shape[0] // tile
    row = lambda w: pl.BlockSpec((tile, w), lambda i: (i, 0))
    tab = pl.BlockSpec((tile, LANES), lambda i: (i % tab_tiles, 0))
    return pl.pallas_call(
        _in_proj_kernel, grid=(n,), name=name,
        in_specs=[row(D_MODEL), _resident(g_mix.shape), _resident(w_in.shape), _resident(g_sgu.shape),
                  _resident(g_qk.shape), _resident(seg.shape), _resident(segt.shape), tab, tab],
        out_specs=[row(A_WIDTH), row(A_WIDTH), row(B_WIDTH), row(KV_WIDTH), row(KV_WIDTH)],
        out_shape=(jax.ShapeDtypeStruct((t, A_WIDTH), BF16), jax.ShapeDtypeStruct((t, A_WIDTH), F32),
                   jax.ShapeDtypeStruct((t, B_WIDTH), BF16), jax.ShapeDtypeStruct((t, KV_WIDTH), F32),
                   jax.ShapeDtypeStruct((t, KV_WIDTH), F32)),
        compiler_params=_params(),
    )(x, g_mix, w_in, g_sgu, g_qk, seg, segt, cos_tab, sin_tab)


def _split_heads(ref_dst_lo, ref_dst_hi, rows, x, swap):
    lane = lax.broadcasted_iota(I32, x.shape, 1)
    low = lane < B_HEAD_DIM
    zero = jnp.zeros_like(x)
    ref_dst_lo[0, rows, :] = jnp.where(low, x, zero).astype(BF16)
    ref_dst_hi[0, rows, :] = jnp.where(low, zero, swap).astype(BF16)
    ref_dst_lo[1, rows, :] = jnp.where(low, swap, zero).astype(BF16)
    ref_dst_hi[1, rows, :] = jnp.where(low, zero, x).astype(BF16)


def _sgu_chunk(u, va, ws_ref, bias):
    lane = lax.broadcasted_iota(I32, (MLP_CHUNK, LANES), 1)
    low = lane < A_HEAD_DIM
    outs = []
    for p in range(A_WIDTH // LANES):
        sl = slice(p * LANES, (p + 1) * LANES)
        v2 = va[:, sl]
        zero = jnp.zeros_like(v2)
        mixed = (_dot(ws_ref[2 * p], jnp.where(low, v2, zero).astype(BF16))
                 + _dot(ws_ref[2 * p + 1], jnp.where(low, zero, v2).astype(BF16)))
        outs.append(u[:, sl].astype(F32) * (mixed + bias[:, sl]))
    return jnp.concatenate(outs, axis=1)


def _attend(q2, kl, kh, vl, vh, mask, sink_even, sink_odd):
    outs = []
    for kk, vv, sink in ((kl, vl, sink_even), (kh, vh, sink_odd)):
        s = _dot_nt(q2, kk)
        if mask is not None:
            s = jnp.where(mask, s, NEG_BIG)
        m = jnp.maximum(jnp.max(s, axis=-1, keepdims=True), sink)
        p = jnp.exp(s - m)
        den = jnp.sum(p, axis=-1, keepdims=True) + jnp.exp(sink - m)
        outs.append(_dot(p.astype(BF16), vv) * (1.0 / den))
    return outs[0] + outs[1]


def _out_proj(oa, ob, ga_ref, gb_ref, wout_ref, x):
    a = (_rms(oa) * ga_ref[...]).astype(BF16)
    b = (_rms(ob) * gb_ref[...]).astype(BF16)
    return x + _dot(a, wout_ref[0:A_WIDTH, :]) + _dot(b, wout_ref[A_WIDTH:, :])


def _mixer_prompt_kernel(tiles_per_seq, sinks_ref, x_ref, u_ref, va_ref, q_ref, kc_ref, vc_ref, kp_ref, vp_ref,
                         ws_ref, bias_ref, ga_ref, gb_ref, wout_ref, o_ref,
                         kl_s, kh_s, vl_s, vh_s, oa_s, ob_s):
    tile = x_ref.shape[0]
    n_sub = tile // MLP_CHUNK
    seq_start = (pl.program_id(0) % tiles_per_seq) == 0

    for src_p, src_c, dl, dh in ((kp_ref, kc_ref, kl_s, kh_s), (vp_ref, vc_ref, vl_s, vh_s)):
        prev = src_p[...]
        cur = src_c[...]
        _split_heads(dl, dh, slice(0, WINDOW), prev, pltpu.roll(prev, B_HEAD_DIM, 1))
        _split_heads(dl, dh, slice(WINDOW, WINDOW + tile), cur, pltpu.roll(cur, B_HEAD_DIM, 1))

    qc = lax.broadcasted_iota(I32, (MLP_CHUNK, 2 * MLP_CHUNK), 0) // CHUNK
    kc = lax.broadcasted_iota(I32, (MLP_CHUNK, 2 * MLP_CHUNK), 1) // CHUNK
    window_mask = (kc >= qc) & (kc <= qc + 2)
    bias = bias_ref[...]

    def sub(j, carry):
        r0 = pl.multiple_of(j * MLP_CHUNK, MLP_CHUNK)
        rows = pl.ds(r0, MLP_CHUNK)
        keys = pl.ds(r0, 2 * MLP_CHUNK)
        first_key_chunk = jnp.where(jnp.logical_and(seq_start, j == 0), 2, 0)
        mask = window_mask & (kc >= first_key_chunk)
        oa_s[rows, :] = _sgu_chunk(u_ref[rows, :], va_ref[rows, :], ws_ref, bias)
        for p in range(B_WIDTH // LANES):
            h = p // (B_WIDTH // LANES // B_KV_HEADS)
            sl = slice(p * LANES, (p + 1) * LANES)
            ob_s[rows, sl] = _attend(q_ref[rows, sl], kl_s[h, keys, :], kh_s[h, keys, :],
                                     vl_s[h, keys, :], vh_s[h, keys, :], mask,
                                     sinks_ref[2 * p], sinks_ref[2 * p + 1])
        return carry

    lax.fori_loop(0, n_sub, sub, 0)
    o_ref[...] = _out_proj(oa_s[...], ob_s[...], ga_ref, gb_ref, wout_ref, x_ref[...])


def _mixer_prompt(x, u, va, q, k, v, sinks, ws, bias, g_a, g_b, w_out, seq, tile):
    t = x.shape[0]
    n = t // tile
    tiles_per_seq = seq // tile
    per = tile // WINDOW
    row = lambda w: pl.BlockSpec((tile, w), lambda i, s: (i, 0))
    prev = pl.BlockSpec((WINDOW, KV_WIDTH), lambda i, s: (jnp.maximum(i * per - 1, 0), 0))
    res = lambda a: pl.BlockSpec(a.shape, lambda i, s: (0,) * a.ndim, pipeline_mode=pl.Buffered(1))
    grid_spec = pltpu.PrefetchScalarGridSpec(
        num_scalar_prefetch=1, grid=(n,),
        in_specs=[row(D_MODEL), row(A_WIDTH), row(A_WIDTH), row(B_WIDTH), row(KV_WIDTH), row(KV_WIDTH), prev, prev,
                  res(ws), res(bias), res(g_a), res(g_b), res(w_out)],
        out_specs=row(D_MODEL),
        scratch_shapes=[pltpu.VMEM((B_KV_HEADS, WINDOW + tile, LANES), BF16) for _ in range(4)]
        + [pltpu.VMEM((tile, A_WIDTH), F32), pltpu.VMEM((tile, B_WIDTH), F32)])
    return pl.pallas_call(
        functools.partial(_mixer_prompt_kernel, tiles_per_seq), grid_spec=grid_spec, name="mixer_prompt",
        out_shape=jax.ShapeDtypeStruct((t, D_MODEL), F32), compiler_params=_params(),
    )(sinks, x, u, va, q, k, v, k, v, ws, bias, g_a, g_b, w_out)


def _mixer_sample_kernel(dec_seq, sinks_ref, x_ref, u_ref, va_ref, q_ref, kc_ref, vc_ref, ck_ref, cv_ref,
                         ws_ref, bias_ref, ga_ref, gb_ref, wout_ref, o_ref,
                         kl_s, kh_s, vl_s, vh_s, ob_s):
    tile = x_ref.shape[0]
    n_seq = tile // dec_seq
    n_cache = ck_ref.shape[1]
    n_keys = n_cache + dec_seq
    oa = _sgu_chunk(u_ref[...], va_ref[...], ws_ref, bias_ref[...])
    for b in range(n_seq):
        rows = slice(b * dec_seq, (b + 1) * dec_seq)
        for src_c, src_n, dl, dh in ((ck_ref, kc_ref, kl_s, kh_s), (cv_ref, vc_ref, vl_s, vh_s)):
            old = src_c[b]
            new = src_n[rows, :]
            _split_heads(dl, dh, slice(0, n_cache), old, pltpu.roll(old, B_HEAD_DIM, 1))
            _split_heads(dl, dh, slice(n_cache, n_keys), new, pltpu.roll(new, B_HEAD_DIM, 1))
        for p in range(B_WIDTH // LANES):
            h = p // (B_WIDTH // LANES // B_KV_HEADS)
            sl = slice(p * LANES, (p + 1) * LANES)
            ob_s[rows, sl] = _attend(q_ref[rows, sl], kl_s[h], kh_s[h], vl_s[h], vh_s[h], None,
                                     sinks_ref[2 * p], sinks_ref[2 * p + 1])
    o_ref[...] = _out_proj(oa, ob_s[...], ga_ref, gb_ref, wout_ref, x_ref[...])


def _mixer_sample(x, u, va, q, k, v, cache_k, cache_v, sinks, ws, bias, g_a, g_b, w_out, dec_seq, tile):
    t = x.shape[0]
    n = t // tile
    n_seq = tile // dec_seq
    n_cache = cache_k.shape[1]
    row = lambda w: pl.BlockSpec((tile, w), lambda i, s: (i, 0))
    cache = pl.BlockSpec((n_seq, n_cache, KV_WIDTH), lambda i, s: (i, 0, 0))
    res = lambda a: pl.BlockSpec(a.shape, lambda i, s: (0,) * a.ndim, pipeline_mode=pl.Buffered(1))
    grid_spec = pltpu.PrefetchScalarGridSpec(
        num_scalar_prefetch=1, grid=(n,),
        in_specs=[row(D_MODEL), row(A_WIDTH), row(A_WIDTH), row(B_WIDTH), row(KV_WIDTH), row(KV_WIDTH), cache, cache,
                  res(ws), res(bias), res(g_a), res(g_b), res(w_out)],
        out_specs=row(D_MODEL),
        scratch_shapes=[pltpu.VMEM((B_KV_HEADS, n_cache + dec_seq, LANES), BF16) for _ in range(4)]
        + [pltpu.VMEM((tile, B_WIDTH), F32)])
    return pl.pallas_call(
        functools.partial(_mixer_sample_kernel, dec_seq), grid_spec=grid_spec, name="mixer_sample",
        out_shape=jax.ShapeDtypeStruct((t, D_MODEL), F32), compiler_params=_params(),
    )(sinks, x, u, va, q, k, v, cache_k, cache_v, ws, bias, g_a, g_b, w_out)


def _memory_router_kernel(rows_per_mem, x_ref, mk_ref, mv_ref, gin_ref, wq_ref, gq_ref, wo_ref,
                          gmoe_ref, wrh_ref, wrl_ref, br_ref,
                          x2_ref, hn_ref, idx_ref, gate_ref, rank_ref, cnt_ref, o_s, base_s):
    tile = x_ref.shape[0]
    x = x_ref[...]
    qz = _dot((_rms(x) * gin_ref[...]).astype(BF16), wq_ref[...])
    gq = gq_ref[...]
    for h in range(MEM_HEADS):
        sl = slice(h * MEM_HEAD_DIM, (h + 1) * MEM_HEAD_DIM)
        qh = (_rms(qz[:, sl]) * gq).astype(BF16)
        for r in range(tile // rows_per_mem):
            rows = slice(r * rows_per_mem, (r + 1) * rows_per_mem)
            s = _dot_nt(qh[rows], mk_ref[r, :, sl].astype(BF16)) * MEM_SCALE
            p = jnp.exp(s - jnp.max(s, axis=-1, keepdims=True))
            den = jnp.sum(p, axis=-1, keepdims=True)
            o_s[rows, sl] = _dot(p.astype(BF16), mv_ref[r, :, sl].astype(BF16)) * (1.0 / den)
    x2 = x + _dot(o_s[...].astype(BF16), wo_ref[...])
    x2_ref[...] = x2
    hn = _rms(x2) * gmoe_ref[...]
    hn_ref[...] = _pack_rows(hn)

    hn_hi = hn.astype(BF16)
    hn_lo = (hn - hn_hi.astype(F32)).astype(BF16)
    logits = ((_dot(hn_lo, wrh_ref[...]) + _dot(hn_hi, wrl_ref[...])) + _dot(hn_hi, wrh_ref[...])) + br_ref[...]
    lane = lax.broadcasted_iota(I32, (tile, LANES), 1)
    lane_f = lane.astype(F32)
    work = jnp.where(lane < N_EXPERTS, logits, -jnp.inf)
    idx_out = jnp.zeros((tile, LANES), F32)
    val_out = jnp.zeros((tile, LANES), F32)
    hot = jnp.zeros((tile, LANES), F32)
    top = None
    picks = []
    for k in range(TOP_K):
        m = jnp.max(work, axis=-1, keepdims=True)
        pick = jnp.min(jnp.where(work == m, lane_f, float(LANES)), axis=-1, keepdims=True)
        chosen = lane_f == pick
        if top is None:
            top = m
        idx_out = jnp.where(lane == k, pick, idx_out)
        val_out = jnp.where(lane == k, jnp.exp(m - top), val_out)
        hot = jnp.where(chosen, 1.0, hot)
        work = jnp.where(chosen, -jnp.inf, work)
        picks.append(chosen)
    idx_ref[...] = idx_out.astype(I32)
    gate_ref[...] = val_out * (1.0 / jnp.sum(val_out, axis=-1, keepdims=True))

    @pl.when(pl.program_id(0) == 0)
    def _():
        base_s[...] = jnp.zeros_like(base_s)

    earlier = (lax.broadcasted_iota(I32, (tile, tile), 1) < lax.broadcasted_iota(I32, (tile, tile), 0))
    pos = _dot(jnp.where(earlier, 1.0, 0.0).astype(BF16), hot.astype(BF16)) + base_s[0:1, :]
    rank_out = jnp.zeros((tile, LANES), F32)
    for k in range(TOP_K):
        rank_out = jnp.where(lane == k, jnp.sum(jnp.where(picks[k], pos, 0.0), axis=-1, keepdims=True), rank_out)
    rank_ref[...] = rank_out.astype(I32)
    total = base_s[0:1, :] + jnp.sum(hot, axis=0, keepdims=True)
    base_s[...] = jnp.broadcast_to(total, base_s.shape)
    cnt_ref[...] = jnp.broadcast_to(total, cnt_ref.shape).astype(I32)


def _memory_router(x, mk, mv, rows_per_mem, g_in, w_mq, g_mq, w_mo, g_moe, w_r_hi, w_r_lo, b_r, tile, name):
    t = x.shape[0]
    n = t // tile
    mems = tile // rows_per_mem if rows_per_mem <= tile else 1
    per_mem_tiles = max(rows_per_mem // tile, 1)
    rpm = min(rows_per_mem, tile)
    row = lambda w: pl.BlockSpec((tile, w), lambda i: (i, 0))
    mem = pl.BlockSpec((mems, mk.shape[1], MEM_WIDTH), lambda i: (i // per_mem_tiles, 0, 0))
    res = lambda a: pl.BlockSpec(a.shape, lambda i: (0,) * a.ndim, pipeline_mode=pl.Buffered(1))
    small = jax.ShapeDtypeStruct((t, LANES), I32)
    return pl.pallas_call(
        functools.partial(_memory_router_kernel, rpm), grid=(n,), name=name,
        in_specs=[row(D_MODEL), mem, mem, res(g_in), res(w_mq), res(g_mq), res(w_mo), res(g_moe), res(w_r_hi), res(w_r_lo), res(b_r)],
        out_specs=[row(D_MODEL), row(D_MODEL // 2), row(LANES), row(LANES), row(LANES),
                   pl.BlockSpec((8, LANES), lambda i: (0, 0))],
        out_shape=(jax.ShapeDtypeStruct((t, D_MODEL), F32), jax.ShapeDtypeStruct((t, D_MODEL // 2), U32),
                   small, jax.ShapeDtypeStruct((t, LANES), F32), small, jax.ShapeDtypeStruct((8, LANES), I32)),
        scratch_shapes=[pltpu.VMEM((tile, MEM_WIDTH), F32), pltpu.VMEM((8, LANES), F32)],
        compiler_params=_params(),
    )(x, mk, mv, g_in, w_mq, g_mq, w_mo, g_moe, w_r_hi, w_r_lo, b_r)


def _dispatch_kernel(dest_ref, hn_ref, *rest):
    xb_ref, sem = rest[-2:]
    tile = hn_ref.shape[0]

    def copy(t, k):
        return pltpu.make_async_copy(hn_ref.at[pl.ds(t, 1), :],
                                     xb_ref.at[pl.ds(dest_ref[0, t * TOP_K + k], 1), :], sem)

    def start(t, carry):
        for k in range(TOP_K):
            copy(t, k).start()
        return carry

    def wait(t, carry):
        for k in range(TOP_K):
            copy(t, k).wait()
        return carry

    lax.fori_loop(0, tile, start, 0)
    lax.fori_loop(0, tile, wait, 0)


def _dispatch(hn, dest, xb, n_rows, tile):
    t = hn.shape[0]
    n = t // tile
    dest = dest.reshape(n, 1, tile * TOP_K)
    in_specs = [pl.BlockSpec((None, 1, tile * TOP_K), lambda i: (i, 0, 0), memory_space=pltpu.SMEM),
                pl.BlockSpec((tile, hn.shape[1]), lambda i: (i, 0))]
    args = [dest, hn]
    if xb is not None:
        in_specs.append(pl.BlockSpec(memory_space=pl.ANY))
        args.append(xb)
    return pl.pallas_call(
        _dispatch_kernel, grid=(n,), name="dispatch",
        in_specs=in_specs,
        out_specs=pl.BlockSpec(memory_space=pl.ANY),
        out_shape=jax.ShapeDtypeStruct((n_rows, hn.shape[1]), hn.dtype),
        scratch_shapes=[pltpu.SemaphoreType.DMA(())],
        input_output_aliases={} if xb is None else {2: 0},
        compiler_params=_params(),
    )(*args)


def _experts_kernel(be_ref, nv_ref, xi_ref, x_ref, wg_ref, wu_ref, bg_ref, bu_ref, wd_ref, bd_ref, y_ref, h_s):
    del be_ref, xi_ref
    b = pl.program_id(0)
    s = pl.program_id(1)
    n_ff = D_FF // FF_TILE
    valid = nv_ref[b]
    n_sub = (valid + EXPERT_SUB - 1) // EXPERT_SUB

    @pl.when(s < n_ff)
    def _():
        wg = wg_ref[0].astype(BF16)
        wu = wu_ref[0].astype(BF16)
        bg = bg_ref[0]
        bu = bu_ref[0]

        def up(i, carry):
            rows = pl.ds(pl.multiple_of(i * EXPERT_SUB, EXPERT_SUB), EXPERT_SUB)
            keep = (i * EXPERT_SUB + lax.broadcasted_iota(I32, (EXPERT_SUB, 1), 0)) < valid
            lo, hi = _unpack_rows(x_ref[rows, :])
            xs = jnp.concatenate([jnp.where(keep, lo, 0.0).astype(BF16), jnp.where(keep, hi, 0.0).astype(BF16)], axis=1)
            gate = jnp.minimum(_dot(xs, wg) + bg, SWIGLU_LIMIT)
            lin = jnp.clip(_dot(xs, wu) + bu, -SWIGLU_LIMIT, SWIGLU_LIMIT)
            act = (lin + 1.0) * (gate * (1.0 / (1.0 + jnp.exp(-SWIGLU_ALPHA * gate))))
            h_s[s, rows, :] = act.astype(BF16)
            return carry
        lax.fori_loop(0, n_sub, up, 0)

    @pl.when(s >= n_ff)
    def _():
        wd = wd_ref[0].astype(BF16)
        bd = bd_ref[0]

        def down(i, carry):
            rows = pl.ds(pl.multiple_of(i * EXPERT_SUB, EXPERT_SUB), EXPERT_SUB)
            h = jnp.concatenate([h_s[f, rows, :] for f in range(n_ff)], axis=1)
            y_ref[rows, :] = _dot(h, wd) + bd
            return carry
        lax.fori_loop(0, n_sub, down, 0)


def _experts(xb, block_expert, block_valid, block_index, w_gu, b_gu, w_d, b_d):
    n_blocks = xb.shape[0] // EXPERT_ROWS
    n_ff = D_FF // FF_TILE
    n_out = D_MODEL // OUT_TILE

    def ff(b, s, nv):
        return jnp.where(nv[b] > 0, jnp.minimum(s, n_ff - 1), n_ff - 1)

    def oc(b, s, nv):
        return jnp.where(nv[b] > 0, jnp.maximum(s - n_ff, 0), n_out - 1)

    b_gu3 = b_gu.reshape(N_EXPERTS, 1, 2 * D_FF)
    b_d3 = b_d.reshape(N_EXPERTS, 1, D_MODEL)
    grid_spec = pltpu.PrefetchScalarGridSpec(
        num_scalar_prefetch=3, grid=(n_blocks, n_ff + n_out),
        in_specs=[
            pl.BlockSpec((EXPERT_ROWS, D_MODEL // 2), lambda b, s, be, nv, xi: (xi[b], 0)),
            pl.BlockSpec((1, D_MODEL, FF_TILE), lambda b, s, be, nv, xi: (be[b], 0, ff(b, s, nv))),
            pl.BlockSpec((1, D_MODEL, FF_TILE), lambda b, s, be, nv, xi: (be[b], 0, ff(b, s, nv) + n_ff)),
            pl.BlockSpec((1, 1, FF_TILE), lambda b, s, be, nv, xi: (be[b], 0, ff(b, s, nv))),
            pl.BlockSpec((1, 1, FF_TILE), lambda b, s, be, nv, xi: (be[b], 0, ff(b, s, nv) + n_ff)),
            pl.BlockSpec((1, D_FF, OUT_TILE), lambda b, s, be, nv, xi: (be[b], 0, oc(b, s, nv))),
            pl.BlockSpec((1, 1, OUT_TILE), lambda b, s, be, nv, xi: (be[b], 0, oc(b, s, nv))),
        ],
        out_specs=pl.BlockSpec((EXPERT_ROWS, OUT_TILE), lambda b, s, be, nv, xi: (xi[b], oc(b, s, nv))),
        scratch_shapes=[pltpu.VMEM((n_ff, EXPERT_ROWS, FF_TILE), BF16)])
    return pl.pallas_call(
        _experts_kernel, grid_spec=grid_spec, name="experts",
        out_shape=jax.ShapeDtypeStruct((n_blocks * EXPERT_ROWS, D_MODEL), F32),
        compiler_params=_params(2),
    )(block_expert, block_valid, block_index, xb, w_gu, w_gu, b_gu3, b_gu3, w_d, b_d3)


def _combine_kernel(n_tiles, dest_ref, next_ref, x_ref, gate_ref, yb_ref, o_ref, buf, sem):
    tile = x_ref.shape[0]
    i = pl.program_id(0)
    slot = i % 2

    def copy(idx_ref, s, t, k):
        return pltpu.make_async_copy(yb_ref.at[pl.ds(idx_ref[0, t * TOP_K + k], 1), :],
                                     buf.at[s, k, pl.ds(t, 1), :], sem.at[s])

    def start_tile(idx_ref, s):
        def start(t, carry):
            for k in range(TOP_K):
                copy(idx_ref, s, t, k).start()
            return carry
        lax.fori_loop(0, tile, start, 0)

    pl.when(i == 0)(lambda: start_tile(dest_ref, 0))
    pl.when(i + 1 < n_tiles)(lambda: start_tile(next_ref, 1 - slot))

    def wait(t, carry):
        for k in range(TOP_K):
            copy(dest_ref, slot, t, k).wait()
        return carry
    lax.fori_loop(0, tile, wait, 0)

    gate = gate_ref[...]
    acc = x_ref[...]
    for k in range(TOP_K):
        acc = acc + gate[:, k:k + 1] * buf[slot, k]
    o_ref[...] = acc


def _combine(x, gate, dest, yb, tile):
    t = x.shape[0]
    n = t // tile
    dest = dest.reshape(n, 1, tile * TOP_K)
    return pl.pallas_call(
        functools.partial(_combine_kernel, n), grid=(n,), name="combine",
        in_specs=[pl.BlockSpec((None, 1, tile * TOP_K), lambda i: (i, 0, 0), memory_space=pltpu.SMEM),
                  pl.BlockSpec((None, 1, tile * TOP_K), lambda i: (jnp.minimum(i + 1, n - 1), 0, 0),
                               memory_space=pltpu.SMEM),
                  pl.BlockSpec((tile, D_MODEL), lambda i: (i, 0)),
                  pl.BlockSpec((tile, LANES), lambda i: (i, 0)),
                  pl.BlockSpec(memory_space=pl.ANY)],
        out_specs=pl.BlockSpec((tile, D_MODEL), lambda i: (i, 0)),
        out_shape=jax.ShapeDtypeStruct((t, D_MODEL), F32),
        scratch_shapes=[pltpu.VMEM((2, TOP_K, tile, D_MODEL), F32), pltpu.SemaphoreType.DMA((2,))],
        compiler_params=_params(),
    )(dest, dest, x, gate, yb)


def _rope_tables(pos):
    half = B_HEAD_DIM // 2
    inv_freq = ROPE_THETA ** (-jnp.arange(half, dtype=F32) / half)
    ang = pos.astype(F32)[:, None] * inv_freq[None, :]
    cos = jnp.cos(ang)
    sin = jnp.sin(ang)
    return jnp.tile(cos, (1, LANES // half)), jnp.tile(jnp.concatenate([-sin, sin], axis=1), (1, LANES // B_HEAD_DIM))


def _layer(l, xp, xs, cache_swa_k, cache_swa_v, cache_mem_k, cache_mem_v, mem_prompt,
           g_mix, w_in, g_sgu, w_s, b_s, g_q, g_k, sinks, g_out_a, g_out_b, w_out,
           g_mem_in, g_mem_src, w_mq, w_mk, w_mv, g_mq, g_mk, w_mo,
           g_moe, w_router, b_router, w_gate_up, b_gate_up, w_down, b_down):
    n_b, seq, _ = xp.shape
    n_db, dec_seq, _ = xs.shape
    tp = n_b * seq
    ts = n_db * dec_seq
    tile = min(TOKEN_TILE, seq)
    row = lambda a: a[l].reshape(1, -1)

    xp2 = xp.reshape(tp, D_MODEL)
    xs2 = xs.reshape(ts, D_MODEL)
    head = jnp.arange(QK_WIDTH, dtype=I32) // B_HEAD_DIM
    seg = (head[:, None] == jnp.arange(LANES, dtype=I32)[None, :]).astype(BF16)
    g_qk = jnp.concatenate([jnp.tile(g_q[l], B_HEADS), jnp.tile(g_k[l], B_KV_HEADS)]).reshape(1, -1)
    w_in_b = w_in[l].astype(BF16)
    cos_p, sin_p = _rope_tables(jnp.arange(seq, dtype=I32))
    cos_s, sin_s = _rope_tables(PAST_LEN + jnp.arange(SAMPLE_TILE, dtype=I32) % dec_seq)
    up, vap, qp, kp, vp = _in_proj(xp2, row(g_mix), w_in_b, row(g_sgu), g_qk, seg, seg.T, cos_p, sin_p,
                                   tile, "in_proj_prompt")
    us, vas, qs, ks, vs = _in_proj(xs2, row(g_mix), w_in_b, row(g_sgu), g_qk, seg, seg.T, cos_s, sin_s,
                                   SAMPLE_TILE, "in_proj_sample")

    w_out_b = w_out[l].astype(BF16)
    tri = jnp.tril(jnp.ones((MLP_CHUNK, MLP_CHUNK), bool))
    ws_p = jnp.where(tri[None], w_s[l], 0.0).astype(BF16)
    bias_p = jnp.repeat(b_s[l].T, A_HEAD_DIM, axis=1)
    x1p = _mixer_prompt(xp2, up, vap, qp, kp, vp, sinks[l], ws_p, bias_p, row(g_out_a), row(g_out_b), w_out_b, seq, tile)

    reps = SAMPLE_TILE // dec_seq
    tri_s = jnp.tril(jnp.ones((dec_seq, dec_seq), bool))
    ws_small = jnp.where(tri_s[None], w_s[l][:, :dec_seq, :dec_seq], 0.0)
    ws_s = jnp.einsum("ab,hts->hatbs", jnp.eye(reps, dtype=F32), ws_small).reshape(A_HEADS, SAMPLE_TILE, SAMPLE_TILE).astype(BF16)
    bias_s = jnp.tile(jnp.repeat(b_s[l][:, :dec_seq].T, A_HEAD_DIM, axis=1), (reps, 1))
    ck = cache_swa_k[l].reshape(n_db, -1, KV_WIDTH)
    cv = cache_swa_v[l].reshape(n_db, -1, KV_WIDTH)
    x1s = _mixer_sample(xs2, us, vas, qs, ks, vs, ck, cv, sinks[l], ws_s, bias_s,
                        row(g_out_a), row(g_out_b), w_out_b, dec_seq, SAMPLE_TILE)

    mk_p, mv_p = _mem_kv(mem_prompt.reshape(-1, D_MODEL), row(g_mem_src), w_mk[l].astype(BF16), w_mv[l].astype(BF16), row(g_mk))
    n_mem = mem_prompt.shape[1]
    w_r = jnp.pad(w_router[l], ((0, 0), (0, LANES - N_EXPERTS)))
    b_r = jnp.pad(b_router[l], (0, LANES - N_EXPERTS)).reshape(1, -1)
    w_r_hi = w_r.astype(BF16)
    w_r_lo = (w_r - w_r_hi.astype(F32)).astype(BF16)
    mem_args = (row(g_mem_in), w_mq[l].astype(BF16), row(g_mq), w_mo[l].astype(BF16), row(g_moe), w_r_hi, w_r_lo, b_r)
    x2p, hnp, idxp, gatep, rankp, cntp = _memory_router(
        x1p, mk_p.reshape(n_b, n_mem, MEM_WIDTH), mv_p.reshape(n_b, n_mem, MEM_WIDTH),
        seq, *mem_args, tile, "memory_router_prompt")
    x2s, hns, idxs, gates, ranks, cnts = _memory_router(
        x1s, cache_mem_k[l].reshape(n_db, -1, MEM_WIDTH), cache_mem_v[l].reshape(n_db, -1, MEM_WIDTH),
        dec_seq, *mem_args, SAMPLE_TILE, "memory_router_sample")

    cnt_p = cntp[0, :N_EXPERTS]
    cnt_s = cnts[0, :N_EXPERTS]
    total = cnt_p + cnt_s
    nblk = (total + EXPERT_ROWS - 1) // EXPERT_ROWS
    blk_end = jnp.cumsum(nblk)
    row_start = (blk_end - nblk) * EXPERT_ROWS
    n_blocks = (tp + ts) * TOP_K // EXPERT_ROWS + N_EXPERTS
    bidx = jnp.arange(n_blocks, dtype=I32)
    used = bidx < blk_end[-1]
    last = jnp.maximum(blk_end[-1] - 1, 0)
    bsafe = jnp.minimum(bidx, last)
    block_expert = jnp.minimum(jnp.searchsorted(blk_end, bsafe, side="right"), N_EXPERTS - 1).astype(I32)
    within = bsafe - (blk_end - nblk)[block_expert]
    block_valid = jnp.where(used, jnp.clip(total[block_expert] - within * EXPERT_ROWS, 0, EXPERT_ROWS), 0).astype(I32)
    dest_p = row_start[idxp[:, :TOP_K]] + rankp[:, :TOP_K]
    dest_s = (row_start + cnt_p)[idxs[:, :TOP_K]] + ranks[:, :TOP_K]

    xb = _dispatch(hnp, dest_p, None, n_blocks * EXPERT_ROWS, min(ROW_TILE, tp))
    xb = _dispatch(hns, dest_s, xb, n_blocks * EXPERT_ROWS, min(ROW_TILE, ts))
    yb = _experts(xb, block_expert, block_valid, bsafe.astype(I32), w_gate_up[l], b_gate_up[l], w_down[l], b_down[l])
    yp = _combine(x2p, gatep, dest_p, yb, min(ROW_TILE, tp))
    ys = _combine(x2s, gates, dest_s, yb, min(ROW_TILE, ts))

    new = dict(
        swa_k_p=kp.reshape(n_b, seq, B_KV_HEADS, B_HEAD_DIM)[:, seq - WINDOW:],
        swa_v_p=vp.reshape(n_b, seq, B_KV_HEADS, B_HEAD_DIM)[:, seq - WINDOW:],
        mem_k_p=mk_p.reshape(n_b, n_mem, MEM_HEADS, MEM_HEAD_DIM),
        mem_v_p=mv_p.reshape(n_b, n_mem, MEM_HEADS, MEM_HEAD_DIM),
        swa_k_s=ks.reshape(n_db, dec_seq, B_KV_HEADS, B_HEAD_DIM),
        swa_v_s=vs.reshape(n_db, dec_seq, B_KV_HEADS, B_HEAD_DIM),
        sgu_v_s=vas.reshape(n_db, dec_seq, A_HEADS, A_HEAD_DIM))
    return yp.reshape(n_b, seq, D_MODEL), ys.reshape(n_db, dec_seq, D_MODEL), new


def kernel(x_prompt, x_sample, cache_swa_k, cache_swa_v, cache_mem_k, cache_mem_v, mem_prompt, g_mix, w_in, g_sgu, w_s, b_s, g_q, g_k, sinks, g_out_a, g_out_b, w_out, g_mem_in, g_mem_src, w_mq, w_mk, w_mv, g_mq, g_mk, w_mo, g_moe, w_router, b_router, w_gate_up, b_gate_up, w_down, b_down):
    xp, xs = x_prompt, x_sample
    news = []
    for l in range(g_mix.shape[0]):
        xp, xs, new = _layer(l, xp, xs, cache_swa_k, cache_swa_v, cache_mem_k, cache_mem_v, mem_prompt,
                             g_mix, w_in, g_sgu, w_s, b_s, g_q, g_k, sinks, g_out_a, g_out_b, w_out,
                             g_mem_in, g_mem_src, w_mq, w_mk, w_mv, g_mq, g_mk, w_mo,
                             g_moe, w_router, b_router, w_gate_up, b_gate_up, w_down, b_down)
        news.append(new)
    stack = lambda name: jnp.stack([n[name] for n in news], 0)
    return (xp, xs, stack("swa_k_p"), stack("swa_v_p"), stack("mem_k_p"), stack("mem_v_p"),
            stack("swa_k_s"), stack("swa_v_s"), stack("sgu_v_s"))
```

```python
import functools

import jax
import jax.numpy as jnp
from jax import lax
from jax.experimental import pallas as pl
from jax.experimental.pallas import tpu as pltpu

F32 = jnp.float32
BF16 = jnp.bfloat16
I32 = jnp.int32
U32 = jnp.uint32

D_MODEL = 2048
PAST_LEN = 2048
CHUNK = 64
EPS = 1e-6
A_HEADS = 16
A_HEAD_DIM = 64
A_WIDTH = A_HEADS * A_HEAD_DIM
MLP_CHUNK = 128
B_HEADS = 16
B_KV_HEADS = 2
B_HEAD_DIM = 64
B_WIDTH = B_HEADS * B_HEAD_DIM
KV_WIDTH = B_KV_HEADS * B_HEAD_DIM
WINDOW = 128
ROPE_THETA = 10000.0
ATTN_SCALE = B_HEAD_DIM ** -0.5
Q_OFF = 2 * A_WIDTH
K_OFF = Q_OFF + B_WIDTH
V_OFF = K_OFF + KV_WIDTH
IN_WIDTH = V_OFF + KV_WIDTH
QK_WIDTH = B_WIDTH + KV_WIDTH
MEM_HEADS = 4
MEM_HEAD_DIM = 128
MEM_WIDTH = MEM_HEADS * MEM_HEAD_DIM
MEM_SCALE = MEM_HEAD_DIM ** -0.5
N_EXPERTS = 32
TOP_K = 4
D_FF = D_MODEL
SWIGLU_LIMIT = 7.0
SWIGLU_ALPHA = 1.702
NEG_BIG = -1e30

LANES = 128
VMEM_LIMIT = 56 * 1024 * 1024

TOKEN_TILE = 512
SAMPLE_TILE = 128
EXPERT_ROWS = 1280
EXPERT_SUB = 256
FF_TILE = 512
OUT_TILE = 512
ROW_TILE = 256


def _dot(a, b):
    return jnp.dot(a, b, preferred_element_type=F32)


def _dot_nt(a, b):
    return lax.dot_general(a, b, (((1,), (1,)), ((), ())), preferred_element_type=F32)


def _rms(x):
    return x * lax.rsqrt(jnp.mean(x * x, axis=-1, keepdims=True) + EPS)


def _gelu(x):
    return 0.5 * x * (1.0 + lax.erf(x * 0.7071067811865476))


def _pack_rows(x):
    c = x.shape[1] // 2
    bits = pltpu.bitcast(x.astype(BF16).astype(F32), U32)
    return (bits[:, :c] >> 16) | (bits[:, c:] & jnp.uint32(0xFFFF0000))


def _unpack_rows(w):
    return pltpu.bitcast(w << 16, F32), pltpu.bitcast(w & jnp.uint32(0xFFFF0000), F32)


def _params(n_axes=1):
    return pltpu.CompilerParams(dimension_semantics=("arbitrary",) * n_axes,
                                vmem_limit_bytes=VMEM_LIMIT)


def _resident(shape):
    nd = len(shape)
    return pl.BlockSpec(shape, lambda *_: (0,) * nd, pipeline_mode=pl.Buffered(1))


def _mem_kv_kernel(mem_ref, gsrc_ref, wk_ref, wv_ref, gk_ref, k_ref, v_ref):
    m = (_rms(mem_ref[...]) * gsrc_ref[...]).astype(BF16)
    kz = _dot(m, wk_ref[...])
    gk = gk_ref[...]
    for h in range(MEM_HEADS):
        sl = slice(h * MEM_HEAD_DIM, (h + 1) * MEM_HEAD_DIM)
        k_ref[:, sl] = _rms(kz[:, sl]) * gk
    v_ref[...] = _dot(m, wv_ref[...])


def _mem_kv(mem, g_src, w_mk, w_mv, g_mk):
    n = mem.shape[0]
    out = jax.ShapeDtypeStruct((n, MEM_WIDTH), F32)
    return pl.pallas_call(
        _mem_kv_kernel, out_shape=(out, out), name="mem_kv",
        compiler_params=pltpu.CompilerParams(vmem_limit_bytes=VMEM_LIMIT),
    )(mem, g_src, w_mk, w_mv, g_mk)


def _in_proj_kernel(x_ref, gmix_ref, w_ref, gsgu_ref, gqk_ref, seg_ref, segt_ref, cos_ref, sin_ref,
                    u_ref, va_ref, q_ref, k_ref, v_ref):
    xn = (_rms(x_ref[...]) * gmix_ref[...]).astype(BF16)
    u_ref[...] = _gelu(_dot(xn, w_ref[:, 0:A_WIDTH])).astype(BF16)
    va_ref[...] = _rms(_gelu(_dot(xn, w_ref[:, A_WIDTH:Q_OFF]))) * gsgu_ref[...]
    v_ref[...] = _dot(xn, w_ref[:, V_OFF:IN_WIDTH])

    qk = _dot(xn, w_ref[:, Q_OFF:V_OFF])
    ssq = _dot((qk * qk).astype(BF16), seg_ref[...])
    inv = lax.rsqrt(ssq * (1.0 / B_HEAD_DIM) + EPS)
    inv_hi = inv.astype(BF16)
    inv_lo = (inv - inv_hi.astype(F32)).astype(BF16)
    inv_b = _dot(inv_hi, segt_ref[...]) + _dot(inv_lo, segt_ref[...])
    qkn = (qk * inv_b) * gqk_ref[...]

    cos = cos_ref[...]
    sin = sin_ref[...]
    lane = lax.broadcasted_iota(I32, cos.shape, 1)
    first_half = (lane % B_HEAD_DIM) < (B_HEAD_DIM // 2)
    for g in range(QK_WIDTH // LANES):
        xg = qkn[:, g * LANES:(g + 1) * LANES]
        rot = jnp.where(first_half, pltpu.roll(xg, LANES - B_HEAD_DIM // 2, 1),
                        pltpu.roll(xg, B_HEAD_DIM // 2, 1))
        r = xg * cos + rot * sin
        if g < B_WIDTH // LANES:
            q_ref[:, g * LANES:(g + 1) * LANES] = (r * ATTN_SCALE).astype(BF16)
        else:
            k_ref[...] = r


def _in_proj(x, g_mix, w_in, g_sgu, g_qk, seg, segt, cos_tab, sin_tab, tile, name):
    t = x.shape[0]
    n = t // tile
    tab_tiles = cos_tab.shape[0] // tile
    row = lambda w: pl.BlockSpec((tile, w), lambda i: (i, 0))
    tab = pl.BlockSpec((tile, LANES), lambda i: (i % tab_tiles, 0))
    return pl.pallas_call(
        _in_proj_kernel, grid=(n,), name=name,
        in_specs=[row(D_MODEL), _resident(g_mix.shape), _resident(w_in.shape), _resident(g_sgu.shape),
                  _resident(g_qk.shape), _resident(seg.shape), _resident(segt.shape), tab, tab],
        out_specs=[row(A_WIDTH), row(A_WIDTH), row(B_WIDTH), row(KV_WIDTH), row(KV_WIDTH)],
        out_shape=(jax.ShapeDtypeStruct((t, A_WIDTH), BF16), jax.ShapeDtypeStruct((t, A_WIDTH), F32),
                   jax.ShapeDtypeStruct((t, B_WIDTH), BF16), jax.ShapeDtypeStruct((t, KV_WIDTH), F32),
                   jax.ShapeDtypeStruct((t, KV_WIDTH), F32)),
        compiler_params=_params(),
    )(x, g_mix, w_in, g_sgu, g_qk, seg, segt, cos_tab, sin_tab)


def _split_heads(ref_dst_lo, ref_dst_hi, rows, x, swap):
    lane = lax.broadcasted_iota(I32, x.shape, 1)
    low = lane < B_HEAD_DIM
    zero = jnp.zeros_like(x)
    ref_dst_lo[0, rows, :] = jnp.where(low, x, zero).astype(BF16)
    ref_dst_hi[0, rows, :] = jnp.where(low, zero, swap).astype(BF16)
    ref_dst_lo[1, rows, :] = jnp.where(low, swap, zero).astype(BF16)
    ref_dst_hi[1, rows, :] = jnp.where(low, zero, x).astype(BF16)


def _sgu_chunk(u, va, ws_ref, bias):
    lane = lax.broadcasted_iota(I32, (MLP_CHUNK, LANES), 1)
    low = lane < A_HEAD_DIM
    outs = []
    for p in range(A_WIDTH // LANES):
        sl = slice(p * LANES, (p + 1) * LANES)
        v2 = va[:, sl]
        zero = jnp.zeros_like(v2)
        mixed = (_dot(ws_ref[2 * p], jnp.where(low, v2, zero).astype(BF16))
                 + _dot(ws_ref[2 * p + 1], jnp.where(low, zero, v2).astype(BF16)))
        outs.append(u[:, sl].astype(F32) * (mixed + bias[:, sl]))
    return jnp.concatenate(outs, axis=1)


def _attend(q2, kl, kh, vl, vh, mask, sink_even, sink_odd):
    outs = []
    for kk, vv, sink in ((kl, vl, sink_even), (kh, vh, sink_odd)):
        s = _dot_nt(q2, kk)
        if mask is not None:
            s = jnp.where(mask, s, NEG_BIG)
        m = jnp.maximum(jnp.max(s, axis=-1, keepdims=True), sink)
        p = jnp.exp(s - m)
        den = jnp.sum(p, axis=-1, keepdims=True) + jnp.exp(sink - m)
        outs.append(_dot(p.astype(BF16), vv) * (1.0 / den))
    return outs[0] + outs[1]


def _out_proj(oa, ob, ga_ref, gb_ref, wout_ref, x):
    a = (_rms(oa) * ga_ref[...]).astype(BF16)
    b = (_rms(ob) * gb_ref[...]).astype(BF16)
    return x + _dot(a, wout_ref[0:A_WIDTH, :]) + _dot(b, wout_ref[A_WIDTH:, :])


def _mixer_prompt_kernel(tiles_per_seq, sinks_ref, x_ref, u_ref, va_ref, q_ref, kc_ref, vc_ref, kp_ref, vp_ref,
                         ws_ref, bias_ref, ga_ref, gb_ref, wout_ref, o_ref,
                         kl_s, kh_s, vl_s, vh_s, oa_s, ob_s):
    tile = x_ref.shape[0]
    n_sub = tile // MLP_CHUNK
    seq_start = (pl.program_id(0) % tiles_per_seq) == 0

    for src_p, src_c, dl, dh in ((kp_ref, kc_ref, kl_s, kh_s), (vp_ref, vc_ref, vl_s, vh_s)):
        prev = src_p[...]
        cur = src_c[...]
        _split_heads(dl, dh, slice(0, WINDOW), prev, pltpu.roll(prev, B_HEAD_DIM, 1))
        _split_heads(dl, dh, slice(WINDOW, WINDOW + tile), cur, pltpu.roll(cur, B_HEAD_DIM, 1))

    qc = lax.broadcasted_iota(I32, (MLP_CHUNK, 2 * MLP_CHUNK), 0) // CHUNK
    kc = lax.broadcasted_iota(I32, (MLP_CHUNK, 2 * MLP_CHUNK), 1) // CHUNK
    window_mask = (kc >= qc) & (kc <= qc + 2)
    bias = bias_ref[...]

    def sub(j, carry):
        r0 = pl.multiple_of(j * MLP_CHUNK, MLP_CHUNK)
        rows = pl.ds(r0, MLP_CHUNK)
        keys = pl.ds(r0, 2 * MLP_CHUNK)
        first_key_chunk = jnp.where(jnp.logical_and(seq_start, j == 0), 2, 0)
        mask = window_mask & (kc >= first_key_chunk)
        oa_s[rows, :] = _sgu_chunk(u_ref[rows, :], va_ref[rows, :], ws_ref, bias)
        for p in range(B_WIDTH // LANES):
            h = p // (B_WIDTH // LANES // B_KV_HEADS)
            sl = slice(p * LANES, (p + 1) * LANES)
            ob_s[rows, sl] = _attend(q_ref[rows, sl], kl_s[h, keys, :], kh_s[h, keys, :],
                                     vl_s[h, keys, :], vh_s[h, keys, :], mask,
                                     sinks_ref[2 * p], sinks_ref[2 * p + 1])
        return carry

    lax.fori_loop(0, n_sub, sub, 0)
    o_ref[...] = _out_proj(oa_s[...], ob_s[...], ga_ref, gb_ref, wout_ref, x_ref[...])


def _mixer_prompt(x, u, va, q, k, v, sinks, ws, bias, g_a, g_b, w_out, seq, tile):
    t = x.shape[0]
    n = t // tile
    tiles_per_seq = seq // tile
    per = tile // WINDOW
    row = lambda w: pl.BlockSpec((tile, w), lambda i, s: (i, 0))
    prev = pl.BlockSpec((WINDOW, KV_WIDTH), lambda i, s: (jnp.maximum(i * per - 1, 0), 0))
    res = lambda a: pl.BlockSpec(a.shape, lambda i, s: (0,) * a.ndim, pipeline_mode=pl.Buffered(1))
    grid_spec = pltpu.PrefetchScalarGridSpec(
        num_scalar_prefetch=1, grid=(n,),
        in_specs=[row(D_MODEL), row(A_WIDTH), row(A_WIDTH), row(B_WIDTH), row(KV_WIDTH), row(KV_WIDTH), prev, prev,
                  res(ws), res(bias), res(g_a), res(g_b), res(w_out)],
        out_specs=row(D_MODEL),
        scratch_shapes=[pltpu.VMEM((B_KV_HEADS, WINDOW + tile, LANES), BF16) for _ in range(4)]
        + [pltpu.VMEM((tile, A_WIDTH), F32), pltpu.VMEM((tile, B_WIDTH), F32)])
    return pl.pallas_call(
        functools.partial(_mixer_prompt_kernel, tiles_per_seq), grid_spec=grid_spec, name="mixer_prompt",
        out_shape=jax.ShapeDtypeStruct((t, D_MODEL), F32), compiler_params=_params(),
    )(sinks, x, u, va, q, k, v, k, v, ws, bias, g_a, g_b, w_out)


def _mixer_sample_kernel(dec_seq, sinks_ref, x_ref, u_ref, va_ref, q_ref, kc_ref, vc_ref, ck_ref, cv_ref,
                         ws_ref, bias_ref, ga_ref, gb_ref, wout_ref, o_ref,
                         kl_s, kh_s, vl_s, vh_s, ob_s):
    tile = x_ref.shape[0]
    n_seq = tile // dec_seq
    n_cache = ck_ref.shape[1]
    n_keys = n_cache + dec_seq
    oa = _sgu_chunk(u_ref[...], va_ref[...], ws_ref, bias_ref[...])
    for b in range(n_seq):
        rows = slice(b * dec_seq, (b + 1) * dec_seq)
        for src_c, src_n, dl, dh in ((ck_ref, kc_ref, kl_s, kh_s), (cv_ref, vc_ref, vl_s, vh_s)):
            old = src_c[b]
            new = src_n[rows, :]
            _split_heads(dl, dh, slice(0, n_cache), old, pltpu.roll(old, B_HEAD_DIM, 1))
            _split_heads(dl, dh, slice(n_cache, n_keys), new, pltpu.roll(new, B_HEAD_DIM, 1))
        for p in range(B_WIDTH // LANES):
            h = p // (B_WIDTH // LANES // B_KV_HEADS)
            sl = slice(p * LANES, (p + 1) * LANES)
            ob_s[rows, sl] = _attend(q_ref[rows, sl], kl_s[h], kh_s[h], vl_s[h], vh_s[h], None,
                                     sinks_ref[2 * p], sinks_ref[2 * p + 1])
    o_ref[...] = _out_proj(oa, ob_s[...], ga_ref, gb_ref, wout_ref, x_ref[...])


def _mixer_sample(x, u, va, q, k, v, cache_k, cache_v, sinks, ws, bias, g_a, g_b, w_out, dec_seq, tile):
    t = x.shape[0]
    n = t // tile
    n_seq = tile // dec_seq
    n_cache = cache_k.shape[1]
    row = lambda w: pl.BlockSpec((tile, w), lambda i, s: (i, 0))
    cache = pl.BlockSpec((n_seq, n_cache, KV_WIDTH), lambda i, s: (i, 0, 0))
    res = lambda a: pl.BlockSpec(a.shape, lambda i, s: (0,) * a.ndim, pipeline_mode=pl.Buffered(1))
    grid_spec = pltpu.PrefetchScalarGridSpec(
        num_scalar_prefetch=1, grid=(n,),
        in_specs=[row(D_MODEL), row(A_WIDTH), row(A_WIDTH), row(B_WIDTH), row(KV_WIDTH), row(KV_WIDTH), cache, cache,
                  res(ws), res(bias), res(g_a), res(g_b), res(w_out)],
        out_specs=row(D_MODEL),
        scratch_shapes=[pltpu.VMEM((B_KV_HEADS, n_cache + dec_seq, LANES), BF16) for _ in range(4)]
        + [pltpu.VMEM((tile, B_WIDTH), F32)])
    return pl.pallas_call(
        functools.partial(_mixer_sample_kernel, dec_seq), grid_spec=grid_spec, name="mixer_sample",
        out_shape=jax.ShapeDtypeStruct((t, D_MODEL), F32), compiler_params=_params(),
    )(sinks, x, u, va, q, k, v, cache_k, cache_v, ws, bias, g_a, g_b, w_out)


def _memory_router_kernel(rows_per_mem, x_ref, mk_ref, mv_ref, gin_ref, wq_ref, gq_ref, wo_ref,
                          gmoe_ref, wrh_ref, wrl_ref, br_ref,
                          x2_ref, hn_ref, idx_ref, gate_ref, rank_ref, cnt_ref, o_s, base_s):
    tile = x_ref.shape[0]
    x = x_ref[...]
    qz = _dot((_rms(x) * gin_ref[...]).astype(BF16), wq_ref[...])
    gq = gq_ref[...]
    for h in range(MEM_HEADS):
        sl = slice(h * MEM_HEAD_DIM, (h + 1) * MEM_HEAD_DIM)
        qh = (_rms(qz[:, sl]) * gq).astype(BF16)
        for r in range(tile // rows_per_mem):
            rows = slice(r * rows_per_mem, (r + 1) * rows_per_mem)
            s = _dot_nt(qh[rows], mk_ref[r, :, sl].astype(BF16)) * MEM_SCALE
            p = jnp.exp(s - jnp.max(s, axis=-1, keepdims=True))
            den = jnp.sum(p, axis=-1, keepdims=True)
            o_s[rows, sl] = _dot(p.astype(BF16), mv_ref[r, :, sl].astype(BF16)) * (1.0 / den)
    x2 = x + _dot(o_s[...].astype(BF16), wo_ref[...])
    x2_ref[...] = x2
    hn = _rms(x2) * gmoe_ref[...]
    hn_ref[...] = _pack_rows(hn)

    hn_hi = hn.astype(BF16)
    hn_lo = (hn - hn_hi.astype(F32)).astype(BF16)
    logits = ((_dot(hn_lo, wrh_ref[...]) + _dot(hn_hi, wrl_ref[...])) + _dot(hn_hi, wrh_ref[...])) + br_ref[...]
    lane = lax.broadcasted_iota(I32, (tile, LANES), 1)
    lane_f = lane.astype(F32)
    work = jnp.where(lane < N_EXPERTS, logits, -jnp.inf)
    idx_out = jnp.zeros((tile, LANES), F32)
    val_out = jnp.zeros((tile, LANES), F32)
    hot = jnp.zeros((tile, LANES), F32)
    top = None
    picks = []
    for k in range(TOP_K):
        m = jnp.max(work, axis=-1, keepdims=True)
        pick = jnp.min(jnp.where(work == m, lane_f, float(LANES)), axis=-1, keepdims=True)
        chosen = lane_f == pick
        if top is None:
            top = m
        idx_out = jnp.where(lane == k, pick, idx_out)
        val_out = jnp.where(lane == k, jnp.exp(m - top), val_out)
        hot = jnp.where(chosen, 1.0, hot)
        work = jnp.where(chosen, -jnp.inf, work)
        picks.append(chosen)
    idx_ref[...] = idx_out.astype(I32)
    gate_ref[...] = val_out * (1.0 / jnp.sum(val_out, axis=-1, keepdims=True))

    @pl.when(pl.program_id(0) == 0)
    def _():
        base_s[...] = jnp.zeros_like(base_s)

    earlier = (lax.broadcasted_iota(I32, (tile, tile), 1) < lax.broadcasted_iota(I32, (tile, tile), 0))
    pos = _dot(jnp.where(earlier, 1.0, 0.0).astype(BF16), hot.astype(BF16)) + base_s[0:1, :]
    rank_out = jnp.zeros((tile, LANES), F32)
    for k in range(TOP_K):
        rank_out = jnp.where(lane == k, jnp.sum(jnp.where(picks[k], pos, 0.0), axis=-1, keepdims=True), rank_out)
    rank_ref[...] = rank_out.astype(I32)
    total = base_s[0:1, :] + jnp.sum(hot, axis=0, keepdims=True)
    base_s[...] = jnp.broadcast_to(total, base_s.shape)
    cnt_ref[...] = jnp.broadcast_to(total, cnt_ref.shape).astype(I32)


def _memory_router(x, mk, mv, rows_per_mem, g_in, w_mq, g_mq, w_mo, g_moe, w_r_hi, w_r_lo, b_r, tile, name):
    t = x.shape[0]
    n = t // tile
    mems = tile // rows_per_mem if rows_per_mem <= tile else 1
    per_mem_tiles = max(rows_per_mem // tile, 1)
    rpm = min(rows_per_mem, tile)
    row = lambda w: pl.BlockSpec((tile, w), lambda i: (i, 0))
    mem = pl.BlockSpec((mems, mk.shape[1], MEM_WIDTH), lambda i: (i // per_mem_tiles, 0, 0))
    res = lambda a: pl.BlockSpec(a.shape, lambda i: (0,) * a.ndim, pipeline_mode=pl.Buffered(1))
    small = jax.ShapeDtypeStruct((t, LANES), I32)
    return pl.pallas_call(
        functools.partial(_memory_router_kernel, rpm), grid=(n,), name=name,
        in_specs=[row(D_MODEL), mem, mem, res(g_in), res(w_mq), res(g_mq), res(w_mo), res(g_moe), res(w_r_hi), res(w_r_lo), res(b_r)],
        out_specs=[row(D_MODEL), row(D_MODEL // 2), row(LANES), row(LANES), row(LANES),
                   pl.BlockSpec((8, LANES), lambda i: (0, 0))],
        out_shape=(jax.ShapeDtypeStruct((t, D_MODEL), F32), jax.ShapeDtypeStruct((t, D_MODEL // 2), U32),
                   small, jax.ShapeDtypeStruct((t, LANES), F32), small, jax.ShapeDtypeStruct((8, LANES), I32)),
        scratch_shapes=[pltpu.VMEM((tile, MEM_WIDTH), F32), pltpu.VMEM((8, LANES), F32)],
        compiler_params=_params(),
    )(x, mk, mv, g_in, w_mq, g_mq, w_mo, g_moe, w_r_hi, w_r_lo, b_r)


def _dispatch_kernel(dest_ref, hn_ref, *rest):
    xb_ref, sem = rest[-2:]
    tile = hn_ref.shape[0]

    def copy(t, k):
        return pltpu.make_async_copy(hn_ref.at[pl.ds(t, 1), :],
                                     xb_ref.at[pl.ds(dest_ref[0, t * TOP_K + k], 1), :], sem)

    def start(t, carry):
        for k in range(TOP_K):
            copy(t, k).start()
        return carry

    def wait(t, carry):
        for k in range(TOP_K):
            copy(t, k).wait()
        return carry

    lax.fori_loop(0, tile, start, 0)
    lax.fori_loop(0, tile, wait, 0)


def _dispatch(hn, dest, xb, n_rows, tile):
    t = hn.shape[0]
    n = t // tile
    dest = dest.reshape(n, 1, tile * TOP_K)
    in_specs = [pl.BlockSpec((None, 1, tile * TOP_K), lambda i: (i, 0, 0), memory_space=pltpu.SMEM),
                pl.BlockSpec((tile, hn.shape[1]), lambda i: (i, 0))]
    args = [dest, hn]
    if xb is not None:
        in_specs.append(pl.BlockSpec(memory_space=pl.ANY))
        args.append(xb)
    return pl.pallas_call(
        _dispatch_kernel, grid=(n,), name="dispatch",
        in_specs=in_specs,
        out_specs=pl.BlockSpec(memory_space=pl.ANY),
        out_shape=jax.ShapeDtypeStruct((n_rows, hn.shape[1]), hn.dtype),
        scratch_shapes=[pltpu.SemaphoreType.DMA(())],
        input_output_aliases={} if xb is None else {2: 0},
        compiler_params=_params(),
    )(*args)


def _for_pairs(n, body):
    def pair(j, carry):
        body(2 * j)
        body(2 * j + 1)
        return carry
    lax.fori_loop(0, n // 2, pair, 0)
    pl.when(n % 2 == 1)(lambda: body(n - 1))


def _experts_kernel(be_ref, nv_ref, xi_ref, x_ref, wg_ref, wu_ref, bg_ref, bu_ref, wd_ref, bd_ref, y_ref, h_s):
    del be_ref, xi_ref
    b = pl.program_id(0)
    s = pl.program_id(1)
    n_ff = D_FF // FF_TILE
    valid = nv_ref[b]
    n_sub = (valid + EXPERT_SUB - 1) // EXPERT_SUB

    @pl.when(s < n_ff)
    def _():
        wg = wg_ref[0].astype(BF16)
        wu = wu_ref[0].astype(BF16)
        bg = bg_ref[0]
        bu = bu_ref[0]

        def up(i):
            rows = pl.ds(pl.multiple_of(i * EXPERT_SUB, EXPERT_SUB), EXPERT_SUB)
            keep = (i * EXPERT_SUB + lax.broadcasted_iota(I32, (EXPERT_SUB, 1), 0)) < valid
            lo, hi = _unpack_rows(x_ref[rows, :])
            xs = jnp.concatenate([jnp.where(keep, lo, 0.0).astype(BF16), jnp.where(keep, hi, 0.0).astype(BF16)], axis=1)
            gate = jnp.minimum(_dot(xs, wg) + bg, SWIGLU_LIMIT)
            lin = jnp.clip(_dot(xs, wu) + bu, -SWIGLU_LIMIT, SWIGLU_LIMIT)
            act = (lin + 1.0) * (gate * (1.0 / (1.0 + jnp.exp(-SWIGLU_ALPHA * gate))))
            h_s[s, rows, :] = act.astype(BF16)
        _for_pairs(n_sub, up)

    @pl.when(s >= n_ff)
    def _():
        wd = wd_ref[0].astype(BF16)
        bd = bd_ref[0]

        def down(i):
            rows = pl.ds(pl.multiple_of(i * EXPERT_SUB, EXPERT_SUB), EXPERT_SUB)
            h = jnp.concatenate([h_s[f, rows, :] for f in range(n_ff)], axis=1)
            y_ref[rows, :] = _pack_rows(_dot(h, wd) + bd)
        _for_pairs(n_sub, down)


def _experts(xb, block_expert, block_valid, block_index, w_gu, b_gu, w_d, b_d):
    n_blocks = xb.shape[0] // EXPERT_ROWS
    n_ff = D_FF // FF_TILE
    n_out = D_MODEL // OUT_TILE

    def ff(b, s, nv):
        return jnp.where(nv[b] > 0, jnp.minimum(s, n_ff - 1), n_ff - 1)

    def oc(b, s, nv):
        return jnp.where(nv[b] > 0, jnp.maximum(s - n_ff, 0), n_out - 1)

    b_gu3 = b_gu.reshape(N_EXPERTS, 1, 2 * D_FF)
    b_d3 = b_d.reshape(N_EXPERTS, 1, D_MODEL)
    grid_spec = pltpu.PrefetchScalarGridSpec(
        num_scalar_prefetch=3, grid=(n_blocks, n_ff + n_out),
        in_specs=[
            pl.BlockSpec((EXPERT_ROWS, D_MODEL // 2), lambda b, s, be, nv, xi: (xi[b], 0)),
            pl.BlockSpec((1, D_MODEL, FF_TILE), lambda b, s, be, nv, xi: (be[b], 0, ff(b, s, nv))),
            pl.BlockSpec((1, D_MODEL, FF_TILE), lambda b, s, be, nv, xi: (be[b], 0, ff(b, s, nv) + n_ff)),
            pl.BlockSpec((1, 1, FF_TILE), lambda b, s, be, nv, xi: (be[b], 0, ff(b, s, nv))),
            pl.BlockSpec((1, 1, FF_TILE), lambda b, s, be, nv, xi: (be[b], 0, ff(b, s, nv) + n_ff)),
            pl.BlockSpec((1, D_FF, OUT_TILE), lambda b, s, be, nv, xi: (be[b], 0, oc(b, s, nv))),
            pl.BlockSpec((1, 1, OUT_TILE), lambda b, s, be, nv, xi: (be[b], 0, oc(b, s, nv))),
        ],
        out_specs=pl.BlockSpec((EXPERT_ROWS, OUT_TILE // 2), lambda b, s, be, nv, xi: (xi[b], oc(b, s, nv))),
        scratch_shapes=[pltpu.VMEM((n_ff, EXPERT_ROWS, FF_TILE), BF16)])
    return pl.pallas_call(
        _experts_kernel, grid_spec=grid_spec, name="experts",
        out_shape=jax.ShapeDtypeStruct((n_blocks * EXPERT_ROWS, D_MODEL // 2), U32),
        compiler_params=_params(2),
    )(block_expert, block_valid, block_index, xb, w_gu, w_gu, b_gu3, b_gu3, w_d, b_d3)


def _combine_kernel(n_tiles, dest_ref, next_ref, x_ref, gate_ref, yb_ref, o_ref, buf, sem):
    tile = x_ref.shape[0]
    i = pl.program_id(0)
    slot = i % 2

    def copy(idx_ref, s, t, k):
        return pltpu.make_async_copy(yb_ref.at[pl.ds(idx_ref[0, t * TOP_K + k], 1), :],
                                     buf.at[s, k, pl.ds(t, 1), :], sem.at[s])

    def start_tile(idx_ref, s):
        def start(t, carry):
            for k in range(TOP_K):
                copy(idx_ref, s, t, k).start()
            return carry
        lax.fori_loop(0, tile, start, 0)

    pl.when(i == 0)(lambda: start_tile(dest_ref, 0))
    pl.when(i + 1 < n_tiles)(lambda: start_tile(next_ref, 1 - slot))

    def wait(t, carry):
        for k in range(TOP_K):
            copy(dest_ref, slot, t, k).wait()
        return carry
    lax.fori_loop(0, tile, wait, 0)

    gate = gate_ref[...]
    half = OUT_TILE // 2
    for g in range(D_MODEL // OUT_TILE):
        c0 = g * OUT_TILE
        acc_lo = x_ref[:, c0:c0 + half]
        acc_hi = x_ref[:, c0 + half:c0 + OUT_TILE]
        for k in range(TOP_K):
            lo, hi = _unpack_rows(buf[slot, k, :, g * half:(g + 1) * half])
            acc_lo = acc_lo + gate[:, k:k + 1] * lo
            acc_hi = acc_hi + gate[:, k:k + 1] * hi
        o_ref[:, c0:c0 + half] = acc_lo
        o_ref[:, c0 + half:c0 + OUT_TILE] = acc_hi


def _combine(x, gate, dest, yb, tile):
    t = x.shape[0]
    n = t // tile
    dest = dest.reshape(n, 1, tile * TOP_K)
    return pl.pallas_call(
        functools.partial(_combine_kernel, n), grid=(n,), name="combine",
        in_specs=[pl.BlockSpec((None, 1, tile * TOP_K), lambda i: (i, 0, 0), memory_space=pltpu.SMEM),
                  pl.BlockSpec((None, 1, tile * TOP_K), lambda i: (jnp.minimum(i + 1, n - 1), 0, 0),
                               memory_space=pltpu.SMEM),
                  pl.BlockSpec((tile, D_MODEL), lambda i: (i, 0)),
                  pl.BlockSpec((tile, LANES), lambda i: (i, 0)),
                  pl.BlockSpec(memory_space=pl.ANY)],
        out_specs=pl.BlockSpec((tile, D_MODEL), lambda i: (i, 0)),
        out_shape=jax.ShapeDtypeStruct((t, D_MODEL), F32),
        scratch_shapes=[pltpu.VMEM((2, TOP_K, tile, D_MODEL // 2), U32), pltpu.SemaphoreType.DMA((2,))],
        compiler_params=_params(),
    )(dest, dest, x, gate, yb)


def _rope_tables(pos):
    half = B_HEAD_DIM // 2
    inv_freq = ROPE_THETA ** (-jnp.arange(half, dtype=F32) / half)
    ang = pos.astype(F32)[:, None] * inv_freq[None, :]
    cos = jnp.cos(ang)
    sin = jnp.sin(ang)
    return jnp.tile(cos, (1, LANES // half)), jnp.tile(jnp.concatenate([-sin, sin], axis=1), (1, LANES // B_HEAD_DIM))


def _layer(l, xp, xs, cache_swa_k, cache_swa_v, cache_mem_k, cache_mem_v, mem_prompt,
           g_mix, w_in, g_sgu, w_s, b_s, g_q, g_k, sinks, g_out_a, g_out_b, w_out,
           g_mem_in, g_mem_src, w_mq, w_mk, w_mv, g_mq, g_mk, w_mo,
           g_moe, w_router, b_router, w_gate_up, b_gate_up, w_down, b_down):
    n_b, seq, _ = xp.shape
    n_db, dec_seq, _ = xs.shape
    tp = n_b * seq
    ts = n_db * dec_seq
    tile = min(TOKEN_TILE, seq)
    row = lambda a: a[l].reshape(1, -1)

    xp2 = xp.reshape(tp, D_MODEL)
    xs2 = xs.reshape(ts, D_MODEL)
    head = jnp.arange(QK_WIDTH, dtype=I32) // B_HEAD_DIM
    seg = (head[:, None] == jnp.arange(LANES, dtype=I32)[None, :]).astype(BF16)
    g_qk = jnp.concatenate([jnp.tile(g_q[l], B_HEADS), jnp.tile(g_k[l], B_KV_HEADS)]).reshape(1, -1)
    w_in_b = w_in[l].astype(BF16)
    cos_p, sin_p = _rope_tables(jnp.arange(seq, dtype=I32))
    cos_s, sin_s = _rope_tables(PAST_LEN + jnp.arange(SAMPLE_TILE, dtype=I32) % dec_seq)
    up, vap, qp, kp, vp = _in_proj(xp2, row(g_mix), w_in_b, row(g_sgu), g_qk, seg, seg.T, cos_p, sin_p,
                                   tile, "in_proj_prompt")
    us, vas, qs, ks, vs = _in_proj(xs2, row(g_mix), w_in_b, row(g_sgu), g_qk, seg, seg.T, cos_s, sin_s,
                                   SAMPLE_TILE, "in_proj_sample")

    w_out_b = w_out[l].astype(BF16)
    tri = jnp.tril(jnp.ones((MLP_CHUNK, MLP_CHUNK), bool))
    ws_p = jnp.where(tri[None], w_s[l], 0.0).astype(BF16)
    bias_p = jnp.repeat(b_s[l].T, A_HEAD_DIM, axis=1)
    x1p = _mixer_prompt(xp2, up, vap, qp, kp, vp, sinks[l], ws_p, bias_p, row(g_out_a), row(g_out_b), w_out_b, seq, tile)

    reps = SAMPLE_TILE // dec_seq
    tri_s = jnp.tril(jnp.ones((dec_seq, dec_seq), bool))
    ws_small = jnp.where(tri_s[None], w_s[l][:, :dec_seq, :dec_seq], 0.0)
    ws_s = jnp.einsum("ab,hts->hatbs", jnp.eye(reps, dtype=F32), ws_small).reshape(A_HEADS, SAMPLE_TILE, SAMPLE_TILE).astype(BF16)
    bias_s = jnp.tile(jnp.repeat(b_s[l][:, :dec_seq].T, A_HEAD_DIM, axis=1), (reps, 1))
    ck = cache_swa_k[l].reshape(n_db, -1, KV_WIDTH)
    cv = cache_swa_v[l].reshape(n_db, -1, KV_WIDTH)
    x1s = _mixer_sample(xs2, us, vas, qs, ks, vs, ck, cv, sinks[l], ws_s, bias_s,
                        row(g_out_a), row(g_out_b), w_out_b, dec_seq, SAMPLE_TILE)

    mk_p, mv_p = _mem_kv(mem_prompt.reshape(-1, D_MODEL), row(g_mem_src), w_mk[l].astype(BF16), w_mv[l].astype(BF16), row(g_mk))
    n_mem = mem_prompt.shape[1]
    w_r = jnp.pad(w_router[l], ((0, 0), (0, LANES - N_EXPERTS)))
    b_r = jnp.pad(b_router[l], (0, LANES - N_EXPERTS)).reshape(1, -1)
    w_r_hi = w_r.astype(BF16)
    w_r_lo = (w_r - w_r_hi.astype(F32)).astype(BF16)
    mem_args = (row(g_mem_in), w_mq[l].astype(BF16), row(g_mq), w_mo[l].astype(BF16), row(g_moe), w_r_hi, w_r_lo, b_r)
    x2p, hnp, idxp, gatep, rankp, cntp = _memory_router(
        x1p, mk_p.reshape(n_b, n_mem, MEM_WIDTH), mv_p.reshape(n_b, n_mem, MEM_WIDTH),
        seq, *mem_args, tile, "memory_router_prompt")
    x2s, hns, idxs, gates, ranks, cnts = _memory_router(
        x1s, cache_mem_k[l].reshape(n_db, -1, MEM_WIDTH), cache_mem_v[l].reshape(n_db, -1, MEM_WIDTH),
        dec_seq, *mem_args, SAMPLE_TILE, "memory_router_sample")

    cnt_p = cntp[0, :N_EXPERTS]
    cnt_s = cnts[0, :N_EXPERTS]
    total = cnt_p + cnt_s
    nblk = (total + EXPERT_ROWS - 1) // EXPERT_ROWS
    blk_end = jnp.cumsum(nblk)
    row_start = (blk_end - nblk) * EXPERT_ROWS
    n_blocks = (tp + ts) * TOP_K // EXPERT_ROWS + N_EXPERTS
    bidx = jnp.arange(n_blocks, dtype=I32)
    used = bidx < blk_end[-1]
    last = jnp.maximum(blk_end[-1] - 1, 0)
    bsafe = jnp.minimum(bidx, last)
    block_expert = jnp.minimum(jnp.searchsorted(blk_end, bsafe, side="right"), N_EXPERTS - 1).astype(I32)
    within = bsafe - (blk_end - nblk)[block_expert]
    block_valid = jnp.where(used, jnp.clip(total[block_expert] - within * EXPERT_ROWS, 0, EXPERT_ROWS), 0).astype(I32)
    dest_p = row_start[idxp[:, :TOP_K]] + rankp[:, :TOP_K]
    dest_s = (row_start + cnt_p)[idxs[:, :TOP_K]] + ranks[:, :TOP_K]

    xb = _dispatch(hnp, dest_p, None, n_blocks * EXPERT_ROWS, min(ROW_TILE, tp))
    xb = _dispatch(hns, dest_s, xb, n_blocks * EXPERT_ROWS, min(ROW_TILE, ts))
    yb = _experts(xb, block_expert, block_valid, bsafe.astype(I32), w_gate_up[l], b_gate_up[l], w_down[l], b_down[l])
    yp = _combine(x2p, gatep, dest_p, yb, min(ROW_TILE, tp))
    ys = _combine(x2s, gates, dest_s, yb, min(ROW_TILE, ts))

    new = dict(
        swa_k_p=kp.reshape(n_b, seq, B_KV_HEADS, B_HEAD_DIM)[:, seq - WINDOW:],
        swa_v_p=vp.reshape(n_b, seq, B_KV_HEADS, B_HEAD_DIM)[:, seq - WINDOW:],
        mem_k_p=mk_p.reshape(n_b, n_mem, MEM_HEADS, MEM_HEAD_DIM),
        mem_v_p=mv_p.reshape(n_b, n_mem, MEM_HEADS, MEM_HEAD_DIM),
        swa_k_s=ks.reshape(n_db, dec_seq, B_KV_HEADS, B_HEAD_DIM),
        swa_v_s=vs.reshape(n_db, dec_seq, B_KV_HEADS, B_HEAD_DIM),
        sgu_v_s=vas.reshape(n_db, dec_seq, A_HEADS, A_HEAD_DIM))
    return yp.reshape(n_b, seq, D_MODEL), ys.reshape(n_db, dec_seq, D_MODEL), new


def kernel(x_prompt, x_sample, cache_swa_k, cache_swa_v, cache_mem_k, cache_mem_v, mem_prompt, g_mix, w_in, g_sgu, w_s, b_s, g_q, g_k, sinks, g_out_a, g_out_b, w_out, g_mem_in, g_mem_src, w_mq, w_mk, w_mv, g_mq, g_mk, w_mo, g_moe, w_router, b_router, w_gate_up, b_gate_up, w_down, b_down):
    xp, xs = x_prompt, x_sample
    news = []
    for l in range(g_mix.shape[0]):
        xp, xs, new = _layer(l, xp, xs, cache_swa_k, cache_swa_v, cache_mem_k, cache_mem_v, mem_prompt,
                             g_mix, w_in, g_sgu, w_s, b_s, g_q, g_k, sinks, g_out_a, g_out_b, w_out,
                             g_mem_in, g_mem_src, w_mq, w_mk, w_mv, g_mq, g_mk, w_mo,
                             g_moe, w_router, b_router, w_gate_up, b_gate_up, w_down, b_down)
        news.append(new)
    stack = lambda name: jnp.stack([n[name] for n in news], 0)
    return (xp, xs, stack("swa_k_p"), stack("swa_v_p"), stack("mem_k_p"), stack("mem_v_p"),
            stack("swa_k_s"), stack("swa_v_s"), stack("sgu_v_s"))
```

```python
import functools

import jax
import jax.numpy as jnp
from jax import lax
from jax.experimental import pallas as pl
from jax.experimental.pallas import tpu as pltpu

F32 = jnp.float32
BF16 = jnp.bfloat16
I32 = jnp.int32
U32 = jnp.uint32

D_MODEL = 2048
PAST_LEN = 2048
CHUNK = 64
EPS = 1e-6
A_HEADS = 16
A_HEAD_DIM = 64
A_WIDTH = A_HEADS * A_HEAD_DIM
MLP_CHUNK = 128
B_HEADS = 16
B_KV_HEADS = 2
B_HEAD_DIM = 64
B_WIDTH = B_HEADS * B_HEAD_DIM
KV_WIDTH = B_KV_HEADS * B_HEAD_DIM
WINDOW = 128
ROPE_THETA = 10000.0
ATTN_SCALE = B_HEAD_DIM ** -0.5
Q_OFF = 2 * A_WIDTH
K_OFF = Q_OFF + B_WIDTH
V_OFF = K_OFF + KV_WIDTH
IN_WIDTH = V_OFF + KV_WIDTH
QK_WIDTH = B_WIDTH + KV_WIDTH
MEM_HEADS = 4
MEM_HEAD_DIM = 128
MEM_WIDTH = MEM_HEADS * MEM_HEAD_DIM
MEM_SCALE = MEM_HEAD_DIM ** -0.5
N_EXPERTS = 32
TOP_K = 4
D_FF = D_MODEL
SWIGLU_LIMIT = 7.0
SWIGLU_ALPHA = 1.702
NEG_BIG = -1e30

LANES = 128
SUBLANES = 8
VMEM_LIMIT = 56 * 1024 * 1024

TOKEN_TILE = 512
SAMPLE_TILE = 128
EXPERT_ROWS = 1280
EXPERT_SUB = 256
FF_TILE = 512
OUT_TILE = 512
ROW_TILE = 256


def _dot(a, b):
    return jnp.dot(a, b, preferred_element_type=F32)


def _dot_nt(a, b):
    return lax.dot_general(a, b, (((1,), (1,)), ((), ())), preferred_element_type=F32)


def _rms(x):
    return x * lax.rsqrt(jnp.mean(x * x, axis=-1, keepdims=True) + EPS)


def _gelu(x):
    return 0.5 * x * (1.0 + lax.erf(x * 0.7071067811865476))


def _pack_rows(x):
    c = x.shape[1] // 2
    bits = pltpu.bitcast(x.astype(BF16).astype(F32), U32)
    return (bits[:, :c] >> 16) | (bits[:, c:] & jnp.uint32(0xFFFF0000))


def _unpack_rows(w):
    return pltpu.bitcast(w << 16, F32), pltpu.bitcast(w & jnp.uint32(0xFFFF0000), F32)


def _params(n_axes=1):
    return pltpu.CompilerParams(dimension_semantics=("arbitrary",) * n_axes,
                                vmem_limit_bytes=VMEM_LIMIT)


def _resident(shape):
    nd = len(shape)
    return pl.BlockSpec(shape, lambda *_: (0,) * nd, pipeline_mode=pl.Buffered(1))


def _mem_kv_kernel(mem_ref, gsrc_ref, wk_ref, wv_ref, gk_ref, k_ref, v_ref):
    m = (_rms(mem_ref[...]) * gsrc_ref[...]).astype(BF16)
    kz = _dot(m, wk_ref[...])
    gk = gk_ref[...]
    for h in range(MEM_HEADS):
        sl = slice(h * MEM_HEAD_DIM, (h + 1) * MEM_HEAD_DIM)
        k_ref[:, sl] = _rms(kz[:, sl]) * gk
    v_ref[...] = _dot(m, wv_ref[...])


def _mem_kv(mem, g_src, w_mk, w_mv, g_mk):
    n = mem.shape[0]
    out = jax.ShapeDtypeStruct((n, MEM_WIDTH), F32)
    return pl.pallas_call(
        _mem_kv_kernel, out_shape=(out, out), name="mem_kv",
        compiler_params=pltpu.CompilerParams(vmem_limit_bytes=VMEM_LIMIT),
    )(mem, g_src, w_mk, w_mv, g_mk)


def _in_proj_kernel(x_ref, gmix_ref, w_ref, gsgu_ref, gqk_ref, seg_ref, segt_ref, cos_ref, sin_ref,
                    u_ref, va_ref, q_ref, k_ref, v_ref):
    xn = (_rms(x_ref[...]) * gmix_ref[...]).astype(BF16)
    u_ref[...] = _gelu(_dot(xn, w_ref[:, 0:A_WIDTH])).astype(BF16)
    va_ref[...] = _rms(_gelu(_dot(xn, w_ref[:, A_WIDTH:Q_OFF]))) * gsgu_ref[...]
    v_ref[...] = _dot(xn, w_ref[:, V_OFF:IN_WIDTH])

    qk = _dot(xn, w_ref[:, Q_OFF:V_OFF])
    ssq = _dot((qk * qk).astype(BF16), seg_ref[...])
    inv = lax.rsqrt(ssq * (1.0 / B_HEAD_DIM) + EPS)
    inv_hi = inv.astype(BF16)
    inv_lo = (inv - inv_hi.astype(F32)).astype(BF16)
    inv_b = _dot(inv_hi, segt_ref[...]) + _dot(inv_lo, segt_ref[...])
    qkn = (qk * inv_b) * gqk_ref[...]

    cos = cos_ref[...]
    sin = sin_ref[...]
    lane = lax.broadcasted_iota(I32, cos.shape, 1)
    first_half = (lane % B_HEAD_DIM) < (B_HEAD_DIM // 2)
    for g in range(QK_WIDTH // LANES):
        xg = qkn[:, g * LANES:(g + 1) * LANES]
        rot = jnp.where(first_half, pltpu.roll(xg, LANES - B_HEAD_DIM // 2, 1),
                        pltpu.roll(xg, B_HEAD_DIM // 2, 1))
        r = xg * cos + rot * sin
        if g < B_WIDTH // LANES:
            q_ref[:, g * LANES:(g + 1) * LANES] = (r * ATTN_SCALE).astype(BF16)
        else:
            k_ref[...] = r


def _in_proj(x, g_mix, w_in, g_sgu, g_qk, seg, segt, cos_tab, sin_tab, tile, name):
    t = x.shape[0]
    n = t // tile
    tab_tiles = cos_tab.shape[0] // tile
    row = lambda w: pl.BlockSpec((tile, w), lambda i: (i, 0))
    tab = pl.BlockSpec((tile, LANES), lambda i: (i % tab_tiles, 0))
    return pl.pallas_call(
        _in_proj_kernel, grid=(n,), name=name,
        in_specs=[row(D_MODEL), _resident(g_mix.shape), _resident(w_in.shape), _resident(g_sgu.shape),
                  _resident(g_qk.shape), _resident(seg.shape), _resident(segt.shape), tab, tab],
        out_specs=[row(A_WIDTH), row(A_WIDTH), row(B_WIDTH), row(KV_WIDTH), row(KV_WIDTH)],
        out_shape=(jax.ShapeDtypeStruct((t, A_WIDTH), BF16), jax.ShapeDtypeStruct((t, A_WIDTH), F32),
                   jax.ShapeDtypeStruct((t, B_WIDTH), BF16), jax.ShapeDtypeStruct((t, KV_WIDTH), F32),
                   jax.ShapeDtypeStruct((t, KV_WIDTH), F32)),
        compiler_params=_params(),
    )(x, g_mix, w_in, g_sgu, g_qk, seg, segt, cos_tab, sin_tab)


def _split_heads(ref_dst_lo, ref_dst_hi, rows, x, swap):
    lane = lax.broadcasted_iota(I32, x.shape, 1)
    low = lane < B_HEAD_DIM
    zero = jnp.zeros_like(x)
    ref_dst_lo[0, rows, :] = jnp.where(low, x, zero).astype(BF16)
    ref_dst_hi[0, rows, :] = jnp.where(low, zero, swap).astype(BF16)
    ref_dst_lo[1, rows, :] = jnp.where(low, swap, zero).astype(BF16)
    ref_dst_hi[1, rows, :] = jnp.where(low, zero, x).astype(BF16)


def _sgu_chunk(u, va, ws_ref, bias):
    lane = lax.broadcasted_iota(I32, (MLP_CHUNK, LANES), 1)
    low = lane < A_HEAD_DIM
    outs = []
    for p in range(A_WIDTH // LANES):
        sl = slice(p * LANES, (p + 1) * LANES)
        v2 = va[:, sl]
        zero = jnp.zeros_like(v2)
        mixed = (_dot(ws_ref[2 * p], jnp.where(low, v2, zero).astype(BF16))
                 + _dot(ws_ref[2 * p + 1], jnp.where(low, zero, v2).astype(BF16)))
        outs.append(u[:, sl].astype(F32) * (mixed + bias[:, sl]))
    return jnp.concatenate(outs, axis=1)


def _attend(q2, kl, kh, vl, vh, mask, sink_even, sink_odd):
    outs = []
    for kk, vv, sink in ((kl, vl, sink_even), (kh, vh, sink_odd)):
        s = _dot_nt(q2, kk)
        if mask is not None:
            s = jnp.where(mask, s, NEG_BIG)
        m = jnp.maximum(jnp.max(s, axis=-1, keepdims=True), sink)
        p = jnp.exp(s - m)
        den = jnp.sum(p, axis=-1, keepdims=True) + jnp.exp(sink - m)
        outs.append(_dot(p.astype(BF16), vv) * (1.0 / den))
    return outs[0] + outs[1]


def _out_proj(oa, ob, ga_ref, gb_ref, wout_ref, x):
    a = (_rms(oa) * ga_ref[...]).astype(BF16)
    b = (_rms(ob) * gb_ref[...]).astype(BF16)
    return x + _dot(a, wout_ref[0:A_WIDTH, :]) + _dot(b, wout_ref[A_WIDTH:, :])


def _mixer_prompt_kernel(tiles_per_seq, sinks_ref, x_ref, u_ref, va_ref, q_ref, kc_ref, vc_ref, kp_ref, vp_ref,
                         ws_ref, bias_ref, ga_ref, gb_ref, wout_ref, o_ref,
                         kl_s, kh_s, vl_s, vh_s, oa_s, ob_s):
    tile = x_ref.shape[0]
    n_sub = tile // MLP_CHUNK
    seq_start = (pl.program_id(0) % tiles_per_seq) == 0

    for src_p, src_c, dl, dh in ((kp_ref, kc_ref, kl_s, kh_s), (vp_ref, vc_ref, vl_s, vh_s)):
        prev = src_p[...]
        cur = src_c[...]
        _split_heads(dl, dh, slice(0, WINDOW), prev, pltpu.roll(prev, B_HEAD_DIM, 1))
        _split_heads(dl, dh, slice(WINDOW, WINDOW + tile), cur, pltpu.roll(cur, B_HEAD_DIM, 1))

    qc = lax.broadcasted_iota(I32, (MLP_CHUNK, 2 * MLP_CHUNK), 0) // CHUNK
    kc = lax.broadcasted_iota(I32, (MLP_CHUNK, 2 * MLP_CHUNK), 1) // CHUNK
    window_mask = (kc >= qc) & (kc <= qc + 2)
    bias = bias_ref[...]

    def sub(j, carry):
        r0 = pl.multiple_of(j * MLP_CHUNK, MLP_CHUNK)
        rows = pl.ds(r0, MLP_CHUNK)
        keys = pl.ds(r0, 2 * MLP_CHUNK)
        first_key_chunk = jnp.where(jnp.logical_and(seq_start, j == 0), 2, 0)
        mask = window_mask & (kc >= first_key_chunk)
        oa_s[rows, :] = _sgu_chunk(u_ref[rows, :], va_ref[rows, :], ws_ref, bias)
        for p in range(B_WIDTH // LANES):
            h = p // (B_WIDTH // LANES // B_KV_HEADS)
            sl = slice(p * LANES, (p + 1) * LANES)
            ob_s[rows, sl] = _attend(q_ref[rows, sl], kl_s[h, keys, :], kh_s[h, keys, :],
                                     vl_s[h, keys, :], vh_s[h, keys, :], mask,
                                     sinks_ref[2 * p], sinks_ref[2 * p + 1])
        return carry

    lax.fori_loop(0, n_sub, sub, 0)
    o_ref[...] = _out_proj(oa_s[...], ob_s[...], ga_ref, gb_ref, wout_ref, x_ref[...])


def _mixer_prompt(x, u, va, q, k, v, sinks, ws, bias, g_a, g_b, w_out, seq, tile):
    t = x.shape[0]
    n = t // tile
    tiles_per_seq = seq // tile
    per = tile // WINDOW
    row = lambda w: pl.BlockSpec((tile, w), lambda i, s: (i, 0))
    prev = pl.BlockSpec((WINDOW, KV_WIDTH), lambda i, s: (jnp.maximum(i * per - 1, 0), 0))
    res = lambda a: pl.BlockSpec(a.shape, lambda i, s: (0,) * a.ndim, pipeline_mode=pl.Buffered(1))
    grid_spec = pltpu.PrefetchScalarGridSpec(
        num_scalar_prefetch=1, grid=(n,),
        in_specs=[row(D_MODEL), row(A_WIDTH), row(A_WIDTH), row(B_WIDTH), row(KV_WIDTH), row(KV_WIDTH), prev, prev,
                  res(ws), res(bias), res(g_a), res(g_b), res(w_out)],
        out_specs=row(D_MODEL),
        scratch_shapes=[pltpu.VMEM((B_KV_HEADS, WINDOW + tile, LANES), BF16) for _ in range(4)]
        + [pltpu.VMEM((tile, A_WIDTH), F32), pltpu.VMEM((tile, B_WIDTH), F32)])
    return pl.pallas_call(
        functools.partial(_mixer_prompt_kernel, tiles_per_seq), grid_spec=grid_spec, name="mixer_prompt",
        out_shape=jax.ShapeDtypeStruct((t, D_MODEL), F32), compiler_params=_params(),
    )(sinks, x, u, va, q, k, v, k, v, ws, bias, g_a, g_b, w_out)


def _mixer_sample_kernel(dec_seq, sinks_ref, x_ref, u_ref, va_ref, q_ref, kc_ref, vc_ref, ck_ref, cv_ref,
                         ws_ref, bias_ref, ga_ref, gb_ref, wout_ref, o_ref,
                         kl_s, kh_s, vl_s, vh_s, ob_s):
    tile = x_ref.shape[0]
    n_seq = tile // dec_seq
    n_cache = ck_ref.shape[1]
    n_keys = n_cache + dec_seq
    oa = _sgu_chunk(u_ref[...], va_ref[...], ws_ref, bias_ref[...])
    for b in range(n_seq):
        rows = slice(b * dec_seq, (b + 1) * dec_seq)
        for src_c, src_n, dl, dh in ((ck_ref, kc_ref, kl_s, kh_s), (cv_ref, vc_ref, vl_s, vh_s)):
            old = src_c[b]
            new = src_n[rows, :]
            _split_heads(dl, dh, slice(0, n_cache), old, pltpu.roll(old, B_HEAD_DIM, 1))
            _split_heads(dl, dh, slice(n_cache, n_keys), new, pltpu.roll(new, B_HEAD_DIM, 1))
        for p in range(B_WIDTH // LANES):
            h = p // (B_WIDTH // LANES // B_KV_HEADS)
            sl = slice(p * LANES, (p + 1) * LANES)
            ob_s[rows, sl] = _attend(q_ref[rows, sl], kl_s[h], kh_s[h], vl_s[h], vh_s[h], None,
                                     sinks_ref[2 * p], sinks_ref[2 * p + 1])
    o_ref[...] = _out_proj(oa, ob_s[...], ga_ref, gb_ref, wout_ref, x_ref[...])


def _mixer_sample(x, u, va, q, k, v, cache_k, cache_v, sinks, ws, bias, g_a, g_b, w_out, dec_seq, tile):
    t = x.shape[0]
    n = t // tile
    n_seq = tile // dec_seq
    n_cache = cache_k.shape[1]
    row = lambda w: pl.BlockSpec((tile, w), lambda i, s: (i, 0))
    cache = pl.BlockSpec((n_seq, n_cache, KV_WIDTH), lambda i, s: (i, 0, 0))
    res = lambda a: pl.BlockSpec(a.shape, lambda i, s: (0,) * a.ndim, pipeline_mode=pl.Buffered(1))
    grid_spec = pltpu.PrefetchScalarGridSpec(
        num_scalar_prefetch=1, grid=(n,),
        in_specs=[row(D_MODEL), row(A_WIDTH), row(A_WIDTH), row(B_WIDTH), row(KV_WIDTH), row(KV_WIDTH), cache, cache,
                  res(ws), res(bias), res(g_a), res(g_b), res(w_out)],
        out_specs=row(D_MODEL),
        scratch_shapes=[pltpu.VMEM((B_KV_HEADS, n_cache + dec_seq, LANES), BF16) for _ in range(4)]
        + [pltpu.VMEM((tile, B_WIDTH), F32)])
    return pl.pallas_call(
        functools.partial(_mixer_sample_kernel, dec_seq), grid_spec=grid_spec, name="mixer_sample",
        out_shape=jax.ShapeDtypeStruct((t, D_MODEL), F32), compiler_params=_params(),
    )(sinks, x, u, va, q, k, v, cache_k, cache_v, ws, bias, g_a, g_b, w_out)


def _memory_router_kernel(rows_per_mem, x_ref, mk_ref, mv_ref, gin_ref, wq_ref, gq_ref, wo_ref,
                          gmoe_ref, wrh_ref, wrl_ref, br_ref,
                          x2_ref, hn_ref, idx_ref, gate_ref, rank_ref, cnt_ref, o_s, base_s):
    tile = x_ref.shape[0]
    x = x_ref[...]
    qz = _dot((_rms(x) * gin_ref[...]).astype(BF16), wq_ref[...])
    gq = gq_ref[...]
    for h in range(MEM_HEADS):
        sl = slice(h * MEM_HEAD_DIM, (h + 1) * MEM_HEAD_DIM)
        qh = (_rms(qz[:, sl]) * gq).astype(BF16)
        for r in range(tile // rows_per_mem):
            rows = slice(r * rows_per_mem, (r + 1) * rows_per_mem)
            s = _dot_nt(qh[rows], mk_ref[r, :, sl].astype(BF16)) * MEM_SCALE
            p = jnp.exp(s - jnp.max(s, axis=-1, keepdims=True))
            den = jnp.sum(p, axis=-1, keepdims=True)
            o_s[rows, sl] = _dot(p.astype(BF16), mv_ref[r, :, sl].astype(BF16)) * (1.0 / den)
    x2 = x + _dot(o_s[...].astype(BF16), wo_ref[...])
    x2_ref[...] = x2
    hn = _rms(x2) * gmoe_ref[...]
    hn_ref[...] = _pack_rows(hn)

    hn_hi = hn.astype(BF16)
    hn_lo = (hn - hn_hi.astype(F32)).astype(BF16)
    logits = ((_dot(hn_lo, wrh_ref[...]) + _dot(hn_hi, wrl_ref[...])) + _dot(hn_hi, wrh_ref[...])) + br_ref[...]
    lane = lax.broadcasted_iota(I32, (tile, LANES), 1)
    lane_f = lane.astype(F32)
    work = jnp.where(lane < N_EXPERTS, logits, -jnp.inf)
    idx_out = jnp.zeros((tile, LANES), F32)
    val_out = jnp.zeros((tile, LANES), F32)
    hot = jnp.zeros((tile, LANES), F32)
    top = None
    picks = []
    for k in range(TOP_K):
        m = jnp.max(work, axis=-1, keepdims=True)
        pick = jnp.min(jnp.where(work == m, lane_f, float(LANES)), axis=-1, keepdims=True)
        chosen = lane_f == pick
        if top is None:
            top = m
        idx_out = jnp.where(lane == k, pick, idx_out)
        val_out = jnp.where(lane == k, jnp.exp(m - top), val_out)
        hot = jnp.where(chosen, 1.0, hot)
        work = jnp.where(chosen, -jnp.inf, work)
        picks.append(chosen)
    idx_ref[...] = idx_out.astype(I32)
    gate_ref[...] = val_out * (1.0 / jnp.sum(val_out, axis=-1, keepdims=True))

    @pl.when(pl.program_id(0) == 0)
    def _():
        base_s[...] = jnp.zeros_like(base_s)

    earlier = (lax.broadcasted_iota(I32, (tile, tile), 1) < lax.broadcasted_iota(I32, (tile, tile), 0))
    pos = _dot(jnp.where(earlier, 1.0, 0.0).astype(BF16), hot.astype(BF16)) + base_s[0:1, :]
    rank_out = jnp.zeros((tile, LANES), F32)
    for k in range(TOP_K):
        rank_out = jnp.where(lane == k, jnp.sum(jnp.where(picks[k], pos, 0.0), axis=-1, keepdims=True), rank_out)
    rank_ref[...] = rank_out.astype(I32)
    total = base_s[0:1, :] + jnp.sum(hot, axis=0, keepdims=True)
    base_s[...] = jnp.broadcast_to(total, base_s.shape)
    cnt_ref[...] = jnp.broadcast_to(total, cnt_ref.shape).astype(I32)


def _memory_router(x, mk, mv, rows_per_mem, g_in, w_mq, g_mq, w_mo, g_moe, w_r_hi, w_r_lo, b_r, tile, name):
    t = x.shape[0]
    n = t // tile
    mems = tile // rows_per_mem if rows_per_mem <= tile else 1
    per_mem_tiles = max(rows_per_mem // tile, 1)
    rpm = min(rows_per_mem, tile)
    row = lambda w: pl.BlockSpec((tile, w), lambda i: (i, 0))
    mem = pl.BlockSpec((mems, mk.shape[1], MEM_WIDTH), lambda i: (i // per_mem_tiles, 0, 0))
    res = lambda a: pl.BlockSpec(a.shape, lambda i: (0,) * a.ndim, pipeline_mode=pl.Buffered(1))
    small = jax.ShapeDtypeStruct((t, LANES), I32)
    return pl.pallas_call(
        functools.partial(_memory_router_kernel, rpm), grid=(n,), name=name,
        in_specs=[row(D_MODEL), mem, mem, res(g_in), res(w_mq), res(g_mq), res(w_mo), res(g_moe), res(w_r_hi), res(w_r_lo), res(b_r)],
        out_specs=[row(D_MODEL), row(D_MODEL // 2), row(LANES), row(LANES), row(LANES),
                   pl.BlockSpec((8, LANES), lambda i: (0, 0))],
        out_shape=(jax.ShapeDtypeStruct((t, D_MODEL), F32), jax.ShapeDtypeStruct((t, D_MODEL // 2), U32),
                   small, jax.ShapeDtypeStruct((t, LANES), F32), small, jax.ShapeDtypeStruct((8, LANES), I32)),
        scratch_shapes=[pltpu.VMEM((tile, MEM_WIDTH), F32), pltpu.VMEM((8, LANES), F32)],
        compiler_params=_params(),
    )(x, mk, mv, g_in, w_mq, g_mq, w_mo, g_moe, w_r_hi, w_r_lo, b_r)


def _dispatch_kernel(dest_ref, hn_ref, *rest):
    xb_ref, sem = rest[-2:]
    groups = hn_ref.shape[0]

    def copy(j, u, k):
        return pltpu.make_async_copy(hn_ref.at[j, pl.ds(u, 1), :],
                                     xb_ref.at[pl.ds(dest_ref[0, j * (SUBLANES * TOP_K) + u * TOP_K + k], 1), :], sem)

    def start(j, carry):
        for u in range(SUBLANES):
            for k in range(TOP_K):
                copy(j, u, k).start()
        return carry

    def wait(j, carry):
        for u in range(SUBLANES):
            for k in range(TOP_K):
                copy(j, u, k).wait()
        return carry

    lax.fori_loop(0, groups, start, 0)
    lax.fori_loop(0, groups, wait, 0)


def _dispatch(hn, dest, xb, n_rows, tile):
    t = hn.shape[0]
    n = t // tile
    dest = dest.reshape(n, 1, tile * TOP_K)
    in_specs = [pl.BlockSpec((None, 1, tile * TOP_K), lambda i: (i, 0, 0), memory_space=pltpu.SMEM),
                pl.BlockSpec((tile // SUBLANES, SUBLANES, hn.shape[1]), lambda i: (i, 0, 0))]
    args = [dest, hn.reshape(t // SUBLANES, SUBLANES, hn.shape[1])]
    if xb is not None:
        in_specs.append(pl.BlockSpec(memory_space=pl.ANY))
        args.append(xb)
    return pl.pallas_call(
        _dispatch_kernel, grid=(n,), name="dispatch",
        in_specs=in_specs,
        out_specs=pl.BlockSpec(memory_space=pl.ANY),
        out_shape=jax.ShapeDtypeStruct((n_rows, hn.shape[1]), hn.dtype),
        scratch_shapes=[pltpu.SemaphoreType.DMA(())],
        input_output_aliases={} if xb is None else {2: 0},
        compiler_params=_params(),
    )(*args)


def _for_pairs(n, body):
    def pair(j, carry):
        body(2 * j)
        body(2 * j + 1)
        return carry
    lax.fori_loop(0, n // 2, pair, 0)
    pl.when(n % 2 == 1)(lambda: body(n - 1))


def _experts_kernel(be_ref, nv_ref, xi_ref, x_ref, wg_ref, wu_ref, bg_ref, bu_ref, wd_ref, bd_ref, y_ref, h_s):
    del be_ref, xi_ref
    b = pl.program_id(0)
    s = pl.program_id(1)
    n_ff = D_FF // FF_TILE
    valid = nv_ref[b]
    n_sub = (valid + EXPERT_SUB - 1) // EXPERT_SUB

    @pl.when(s < n_ff)
    def _():
        wg = wg_ref[0].astype(BF16)
        wu = wu_ref[0].astype(BF16)
        bg = bg_ref[0]
        bu = bu_ref[0]

        def up(i):
            rows = pl.ds(pl.multiple_of(i * EXPERT_SUB, EXPERT_SUB), EXPERT_SUB)
            keep = (i * EXPERT_SUB + lax.broadcasted_iota(I32, (EXPERT_SUB, 1), 0)) < valid
            lo, hi = _unpack_rows(x_ref[rows, :])
            xs = jnp.concatenate([jnp.where(keep, lo, 0.0).astype(BF16), jnp.where(keep, hi, 0.0).astype(BF16)], axis=1)
            gate = jnp.minimum(_dot(xs, wg) + bg, SWIGLU_LIMIT)
            lin = jnp.clip(_dot(xs, wu) + bu, -SWIGLU_LIMIT, SWIGLU_LIMIT)
            act = (lin + 1.0) * (gate * (1.0 / (1.0 + jnp.exp(-SWIGLU_ALPHA * gate))))
            h_s[s, rows, :] = act.astype(BF16)
        _for_pairs(n_sub, up)

    @pl.when(s >= n_ff)
    def _():
        wd = wd_ref[0].astype(BF16)
        bd = bd_ref[0]

        def down(i):
            rows = pl.ds(pl.multiple_of(i * EXPERT_SUB, EXPERT_SUB), EXPERT_SUB)
            h = jnp.concatenate([h_s[f, rows, :] for f in range(n_ff)], axis=1)
            y_ref[rows, :] = _pack_rows(_dot(h, wd) + bd)
        _for_pairs(n_sub, down)


def _experts(xb, block_expert, block_valid, block_index, w_gu, b_gu, w_d, b_d):
    n_blocks = xb.shape[0] // EXPERT_ROWS
    n_ff = D_FF // FF_TILE
    n_out = D_MODEL // OUT_TILE

    def ff(b, s, nv):
        return jnp.where(nv[b] > 0, jnp.minimum(s, n_ff - 1), n_ff - 1)

    def oc(b, s, nv):
        return jnp.where(nv[b] > 0, jnp.maximum(s - n_ff, 0), n_out - 1)

    b_gu3 = b_gu.reshape(N_EXPERTS, 1, 2 * D_FF)
    b_d3 = b_d.reshape(N_EXPERTS, 1, D_MODEL)
    grid_spec = pltpu.PrefetchScalarGridSpec(
        num_scalar_prefetch=3, grid=(n_blocks, n_ff + n_out),
        in_specs=[
            pl.BlockSpec((EXPERT_ROWS, D_MODEL // 2), lambda b, s, be, nv, xi: (xi[b], 0)),
            pl.BlockSpec((1, D_MODEL, FF_TILE), lambda b, s, be, nv, xi: (be[b], 0, ff(b, s, nv))),
            pl.BlockSpec((1, D_MODEL, FF_TILE), lambda b, s, be, nv, xi: (be[b], 0, ff(b, s, nv) + n_ff)),
            pl.BlockSpec((1, 1, FF_TILE), lambda b, s, be, nv, xi: (be[b], 0, ff(b, s, nv))),
            pl.BlockSpec((1, 1, FF_TILE), lambda b, s, be, nv, xi: (be[b], 0, ff(b, s, nv) + n_ff)),
            pl.BlockSpec((1, D_FF, OUT_TILE), lambda b, s, be, nv, xi: (be[b], 0, oc(b, s, nv))),
            pl.BlockSpec((1, 1, OUT_TILE), lambda b, s, be, nv, xi: (be[b], 0, oc(b, s, nv))),
        ],
        out_specs=pl.BlockSpec((EXPERT_ROWS, OUT_TILE // 2), lambda b, s, be, nv, xi: (xi[b], oc(b, s, nv))),
        scratch_shapes=[pltpu.VMEM((n_ff, EXPERT_ROWS, FF_TILE), BF16)])
    return pl.pallas_call(
        _experts_kernel, grid_spec=grid_spec, name="experts",
        out_shape=jax.ShapeDtypeStruct((n_blocks * EXPERT_ROWS, D_MODEL // 2), U32),
        compiler_params=_params(2),
    )(block_expert, block_valid, block_index, xb, w_gu, w_gu, b_gu3, b_gu3, w_d, b_d3)


def _combine_kernel(n_tiles, dest_ref, next_ref, x_ref, gate_ref, yb_ref, o_ref, buf, sem):
    tile = x_ref.shape[0]
    i = pl.program_id(0)
    slot = i % 2

    groups = tile // SUBLANES

    def copy(idx_ref, s, j, u, k):
        return pltpu.make_async_copy(yb_ref.at[pl.ds(idx_ref[0, j * (SUBLANES * TOP_K) + u * TOP_K + k], 1), :],
                                     buf.at[s, k, j, pl.ds(u, 1), :], sem.at[s])

    def start_tile(idx_ref, s):
        def start(j, carry):
            for u in range(SUBLANES):
                for k in range(TOP_K):
                    copy(idx_ref, s, j, u, k).start()
            return carry
        lax.fori_loop(0, groups, start, 0)

    pl.when(i == 0)(lambda: start_tile(dest_ref, 0))
    pl.when(i + 1 < n_tiles)(lambda: start_tile(next_ref, 1 - slot))

    def wait(j, carry):
        for u in range(SUBLANES):
            for k in range(TOP_K):
                copy(dest_ref, slot, j, u, k).wait()
        return carry
    lax.fori_loop(0, groups, wait, 0)

    gate = gate_ref[...]
    half = OUT_TILE // 2
    for g in range(D_MODEL // OUT_TILE):
        c0 = g * OUT_TILE
        acc_lo = x_ref[:, c0:c0 + half]
        acc_hi = x_ref[:, c0 + half:c0 + OUT_TILE]
        for k in range(TOP_K):
            lo, hi = _unpack_rows(buf[slot, k, :, :, g * half:(g + 1) * half].reshape(tile, half))
            acc_lo = acc_lo + gate[:, k:k + 1] * lo
            acc_hi = acc_hi + gate[:, k:k + 1] * hi
        o_ref[:, c0:c0 + half] = acc_lo
        o_ref[:, c0 + half:c0 + OUT_TILE] = acc_hi


def _combine(x, gate, dest, yb, tile):
    t = x.shape[0]
    n = t // tile
    dest = dest.reshape(n, 1, tile * TOP_K)
    return pl.pallas_call(
        functools.partial(_combine_kernel, n), grid=(n,), name="combine",
        in_specs=[pl.BlockSpec((None, 1, tile * TOP_K), lambda i: (i, 0, 0), memory_space=pltpu.SMEM),
                  pl.BlockSpec((None, 1, tile * TOP_K), lambda i: (jnp.minimum(i + 1, n - 1), 0, 0),
                               memory_space=pltpu.SMEM),
                  pl.BlockSpec((tile, D_MODEL), lambda i: (i, 0)),
                  pl.BlockSpec((tile, LANES), lambda i: (i, 0)),
                  pl.BlockSpec(memory_space=pl.ANY)],
        out_specs=pl.BlockSpec((tile, D_MODEL), lambda i: (i, 0)),
        out_shape=jax.ShapeDtypeStruct((t, D_MODEL), F32),
        scratch_shapes=[pltpu.VMEM((2, TOP_K, tile // SUBLANES, SUBLANES, D_MODEL // 2), U32),
                        pltpu.SemaphoreType.DMA((2,))],
        compiler_params=_params(),
    )(dest, dest, x, gate, yb)


def _rope_tables(pos):
    half = B_HEAD_DIM // 2
    inv_freq = ROPE_THETA ** (-jnp.arange(half, dtype=F32) / half)
    ang = pos.astype(F32)[:, None] * inv_freq[None, :]
    cos = jnp.cos(ang)
    sin = jnp.sin(ang)
    return jnp.tile(cos, (1, LANES // half)), jnp.tile(jnp.concatenate([-sin, sin], axis=1), (1, LANES // B_HEAD_DIM))


def _layer(l, xp, xs, cache_swa_k, cache_swa_v, cache_mem_k, cache_mem_v, mem_prompt,
           g_mix, w_in, g_sgu, w_s, b_s, g_q, g_k, sinks, g_out_a, g_out_b, w_out,
           g_mem_in, g_mem_src, w_mq, w_mk, w_mv, g_mq, g_mk, w_mo,
           g_moe, w_router, b_router, w_gate_up, b_gate_up, w_down, b_down):
    n_b, seq, _ = xp.shape
    n_db, dec_seq, _ = xs.shape
    tp = n_b * seq
    ts = n_db * dec_seq
    tile = min(TOKEN_TILE, seq)
    row = lambda a: a[l].reshape(1, -1)

    xp2 = xp.reshape(tp, D_MODEL)
    xs2 = xs.reshape(ts, D_MODEL)
    head = jnp.arange(QK_WIDTH, dtype=I32) // B_HEAD_DIM
    seg = (head[:, None] == jnp.arange(LANES, dtype=I32)[None, :]).astype(BF16)
    g_qk = jnp.concatenate([jnp.tile(g_q[l], B_HEADS), jnp.tile(g_k[l], B_KV_HEADS)]).reshape(1, -1)
    w_in_b = w_in[l].astype(BF16)
    cos_p, sin_p = _rope_tables(jnp.arange(seq, dtype=I32))
    cos_s, sin_s = _rope_tables(PAST_LEN + jnp.arange(SAMPLE_TILE, dtype=I32) % dec_seq)
    up, vap, qp, kp, vp = _in_proj(xp2, row(g_mix), w_in_b, row(g_sgu), g_qk, seg, seg.T, cos_p, sin_p,
                                   tile, "in_proj_prompt")
    us, vas, qs, ks, vs = _in_proj(xs2, row(g_mix), w_in_b, row(g_sgu), g_qk, seg, seg.T, cos_s, sin_s,
                                   SAMPLE_TILE, "in_proj_sample")

    w_out_b = w_out[l].astype(BF16)
    tri = jnp.tril(jnp.ones((MLP_CHUNK, MLP_CHUNK), bool))
    ws_p = jnp.where(tri[None], w_s[l], 0.0).astype(BF16)
    bias_p = jnp.repeat(b_s[l].T, A_HEAD_DIM, axis=1)
    x1p = _mixer_prompt(xp2, up, vap, qp, kp, vp, sinks[l], ws_p, bias_p, row(g_out_a), row(g_out_b), w_out_b, seq, tile)

    reps = SAMPLE_TILE // dec_seq
    tri_s = jnp.tril(jnp.ones((dec_seq, dec_seq), bool))
    ws_small = jnp.where(tri_s[None], w_s[l][:, :dec_seq, :dec_seq], 0.0)
    ws_s = jnp.einsum("ab,hts->hatbs", jnp.eye(reps, dtype=F32), ws_small).reshape(A_HEADS, SAMPLE_TILE, SAMPLE_TILE).astype(BF16)
    bias_s = jnp.tile(jnp.repeat(b_s[l][:, :dec_seq].T, A_HEAD_DIM, axis=1), (reps, 1))
    ck = cache_swa_k[l].reshape(n_db, -1, KV_WIDTH)
    cv = cache_swa_v[l].reshape(n_db, -1, KV_WIDTH)
    x1s = _mixer_sample(xs2, us, vas, qs, ks, vs, ck, cv, sinks[l], ws_s, bias_s,
                        row(g_out_a), row(g_out_b), w_out_b, dec_seq, SAMPLE_TILE)

    mk_p, mv_p = _mem_kv(mem_prompt.reshape(-1, D_MODEL), row(g_mem_src), w_mk[l].astype(BF16), w_mv[l].astype(BF16), row(g_mk))
    n_mem = mem_prompt.shape[1]
    w_r = jnp.pad(w_router[l], ((0, 0), (0, LANES - N_EXPERTS)))
    b_r = jnp.pad(b_router[l], (0, LANES - N_EXPERTS)).reshape(1, -1)
    w_r_hi = w_r.astype(BF16)
    w_r_lo = (w_r - w_r_hi.astype(F32)).astype(BF16)
    mem_args = (row(g_mem_in), w_mq[l].astype(BF16), row(g_mq), w_mo[l].astype(BF16), row(g_moe), w_r_hi, w_r_lo, b_r)
    x2p, hnp, idxp, gatep, rankp, cntp = _memory_router(
        x1p, mk_p.reshape(n_b, n_mem, MEM_WIDTH), mv_p.reshape(n_b, n_mem, MEM_WIDTH),
        seq, *mem_args, tile, "memory_router_prompt")
    x2s, hns, idxs, gates, ranks, cnts = _memory_router(
        x1s, cache_mem_k[l].reshape(n_db, -1, MEM_WIDTH), cache_mem_v[l].reshape(n_db, -1, MEM_WIDTH),
        dec_seq, *mem_args, SAMPLE_TILE, "memory_router_sample")

    cnt_p = cntp[0, :N_EXPERTS]
    cnt_s = cnts[0, :N_EXPERTS]
    total = cnt_p + cnt_s
    nblk = (total + EXPERT_ROWS - 1) // EXPERT_ROWS
    blk_end = jnp.cumsum(nblk)
    row_start = (blk_end - nblk) * EXPERT_ROWS
    n_blocks = (tp + ts) * TOP_K // EXPERT_ROWS + N_EXPERTS
    bidx = jnp.arange(n_blocks, dtype=I32)
    used = bidx < blk_end[-1]
    last = jnp.maximum(blk_end[-1] - 1, 0)
    bsafe = jnp.minimum(bidx, last)
    block_expert = jnp.minimum(jnp.searchsorted(blk_end, bsafe, side="right"), N_EXPERTS - 1).astype(I32)
    within = bsafe - (blk_end - nblk)[block_expert]
    block_valid = jnp.where(used, jnp.clip(total[block_expert] - within * EXPERT_ROWS, 0, EXPERT_ROWS), 0).astype(I32)
    dest_p = row_start[idxp[:, :TOP_K]] + rankp[:, :TOP_K]
    dest_s = (row_start + cnt_p)[idxs[:, :TOP_K]] + ranks[:, :TOP_K]

    xb = _dispatch(hnp, dest_p, None, n_blocks * EXPERT_ROWS, min(ROW_TILE, tp))
    xb = _dispatch(hns, dest_s, xb, n_blocks * EXPERT_ROWS, min(ROW_TILE, ts))
    yb = _experts(xb, block_expert, block_valid, bsafe.astype(I32), w_gate_up[l], b_gate_up[l], w_down[l], b_down[l])
    yp = _combine(x2p, gatep, dest_p, yb, min(ROW_TILE, tp))
    ys = _combine(x2s, gates, dest_s, yb, min(ROW_TILE, ts))

    new = dict(
        swa_k_p=kp.reshape(n_b, seq, B_KV_HEADS, B_HEAD_DIM)[:, seq - WINDOW:],
        swa_v_p=vp.reshape(n_b, seq, B_KV_HEADS, B_HEAD_DIM)[:, seq - WINDOW:],
        mem_k_p=mk_p.reshape(n_b, n_mem, MEM_HEADS, MEM_HEAD_DIM),
        mem_v_p=mv_p.reshape(n_b, n_mem, MEM_HEADS, MEM_HEAD_DIM),
        swa_k_s=ks.reshape(n_db, dec_seq, B_KV_HEADS, B_HEAD_DIM),
        swa_v_s=vs.reshape(n_db, dec_seq, B_KV_HEADS, B_HEAD_DIM),
        sgu_v_s=vas.reshape(n_db, dec_seq, A_HEADS, A_HEAD_DIM))
    return yp.reshape(n_b, seq, D_MODEL), ys.reshape(n_db, dec_seq, D_MODEL), new


def kernel(x_prompt, x_sample, cache_swa_k, cache_swa_v, cache_mem_k, cache_mem_v, mem_prompt, g_mix, w_in, g_sgu, w_s, b_s, g_q, g_k, sinks, g_out_a, g_out_b, w_out, g_mem_in, g_mem_src, w_mq, w_mk, w_mv, g_mq, g_mk, w_mo, g_moe, w_router, b_router, w_gate_up, b_gate_up, w_down, b_down):
    xp, xs = x_prompt, x_sample
    news = []
    for l in range(g_mix.shape[0]):
        xp, xs, new = _layer(l, xp, xs, cache_swa_k, cache_swa_v, cache_mem_k, cache_mem_v, mem_prompt,
                             g_mix, w_in, g_sgu, w_s, b_s, g_q, g_k, sinks, g_out_a, g_out_b, w_out,
                             g_mem_in, g_mem_src, w_mq, w_mk, w_mv, g_mq, g_mk, w_mo,
                             g_moe, w_router, b_router, w_gate_up, b_gate_up, w_down, b_down)
        news.append(new)
    stack = lambda name: jnp.stack([n[name] for n in news], 0)
    return (xp, xs, stack("swa_k_p"), stack("swa_v_p"), stack("mem_k_p"), stack("mem_v_p"),
            stack("swa_k_s"), stack("swa_v_s"), stack("sgu_v_s"))
```

```python
import functools

import jax
import jax.numpy as jnp
from jax import lax
from jax.experimental import pallas as pl
from jax.experimental.pallas import tpu as pltpu

F32 = jnp.float32
BF16 = jnp.bfloat16
I32 = jnp.int32
U32 = jnp.uint32

D_MODEL = 2048
PAST_LEN = 2048
CHUNK = 64
EPS = 1e-6
A_HEADS = 16
A_HEAD_DIM = 64
A_WIDTH = A_HEADS * A_HEAD_DIM
MLP_CHUNK = 128
B_HEADS = 16
B_KV_HEADS = 2
B_HEAD_DIM = 64
B_WIDTH = B_HEADS * B_HEAD_DIM
KV_WIDTH = B_KV_HEADS * B_HEAD_DIM
WINDOW = 128
ROPE_THETA = 10000.0
ATTN_SCALE = B_HEAD_DIM ** -0.5
Q_OFF = 2 * A_WIDTH
K_OFF = Q_OFF + B_WIDTH
V_OFF = K_OFF + KV_WIDTH
IN_WIDTH = V_OFF + KV_WIDTH
QK_WIDTH = B_WIDTH + KV_WIDTH
MEM_HEADS = 4
MEM_HEAD_DIM = 128
MEM_WIDTH = MEM_HEADS * MEM_HEAD_DIM
MEM_SCALE = MEM_HEAD_DIM ** -0.5
N_EXPERTS = 32
TOP_K = 4
D_FF = D_MODEL
SWIGLU_LIMIT = 7.0
SWIGLU_ALPHA = 1.702
NEG_BIG = -1e30

LANES = 128
SUBLANES = 8
VMEM_LIMIT = 56 * 1024 * 1024

TOKEN_TILE = 512
SAMPLE_TILE = 128
EXPERT_ROWS = 1280
EXPERT_SUB = 256
FF_TILE = 512
OUT_TILE = 512
ROW_TILE = 256


def _dot(a, b):
    return jnp.dot(a, b, preferred_element_type=F32)


def _dot_nt(a, b):
    return lax.dot_general(a, b, (((1,), (1,)), ((), ())), preferred_element_type=F32)


def _rms(x):
    return x * lax.rsqrt(jnp.mean(x * x, axis=-1, keepdims=True) + EPS)


def _gelu(x):
    return 0.5 * x * (1.0 + lax.erf(x * 0.7071067811865476))


def _pack_rows(x):
    c = x.shape[1] // 2
    bits = pltpu.bitcast(x.astype(BF16).astype(F32), U32)
    return (bits[:, :c] >> 16) | (bits[:, c:] & jnp.uint32(0xFFFF0000))


def _unpack_rows(w):
    return pltpu.bitcast(w << 16, F32), pltpu.bitcast(w & jnp.uint32(0xFFFF0000), F32)


def _params(n_axes=1):
    return pltpu.CompilerParams(dimension_semantics=("arbitrary",) * n_axes,
                                vmem_limit_bytes=VMEM_LIMIT)


def _resident(shape):
    nd = len(shape)
    return pl.BlockSpec(shape, lambda *_: (0,) * nd, pipeline_mode=pl.Buffered(1))


def _mem_kv_kernel(mem_ref, gsrc_ref, wk_ref, wv_ref, gk_ref, k_ref, v_ref):
    m = (_rms(mem_ref[...]) * gsrc_ref[...]).astype(BF16)
    kz = _dot(m, wk_ref[...])
    gk = gk_ref[...]
    for h in range(MEM_HEADS):
        sl = slice(h * MEM_HEAD_DIM, (h + 1) * MEM_HEAD_DIM)
        k_ref[:, sl] = _rms(kz[:, sl]) * gk
    v_ref[...] = _dot(m, wv_ref[...])


def _mem_kv(mem, g_src, w_mk, w_mv, g_mk):
    n = mem.shape[0]
    out = jax.ShapeDtypeStruct((n, MEM_WIDTH), F32)
    return pl.pallas_call(
        _mem_kv_kernel, out_shape=(out, out), name="mem_kv",
        compiler_params=pltpu.CompilerParams(vmem_limit_bytes=VMEM_LIMIT),
    )(mem, g_src, w_mk, w_mv, g_mk)


def _in_proj_kernel(x_ref, gmix_ref, w_ref, gsgu_ref, gqk_ref, seg_ref, segt_ref, cos_ref, sin_ref,
                    u_ref, va_ref, q_ref, k_ref, v_ref):
    xn = (_rms(x_ref[...]) * gmix_ref[...]).astype(BF16)
    u_ref[...] = _gelu(_dot(xn, w_ref[:, 0:A_WIDTH])).astype(BF16)
    va_ref[...] = _rms(_gelu(_dot(xn, w_ref[:, A_WIDTH:Q_OFF]))) * gsgu_ref[...]
    v_ref[...] = _dot(xn, w_ref[:, V_OFF:IN_WIDTH])

    qk = _dot(xn, w_ref[:, Q_OFF:V_OFF])
    ssq = _dot((qk * qk).astype(BF16), seg_ref[...])
    inv = lax.rsqrt(ssq * (1.0 / B_HEAD_DIM) + EPS)
    inv_hi = inv.astype(BF16)
    inv_lo = (inv - inv_hi.astype(F32)).astype(BF16)
    inv_b = _dot(inv_hi, segt_ref[...]) + _dot(inv_lo, segt_ref[...])
    qkn = (qk * inv_b) * gqk_ref[...]

    cos = cos_ref[...]
    sin = sin_ref[...]
    lane = lax.broadcasted_iota(I32, cos.shape, 1)
    first_half = (lane % B_HEAD_DIM) < (B_HEAD_DIM // 2)
    for g in range(QK_WIDTH // LANES):
        xg = qkn[:, g * LANES:(g + 1) * LANES]
        rot = jnp.where(first_half, pltpu.roll(xg, LANES - B_HEAD_DIM // 2, 1),
                        pltpu.roll(xg, B_HEAD_DIM // 2, 1))
        r = xg * cos + rot * sin
        if g < B_WIDTH // LANES:
            q_ref[:, g * LANES:(g + 1) * LANES] = (r * ATTN_SCALE).astype(BF16)
        else:
            k_ref[...] = r


def _in_proj(x, g_mix, w_in, g_sgu, g_qk, seg, segt, cos_tab, sin_tab, tile, name):
    t = x.shape[0]
    n = t // tile
    tab_tiles = cos_tab.shape[0] // tile
    row = lambda w: pl.BlockSpec((tile, w), lambda i: (i, 0))
    tab = pl.BlockSpec((tile, LANES), lambda i: (i % tab_tiles, 0))
    return pl.pallas_call(
        _in_proj_kernel, grid=(n,), name=name,
        in_specs=[row(D_MODEL), _resident(g_mix.shape), _resident(w_in.shape), _resident(g_sgu.shape),
                  _resident(g_qk.shape), _resident(seg.shape), _resident(segt.shape), tab, tab],
        out_specs=[row(A_WIDTH), row(A_WIDTH), row(B_WIDTH), row(KV_WIDTH), row(KV_WIDTH)],
        out_shape=(jax.ShapeDtypeStruct((t, A_WIDTH), BF16), jax.ShapeDtypeStruct((t, A_WIDTH), F32),
                   jax.ShapeDtypeStruct((t, B_WIDTH), BF16), jax.ShapeDtypeStruct((t, KV_WIDTH), F32),
                   jax.ShapeDtypeStruct((t, KV_WIDTH), F32)),
        compiler_params=_params(),
    )(x, g_mix, w_in, g_sgu, g_qk, seg, segt, cos_tab, sin_tab)


def _split_heads(ref_dst_lo, ref_dst_hi, rows, x, swap):
    lane = lax.broadcasted_iota(I32, x.shape, 1)
    low = lane < B_HEAD_DIM
    zero = jnp.zeros_like(x)
    ref_dst_lo[0, rows, :] = jnp.where(low, x, zero).astype(BF16)
    ref_dst_hi[0, rows, :] = jnp.where(low, zero, swap).astype(BF16)
    ref_dst_lo[1, rows, :] = jnp.where(low, swap, zero).astype(BF16)
    ref_dst_hi[1, rows, :] = jnp.where(low, zero, x).astype(BF16)


def _sgu_chunk(u, va, ws_ref, bias):
    lane = lax.broadcasted_iota(I32, (MLP_CHUNK, LANES), 1)
    low = lane < A_HEAD_DIM
    outs = []
    for p in range(A_WIDTH // LANES):
        sl = slice(p * LANES, (p + 1) * LANES)
        v2 = va[:, sl]
        zero = jnp.zeros_like(v2)
        mixed = (_dot(ws_ref[2 * p], jnp.where(low, v2, zero).astype(BF16))
                 + _dot(ws_ref[2 * p + 1], jnp.where(low, zero, v2).astype(BF16)))
        outs.append(u[:, sl].astype(F32) * (mixed + bias[:, sl]))
    return jnp.concatenate(outs, axis=1)


def _attend(q2, kl, kh, vl, vh, mask, sink_even, sink_odd):
    outs = []
    for kk, vv, sink in ((kl, vl, sink_even), (kh, vh, sink_odd)):
        s = _dot_nt(q2, kk)
        if mask is not None:
            s = jnp.where(mask, s, NEG_BIG)
        m = jnp.maximum(jnp.max(s, axis=-1, keepdims=True), sink)
        p = jnp.exp(s - m)
        den = jnp.sum(p, axis=-1, keepdims=True) + jnp.exp(sink - m)
        outs.append(_dot(p.astype(BF16), vv) * (1.0 / den))
    return outs[0] + outs[1]


def _out_proj(oa, ob, ga_ref, gb_ref, wout_ref, x):
    a = (_rms(oa) * ga_ref[...]).astype(BF16)
    b = (_rms(ob) * gb_ref[...]).astype(BF16)
    return x + _dot(a, wout_ref[0:A_WIDTH, :]) + _dot(b, wout_ref[A_WIDTH:, :])


def _mixer_prompt_kernel(tiles_per_seq, sinks_ref, x_ref, u_ref, va_ref, q_ref, kc_ref, vc_ref, kp_ref, vp_ref,
                         ws_ref, bias_ref, ga_ref, gb_ref, wout_ref, o_ref,
                         kl_s, kh_s, vl_s, vh_s, oa_s, ob_s):
    tile = x_ref.shape[0]
    n_sub = tile // MLP_CHUNK
    seq_start = (pl.program_id(0) % tiles_per_seq) == 0

    for src_p, src_c, dl, dh in ((kp_ref, kc_ref, kl_s, kh_s), (vp_ref, vc_ref, vl_s, vh_s)):
        prev = src_p[...]
        cur = src_c[...]
        _split_heads(dl, dh, slice(0, WINDOW), prev, pltpu.roll(prev, B_HEAD_DIM, 1))
        _split_heads(dl, dh, slice(WINDOW, WINDOW + tile), cur, pltpu.roll(cur, B_HEAD_DIM, 1))

    qc = lax.broadcasted_iota(I32, (MLP_CHUNK, 2 * MLP_CHUNK), 0) // CHUNK
    kc = lax.broadcasted_iota(I32, (MLP_CHUNK, 2 * MLP_CHUNK), 1) // CHUNK
    window_mask = (kc >= qc) & (kc <= qc + 2)
    bias = bias_ref[...]

    def sub(j, carry):
        r0 = pl.multiple_of(j * MLP_CHUNK, MLP_CHUNK)
        rows = pl.ds(r0, MLP_CHUNK)
        keys = pl.ds(r0, 2 * MLP_CHUNK)
        first_key_chunk = jnp.where(jnp.logical_and(seq_start, j == 0), 2, 0)
        mask = window_mask & (kc >= first_key_chunk)
        oa_s[rows, :] = _sgu_chunk(u_ref[rows, :], va_ref[rows, :], ws_ref, bias)
        for p in range(B_WIDTH // LANES):
            h = p // (B_WIDTH // LANES // B_KV_HEADS)
            sl = slice(p * LANES, (p + 1) * LANES)
            ob_s[rows, sl] = _attend(q_ref[rows, sl], kl_s[h, keys, :], kh_s[h, keys, :],
                                     vl_s[h, keys, :], vh_s[h, keys, :], mask,
                                     sinks_ref[2 * p], sinks_ref[2 * p + 1])
        return carry

    lax.fori_loop(0, n_sub, sub, 0)
    o_ref[...] = _out_proj(oa_s[...], ob_s[...], ga_ref, gb_ref, wout_ref, x_ref[...])


def _mixer_prompt(x, u, va, q, k, v, sinks, ws, bias, g_a, g_b, w_out, seq, tile):
    t = x.shape[0]
    n = t // tile
    tiles_per_seq = seq // tile
    per = tile // WINDOW
    row = lambda w: pl.BlockSpec((tile, w), lambda i, s: (i, 0))
    prev = pl.BlockSpec((WINDOW, KV_WIDTH), lambda i, s: (jnp.maximum(i * per - 1, 0), 0))
    res = lambda a: pl.BlockSpec(a.shape, lambda i, s: (0,) * a.ndim, pipeline_mode=pl.Buffered(1))
    grid_spec = pltpu.PrefetchScalarGridSpec(
        num_scalar_prefetch=1, grid=(n,),
        in_specs=[row(D_MODEL), row(A_WIDTH), row(A_WIDTH), row(B_WIDTH), row(KV_WIDTH), row(KV_WIDTH), prev, prev,
                  res(ws), res(bias), res(g_a), res(g_b), res(w_out)],
        out_specs=row(D_MODEL),
        scratch_shapes=[pltpu.VMEM((B_KV_HEADS, WINDOW + tile, LANES), BF16) for _ in range(4)]
        + [pltpu.VMEM((tile, A_WIDTH), F32), pltpu.VMEM((tile, B_WIDTH), F32)])
    return pl.pallas_call(
        functools.partial(_mixer_prompt_kernel, tiles_per_seq), grid_spec=grid_spec, name="mixer_prompt",
        out_shape=jax.ShapeDtypeStruct((t, D_MODEL), F32), compiler_params=_params(),
    )(sinks, x, u, va, q, k, v, k, v, ws, bias, g_a, g_b, w_out)


def _mixer_sample_kernel(dec_seq, sinks_ref, x_ref, u_ref, va_ref, q_ref, kc_ref, vc_ref, ck_ref, cv_ref,
                         ws_ref, bias_ref, ga_ref, gb_ref, wout_ref, o_ref,
                         kl_s, kh_s, vl_s, vh_s, ob_s):
    tile = x_ref.shape[0]
    n_seq = tile // dec_seq
    n_cache = ck_ref.shape[1]
    n_keys = n_cache + dec_seq
    oa = _sgu_chunk(u_ref[...], va_ref[...], ws_ref, bias_ref[...])
    for b in range(n_seq):
        rows = slice(b * dec_seq, (b + 1) * dec_seq)
        for src_c, src_n, dl, dh in ((ck_ref, kc_ref, kl_s, kh_s), (cv_ref, vc_ref, vl_s, vh_s)):
            old = src_c[b]
            new = src_n[rows, :]
            _split_heads(dl, dh, slice(b * n_keys, b * n_keys + n_cache), old, pltpu.roll(old, B_HEAD_DIM, 1))
            _split_heads(dl, dh, slice(b * n_keys + n_cache, (b + 1) * n_keys), new, pltpu.roll(new, B_HEAD_DIM, 1))
    q_seq = lax.broadcasted_iota(I32, (tile, n_seq * n_keys), 0) // dec_seq
    k_seq = lax.broadcasted_iota(I32, (tile, n_seq * n_keys), 1) // n_keys
    mask = q_seq == k_seq
    for p in range(B_WIDTH // LANES):
        h = p // (B_WIDTH // LANES // B_KV_HEADS)
        sl = slice(p * LANES, (p + 1) * LANES)
        ob_s[:, sl] = _attend(q_ref[:, sl], kl_s[h], kh_s[h], vl_s[h], vh_s[h], mask,
                              sinks_ref[2 * p], sinks_ref[2 * p + 1])
    o_ref[...] = _out_proj(oa, ob_s[...], ga_ref, gb_ref, wout_ref, x_ref[...])


def _mixer_sample(x, u, va, q, k, v, cache_k, cache_v, sinks, ws, bias, g_a, g_b, w_out, dec_seq, tile):
    t = x.shape[0]
    n = t // tile
    n_seq = tile // dec_seq
    n_cache = cache_k.shape[1]
    row = lambda w: pl.BlockSpec((tile, w), lambda i, s: (i, 0))
    cache = pl.BlockSpec((n_seq, n_cache, KV_WIDTH), lambda i, s: (i, 0, 0))
    res = lambda a: pl.BlockSpec(a.shape, lambda i, s: (0,) * a.ndim, pipeline_mode=pl.Buffered(1))
    grid_spec = pltpu.PrefetchScalarGridSpec(
        num_scalar_prefetch=1, grid=(n,),
        in_specs=[row(D_MODEL), row(A_WIDTH), row(A_WIDTH), row(B_WIDTH), row(KV_WIDTH), row(KV_WIDTH), cache, cache,
                  res(ws), res(bias), res(g_a), res(g_b), res(w_out)],
        out_specs=row(D_MODEL),
        scratch_shapes=[pltpu.VMEM((B_KV_HEADS, n_seq * (n_cache + dec_seq), LANES), BF16) for _ in range(4)]
        + [pltpu.VMEM((tile, B_WIDTH), F32)])
    return pl.pallas_call(
        functools.partial(_mixer_sample_kernel, dec_seq), grid_spec=grid_spec, name="mixer_sample",
        out_shape=jax.ShapeDtypeStruct((t, D_MODEL), F32), compiler_params=_params(),
    )(sinks, x, u, va, q, k, v, cache_k, cache_v, ws, bias, g_a, g_b, w_out)


def _memory_router_kernel(rows_per_mem, x_ref, mk_ref, mv_ref, gin_ref, wq_ref, gq_ref, wo_ref,
                          gmoe_ref, wrh_ref, wrl_ref, br_ref,
                          x2_ref, hn_ref, idx_ref, gate_ref, rank_ref, cnt_ref, o_s, base_s):
    tile = x_ref.shape[0]
    x = x_ref[...]
    qz = _dot((_rms(x) * gin_ref[...]).astype(BF16), wq_ref[...])
    gq = gq_ref[...]
    for h in range(MEM_HEADS):
        sl = slice(h * MEM_HEAD_DIM, (h + 1) * MEM_HEAD_DIM)
        qh = (_rms(qz[:, sl]) * gq).astype(BF16)
        for r in range(tile // rows_per_mem):
            rows = slice(r * rows_per_mem, (r + 1) * rows_per_mem)
            s = _dot_nt(qh[rows], mk_ref[r, :, sl].astype(BF16)) * MEM_SCALE
            p = jnp.exp(s - jnp.max(s, axis=-1, keepdims=True))
            den = jnp.sum(p, axis=-1, keepdims=True)
            o_s[rows, sl] = _dot(p.astype(BF16), mv_ref[r, :, sl].astype(BF16)) * (1.0 / den)
    x2 = x + _dot(o_s[...].astype(BF16), wo_ref[...])
    x2_ref[...] = x2
    hn = _rms(x2) * gmoe_ref[...]
    hn_ref[...] = _pack_rows(hn)

    hn_hi = hn.astype(BF16)
    hn_lo = (hn - hn_hi.astype(F32)).astype(BF16)
    logits = ((_dot(hn_lo, wrh_ref[...]) + _dot(hn_hi, wrl_ref[...])) + _dot(hn_hi, wrh_ref[...])) + br_ref[...]
    lane = lax.broadcasted_iota(I32, (tile, LANES), 1)
    lane_f = lane.astype(F32)
    work = jnp.where(lane < N_EXPERTS, logits, -jnp.inf)
    idx_out = jnp.zeros((tile, LANES), F32)
    val_out = jnp.zeros((tile, LANES), F32)
    hot = jnp.zeros((tile, LANES), F32)
    top = None
    picks = []
    for k in range(TOP_K):
        m = jnp.max(work, axis=-1, keepdims=True)
        pick = jnp.min(jnp.where(work == m, lane_f, float(LANES)), axis=-1, keepdims=True)
        chosen = lane_f == pick
        if top is None:
            top = m
        idx_out = jnp.where(lane == k, pick, idx_out)
        val_out = jnp.where(lane == k, jnp.exp(m - top), val_out)
        hot = jnp.where(chosen, 1.0, hot)
        work = jnp.where(chosen, -jnp.inf, work)
        picks.append(chosen)
    idx_ref[...] = idx_out.astype(I32)
    gate_ref[...] = val_out * (1.0 / jnp.sum(val_out, axis=-1, keepdims=True))

    @pl.when(pl.program_id(0) == 0)
    def _():
        base_s[...] = jnp.zeros_like(base_s)

    earlier = (lax.broadcasted_iota(I32, (tile, tile), 1) < lax.broadcasted_iota(I32, (tile, tile), 0))
    pos = _dot(jnp.where(earlier, 1.0, 0.0).astype(BF16), hot.astype(BF16)) + base_s[0:1, :]
    rank_out = jnp.zeros((tile, LANES), F32)
    for k in range(TOP_K):
        rank_out = jnp.where(lane == k, jnp.sum(jnp.where(picks[k], pos, 0.0), axis=-1, keepdims=True), rank_out)
    rank_ref[...] = rank_out.astype(I32)
    total = base_s[0:1, :] + jnp.sum(hot, axis=0, keepdims=True)
    base_s[...] = jnp.broadcast_to(total, base_s.shape)
    cnt_ref[...] = jnp.broadcast_to(total, cnt_ref.shape).astype(I32)


def _memory_router(x, mk, mv, rows_per_mem, g_in, w_mq, g_mq, w_mo, g_moe, w_r_hi, w_r_lo, b_r, tile, name):
    t = x.shape[0]
    n = t // tile
    mems = tile // rows_per_mem if rows_per_mem <= tile else 1
    per_mem_tiles = max(rows_per_mem // tile, 1)
    rpm = min(rows_per_mem, tile)
    row = lambda w: pl.BlockSpec((tile, w), lambda i: (i, 0))
    mem = pl.BlockSpec((mems, mk.shape[1], MEM_WIDTH), lambda i: (i // per_mem_tiles, 0, 0))
    res = lambda a: pl.BlockSpec(a.shape, lambda i: (0,) * a.ndim, pipeline_mode=pl.Buffered(1))
    small = jax.ShapeDtypeStruct((t, LANES), I32)
    return pl.pallas_call(
        functools.partial(_memory_router_kernel, rpm), grid=(n,), name=name,
        in_specs=[row(D_MODEL), mem, mem, res(g_in), res(w_mq), res(g_mq), res(w_mo), res(g_moe), res(w_r_hi), res(w_r_lo), res(b_r)],
        out_specs=[row(D_MODEL), row(D_MODEL // 2), row(LANES), row(LANES), row(LANES),
                   pl.BlockSpec((8, LANES), lambda i: (0, 0))],
        out_shape=(jax.ShapeDtypeStruct((t, D_MODEL), F32), jax.ShapeDtypeStruct((t, D_MODEL // 2), U32),
                   small, jax.ShapeDtypeStruct((t, LANES), F32), small, jax.ShapeDtypeStruct((8, LANES), I32)),
        scratch_shapes=[pltpu.VMEM((tile, MEM_WIDTH), F32), pltpu.VMEM((8, LANES), F32)],
        compiler_params=_params(),
    )(x, mk, mv, g_in, w_mq, g_mq, w_mo, g_moe, w_r_hi, w_r_lo, b_r)


def _dispatch_kernel(dest_ref, hn_ref, *rest):
    xb_ref, sem = rest[-2:]
    groups = hn_ref.shape[0]

    def copy(j, u, k):
        return pltpu.make_async_copy(hn_ref.at[j, pl.ds(u, 1), :],
                                     xb_ref.at[pl.ds(dest_ref[0, j * (SUBLANES * TOP_K) + u * TOP_K + k], 1), :], sem)

    def start(j, carry):
        for u in range(SUBLANES):
            for k in range(TOP_K):
                copy(j, u, k).start()
        return carry

    def wait(j, carry):
        for u in range(SUBLANES):
            for k in range(TOP_K):
                copy(j, u, k).wait()
        return carry

    lax.fori_loop(0, groups, start, 0)
    lax.fori_loop(0, groups, wait, 0)


def _dispatch(hn, dest, xb, n_rows, tile):
    t = hn.shape[0]
    n = t // tile
    dest = dest.reshape(n, 1, tile * TOP_K)
    in_specs = [pl.BlockSpec((None, 1, tile * TOP_K), lambda i: (i, 0, 0), memory_space=pltpu.SMEM),
                pl.BlockSpec((tile // SUBLANES, SUBLANES, hn.shape[1]), lambda i: (i, 0, 0))]
    args = [dest, hn.reshape(t // SUBLANES, SUBLANES, hn.shape[1])]
    if xb is not None:
        in_specs.append(pl.BlockSpec(memory_space=pl.ANY))
        args.append(xb)
    return pl.pallas_call(
        _dispatch_kernel, grid=(n,), name="dispatch",
        in_specs=in_specs,
        out_specs=pl.BlockSpec(memory_space=pl.ANY),
        out_shape=jax.ShapeDtypeStruct((n_rows, hn.shape[1]), hn.dtype),
        scratch_shapes=[pltpu.SemaphoreType.DMA(())],
        input_output_aliases={} if xb is None else {2: 0},
        compiler_params=_params(),
    )(*args)


def _for_pairs(n, body):
    def pair(j, carry):
        body(2 * j)
        body(2 * j + 1)
        return carry
    lax.fori_loop(0, n // 2, pair, 0)
    pl.when(n % 2 == 1)(lambda: body(n - 1))


def _experts_kernel(be_ref, nv_ref, xi_ref, x_ref, wg_ref, wu_ref, bg_ref, bu_ref, wd_ref, bd_ref, y_ref, h_s):
    del be_ref, xi_ref
    b = pl.program_id(0)
    s = pl.program_id(1)
    n_ff = D_FF // FF_TILE
    valid = nv_ref[b]
    n_sub = (valid + EXPERT_SUB - 1) // EXPERT_SUB

    @pl.when(s < n_ff)
    def _():
        bg = bg_ref[0]
        bu = bu_ref[0]

        def up(i):
            rows = pl.ds(pl.multiple_of(i * EXPERT_SUB, EXPERT_SUB), EXPERT_SUB)
            keep = (i * EXPERT_SUB + lax.broadcasted_iota(I32, (EXPERT_SUB, 1), 0)) < valid
            lo, hi = _unpack_rows(x_ref[rows, :])
            xs = jnp.concatenate([jnp.where(keep, lo, 0.0), jnp.where(keep, hi, 0.0)], axis=1)
            gate = jnp.minimum(_dot(xs, wg_ref[0]) + bg, SWIGLU_LIMIT)
            lin = jnp.clip(_dot(xs, wu_ref[0]) + bu, -SWIGLU_LIMIT, SWIGLU_LIMIT)
            act = (lin + 1.0) * (gate * (1.0 / (1.0 + jnp.exp(-SWIGLU_ALPHA * gate))))
            h_s[s, rows, :] = act.astype(BF16)
        _for_pairs(n_sub, up)

    @pl.when(s >= n_ff)
    def _():
        bd = bd_ref[0]

        def down(i):
            rows = pl.ds(pl.multiple_of(i * EXPERT_SUB, EXPERT_SUB), EXPERT_SUB)
            h = jnp.concatenate([h_s[f, rows, :] for f in range(n_ff)], axis=1)
            y_ref[rows, :] = _pack_rows(_dot(h.astype(F32), wd_ref[0]) + bd)
        _for_pairs(n_sub, down)


def _experts(xb, block_expert, block_valid, block_index, w_gu, b_gu, w_d, b_d):
    n_blocks = xb.shape[0] // EXPERT_ROWS
    n_ff = D_FF // FF_TILE
    n_out = D_MODEL // OUT_TILE

    def ff(b, s, nv):
        return jnp.where(nv[b] > 0, jnp.minimum(s, n_ff - 1), n_ff - 1)

    def oc(b, s, nv):
        return jnp.where(nv[b] > 0, jnp.maximum(s - n_ff, 0), n_out - 1)

    b_gu3 = b_gu.reshape(N_EXPERTS, 1, 2 * D_FF)
    b_d3 = b_d.reshape(N_EXPERTS, 1, D_MODEL)
    grid_spec = pltpu.PrefetchScalarGridSpec(
        num_scalar_prefetch=3, grid=(n_blocks, n_ff + n_out),
        in_specs=[
            pl.BlockSpec((EXPERT_ROWS, D_MODEL // 2), lambda b, s, be, nv, xi: (xi[b], 0)),
            pl.BlockSpec((1, D_MODEL, FF_TILE), lambda b, s, be, nv, xi: (be[b], 0, ff(b, s, nv))),
            pl.BlockSpec((1, D_MODEL, FF_TILE), lambda b, s, be, nv, xi: (be[b], 0, ff(b, s, nv) + n_ff)),
            pl.BlockSpec((1, 1, FF_TILE), lambda b, s, be, nv, xi: (be[b], 0, ff(b, s, nv))),
            pl.BlockSpec((1, 1, FF_TILE), lambda b, s, be, nv, xi: (be[b], 0, ff(b, s, nv) + n_ff)),
            pl.BlockSpec((1, D_FF, OUT_TILE), lambda b, s, be, nv, xi: (be[b], 0, oc(b, s, nv))),
            pl.BlockSpec((1, 1, OUT_TILE), lambda b, s, be, nv, xi: (be[b], 0, oc(b, s, nv))),
        ],
        out_specs=pl.BlockSpec((EXPERT_ROWS, OUT_TILE // 2), lambda b, s, be, nv, xi: (xi[b], oc(b, s, nv))),
        scratch_shapes=[pltpu.VMEM((n_ff, EXPERT_ROWS, FF_TILE), BF16)])
    return pl.pallas_call(
        _experts_kernel, grid_spec=grid_spec, name="experts",
        out_shape=jax.ShapeDtypeStruct((n_blocks * EXPERT_ROWS, D_MODEL // 2), U32),
        compiler_params=_params(2),
    )(block_expert, block_valid, block_index, xb, w_gu, w_gu, b_gu3, b_gu3, w_d, b_d3)


def _combine_kernel(n_tiles, dest_ref, next_ref, x_ref, gate_ref, yb_ref, o_ref, buf, sem):
    tile = x_ref.shape[0]
    i = pl.program_id(0)
    slot = i % 2

    groups = tile // SUBLANES

    def copy(idx_ref, s, j, u, k):
        return pltpu.make_async_copy(yb_ref.at[pl.ds(idx_ref[0, j * (SUBLANES * TOP_K) + u * TOP_K + k], 1), :],
                                     buf.at[s, k, j, pl.ds(u, 1), :], sem.at[s])

    def start_tile(idx_ref, s):
        def start(j, carry):
            for u in range(SUBLANES):
                for k in range(TOP_K):
                    copy(idx_ref, s, j, u, k).start()
            return carry
        lax.fori_loop(0, groups, start, 0)

    pl.when(i == 0)(lambda: start_tile(dest_ref, 0))
    pl.when(i + 1 < n_tiles)(lambda: start_tile(next_ref, 1 - slot))

    def wait(j, carry):
        for u in range(SUBLANES):
            for k in range(TOP_K):
                copy(dest_ref, slot, j, u, k).wait()
        return carry
    lax.fori_loop(0, groups, wait, 0)

    gate = gate_ref[...]
    half = OUT_TILE // 2
    for g in range(D_MODEL // OUT_TILE):
        c0 = g * OUT_TILE
        acc_lo = x_ref[:, c0:c0 + half]
        acc_hi = x_ref[:, c0 + half:c0 + OUT_TILE]
        for k in range(TOP_K):
            lo, hi = _unpack_rows(buf[slot, k, :, :, g * half:(g + 1) * half].reshape(tile, half))
            acc_lo = acc_lo + gate[:, k:k + 1] * lo
            acc_hi = acc_hi + gate[:, k:k + 1] * hi
        o_ref[:, c0:c0 + half] = acc_lo
        o_ref[:, c0 + half:c0 + OUT_TILE] = acc_hi


def _combine(x, gate, dest, yb, tile):
    t = x.shape[0]
    n = t // tile
    dest = dest.reshape(n, 1, tile * TOP_K)
    return pl.pallas_call(
        functools.partial(_combine_kernel, n), grid=(n,), name="combine",
        in_specs=[pl.BlockSpec((None, 1, tile * TOP_K), lambda i: (i, 0, 0), memory_space=pltpu.SMEM),
                  pl.BlockSpec((None, 1, tile * TOP_K), lambda i: (jnp.minimum(i + 1, n - 1), 0, 0),
                               memory_space=pltpu.SMEM),
                  pl.BlockSpec((tile, D_MODEL), lambda i: (i, 0)),
                  pl.BlockSpec((tile, LANES), lambda i: (i, 0)),
                  pl.BlockSpec(memory_space=pl.ANY)],
        out_specs=pl.BlockSpec((tile, D_MODEL), lambda i: (i, 0)),
        out_shape=jax.ShapeDtypeStruct((t, D_MODEL), F32),
        scratch_shapes=[pltpu.VMEM((2, TOP_K, tile // SUBLANES, SUBLANES, D_MODEL // 2), U32),
                        pltpu.SemaphoreType.DMA((2,))],
        compiler_params=_params(),
    )(dest, dest, x, gate, yb)


def _rope_tables(pos):
    half = B_HEAD_DIM // 2
    inv_freq = ROPE_THETA ** (-jnp.arange(half, dtype=F32) / half)
    ang = pos.astype(F32)[:, None] * inv_freq[None, :]
    cos = jnp.cos(ang)
    sin = jnp.sin(ang)
    return jnp.tile(cos, (1, LANES // half)), jnp.tile(jnp.concatenate([-sin, sin], axis=1), (1, LANES // B_HEAD_DIM))


def _layer(l, xp, xs, cache_swa_k, cache_swa_v, cache_mem_k, cache_mem_v, mem_prompt,
           g_mix, w_in, g_sgu, w_s, b_s, g_q, g_k, sinks, g_out_a, g_out_b, w_out,
           g_mem_in, g_mem_src, w_mq, w_mk, w_mv, g_mq, g_mk, w_mo,
           g_moe, w_router, b_router, w_gate_up, b_gate_up, w_down, b_down):
    n_b, seq, _ = xp.shape
    n_db, dec_seq, _ = xs.shape
    tp = n_b * seq
    ts = n_db * dec_seq
    tile = min(TOKEN_TILE, seq)
    row = lambda a: a[l].reshape(1, -1)

    xp2 = xp.reshape(tp, D_MODEL)
    xs2 = xs.reshape(ts, D_MODEL)
    head = jnp.arange(QK_WIDTH, dtype=I32) // B_HEAD_DIM
    seg = (head[:, None] == jnp.arange(LANES, dtype=I32)[None, :]).astype(BF16)
    g_qk = jnp.concatenate([jnp.tile(g_q[l], B_HEADS), jnp.tile(g_k[l], B_KV_HEADS)]).reshape(1, -1)
    w_in_b = w_in[l].astype(BF16)
    cos_p, sin_p = _rope_tables(jnp.arange(seq, dtype=I32))
    cos_s, sin_s = _rope_tables(PAST_LEN + jnp.arange(SAMPLE_TILE, dtype=I32) % dec_seq)
    up, vap, qp, kp, vp = _in_proj(xp2, row(g_mix), w_in_b, row(g_sgu), g_qk, seg, seg.T, cos_p, sin_p,
                                   tile, "in_proj_prompt")
    us, vas, qs, ks, vs = _in_proj(xs2, row(g_mix), w_in_b, row(g_sgu), g_qk, seg, seg.T, cos_s, sin_s,
                                   SAMPLE_TILE, "in_proj_sample")

    w_out_b = w_out[l].astype(BF16)
    tri = jnp.tril(jnp.ones((MLP_CHUNK, MLP_CHUNK), bool))
    ws_p = jnp.where(tri[None], w_s[l], 0.0).astype(BF16)
    bias_p = jnp.repeat(b_s[l].T, A_HEAD_DIM, axis=1)
    x1p = _mixer_prompt(xp2, up, vap, qp, kp, vp, sinks[l], ws_p, bias_p, row(g_out_a), row(g_out_b), w_out_b, seq, tile)

    reps = SAMPLE_TILE // dec_seq
    tri_s = jnp.tril(jnp.ones((dec_seq, dec_seq), bool))
    ws_small = jnp.where(tri_s[None], w_s[l][:, :dec_seq, :dec_seq], 0.0)
    ws_s = jnp.einsum("ab,hts->hatbs", jnp.eye(reps, dtype=F32), ws_small).reshape(A_HEADS, SAMPLE_TILE, SAMPLE_TILE).astype(BF16)
    bias_s = jnp.tile(jnp.repeat(b_s[l][:, :dec_seq].T, A_HEAD_DIM, axis=1), (reps, 1))
    ck = cache_swa_k[l].reshape(n_db, -1, KV_WIDTH)
    cv = cache_swa_v[l].reshape(n_db, -1, KV_WIDTH)
    x1s = _mixer_sample(xs2, us, vas, qs, ks, vs, ck, cv, sinks[l], ws_s, bias_s,
                        row(g_out_a), row(g_out_b), w_out_b, dec_seq, SAMPLE_TILE)

    mk_p, mv_p = _mem_kv(mem_prompt.reshape(-1, D_MODEL), row(g_mem_src), w_mk[l].astype(BF16), w_mv[l].astype(BF16), row(g_mk))
    n_mem = mem_prompt.shape[1]
    w_r = jnp.pad(w_router[l], ((0, 0), (0, LANES - N_EXPERTS)))
    b_r = jnp.pad(b_router[l], (0, LANES - N_EXPERTS)).reshape(1, -1)
    w_r_hi = w_r.astype(BF16)
    w_r_lo = (w_r - w_r_hi.astype(F32)).astype(BF16)
    mem_args = (row(g_mem_in), w_mq[l].astype(BF16), row(g_mq), w_mo[l].astype(BF16), row(g_moe), w_r_hi, w_r_lo, b_r)
    x2p, hnp, idxp, gatep, rankp, cntp = _memory_router(
        x1p, mk_p.reshape(n_b, n_mem, MEM_WIDTH), mv_p.reshape(n_b, n_mem, MEM_WIDTH),
        seq, *mem_args, tile, "memory_router_prompt")
    x2s, hns, idxs, gates, ranks, cnts = _memory_router(
        x1s, cache_mem_k[l].reshape(n_db, -1, MEM_WIDTH), cache_mem_v[l].reshape(n_db, -1, MEM_WIDTH),
        dec_seq, *mem_args, SAMPLE_TILE, "memory_router_sample")

    cnt_p = cntp[0, :N_EXPERTS]
    cnt_s = cnts[0, :N_EXPERTS]
    total = cnt_p + cnt_s
    nblk = (total + EXPERT_ROWS - 1) // EXPERT_ROWS
    blk_end = jnp.cumsum(nblk)
    row_start = (blk_end - nblk) * EXPERT_ROWS
    n_blocks = (tp + ts) * TOP_K // EXPERT_ROWS + N_EXPERTS
    bidx = jnp.arange(n_blocks, dtype=I32)
    used = bidx < blk_end[-1]
    last = jnp.maximum(blk_end[-1] - 1, 0)
    bsafe = jnp.minimum(bidx, last)
    block_expert = jnp.minimum(jnp.searchsorted(blk_end, bsafe, side="right"), N_EXPERTS - 1).astype(I32)
    within = bsafe - (blk_end - nblk)[block_expert]
    block_valid = jnp.where(used, jnp.clip(total[block_expert] - within * EXPERT_ROWS, 0, EXPERT_ROWS), 0).astype(I32)
    dest_p = row_start[idxp[:, :TOP_K]] + rankp[:, :TOP_K]
    dest_s = (row_start + cnt_p)[idxs[:, :TOP_K]] + ranks[:, :TOP_K]

    xb = _dispatch(hnp, dest_p, None, n_blocks * EXPERT_ROWS, min(ROW_TILE, tp))
    xb = _dispatch(hns, dest_s, xb, n_blocks * EXPERT_ROWS, min(ROW_TILE, ts))
    yb = _experts(xb, block_expert, block_valid, bsafe.astype(I32), w_gate_up[l], b_gate_up[l], w_down[l], b_down[l])
    yp = _combine(x2p, gatep, dest_p, yb, min(ROW_TILE, tp))
    ys = _combine(x2s, gates, dest_s, yb, min(ROW_TILE, ts))

    new = dict(
        swa_k_p=kp.reshape(n_b, seq, B_KV_HEADS, B_HEAD_DIM)[:, seq - WINDOW:],
        swa_v_p=vp.reshape(n_b, seq, B_KV_HEADS, B_HEAD_DIM)[:, seq - WINDOW:],
        mem_k_p=mk_p.reshape(n_b, n_mem, MEM_HEADS, MEM_HEAD_DIM),
        mem_v_p=mv_p.reshape(n_b, n_mem, MEM_HEADS, MEM_HEAD_DIM),
        swa_k_s=ks.reshape(n_db, dec_seq, B_KV_HEADS, B_HEAD_DIM),
        swa_v_s=vs.reshape(n_db, dec_seq, B_KV_HEADS, B_HEAD_DIM),
        sgu_v_s=vas.reshape(n_db, dec_seq, A_HEADS, A_HEAD_DIM))
    return yp.reshape(n_b, seq, D_MODEL), ys.reshape(n_db, dec_seq, D_MODEL), new


def kernel(x_prompt, x_sample, cache_swa_k, cache_swa_v, cache_mem_k, cache_mem_v, mem_prompt, g_mix, w_in, g_sgu, w_s, b_s, g_q, g_k, sinks, g_out_a, g_out_b, w_out, g_mem_in, g_mem_src, w_mq, w_mk, w_mv, g_mq, g_mk, w_mo, g_moe, w_router, b_router, w_gate_up, b_gate_up, w_down, b_down):
    xp, xs = x_prompt, x_sample
    news = []
    for l in range(g_mix.shape[0]):
        xp, xs, new = _layer(l, xp, xs, cache_swa_k, cache_swa_v, cache_mem_k, cache_mem_v, mem_prompt,
                             g_mix, w_in, g_sgu, w_s, b_s, g_q, g_k, sinks, g_out_a, g_out_b, w_out,
                             g_mem_in, g_mem_src, w_mq, w_mk, w_mv, g_mq, g_mk, w_mo,
                             g_moe, w_router, b_router, w_gate_up, b_gate_up, w_down, b_down)
        news.append(new)
    stack = lambda name: jnp.stack([n[name] for n in news], 0)
    return (xp, xs, stack("swa_k_p"), stack("swa_v_p"), stack("mem_k_p"), stack("mem_v_p"),
            stack("swa_k_s"), stack("swa_v_s"), stack("sgu_v_s"))
```

```python
import functools

import jax
import jax.numpy as jnp
from jax import lax
from jax.experimental import pallas as pl
from jax.experimental.pallas import tpu as pltpu

F32 = jnp.float32
BF16 = jnp.bfloat16
I32 = jnp.int32
U32 = jnp.uint32

D_MODEL = 2048
PAST_LEN = 2048
CHUNK = 64
EPS = 1e-6
A_HEADS = 16
A_HEAD_DIM = 64
A_WIDTH = A_HEADS * A_HEAD_DIM
MLP_CHUNK = 128
B_HEADS = 16
B_KV_HEADS = 2
B_HEAD_DIM = 64
B_WIDTH = B_HEADS * B_HEAD_DIM
KV_WIDTH = B_KV_HEADS * B_HEAD_DIM
WINDOW = 128
ROPE_THETA = 10000.0
ATTN_SCALE = B_HEAD_DIM ** -0.5
Q_OFF = 2 * A_WIDTH
K_OFF = Q_OFF + B_WIDTH
V_OFF = K_OFF + KV_WIDTH
IN_WIDTH = V_OFF + KV_WIDTH
QK_WIDTH = B_WIDTH + KV_WIDTH
MEM_HEADS = 4
MEM_HEAD_DIM = 128
MEM_WIDTH = MEM_HEADS * MEM_HEAD_DIM
MEM_SCALE = MEM_HEAD_DIM ** -0.5
N_EXPERTS = 32
TOP_K = 4
D_FF = D_MODEL
SWIGLU_LIMIT = 7.0
SWIGLU_ALPHA = 1.702
NEG_BIG = -1e30

LANES = 128
SUBLANES = 8
VMEM_LIMIT = 56 * 1024 * 1024

TOKEN_TILE = 512
SAMPLE_TILE = 128
EXPERT_ROWS = 1280
EXPERT_SUB = 320
FF_TILE = 512
OUT_TILE = 1024
ROW_TILE = 256


def _dot(a, b):
    return jnp.dot(a, b, preferred_element_type=F32)


def _dot_nt(a, b):
    return lax.dot_general(a, b, (((1,), (1,)), ((), ())), preferred_element_type=F32)


def _rms(x):
    return x * lax.rsqrt(jnp.mean(x * x, axis=-1, keepdims=True) + EPS)


def _gelu(x):
    return 0.5 * x * (1.0 + lax.erf(x * 0.7071067811865476))


def _pack_rows(x):
    c = x.shape[1] // 2
    bits = pltpu.bitcast(x.astype(BF16).astype(F32), U32)
    return (bits[:, :c] >> 16) | (bits[:, c:] & jnp.uint32(0xFFFF0000))


def _unpack_rows(w):
    return pltpu.bitcast(w << 16, F32), pltpu.bitcast(w & jnp.uint32(0xFFFF0000), F32)


def _params(n_axes=1):
    return pltpu.CompilerParams(dimension_semantics=("arbitrary",) * n_axes,
                                vmem_limit_bytes=VMEM_LIMIT)


def _resident(shape):
    nd = len(shape)
    return pl.BlockSpec(shape, lambda *_: (0,) * nd, pipeline_mode=pl.Buffered(1))


def _mem_kv_kernel(mem_ref, gsrc_ref, wk_ref, wv_ref, gk_ref, k_ref, v_ref):
    m = (_rms(mem_ref[...]) * gsrc_ref[...]).astype(BF16)
    kz = _dot(m, wk_ref[...])
    gk = gk_ref[...]
    for h in range(MEM_HEADS):
        sl = slice(h * MEM_HEAD_DIM, (h + 1) * MEM_HEAD_DIM)
        k_ref[:, sl] = _rms(kz[:, sl]) * gk
    v_ref[...] = _dot(m, wv_ref[...])


def _mem_kv(mem, g_src, w_mk, w_mv, g_mk):
    n = mem.shape[0]
    out = jax.ShapeDtypeStruct((n, MEM_WIDTH), F32)
    return pl.pallas_call(
        _mem_kv_kernel, out_shape=(out, out), name="mem_kv",
        compiler_params=pltpu.CompilerParams(vmem_limit_bytes=VMEM_LIMIT),
    )(mem, g_src, w_mk, w_mv, g_mk)


def _in_proj_kernel(x_ref, gmix_ref, w_ref, gsgu_ref, gqk_ref, seg_ref, segt_ref, cos_ref, sin_ref,
                    u_ref, va_ref, q_ref, k_ref, v_ref):
    xn = (_rms(x_ref[...]) * gmix_ref[...]).astype(BF16)
    u_ref[...] = _gelu(_dot(xn, w_ref[:, 0:A_WIDTH])).astype(BF16)
    va_ref[...] = _rms(_gelu(_dot(xn, w_ref[:, A_WIDTH:Q_OFF]))) * gsgu_ref[...]
    v_ref[...] = _dot(xn, w_ref[:, V_OFF:IN_WIDTH])

    qk = _dot(xn, w_ref[:, Q_OFF:V_OFF])
    ssq = _dot((qk * qk).astype(BF16), seg_ref[...])
    inv = lax.rsqrt(ssq * (1.0 / B_HEAD_DIM) + EPS)
    inv_hi = inv.astype(BF16)
    inv_lo = (inv - inv_hi.astype(F32)).astype(BF16)
    inv_b = _dot(inv_hi, segt_ref[...]) + _dot(inv_lo, segt_ref[...])
    qkn = (qk * inv_b) * gqk_ref[...]

    cos = cos_ref[...]
    sin = sin_ref[...]
    lane = lax.broadcasted_iota(I32, cos.shape, 1)
    first_half = (lane % B_HEAD_DIM) < (B_HEAD_DIM // 2)
    for g in range(QK_WIDTH // LANES):
        xg = qkn[:, g * LANES:(g + 1) * LANES]
        rot = jnp.where(first_half, pltpu.roll(xg, LANES - B_HEAD_DIM // 2, 1),
                        pltpu.roll(xg, B_HEAD_DIM // 2, 1))
        r = xg * cos + rot * sin
        if g < B_WIDTH // LANES:
            q_ref[:, g * LANES:(g + 1) * LANES] = (r * ATTN_SCALE).astype(BF16)
        else:
            k_ref[...] = r


def _in_proj(x, g_mix, w_in, g_sgu, g_qk, seg, segt, cos_tab, sin_tab, tile, name):
    t = x.shape[0]
    n = t // tile
    tab_tiles = cos_tab.shape[0] // tile
    row = lambda w: pl.BlockSpec((tile, w), lambda i: (i, 0))
    tab = pl.BlockSpec((tile, LANES), lambda i: (i % tab_tiles, 0))
    return pl.pallas_call(
        _in_proj_kernel, grid=(n,), name=name,
        in_specs=[row(D_MODEL), _resident(g_mix.shape), _resident(w_in.shape), _resident(g_sgu.shape),
                  _resident(g_qk.shape), _resident(seg.shape), _resident(segt.shape), tab, tab],
        out_specs=[row(A_WIDTH), row(A_WIDTH), row(B_WIDTH), row(KV_WIDTH), row(KV_WIDTH)],
        out_shape=(jax.ShapeDtypeStruct((t, A_WIDTH), BF16), jax.ShapeDtypeStruct((t, A_WIDTH), F32),
                   jax.ShapeDtypeStruct((t, B_WIDTH), BF16), jax.ShapeDtypeStruct((t, KV_WIDTH), F32),
                   jax.ShapeDtypeStruct((t, KV_WIDTH), F32)),
        compiler_params=_params(),
    )(x, g_mix, w_in, g_sgu, g_qk, seg, segt, cos_tab, sin_tab)


def _split_heads(ref_dst_lo, ref_dst_hi, rows, x, swap):
    lane = lax.broadcasted_iota(I32, x.shape, 1)
    low = lane < B_HEAD_DIM
    zero = jnp.zeros_like(x)
    ref_dst_lo[0, rows, :] = jnp.where(low, x, zero).astype(BF16)
    ref_dst_hi[0, rows, :] = jnp.where(low, zero, swap).astype(BF16)
    ref_dst_lo[1, rows, :] = jnp.where(low, swap, zero).astype(BF16)
    ref_dst_hi[1, rows, :] = jnp.where(low, zero, x).astype(BF16)


def _sgu_chunk(u, va, ws_ref, bias):
    lane = lax.broadcasted_iota(I32, (MLP_CHUNK, LANES), 1)
    low = lane < A_HEAD_DIM
    outs = []
    for p in range(A_WIDTH // LANES):
        sl = slice(p * LANES, (p + 1) * LANES)
        v2 = va[:, sl]
        zero = jnp.zeros_like(v2)
        mixed = (_dot(ws_ref[2 * p], jnp.where(low, v2, zero).astype(BF16))
                 + _dot(ws_ref[2 * p + 1], jnp.where(low, zero, v2).astype(BF16)))
        outs.append(u[:, sl].astype(F32) * (mixed + bias[:, sl]))
    return jnp.concatenate(outs, axis=1)


def _attend(q2, kl, kh, vl, vh, mask, sink_even, sink_odd):
    outs = []
    for kk, vv, sink in ((kl, vl, sink_even), (kh, vh, sink_odd)):
        s = _dot_nt(q2, kk)
        if mask is not None:
            s = jnp.where(mask, s, NEG_BIG)
        m = jnp.maximum(jnp.max(s, axis=-1, keepdims=True), sink)
        p = jnp.exp(s - m)
        den = jnp.sum(p, axis=-1, keepdims=True) + jnp.exp(sink - m)
        outs.append(_dot(p.astype(BF16), vv) * (1.0 / den))
    return outs[0] + outs[1]


def _out_proj(oa, ob, ga_ref, gb_ref, wout_ref, x):
    a = (_rms(oa) * ga_ref[...]).astype(BF16)
    b = (_rms(ob) * gb_ref[...]).astype(BF16)
    return x + _dot(a, wout_ref[0:A_WIDTH, :]) + _dot(b, wout_ref[A_WIDTH:, :])


def _mixer_prompt_kernel(tiles_per_seq, sinks_ref, x_ref, u_ref, va_ref, q_ref, kc_ref, vc_ref, kp_ref, vp_ref,
                         ws_ref, bias_ref, ga_ref, gb_ref, wout_ref, o_ref,
                         kl_s, kh_s, vl_s, vh_s, oa_s, ob_s):
    tile = x_ref.shape[0]
    n_sub = tile // MLP_CHUNK
    seq_start = (pl.program_id(0) % tiles_per_seq) == 0

    for src_p, src_c, dl, dh in ((kp_ref, kc_ref, kl_s, kh_s), (vp_ref, vc_ref, vl_s, vh_s)):
        prev = src_p[...]
        cur = src_c[...]
        _split_heads(dl, dh, slice(0, WINDOW), prev, pltpu.roll(prev, B_HEAD_DIM, 1))
        _split_heads(dl, dh, slice(WINDOW, WINDOW + tile), cur, pltpu.roll(cur, B_HEAD_DIM, 1))

    qc = lax.broadcasted_iota(I32, (MLP_CHUNK, 2 * MLP_CHUNK), 0) // CHUNK
    kc = lax.broadcasted_iota(I32, (MLP_CHUNK, 2 * MLP_CHUNK), 1) // CHUNK
    window_mask = (kc >= qc) & (kc <= qc + 2)
    bias = bias_ref[...]

    def sub(j, carry):
        r0 = pl.multiple_of(j * MLP_CHUNK, MLP_CHUNK)
        rows = pl.ds(r0, MLP_CHUNK)
        keys = pl.ds(r0, 2 * MLP_CHUNK)
        first_key_chunk = jnp.where(jnp.logical_and(seq_start, j == 0), 2, 0)
        mask = window_mask & (kc >= first_key_chunk)
        oa_s[rows, :] = _sgu_chunk(u_ref[rows, :], va_ref[rows, :], ws_ref, bias)
        for p in range(B_WIDTH // LANES):
            h = p // (B_WIDTH // LANES // B_KV_HEADS)
            sl = slice(p * LANES, (p + 1) * LANES)
            ob_s[rows, sl] = _attend(q_ref[rows, sl], kl_s[h, keys, :], kh_s[h, keys, :],
                                     vl_s[h, keys, :], vh_s[h, keys, :], mask,
                                     sinks_ref[2 * p], sinks_ref[2 * p + 1])
        return carry

    lax.fori_loop(0, n_sub, sub, 0)
    o_ref[...] = _out_proj(oa_s[...], ob_s[...], ga_ref, gb_ref, wout_ref, x_ref[...])


def _mixer_prompt(x, u, va, q, k, v, sinks, ws, bias, g_a, g_b, w_out, seq, tile):
    t = x.shape[0]
    n = t // tile
    tiles_per_seq = seq // tile
    per = tile // WINDOW
    row = lambda w: pl.BlockSpec((tile, w), lambda i, s: (i, 0))
    prev = pl.BlockSpec((WINDOW, KV_WIDTH), lambda i, s: (jnp.maximum(i * per - 1, 0), 0))
    res = lambda a: pl.BlockSpec(a.shape, lambda i, s: (0,) * a.ndim, pipeline_mode=pl.Buffered(1))
    grid_spec = pltpu.PrefetchScalarGridSpec(
        num_scalar_prefetch=1, grid=(n,),
        in_specs=[row(D_MODEL), row(A_WIDTH), row(A_WIDTH), row(B_WIDTH), row(KV_WIDTH), row(KV_WIDTH), prev, prev,
                  res(ws), res(bias), res(g_a), res(g_b), res(w_out)],
        out_specs=row(D_MODEL),
        scratch_shapes=[pltpu.VMEM((B_KV_HEADS, WINDOW + tile, LANES), BF16) for _ in range(4)]
        + [pltpu.VMEM((tile, A_WIDTH), F32), pltpu.VMEM((tile, B_WIDTH), F32)])
    return pl.pallas_call(
        functools.partial(_mixer_prompt_kernel, tiles_per_seq), grid_spec=grid_spec, name="mixer_prompt",
        out_shape=jax.ShapeDtypeStruct((t, D_MODEL), F32), compiler_params=_params(),
    )(sinks, x, u, va, q, k, v, k, v, ws, bias, g_a, g_b, w_out)


def _mixer_sample_kernel(dec_seq, sinks_ref, x_ref, u_ref, va_ref, q_ref, kc_ref, vc_ref, ck_ref, cv_ref,
                         ws_ref, bias_ref, ga_ref, gb_ref, wout_ref, o_ref,
                         kl_s, kh_s, vl_s, vh_s, ob_s):
    tile = x_ref.shape[0]
    n_seq = tile // dec_seq
    n_cache = ck_ref.shape[1]
    n_keys = n_cache + dec_seq
    oa = _sgu_chunk(u_ref[...], va_ref[...], ws_ref, bias_ref[...])
    for b in range(n_seq):
        rows = slice(b * dec_seq, (b + 1) * dec_seq)
        for src_c, src_n, dl, dh in ((ck_ref, kc_ref, kl_s, kh_s), (cv_ref, vc_ref, vl_s, vh_s)):
            old = src_c[b]
            new = src_n[rows, :]
            _split_heads(dl, dh, slice(b * n_keys, b * n_keys + n_cache), old, pltpu.roll(old, B_HEAD_DIM, 1))
            _split_heads(dl, dh, slice(b * n_keys + n_cache, (b + 1) * n_keys), new, pltpu.roll(new, B_HEAD_DIM, 1))
    q_seq = lax.broadcasted_iota(I32, (tile, n_seq * n_keys), 0) // dec_seq
    k_seq = lax.broadcasted_iota(I32, (tile, n_seq * n_keys), 1) // n_keys
    mask = q_seq == k_seq
    for p in range(B_WIDTH // LANES):
        h = p // (B_WIDTH // LANES // B_KV_HEADS)
        sl = slice(p * LANES, (p + 1) * LANES)
        ob_s[:, sl] = _attend(q_ref[:, sl], kl_s[h], kh_s[h], vl_s[h], vh_s[h], mask,
                              sinks_ref[2 * p], sinks_ref[2 * p + 1])
    o_ref[...] = _out_proj(oa, ob_s[...], ga_ref, gb_ref, wout_ref, x_ref[...])


def _mixer_sample(x, u, va, q, k, v, cache_k, cache_v, sinks, ws, bias, g_a, g_b, w_out, dec_seq, tile):
    t = x.shape[0]
    n = t // tile
    n_seq = tile // dec_seq
    n_cache = cache_k.shape[1]
    row = lambda w: pl.BlockSpec((tile, w), lambda i, s: (i, 0))
    cache = pl.BlockSpec((n_seq, n_cache, KV_WIDTH), lambda i, s: (i, 0, 0))
    res = lambda a: pl.BlockSpec(a.shape, lambda i, s: (0,) * a.ndim, pipeline_mode=pl.Buffered(1))
    grid_spec = pltpu.PrefetchScalarGridSpec(
        num_scalar_prefetch=1, grid=(n,),
        in_specs=[row(D_MODEL), row(A_WIDTH), row(A_WIDTH), row(B_WIDTH), row(KV_WIDTH), row(KV_WIDTH), cache, cache,
                  res(ws), res(bias), res(g_a), res(g_b), res(w_out)],
        out_specs=row(D_MODEL),
        scratch_shapes=[pltpu.VMEM((B_KV_HEADS, n_seq * (n_cache + dec_seq), LANES), BF16) for _ in range(4)]
        + [pltpu.VMEM((tile, B_WIDTH), F32)])
    return pl.pallas_call(
        functools.partial(_mixer_sample_kernel, dec_seq), grid_spec=grid_spec, name="mixer_sample",
        out_shape=jax.ShapeDtypeStruct((t, D_MODEL), F32), compiler_params=_params(),
    )(sinks, x, u, va, q, k, v, cache_k, cache_v, ws, bias, g_a, g_b, w_out)


def _memory_router_kernel(rows_per_mem, x_ref, mk_ref, mv_ref, gin_ref, wq_ref, gq_ref, wo_ref,
                          gmoe_ref, wrh_ref, wrl_ref, br_ref,
                          x2_ref, hn_ref, idx_ref, gate_ref, rank_ref, cnt_ref, o_s, base_s):
    tile = x_ref.shape[0]
    x = x_ref[...]
    qz = _dot((_rms(x) * gin_ref[...]).astype(BF16), wq_ref[...])
    gq = gq_ref[...]
    for h in range(MEM_HEADS):
        sl = slice(h * MEM_HEAD_DIM, (h + 1) * MEM_HEAD_DIM)
        qh = (_rms(qz[:, sl]) * gq).astype(BF16)
        for r in range(tile // rows_per_mem):
            rows = slice(r * rows_per_mem, (r + 1) * rows_per_mem)
            s = _dot_nt(qh[rows], mk_ref[r, :, sl].astype(BF16)) * MEM_SCALE
            p = jnp.exp(s - jnp.max(s, axis=-1, keepdims=True))
            den = jnp.sum(p, axis=-1, keepdims=True)
            o_s[rows, sl] = _dot(p.astype(BF16), mv_ref[r, :, sl].astype(BF16)) * (1.0 / den)
    x2 = x + _dot(o_s[...].astype(BF16), wo_ref[...])
    x2_ref[...] = x2
    hn = _rms(x2) * gmoe_ref[...]
    hn_ref[...] = _pack_rows(hn)

    hn_hi = hn.astype(BF16)
    hn_lo = (hn - hn_hi.astype(F32)).astype(BF16)
    logits = ((_dot(hn_lo, wrh_ref[...]) + _dot(hn_hi, wrl_ref[...])) + _dot(hn_hi, wrh_ref[...])) + br_ref[...]
    lane = lax.broadcasted_iota(I32, (tile, LANES), 1)
    lane_f = lane.astype(F32)
    work = jnp.where(lane < N_EXPERTS, logits, -jnp.inf)
    idx_out = jnp.zeros((tile, LANES), F32)
    val_out = jnp.zeros((tile, LANES), F32)
    hot = jnp.zeros((tile, LANES), F32)
    top = None
    picks = []
    for k in range(TOP_K):
        m = jnp.max(work, axis=-1, keepdims=True)
        pick = jnp.min(jnp.where(work == m, lane_f, float(LANES)), axis=-1, keepdims=True)
        chosen = lane_f == pick
        if top is None:
            top = m
        idx_out = jnp.where(lane == k, pick, idx_out)
        val_out = jnp.where(lane == k, jnp.exp(m - top), val_out)
        hot = jnp.where(chosen, 1.0, hot)
        work = jnp.where(chosen, -jnp.inf, work)
        picks.append(chosen)
    idx_ref[...] = idx_out.astype(I32)
    gate_ref[...] = val_out * (1.0 / jnp.sum(val_out, axis=-1, keepdims=True))

    @pl.when(pl.program_id(0) == 0)
    def _():
        base_s[...] = jnp.zeros_like(base_s)

    earlier = (lax.broadcasted_iota(I32, (tile, tile), 1) < lax.broadcasted_iota(I32, (tile, tile), 0))
    pos = _dot(jnp.where(earlier, 1.0, 0.0).astype(BF16), hot.astype(BF16)) + base_s[0:1, :]
    rank_out = jnp.zeros((tile, LANES), F32)
    for k in range(TOP_K):
        rank_out = jnp.where(lane == k, jnp.sum(jnp.where(picks[k], pos, 0.0), axis=-1, keepdims=True), rank_out)
    rank_ref[...] = rank_out.astype(I32)
    total = base_s[0:1, :] + jnp.sum(hot, axis=0, keepdims=True)
    base_s[...] = jnp.broadcast_to(total, base_s.shape)
    cnt_ref[...] = jnp.broadcast_to(total, cnt_ref.shape).astype(I32)


def _memory_router(x, mk, mv, rows_per_mem, g_in, w_mq, g_mq, w_mo, g_moe, w_r_hi, w_r_lo, b_r, tile, name):
    t = x.shape[0]
    n = t // tile
    mems = tile // rows_per_mem if rows_per_mem <= tile else 1
    per_mem_tiles = max(rows_per_mem // tile, 1)
    rpm = min(rows_per_mem, tile)
    row = lambda w: pl.BlockSpec((tile, w), lambda i: (i, 0))
    mem = pl.BlockSpec((mems, mk.shape[1], MEM_WIDTH), lambda i: (i // per_mem_tiles, 0, 0))
    res = lambda a: pl.BlockSpec(a.shape, lambda i: (0,) * a.ndim, pipeline_mode=pl.Buffered(1))
    small = jax.ShapeDtypeStruct((t, LANES), I32)
    return pl.pallas_call(
        functools.partial(_memory_router_kernel, rpm), grid=(n,), name=name,
        in_specs=[row(D_MODEL), mem, mem, res(g_in), res(w_mq), res(g_mq), res(w_mo), res(g_moe), res(w_r_hi), res(w_r_lo), res(b_r)],
        out_specs=[row(D_MODEL), row(D_MODEL // 2), row(LANES), row(LANES), row(LANES),
                   pl.BlockSpec((8, LANES), lambda i: (0, 0))],
        out_shape=(jax.ShapeDtypeStruct((t, D_MODEL), F32), jax.ShapeDtypeStruct((t, D_MODEL // 2), U32),
                   small, jax.ShapeDtypeStruct((t, LANES), F32), small, jax.ShapeDtypeStruct((8, LANES), I32)),
        scratch_shapes=[pltpu.VMEM((tile, MEM_WIDTH), F32), pltpu.VMEM((8, LANES), F32)],
        compiler_params=_params(),
    )(x, mk, mv, g_in, w_mq, g_mq, w_mo, g_moe, w_r_hi, w_r_lo, b_r)


def _dispatch_kernel(dest_ref, hn_ref, *rest):
    xb_ref, sem = rest[-2:]
    groups = hn_ref.shape[0]

    def copy(j, u, k):
        return pltpu.make_async_copy(hn_ref.at[j, pl.ds(u, 1), :],
                                     xb_ref.at[pl.ds(dest_ref[0, j * (SUBLANES * TOP_K) + u * TOP_K + k], 1), :], sem)

    def start(j, carry):
        for u in range(SUBLANES):
            for k in range(TOP_K):
                copy(j, u, k).start()
        return carry

    def wait(j, carry):
        for u in range(SUBLANES):
            for k in range(TOP_K):
                copy(j, u, k).wait()
        return carry

    lax.fori_loop(0, groups, start, 0)
    lax.fori_loop(0, groups, wait, 0)


def _dispatch(hn, dest, xb, n_rows, tile):
    t = hn.shape[0]
    n = t // tile
    dest = dest.reshape(n, 1, tile * TOP_K)
    in_specs = [pl.BlockSpec((None, 1, tile * TOP_K), lambda i: (i, 0, 0), memory_space=pltpu.SMEM),
                pl.BlockSpec((tile // SUBLANES, SUBLANES, hn.shape[1]), lambda i: (i, 0, 0))]
    args = [dest, hn.reshape(t // SUBLANES, SUBLANES, hn.shape[1])]
    if xb is not None:
        in_specs.append(pl.BlockSpec(memory_space=pl.ANY))
        args.append(xb)
    return pl.pallas_call(
        _dispatch_kernel, grid=(n,), name="dispatch",
        in_specs=in_specs,
        out_specs=pl.BlockSpec(memory_space=pl.ANY),
        out_shape=jax.ShapeDtypeStruct((n_rows, hn.shape[1]), hn.dtype),
        scratch_shapes=[pltpu.SemaphoreType.DMA(())],
        input_output_aliases={} if xb is None else {2: 0},
        compiler_params=_params(),
    )(*args)


def _for_overlapped(n, body):
    def trip(j, carry):
        for g in range(4):
            body(4 * j + g)
        return carry
    lax.fori_loop(0, n // 4, trip, 0)
    base = (n // 4) * 4

    @pl.when(n % 4 >= 2)
    def _():
        body(base)
        body(base + 1)

    @pl.when(n % 2 == 1)
    def _():
        body(n - 1)


def _experts_kernel(be_ref, nv_ref, xi_ref, x_ref, wg_ref, wu_ref, bg_ref, bu_ref, wd_ref, bd_ref, y_ref, h_s):
    del be_ref, xi_ref
    b = pl.program_id(0)
    s = pl.program_id(1)
    n_ff = D_FF // FF_TILE
    valid = nv_ref[b]
    n_sub = (valid + EXPERT_SUB - 1) // EXPERT_SUB

    @pl.when(s < n_ff)
    def _():
        bg = bg_ref[0]
        bu = bu_ref[0]

        def up(i):
            rows = pl.ds(pl.multiple_of(i * EXPERT_SUB, EXPERT_SUB), EXPERT_SUB)
            keep = (i * EXPERT_SUB + lax.broadcasted_iota(I32, (EXPERT_SUB, 1), 0)) < valid
            lo, hi = _unpack_rows(x_ref[rows, :])
            xs = jnp.concatenate([jnp.where(keep, lo, 0.0), jnp.where(keep, hi, 0.0)], axis=1)
            gate = jnp.minimum(_dot(xs, wg_ref[0]) + bg, SWIGLU_LIMIT)
            lin = jnp.clip(_dot(xs, wu_ref[0]) + bu, -SWIGLU_LIMIT, SWIGLU_LIMIT)
            act = (lin + 1.0) * (gate * (1.0 / (1.0 + jnp.exp(-SWIGLU_ALPHA * gate))))
            h_s[s, rows, :] = act.astype(BF16)
        _for_overlapped(n_sub, up)

    @pl.when(s >= n_ff)
    def _():
        bd = bd_ref[0]

        def down(i):
            rows = pl.ds(pl.multiple_of(i * EXPERT_SUB, EXPERT_SUB), EXPERT_SUB)
            h = jnp.concatenate([h_s[f, rows, :] for f in range(n_ff)], axis=1)
            y_ref[rows, :] = _pack_rows(_dot(h.astype(F32), wd_ref[0]) + bd)
        _for_overlapped(n_sub, down)


def _experts(xb, block_expert, block_valid, block_index, w_gu, b_gu, w_d, b_d):
    n_blocks = xb.shape[0] // EXPERT_ROWS
    n_ff = D_FF // FF_TILE
    n_out = D_MODEL // OUT_TILE

    def up_window(b, s, be, nv, half):
        nxt = jnp.minimum(b + 1, n_blocks - 1)
        ahead = jnp.logical_and(s >= n_ff, nv[nxt] > 0)
        e = jnp.where(ahead, be[nxt], be[b])
        tile = jnp.where(ahead, 0, jnp.where(nv[b] > 0, jnp.minimum(s, n_ff - 1), n_ff - 1))
        return e, 0, tile + half * n_ff

    def oc(b, s, nv):
        return jnp.where(nv[b] > 0, jnp.maximum(s - n_ff, 0), n_out - 1)

    def x_block(b, s, xi):
        return xi[jnp.where(s < n_ff, b, jnp.minimum(b + 1, n_blocks - 1))]

    def down_window(b, s, be, nv):
        live = jnp.logical_and(nv[b] > 0, s >= n_ff - 1)
        return (jnp.where(live, be[b], be[jnp.maximum(b - 1, 0)]), 0,
                jnp.where(live, jnp.maximum(s - n_ff, 0), n_out - 1))

    b_gu3 = b_gu.reshape(N_EXPERTS, 1, 2 * D_FF)
    b_d3 = b_d.reshape(N_EXPERTS, 1, D_MODEL)
    grid_spec = pltpu.PrefetchScalarGridSpec(
        num_scalar_prefetch=3, grid=(n_blocks, n_ff + n_out),
        in_specs=[
            pl.BlockSpec((EXPERT_ROWS, D_MODEL // 2), lambda b, s, be, nv, xi: (x_block(b, s, xi), 0)),
            pl.BlockSpec((1, D_MODEL, FF_TILE), lambda b, s, be, nv, xi: up_window(b, s, be, nv, 0)),
            pl.BlockSpec((1, D_MODEL, FF_TILE), lambda b, s, be, nv, xi: up_window(b, s, be, nv, 1)),
            pl.BlockSpec((1, 1, FF_TILE), lambda b, s, be, nv, xi: up_window(b, s, be, nv, 0)),
            pl.BlockSpec((1, 1, FF_TILE), lambda b, s, be, nv, xi: up_window(b, s, be, nv, 1)),
            pl.BlockSpec((1, D_FF, OUT_TILE), lambda b, s, be, nv, xi: down_window(b, s, be, nv)),
            pl.BlockSpec((1, 1, OUT_TILE), lambda b, s, be, nv, xi: down_window(b, s, be, nv)),
        ],
        out_specs=pl.BlockSpec((EXPERT_ROWS, OUT_TILE // 2), lambda b, s, be, nv, xi: (xi[b], oc(b, s, nv))),
        scratch_shapes=[pltpu.VMEM((n_ff, EXPERT_ROWS, FF_TILE), BF16)])
    return pl.pallas_call(
        _experts_kernel, grid_spec=grid_spec, name="experts",
        out_shape=jax.ShapeDtypeStruct((n_blocks * EXPERT_ROWS, D_MODEL // 2), U32),
        compiler_params=_params(2),
    )(block_expert, block_valid, block_index, xb, w_gu, w_gu, b_gu3, b_gu3, w_d, b_d3)


def _combine_kernel(n_tiles, dest_ref, next_ref, x_ref, gate_ref, yb_ref, o_ref, buf, sem):
    tile = x_ref.shape[0]
    i = pl.program_id(0)
    slot = i % 2

    groups = tile // SUBLANES

    def copy(idx_ref, s, j, u, k):
        return pltpu.make_async_copy(yb_ref.at[pl.ds(idx_ref[0, j * (SUBLANES * TOP_K) + u * TOP_K + k], 1), :],
                                     buf.at[s, k, j, pl.ds(u, 1), :], sem.at[s])

    def start_tile(idx_ref, s):
        def start(j, carry):
            for u in range(SUBLANES):
                for k in range(TOP_K):
                    copy(idx_ref, s, j, u, k).start()
            return carry
        lax.fori_loop(0, groups, start, 0)

    pl.when(i == 0)(lambda: start_tile(dest_ref, 0))
    pl.when(i + 1 < n_tiles)(lambda: start_tile(next_ref, 1 - slot))

    def wait(j, carry):
        for u in range(SUBLANES):
            for k in range(TOP_K):
                copy(dest_ref, slot, j, u, k).wait()
        return carry
    lax.fori_loop(0, groups, wait, 0)

    gate = gate_ref[...]
    half = OUT_TILE // 2
    for g in range(D_MODEL // OUT_TILE):
        c0 = g * OUT_TILE
        acc_lo = x_ref[:, c0:c0 + half]
        acc_hi = x_ref[:, c0 + half:c0 + OUT_TILE]
        for k in range(TOP_K):
            lo, hi = _unpack_rows(buf[slot, k, :, :, g * half:(g + 1) * half].reshape(tile, half))
            acc_lo = acc_lo + gate[:, k:k + 1] * lo
            acc_hi = acc_hi + gate[:, k:k + 1] * hi
        o_ref[:, c0:c0 + half] = acc_lo
        o_ref[:, c0 + half:c0 + OUT_TILE] = acc_hi


def _combine(x, gate, dest, yb, tile):
    t = x.shape[0]
    n = t // tile
    dest = dest.reshape(n, 1, tile * TOP_K)
    return pl.pallas_call(
        functools.partial(_combine_kernel, n), grid=(n,), name="combine",
        in_specs=[pl.BlockSpec((None, 1, tile * TOP_K), lambda i: (i, 0, 0), memory_space=pltpu.SMEM),
                  pl.BlockSpec((None, 1, tile * TOP_K), lambda i: (jnp.minimum(i + 1, n - 1), 0, 0),
                               memory_space=pltpu.SMEM),
                  pl.BlockSpec((tile, D_MODEL), lambda i: (i, 0)),
                  pl.BlockSpec((tile, LANES), lambda i: (i, 0)),
                  pl.BlockSpec(memory_space=pl.ANY)],
        out_specs=pl.BlockSpec((tile, D_MODEL), lambda i: (i, 0)),
        out_shape=jax.ShapeDtypeStruct((t, D_MODEL), F32),
        scratch_shapes=[pltpu.VMEM((2, TOP_K, tile // SUBLANES, SUBLANES, D_MODEL // 2), U32),
                        pltpu.SemaphoreType.DMA((2,))],
        compiler_params=_params(),
    )(dest, dest, x, gate, yb)


def _rope_tables(pos):
    half = B_HEAD_DIM // 2
    inv_freq = ROPE_THETA ** (-jnp.arange(half, dtype=F32) / half)
    ang = pos.astype(F32)[:, None] * inv_freq[None, :]
    cos = jnp.cos(ang)
    sin = jnp.sin(ang)
    return jnp.tile(cos, (1, LANES // half)), jnp.tile(jnp.concatenate([-sin, sin], axis=1), (1, LANES // B_HEAD_DIM))


def _layer(l, xp, xs, cache_swa_k, cache_swa_v, cache_mem_k, cache_mem_v, mem_prompt,
           g_mix, w_in, g_sgu, w_s, b_s, g_q, g_k, sinks, g_out_a, g_out_b, w_out,
           g_mem_in, g_mem_src, w_mq, w_mk, w_mv, g_mq, g_mk, w_mo,
           g_moe, w_router, b_router, w_gate_up, b_gate_up, w_down, b_down):
    n_b, seq, _ = xp.shape
    n_db, dec_seq, _ = xs.shape
    tp = n_b * seq
    ts = n_db * dec_seq
    tile = min(TOKEN_TILE, seq)
    row = lambda a: a[l].reshape(1, -1)

    xp2 = xp.reshape(tp, D_MODEL)
    xs2 = xs.reshape(ts, D_MODEL)
    head = jnp.arange(QK_WIDTH, dtype=I32) // B_HEAD_DIM
    seg = (head[:, None] == jnp.arange(LANES, dtype=I32)[None, :]).astype(BF16)
    g_qk = jnp.concatenate([jnp.tile(g_q[l], B_HEADS), jnp.tile(g_k[l], B_KV_HEADS)]).reshape(1, -1)
    w_in_b = w_in[l].astype(BF16)
    cos_p, sin_p = _rope_tables(jnp.arange(seq, dtype=I32))
    cos_s, sin_s = _rope_tables(PAST_LEN + jnp.arange(SAMPLE_TILE, dtype=I32) % dec_seq)
    up, vap, qp, kp, vp = _in_proj(xp2, row(g_mix), w_in_b, row(g_sgu), g_qk, seg, seg.T, cos_p, sin_p,
                                   tile, "in_proj_prompt")
    us, vas, qs, ks, vs = _in_proj(xs2, row(g_mix), w_in_b, row(g_sgu), g_qk, seg, seg.T, cos_s, sin_s,
                                   SAMPLE_TILE, "in_proj_sample")

    w_out_b = w_out[l].astype(BF16)
    tri = jnp.tril(jnp.ones((MLP_CHUNK, MLP_CHUNK), bool))
    ws_p = jnp.where(tri[None], w_s[l], 0.0).astype(BF16)
    bias_p = jnp.repeat(b_s[l].T, A_HEAD_DIM, axis=1)
    x1p = _mixer_prompt(xp2, up, vap, qp, kp, vp, sinks[l], ws_p, bias_p, row(g_out_a), row(g_out_b), w_out_b, seq, tile)

    reps = SAMPLE_TILE // dec_seq
    tri_s = jnp.tril(jnp.ones((dec_seq, dec_seq), bool))
    ws_small = jnp.where(tri_s[None], w_s[l][:, :dec_seq, :dec_seq], 0.0)
    ws_s = jnp.einsum("ab,hts->hatbs", jnp.eye(reps, dtype=F32), ws_small).reshape(A_HEADS, SAMPLE_TILE, SAMPLE_TILE).astype(BF16)
    bias_s = jnp.tile(jnp.repeat(b_s[l][:, :dec_seq].T, A_HEAD_DIM, axis=1), (reps, 1))
    ck = cache_swa_k[l].reshape(n_db, -1, KV_WIDTH)
    cv = cache_swa_v[l].reshape(n_db, -1, KV_WIDTH)
    x1s = _mixer_sample(xs2, us, vas, qs, ks, vs, ck, cv, sinks[l], ws_s, bias_s,
                        row(g_out_a), row(g_out_b), w_out_b, dec_seq, SAMPLE_TILE)

    mk_p, mv_p = _mem_kv(mem_prompt.reshape(-1, D_MODEL), row(g_mem_src), w_mk[l].astype(BF16), w_mv[l].astype(BF16), row(g_mk))
    n_mem = mem_prompt.shape[1]
    w_r = jnp.pad(w_router[l], ((0, 0), (0, LANES - N_EXPERTS)))
    b_r = jnp.pad(b_router[l], (0, LANES - N_EXPERTS)).reshape(1, -1)
    w_r_hi = w_r.astype(BF16)
    w_r_lo = (w_r - w_r_hi.astype(F32)).astype(BF16)
    mem_args = (row(g_mem_in), w_mq[l].astype(BF16), row(g_mq), w_mo[l].astype(BF16), row(g_moe), w_r_hi, w_r_lo, b_r)
    x2p, hnp, idxp, gatep, rankp, cntp = _memory_router(
        x1p, mk_p.reshape(n_b, n_mem, MEM_WIDTH), mv_p.reshape(n_b, n_mem, MEM_WIDTH),
        seq, *mem_args, tile, "memory_router_prompt")
    x2s, hns, idxs, gates, ranks, cnts = _memory_router(
        x1s, cache_mem_k[l].reshape(n_db, -1, MEM_WIDTH), cache_mem_v[l].reshape(n_db, -1, MEM_WIDTH),
        dec_seq, *mem_args, SAMPLE_TILE, "memory_router_sample")

    cnt_p = cntp[0, :N_EXPERTS]
    cnt_s = cnts[0, :N_EXPERTS]
    total = cnt_p + cnt_s
    nblk = (total + EXPERT_ROWS - 1) // EXPERT_ROWS
    blk_end = jnp.cumsum(nblk)
    row_start = (blk_end - nblk) * EXPERT_ROWS
    n_blocks = (tp + ts) * TOP_K // EXPERT_ROWS + N_EXPERTS
    bidx = jnp.arange(n_blocks, dtype=I32)
    used = bidx < blk_end[-1]
    last = jnp.maximum(blk_end[-1] - 1, 0)
    bsafe = jnp.minimum(bidx, last)
    block_expert = jnp.minimum(jnp.searchsorted(blk_end, bsafe, side="right"), N_EXPERTS - 1).astype(I32)
    within = bsafe - (blk_end - nblk)[block_expert]
    block_valid = jnp.where(used, jnp.clip(total[block_expert] - within * EXPERT_ROWS, 0, EXPERT_ROWS), 0).astype(I32)
    dest_p = row_start[idxp[:, :TOP_K]] + rankp[:, :TOP_K]
    dest_s = (row_start + cnt_p)[idxs[:, :TOP_K]] + ranks[:, :TOP_K]

    xb = _dispatch(hnp, dest_p, None, n_blocks * EXPERT_ROWS, min(ROW_TILE, tp))
    xb = _dispatch(hns, dest_s, xb, n_blocks * EXPERT_ROWS, min(ROW_TILE, ts))
    yb = _experts(xb, block_expert, block_valid, bsafe.astype(I32), w_gate_up[l], b_gate_up[l], w_down[l], b_down[l])
    yp = _combine(x2p, gatep, dest_p, yb, min(ROW_TILE, tp))
    ys = _combine(x2s, gates, dest_s, yb, min(ROW_TILE, ts))

    new = dict(
        swa_k_p=kp.reshape(n_b, seq, B_KV_HEADS, B_HEAD_DIM)[:, seq - WINDOW:],
        swa_v_p=vp.reshape(n_b, seq, B_KV_HEADS, B_HEAD_DIM)[:, seq - WINDOW:],
        mem_k_p=mk_p.reshape(n_b, n_mem, MEM_HEADS, MEM_HEAD_DIM),
        mem_v_p=mv_p.reshape(n_b, n_mem, MEM_HEADS, MEM_HEAD_DIM),
        swa_k_s=ks.reshape(n_db, dec_seq, B_KV_HEADS, B_HEAD_DIM),
        swa_v_s=vs.reshape(n_db, dec_seq, B_KV_HEADS, B_HEAD_DIM),
        sgu_v_s=vas.reshape(n_db, dec_seq, A_HEADS, A_HEAD_DIM))
    return yp.reshape(n_b, seq, D_MODEL), ys.reshape(n_db, dec_seq, D_MODEL), new


def kernel(x_prompt, x_sample, cache_swa_k, cache_swa_v, cache_mem_k, cache_mem_v, mem_prompt, g_mix, w_in, g_sgu, w_s, b_s, g_q, g_k, sinks, g_out_a, g_out_b, w_out, g_mem_in, g_mem_src, w_mq, w_mk, w_mv, g_mq, g_mk, w_mo, g_moe, w_router, b_router, w_gate_up, b_gate_up, w_down, b_down):
    xp, xs = x_prompt, x_sample
    news = []
    for l in range(g_mix.shape[0]):
        xp, xs, new = _layer(l, xp, xs, cache_swa_k, cache_swa_v, cache_mem_k, cache_mem_v, mem_prompt,
                             g_mix, w_in, g_sgu, w_s, b_s, g_q, g_k, sinks, g_out_a, g_out_b, w_out,
                             g_mem_in, g_mem_src, w_mq, w_mk, w_mv, g_mq, g_mk, w_mo,
                             g_moe, w_router, b_router, w_gate_up, b_gate_up, w_down, b_down)
        news.append(new)
    stack = lambda name: jnp.stack([n[name] for n in news], 0)
    return (xp, xs, stack("swa_k_p"), stack("swa_v_p"), stack("mem_k_p"), stack("mem_v_p"),
            stack("swa_k_s"), stack("swa_v_s"), stack("sgu_v_s"))
```

```python
import functools

import jax
import jax.numpy as jnp
from jax import lax
from jax.experimental import pallas as pl
from jax.experimental.pallas import tpu as pltpu

F32 = jnp.float32
BF16 = jnp.bfloat16
I32 = jnp.int32
U32 = jnp.uint32

D_MODEL = 2048
PAST_LEN = 2048
CHUNK = 64
EPS = 1e-6
A_HEADS = 16
A_HEAD_DIM = 64
A_WIDTH = A_HEADS * A_HEAD_DIM
MLP_CHUNK = 128
B_HEADS = 16
B_KV_HEADS = 2
B_HEAD_DIM = 64
B_WIDTH = B_HEADS * B_HEAD_DIM
KV_WIDTH = B_KV_HEADS * B_HEAD_DIM
WINDOW = 128
ROPE_THETA = 10000.0
ATTN_SCALE = B_HEAD_DIM ** -0.5
Q_OFF = 2 * A_WIDTH
K_OFF = Q_OFF + B_WIDTH
V_OFF = K_OFF + KV_WIDTH
IN_WIDTH = V_OFF + KV_WIDTH
QK_WIDTH = B_WIDTH + KV_WIDTH
MEM_HEADS = 4
MEM_HEAD_DIM = 128
MEM_WIDTH = MEM_HEADS * MEM_HEAD_DIM
MEM_SCALE = MEM_HEAD_DIM ** -0.5
N_EXPERTS = 32
TOP_K = 4
D_FF = D_MODEL
SWIGLU_LIMIT = 7.0
SWIGLU_ALPHA = 1.702
NEG_BIG = -1e30

LANES = 128
SUBLANES = 8
VMEM_LIMIT = 56 * 1024 * 1024

TOKEN_TILE = 512
SAMPLE_TILE = 128
EXPERT_ROWS = 1280
EXPERT_SUB = 320
FF_TILE = 512
OUT_TILE = 1024
ROW_TILE = 256


def _dot(a, b):
    return jnp.dot(a, b, preferred_element_type=F32)


def _dot_nt(a, b):
    return lax.dot_general(a, b, (((1,), (1,)), ((), ())), preferred_element_type=F32)


def _rms(x):
    return x * lax.rsqrt(jnp.mean(x * x, axis=-1, keepdims=True) + EPS)


def _gelu(x):
    return 0.5 * x * (1.0 + lax.erf(x * 0.7071067811865476))


def _pack_rows(x):
    c = x.shape[1] // 2
    bits = pltpu.bitcast(x.astype(BF16).astype(F32), U32)
    return (bits[:, :c] >> 16) | (bits[:, c:] & jnp.uint32(0xFFFF0000))


def _unpack_rows(w):
    return pltpu.bitcast(w << 16, F32), pltpu.bitcast(w & jnp.uint32(0xFFFF0000), F32)


def _params(n_axes=1):
    return pltpu.CompilerParams(dimension_semantics=("arbitrary",) * n_axes,
                                vmem_limit_bytes=VMEM_LIMIT)


def _resident(shape):
    nd = len(shape)
    return pl.BlockSpec(shape, lambda *_: (0,) * nd, pipeline_mode=pl.Buffered(1))


def _mem_kv_kernel(mem_ref, gsrc_ref, wk_ref, wv_ref, gk_ref, k_ref, v_ref):
    m = (_rms(mem_ref[...]) * gsrc_ref[...]).astype(BF16)
    kz = _dot(m, wk_ref[...])
    gk = gk_ref[...]
    for h in range(MEM_HEADS):
        sl = slice(h * MEM_HEAD_DIM, (h + 1) * MEM_HEAD_DIM)
        k_ref[:, sl] = _rms(kz[:, sl]) * gk
    v_ref[...] = _dot(m, wv_ref[...])


def _mem_kv(mem, g_src, w_mk, w_mv, g_mk):
    n = mem.shape[0]
    out = jax.ShapeDtypeStruct((n, MEM_WIDTH), F32)
    return pl.pallas_call(
        _mem_kv_kernel, out_shape=(out, out), name="mem_kv",
        compiler_params=pltpu.CompilerParams(vmem_limit_bytes=VMEM_LIMIT),
    )(mem, g_src, w_mk, w_mv, g_mk)


def _in_proj_kernel(x_ref, gmix_ref, w_ref, gsgu_ref, gqk_ref, seg_ref, segt_ref, cos_ref, sin_ref,
                    u_ref, va_ref, q_ref, k_ref, v_ref):
    xn = (_rms(x_ref[...]) * gmix_ref[...]).astype(BF16)
    u_ref[...] = _gelu(_dot(xn, w_ref[:, 0:A_WIDTH])).astype(BF16)
    va_ref[...] = _rms(_gelu(_dot(xn, w_ref[:, A_WIDTH:Q_OFF]))) * gsgu_ref[...]
    v_ref[...] = _dot(xn, w_ref[:, V_OFF:IN_WIDTH])

    qk = _dot(xn, w_ref[:, Q_OFF:V_OFF])
    ssq = _dot((qk * qk).astype(BF16), seg_ref[...])
    inv = lax.rsqrt(ssq * (1.0 / B_HEAD_DIM) + EPS)
    inv_hi = inv.astype(BF16)
    inv_lo = (inv - inv_hi.astype(F32)).astype(BF16)
    inv_b = _dot(inv_hi, segt_ref[...]) + _dot(inv_lo, segt_ref[...])
    qkn = (qk * inv_b) * gqk_ref[...]

    cos = cos_ref[...]
    sin = sin_ref[...]
    lane = lax.broadcasted_iota(I32, cos.shape, 1)
    first_half = (lane % B_HEAD_DIM) < (B_HEAD_DIM // 2)
    for g in range(QK_WIDTH // LANES):
        xg = qkn[:, g * LANES:(g + 1) * LANES]
        rot = jnp.where(first_half, pltpu.roll(xg, LANES - B_HEAD_DIM // 2, 1),
                        pltpu.roll(xg, B_HEAD_DIM // 2, 1))
        r = xg * cos + rot * sin
        if g < B_WIDTH // LANES:
            q_ref[:, g * LANES:(g + 1) * LANES] = (r * ATTN_SCALE).astype(BF16)
        else:
            k_ref[...] = r


def _in_proj(x, g_mix, w_in, g_sgu, g_qk, seg, segt, cos_tab, sin_tab, tile, name):
    t = x.shape[0]
    n = t // tile
    tab_tiles = cos_tab.shape[0] // tile
    row = lambda w: pl.BlockSpec((tile, w), lambda i: (i, 0))
    tab = pl.BlockSpec((tile, LANES), lambda i: (i % tab_tiles, 0))
    return pl.pallas_call(
        _in_proj_kernel, grid=(n,), name=name,
        in_specs=[row(D_MODEL), _resident(g_mix.shape), _resident(w_in.shape), _resident(g_sgu.shape),
                  _resident(g_qk.shape), _resident(seg.shape), _resident(segt.shape), tab, tab],
        out_specs=[row(A_WIDTH), row(A_WIDTH), row(B_WIDTH), row(KV_WIDTH), row(KV_WIDTH)],
        out_shape=(jax.ShapeDtypeStruct((t, A_WIDTH), BF16), jax.ShapeDtypeStruct((t, A_WIDTH), F32),
                   jax.ShapeDtypeStruct((t, B_WIDTH), BF16), jax.ShapeDtypeStruct((t, KV_WIDTH), F32),
                   jax.ShapeDtypeStruct((t, KV_WIDTH), F32)),
        compiler_params=_params(),
    )(x, g_mix, w_in, g_sgu, g_qk, seg, segt, cos_tab, sin_tab)


def _split_heads(ref_dst_lo, ref_dst_hi, rows, x, swap):
    lane = lax.broadcasted_iota(I32, x.shape, 1)
    low = lane < B_HEAD_DIM
    zero = jnp.zeros_like(x)
    ref_dst_lo[0, rows, :] = jnp.where(low, x, zero).astype(BF16)
    ref_dst_hi[0, rows, :] = jnp.where(low, zero, swap).astype(BF16)
    ref_dst_lo[1, rows, :] = jnp.where(low, swap, zero).astype(BF16)
    ref_dst_hi[1, rows, :] = jnp.where(low, zero, x).astype(BF16)


def _sgu_chunk(u, va, ws_ref, bias):
    lane = lax.broadcasted_iota(I32, (MLP_CHUNK, LANES), 1)
    low = lane < A_HEAD_DIM
    outs = []
    for p in range(A_WIDTH // LANES):
        sl = slice(p * LANES, (p + 1) * LANES)
        v2 = va[:, sl]
        zero = jnp.zeros_like(v2)
        mixed = (_dot(ws_ref[2 * p], jnp.where(low, v2, zero).astype(BF16))
                 + _dot(ws_ref[2 * p + 1], jnp.where(low, zero, v2).astype(BF16)))
        outs.append(u[:, sl].astype(F32) * (mixed + bias[:, sl]))
    return jnp.concatenate(outs, axis=1)


def _attend(q2, kl, kh, vl, vh, mask, sink_even, sink_odd):
    outs = []
    for kk, vv, sink in ((kl, vl, sink_even), (kh, vh, sink_odd)):
        s = _dot_nt(q2, kk)
        if mask is not None:
            s = jnp.where(mask, s, NEG_BIG)
        m = jnp.maximum(jnp.max(s, axis=-1, keepdims=True), sink)
        p = jnp.exp(s - m)
        den = jnp.sum(p, axis=-1, keepdims=True) + jnp.exp(sink - m)
        outs.append(_dot(p.astype(BF16), vv) * (1.0 / den))
    return outs[0] + outs[1]


def _out_proj(oa, ob, ga_ref, gb_ref, wout_ref, x):
    a = (_rms(oa) * ga_ref[...]).astype(BF16)
    b = (_rms(ob) * gb_ref[...]).astype(BF16)
    return x + _dot(a, wout_ref[0:A_WIDTH, :]) + _dot(b, wout_ref[A_WIDTH:, :])


def _mixer_prompt_kernel(tiles_per_seq, sinks_ref, x_ref, u_ref, va_ref, q_ref, kc_ref, vc_ref, kp_ref, vp_ref,
                         ws_ref, bias_ref, ga_ref, gb_ref, wout_ref, o_ref,
                         kl_s, kh_s, vl_s, vh_s, oa_s, ob_s):
    tile = x_ref.shape[0]
    n_sub = tile // MLP_CHUNK
    seq_start = (pl.program_id(0) % tiles_per_seq) == 0

    for src_p, src_c, dl, dh in ((kp_ref, kc_ref, kl_s, kh_s), (vp_ref, vc_ref, vl_s, vh_s)):
        prev = src_p[...]
        cur = src_c[...]
        _split_heads(dl, dh, slice(0, WINDOW), prev, pltpu.roll(prev, B_HEAD_DIM, 1))
        _split_heads(dl, dh, slice(WINDOW, WINDOW + tile), cur, pltpu.roll(cur, B_HEAD_DIM, 1))

    qc = lax.broadcasted_iota(I32, (MLP_CHUNK, 2 * MLP_CHUNK), 0) // CHUNK
    kc = lax.broadcasted_iota(I32, (MLP_CHUNK, 2 * MLP_CHUNK), 1) // CHUNK
    window_mask = (kc >= qc) & (kc <= qc + 2)
    bias = bias_ref[...]

    def sub(j, carry):
        r0 = pl.multiple_of(j * MLP_CHUNK, MLP_CHUNK)
        rows = pl.ds(r0, MLP_CHUNK)
        keys = pl.ds(r0, 2 * MLP_CHUNK)
        first_key_chunk = jnp.where(jnp.logical_and(seq_start, j == 0), 2, 0)
        mask = window_mask & (kc >= first_key_chunk)
        oa_s[rows, :] = _sgu_chunk(u_ref[rows, :], va_ref[rows, :], ws_ref, bias)
        for p in range(B_WIDTH // LANES):
            h = p // (B_WIDTH // LANES // B_KV_HEADS)
            sl = slice(p * LANES, (p + 1) * LANES)
            ob_s[rows, sl] = _attend(q_ref[rows, sl], kl_s[h, keys, :], kh_s[h, keys, :],
                                     vl_s[h, keys, :], vh_s[h, keys, :], mask,
                                     sinks_ref[2 * p], sinks_ref[2 * p + 1])
        return carry

    lax.fori_loop(0, n_sub, sub, 0)
    o_ref[...] = _out_proj(oa_s[...], ob_s[...], ga_ref, gb_ref, wout_ref, x_ref[...])


def _mixer_prompt(x, u, va, q, k, v, sinks, ws, bias, g_a, g_b, w_out, seq, tile):
    t = x.shape[0]
    n = t // tile
    tiles_per_seq = seq // tile
    per = tile // WINDOW
    row = lambda w: pl.BlockSpec((tile, w), lambda i, s: (i, 0))
    prev = pl.BlockSpec((WINDOW, KV_WIDTH), lambda i, s: (jnp.maximum(i * per - 1, 0), 0))
    res = lambda a: pl.BlockSpec(a.shape, lambda i, s: (0,) * a.ndim, pipeline_mode=pl.Buffered(1))
    grid_spec = pltpu.PrefetchScalarGridSpec(
        num_scalar_prefetch=1, grid=(n,),
        in_specs=[row(D_MODEL), row(A_WIDTH), row(A_WIDTH), row(B_WIDTH), row(KV_WIDTH), row(KV_WIDTH), prev, prev,
                  res(ws), res(bias), res(g_a), res(g_b), res(w_out)],
        out_specs=row(D_MODEL),
        scratch_shapes=[pltpu.VMEM((B_KV_HEADS, WINDOW + tile, LANES), BF16) for _ in range(4)]
        + [pltpu.VMEM((tile, A_WIDTH), F32), pltpu.VMEM((tile, B_WIDTH), F32)])
    return pl.pallas_call(
        functools.partial(_mixer_prompt_kernel, tiles_per_seq), grid_spec=grid_spec, name="mixer_prompt",
        out_shape=jax.ShapeDtypeStruct((t, D_MODEL), F32), compiler_params=_params(),
    )(sinks, x, u, va, q, k, v, k, v, ws, bias, g_a, g_b, w_out)


def _mixer_sample_kernel(dec_seq, sinks_ref, x_ref, u_ref, va_ref, q_ref, kc_ref, vc_ref, ck_ref, cv_ref,
                         ws_ref, bias_ref, ga_ref, gb_ref, wout_ref, o_ref,
                         kl_s, kh_s, vl_s, vh_s, ob_s):
    tile = x_ref.shape[0]
    n_seq = tile // dec_seq
    n_cache = ck_ref.shape[1]
    n_keys = n_cache + dec_seq
    oa = _sgu_chunk(u_ref[...], va_ref[...], ws_ref, bias_ref[...])
    for b in range(n_seq):
        rows = slice(b * dec_seq, (b + 1) * dec_seq)
        for src_c, src_n, dl, dh in ((ck_ref, kc_ref, kl_s, kh_s), (cv_ref, vc_ref, vl_s, vh_s)):
            heads = [src_c[b, :, h, :] for h in range(B_KV_HEADS)]
            new = src_n[rows, :]
            _split_heads(dl, dh, slice(b * n_keys, b * n_keys + n_cache), jnp.concatenate(heads, axis=1),
                         jnp.concatenate(heads[::-1], axis=1))
            _split_heads(dl, dh, slice(b * n_keys + n_cache, (b + 1) * n_keys), new, pltpu.roll(new, B_HEAD_DIM, 1))
    q_seq = lax.broadcasted_iota(I32, (tile, n_seq * n_keys), 0) // dec_seq
    k_seq = lax.broadcasted_iota(I32, (tile, n_seq * n_keys), 1) // n_keys
    mask = q_seq == k_seq
    for p in range(B_WIDTH // LANES):
        h = p // (B_WIDTH // LANES // B_KV_HEADS)
        sl = slice(p * LANES, (p + 1) * LANES)
        ob_s[:, sl] = _attend(q_ref[:, sl], kl_s[h], kh_s[h], vl_s[h], vh_s[h], mask,
                              sinks_ref[2 * p], sinks_ref[2 * p + 1])
    o_ref[...] = _out_proj(oa, ob_s[...], ga_ref, gb_ref, wout_ref, x_ref[...])


def _mixer_sample(x, u, va, q, k, v, cache_k, cache_v, sinks, ws, bias, g_a, g_b, w_out, dec_seq, tile):
    t = x.shape[0]
    n = t // tile
    n_seq = tile // dec_seq
    n_cache = cache_k.shape[1]
    row = lambda w: pl.BlockSpec((tile, w), lambda i, s: (i, 0))
    cache = pl.BlockSpec((n_seq, n_cache, B_KV_HEADS, B_HEAD_DIM), lambda i, s: (i, 0, 0, 0))
    res = lambda a: pl.BlockSpec(a.shape, lambda i, s: (0,) * a.ndim, pipeline_mode=pl.Buffered(1))
    grid_spec = pltpu.PrefetchScalarGridSpec(
        num_scalar_prefetch=1, grid=(n,),
        in_specs=[row(D_MODEL), row(A_WIDTH), row(A_WIDTH), row(B_WIDTH), row(KV_WIDTH), row(KV_WIDTH), cache, cache,
                  res(ws), res(bias), res(g_a), res(g_b), res(w_out)],
        out_specs=row(D_MODEL),
        scratch_shapes=[pltpu.VMEM((B_KV_HEADS, n_seq * (n_cache + dec_seq), LANES), BF16) for _ in range(4)]
        + [pltpu.VMEM((tile, B_WIDTH), F32)])
    return pl.pallas_call(
        functools.partial(_mixer_sample_kernel, dec_seq), grid_spec=grid_spec, name="mixer_sample",
        out_shape=jax.ShapeDtypeStruct((t, D_MODEL), F32), compiler_params=_params(),
    )(sinks, x, u, va, q, k, v, cache_k, cache_v, ws, bias, g_a, g_b, w_out)


def _memory_router_kernel(rows_per_mem, x_ref, mk_ref, mv_ref, gin_ref, wq_ref, gq_ref, wo_ref,
                          gmoe_ref, wrh_ref, wrl_ref, br_ref,
                          x2_ref, hn_ref, gate_ref, route_ref, cnt_ref, o_s, base_s):
    tile = x_ref.shape[0]
    x = x_ref[...]
    qz = _dot((_rms(x) * gin_ref[...]).astype(BF16), wq_ref[...])
    gq = gq_ref[...]
    for h in range(MEM_HEADS):
        sl = slice(h * MEM_HEAD_DIM, (h + 1) * MEM_HEAD_DIM)
        qh = (_rms(qz[:, sl]) * gq).astype(BF16)
        for r in range(tile // rows_per_mem):
            rows = slice(r * rows_per_mem, (r + 1) * rows_per_mem)
            s = _dot_nt(qh[rows], mk_ref[r, :, h, :].astype(BF16)) * MEM_SCALE
            p = jnp.exp(s - jnp.max(s, axis=-1, keepdims=True))
            den = jnp.sum(p, axis=-1, keepdims=True)
            o_s[rows, sl] = _dot(p.astype(BF16), mv_ref[r, :, h, :].astype(BF16)) * (1.0 / den)
    x2 = x + _dot(o_s[...].astype(BF16), wo_ref[...])
    x2_ref[...] = x2
    hn = _rms(x2) * gmoe_ref[...]
    hn_ref[...] = _pack_rows(hn)

    hn_hi = hn.astype(BF16)
    hn_lo = (hn - hn_hi.astype(F32)).astype(BF16)
    logits = ((_dot(hn_lo, wrh_ref[...]) + _dot(hn_hi, wrl_ref[...])) + _dot(hn_hi, wrh_ref[...])) + br_ref[...]
    lane = lax.broadcasted_iota(I32, (tile, LANES), 1)
    lane_f = lane.astype(F32)
    work = jnp.where(lane < N_EXPERTS, logits, -jnp.inf)
    idx_out = jnp.zeros((tile, LANES), F32)
    val_out = jnp.zeros((tile, LANES), F32)
    hot = jnp.zeros((tile, LANES), F32)
    top = None
    picks = []
    for k in range(TOP_K):
        m = jnp.max(work, axis=-1, keepdims=True)
        pick = jnp.min(jnp.where(work == m, lane_f, float(LANES)), axis=-1, keepdims=True)
        chosen = lane_f == pick
        if top is None:
            top = m
        idx_out = jnp.where(lane == k, pick, idx_out)
        val_out = jnp.where(lane == k, jnp.exp(m - top), val_out)
        hot = jnp.where(chosen, 1.0, hot)
        work = jnp.where(chosen, -jnp.inf, work)
        picks.append(chosen)
    gate_ref[...] = val_out * (1.0 / jnp.sum(val_out, axis=-1, keepdims=True))

    @pl.when(pl.program_id(0) == 0)
    def _():
        base_s[...] = jnp.zeros_like(base_s)

    earlier = (lax.broadcasted_iota(I32, (tile, tile), 1) < lax.broadcasted_iota(I32, (tile, tile), 0))
    pos = _dot(jnp.where(earlier, 1.0, 0.0).astype(BF16), hot.astype(BF16)) + base_s[0:1, :]
    route = idx_out
    for k in range(TOP_K):
        route = jnp.where(lane == TOP_K + k, jnp.sum(jnp.where(picks[k], pos, 0.0), axis=-1, keepdims=True), route)
    route_ref[...] = route.T[0:2 * TOP_K, :].astype(I32)
    total = base_s[0:1, :] + jnp.sum(hot, axis=0, keepdims=True)
    base_s[...] = jnp.broadcast_to(total, base_s.shape)
    cnt_ref[...] = jnp.broadcast_to(total, cnt_ref.shape).astype(I32)


def _memory_router(x, mk, mv, rows_per_mem, g_in, w_mq, g_mq, w_mo, g_moe, w_r_hi, w_r_lo, b_r, tile, name):
    t = x.shape[0]
    n = t // tile
    mems = tile // rows_per_mem if rows_per_mem <= tile else 1
    per_mem_tiles = max(rows_per_mem // tile, 1)
    rpm = min(rows_per_mem, tile)
    row = lambda w: pl.BlockSpec((tile, w), lambda i: (i, 0))
    mem = pl.BlockSpec((mems,) + mk.shape[1:], lambda i: (i // per_mem_tiles, 0, 0, 0))
    res = lambda a: pl.BlockSpec(a.shape, lambda i: (0,) * a.ndim, pipeline_mode=pl.Buffered(1))
    return pl.pallas_call(
        functools.partial(_memory_router_kernel, rpm), grid=(n,), name=name,
        in_specs=[row(D_MODEL), mem, mem, res(g_in), res(w_mq), res(g_mq), res(w_mo), res(g_moe), res(w_r_hi), res(w_r_lo), res(b_r)],
        out_specs=[row(D_MODEL), row(D_MODEL // 2), row(LANES),
                   pl.BlockSpec((2 * TOP_K, tile), lambda i: (0, i)), pl.BlockSpec((8, LANES), lambda i: (0, 0))],
        out_shape=(jax.ShapeDtypeStruct((t, D_MODEL), F32), jax.ShapeDtypeStruct((t, D_MODEL // 2), U32),
                   jax.ShapeDtypeStruct((t, LANES), F32), jax.ShapeDtypeStruct((2 * TOP_K, t), I32),
                   jax.ShapeDtypeStruct((8, LANES), I32)),
        scratch_shapes=[pltpu.VMEM((tile, MEM_WIDTH), F32), pltpu.VMEM((8, LANES), F32)],
        compiler_params=_params(),
    )(x, mk, mv, g_in, w_mq, g_mq, w_mo, g_moe, w_r_hi, w_r_lo, b_r)


def _dispatch_kernel(dest_ref, hn_ref, *rest):
    xb_ref, sem = rest[-2:]
    groups = hn_ref.shape[0]

    def copy(j, u, k):
        return pltpu.make_async_copy(hn_ref.at[j, pl.ds(u, 1), :],
                                     xb_ref.at[pl.ds(dest_ref[0, j * (SUBLANES * TOP_K) + u * TOP_K + k], 1), :], sem)

    def start(j, carry):
        for u in range(SUBLANES):
            for k in range(TOP_K):
                copy(j, u, k).start()
        return carry

    def wait(j, carry):
        for u in range(SUBLANES):
            for k in range(TOP_K):
                copy(j, u, k).wait()
        return carry

    lax.fori_loop(0, groups, start, 0)
    lax.fori_loop(0, groups, wait, 0)


def _dispatch(hn, dest, xb, n_rows, tile):
    t = hn.shape[0]
    n = t // tile
    dest = dest.reshape(n, 1, tile * TOP_K)
    in_specs = [pl.BlockSpec((None, 1, tile * TOP_K), lambda i: (i, 0, 0), memory_space=pltpu.SMEM),
                pl.BlockSpec((tile // SUBLANES, SUBLANES, hn.shape[1]), lambda i: (i, 0, 0))]
    args = [dest, hn.reshape(t // SUBLANES, SUBLANES, hn.shape[1])]
    if xb is not None:
        in_specs.append(pl.BlockSpec(memory_space=pl.ANY))
        args.append(xb)
    return pl.pallas_call(
        _dispatch_kernel, grid=(n,), name="dispatch",
        in_specs=in_specs,
        out_specs=pl.BlockSpec(memory_space=pl.ANY),
        out_shape=jax.ShapeDtypeStruct((n_rows, hn.shape[1]), hn.dtype),
        scratch_shapes=[pltpu.SemaphoreType.DMA(())],
        input_output_aliases={} if xb is None else {2: 0},
        compiler_params=_params(),
    )(*args)


def _for_overlapped(n, body):
    def trip(j, carry):
        for g in range(4):
            body(4 * j + g)
        return carry
    lax.fori_loop(0, n // 4, trip, 0)
    base = (n // 4) * 4

    @pl.when(n % 4 >= 2)
    def _():
        body(base)
        body(base + 1)

    @pl.when(n % 2 == 1)
    def _():
        body(n - 1)


def _experts_kernel(be_ref, nv_ref, xi_ref, x_ref, wg_ref, wu_ref, bg_ref, bu_ref, wd_ref, bd_ref, y_ref, h_s):
    del be_ref, xi_ref
    b = pl.program_id(0)
    s = pl.program_id(1)
    n_ff = D_FF // FF_TILE
    valid = nv_ref[b]
    n_sub = (valid + EXPERT_SUB - 1) // EXPERT_SUB

    @pl.when(s < n_ff)
    def _():
        bg = bg_ref[0]
        bu = bu_ref[0]

        def up(i):
            rows = pl.ds(pl.multiple_of(i * EXPERT_SUB, EXPERT_SUB), EXPERT_SUB)
            keep = (i * EXPERT_SUB + lax.broadcasted_iota(I32, (EXPERT_SUB, 1), 0)) < valid
            lo, hi = _unpack_rows(x_ref[rows, :])
            xs = jnp.concatenate([jnp.where(keep, lo, 0.0), jnp.where(keep, hi, 0.0)], axis=1)
            gate = jnp.minimum(_dot(xs, wg_ref[0]) + bg, SWIGLU_LIMIT)
            lin = jnp.clip(_dot(xs, wu_ref[0]) + bu, -SWIGLU_LIMIT, SWIGLU_LIMIT)
            act = (lin + 1.0) * (gate * (1.0 / (1.0 + jnp.exp(-SWIGLU_ALPHA * gate))))
            h_s[s, rows, :] = act.astype(BF16)
        _for_overlapped(n_sub, up)

    @pl.when(s >= n_ff)
    def _():
        bd = bd_ref[0]

        def down(i):
            rows = pl.ds(pl.multiple_of(i * EXPERT_SUB, EXPERT_SUB), EXPERT_SUB)
            h = jnp.concatenate([h_s[f, rows, :] for f in range(n_ff)], axis=1)
            y_ref[rows, :] = _pack_rows(_dot(h.astype(F32), wd_ref[0]) + bd)
        _for_overlapped(n_sub, down)


def _experts(xb, block_expert, block_valid, block_index, w_gu, b_gu, w_d, b_d):
    n_blocks = xb.shape[0] // EXPERT_ROWS
    n_ff = D_FF // FF_TILE
    n_out = D_MODEL // OUT_TILE

    def up_window(b, s, be, nv, half):
        nxt = jnp.minimum(b + 1, n_blocks - 1)
        ahead = jnp.logical_and(s >= n_ff, nv[nxt] > 0)
        e = jnp.where(ahead, be[nxt], be[b])
        tile = jnp.where(ahead, 0, jnp.where(nv[b] > 0, jnp.minimum(s, n_ff - 1), n_ff - 1))
        return e, 0, tile + half * n_ff

    def oc(b, s, nv):
        return jnp.where(nv[b] > 0, jnp.maximum(s - n_ff, 0), n_out - 1)

    def x_block(b, s, xi):
        return xi[jnp.where(s < n_ff, b, jnp.minimum(b + 1, n_blocks - 1))]

    def down_window(b, s, be, nv):
        live = jnp.logical_and(nv[b] > 0, s >= n_ff - 1)
        return (jnp.where(live, be[b], be[jnp.maximum(b - 1, 0)]), 0,
                jnp.where(live, jnp.maximum(s - n_ff, 0), n_out - 1))

    b_gu3 = b_gu.reshape(N_EXPERTS, 1, 2 * D_FF)
    b_d3 = b_d.reshape(N_EXPERTS, 1, D_MODEL)
    grid_spec = pltpu.PrefetchScalarGridSpec(
        num_scalar_prefetch=3, grid=(n_blocks, n_ff + n_out),
        in_specs=[
            pl.BlockSpec((EXPERT_ROWS, D_MODEL // 2), lambda b, s, be, nv, xi: (x_block(b, s, xi), 0)),
            pl.BlockSpec((1, D_MODEL, FF_TILE), lambda b, s, be, nv, xi: up_window(b, s, be, nv, 0)),
            pl.BlockSpec((1, D_MODEL, FF_TILE), lambda b, s, be, nv, xi: up_window(b, s, be, nv, 1)),
            pl.BlockSpec((1, 1, FF_TILE), lambda b, s, be, nv, xi: up_window(b, s, be, nv, 0)),
            pl.BlockSpec((1, 1, FF_TILE), lambda b, s, be, nv, xi: up_window(b, s, be, nv, 1)),
            pl.BlockSpec((1, D_FF, OUT_TILE), lambda b, s, be, nv, xi: down_window(b, s, be, nv)),
            pl.BlockSpec((1, 1, OUT_TILE), lambda b, s, be, nv, xi: down_window(b, s, be, nv)),
        ],
        out_specs=pl.BlockSpec((EXPERT_ROWS, OUT_TILE // 2), lambda b, s, be, nv, xi: (xi[b], oc(b, s, nv))),
        scratch_shapes=[pltpu.VMEM((n_ff, EXPERT_ROWS, FF_TILE), BF16)])
    return pl.pallas_call(
        _experts_kernel, grid_spec=grid_spec, name="experts",
        out_shape=jax.ShapeDtypeStruct((n_blocks * EXPERT_ROWS, D_MODEL // 2), U32),
        compiler_params=_params(2),
    )(block_expert, block_valid, block_index, xb, w_gu, w_gu, b_gu3, b_gu3, w_d, b_d3)


def _combine_kernel(n_tiles, dest_ref, next_ref, x_ref, gate_ref, yb_ref, o_ref, buf, sem):
    tile = x_ref.shape[0]
    i = pl.program_id(0)
    slot = i % 2

    groups = tile // SUBLANES

    def copy(idx_ref, s, j, u, k):
        return pltpu.make_async_copy(yb_ref.at[pl.ds(idx_ref[0, j * (SUBLANES * TOP_K) + u * TOP_K + k], 1), :],
                                     buf.at[s, k, j, pl.ds(u, 1), :], sem.at[s])

    def start_tile(idx_ref, s):
        def start(j, carry):
            for u in range(SUBLANES):
                for k in range(TOP_K):
                    copy(idx_ref, s, j, u, k).start()
            return carry
        lax.fori_loop(0, groups, start, 0)

    pl.when(i == 0)(lambda: start_tile(dest_ref, 0))
    pl.when(i + 1 < n_tiles)(lambda: start_tile(next_ref, 1 - slot))

    def wait(j, carry):
        for u in range(SUBLANES):
            for k in range(TOP_K):
                copy(dest_ref, slot, j, u, k).wait()
        return carry
    lax.fori_loop(0, groups, wait, 0)

    gate = gate_ref[...]
    half = OUT_TILE // 2
    for g in range(D_MODEL // OUT_TILE):
        c0 = g * OUT_TILE
        acc_lo = x_ref[:, c0:c0 + half]
        acc_hi = x_ref[:, c0 + half:c0 + OUT_TILE]
        for k in range(TOP_K):
            lo, hi = _unpack_rows(buf[slot, k, :, :, g * half:(g + 1) * half].reshape(tile, half))
            acc_lo = acc_lo + gate[:, k:k + 1] * lo
            acc_hi = acc_hi + gate[:, k:k + 1] * hi
        o_ref[:, c0:c0 + half] = acc_lo
        o_ref[:, c0 + half:c0 + OUT_TILE] = acc_hi


def _combine(x, gate, dest, yb, tile):
    t = x.shape[0]
    n = t // tile
    dest = dest.reshape(n, 1, tile * TOP_K)
    return pl.pallas_call(
        functools.partial(_combine_kernel, n), grid=(n,), name="combine",
        in_specs=[pl.BlockSpec((None, 1, tile * TOP_K), lambda i: (i, 0, 0), memory_space=pltpu.SMEM),
                  pl.BlockSpec((None, 1, tile * TOP_K), lambda i: (jnp.minimum(i + 1, n - 1), 0, 0),
                               memory_space=pltpu.SMEM),
                  pl.BlockSpec((tile, D_MODEL), lambda i: (i, 0)),
                  pl.BlockSpec((tile, LANES), lambda i: (i, 0)),
                  pl.BlockSpec(memory_space=pl.ANY)],
        out_specs=pl.BlockSpec((tile, D_MODEL), lambda i: (i, 0)),
        out_shape=jax.ShapeDtypeStruct((t, D_MODEL), F32),
        scratch_shapes=[pltpu.VMEM((2, TOP_K, tile // SUBLANES, SUBLANES, D_MODEL // 2), U32),
                        pltpu.SemaphoreType.DMA((2,))],
        compiler_params=_params(),
    )(dest, dest, x, gate, yb)


def _rope_tables(pos):
    half = B_HEAD_DIM // 2
    inv_freq = ROPE_THETA ** (-jnp.arange(half, dtype=F32) / half)
    ang = pos.astype(F32)[:, None] * inv_freq[None, :]
    cos = jnp.cos(ang)
    sin = jnp.sin(ang)
    return jnp.tile(cos, (1, LANES // half)), jnp.tile(jnp.concatenate([-sin, sin], axis=1), (1, LANES // B_HEAD_DIM))


def _layer(l, xp, xs, cache_swa_k, cache_swa_v, cache_mem_k, cache_mem_v, mem_prompt,
           g_mix, w_in, g_sgu, w_s, b_s, g_q, g_k, sinks, g_out_a, g_out_b, w_out,
           g_mem_in, g_mem_src, w_mq, w_mk, w_mv, g_mq, g_mk, w_mo,
           g_moe, w_router, b_router, w_gate_up, b_gate_up, w_down, b_down):
    n_b, seq, _ = xp.shape
    n_db, dec_seq, _ = xs.shape
    tp = n_b * seq
    ts = n_db * dec_seq
    tile = min(TOKEN_TILE, seq)
    row = lambda a: a[l].reshape(1, -1)

    xp2 = xp.reshape(tp, D_MODEL)
    xs2 = xs.reshape(ts, D_MODEL)
    head = jnp.arange(QK_WIDTH, dtype=I32) // B_HEAD_DIM
    seg = (head[:, None] == jnp.arange(LANES, dtype=I32)[None, :]).astype(BF16)
    g_qk = jnp.concatenate([jnp.tile(g_q[l], B_HEADS), jnp.tile(g_k[l], B_KV_HEADS)]).reshape(1, -1)
    w_in_b = w_in[l].astype(BF16)
    cos_p, sin_p = _rope_tables(jnp.arange(seq, dtype=I32))
    cos_s, sin_s = _rope_tables(PAST_LEN + jnp.arange(SAMPLE_TILE, dtype=I32) % dec_seq)
    up, vap, qp, kp, vp = _in_proj(xp2, row(g_mix), w_in_b, row(g_sgu), g_qk, seg, seg.T, cos_p, sin_p,
                                   tile, "in_proj_prompt")
    us, vas, qs, ks, vs = _in_proj(xs2, row(g_mix), w_in_b, row(g_sgu), g_qk, seg, seg.T, cos_s, sin_s,
                                   SAMPLE_TILE, "in_proj_sample")

    w_out_b = w_out[l].astype(BF16)
    tri = jnp.tril(jnp.ones((MLP_CHUNK, MLP_CHUNK), bool))
    ws_p = jnp.where(tri[None], w_s[l], 0.0).astype(BF16)
    bias_p = jnp.repeat(b_s[l].T, A_HEAD_DIM, axis=1)
    x1p = _mixer_prompt(xp2, up, vap, qp, kp, vp, sinks[l], ws_p, bias_p, row(g_out_a), row(g_out_b), w_out_b, seq, tile)

    reps = SAMPLE_TILE // dec_seq
    tri_s = jnp.tril(jnp.ones((dec_seq, dec_seq), bool))
    ws_small = jnp.where(tri_s[None], w_s[l][:, :dec_seq, :dec_seq], 0.0)
    ws_s = jnp.einsum("ab,hts->hatbs", jnp.eye(reps, dtype=F32), ws_small).reshape(A_HEADS, SAMPLE_TILE, SAMPLE_TILE).astype(BF16)
    bias_s = jnp.tile(jnp.repeat(b_s[l][:, :dec_seq].T, A_HEAD_DIM, axis=1), (reps, 1))
    ck = cache_swa_k[l]
    cv = cache_swa_v[l]
    x1s = _mixer_sample(xs2, us, vas, qs, ks, vs, ck, cv, sinks[l], ws_s, bias_s,
                        row(g_out_a), row(g_out_b), w_out_b, dec_seq, SAMPLE_TILE)

    mk_p, mv_p = _mem_kv(mem_prompt.reshape(-1, D_MODEL), row(g_mem_src), w_mk[l].astype(BF16), w_mv[l].astype(BF16), row(g_mk))
    n_mem = mem_prompt.shape[1]
    mk_p4 = mk_p.reshape(n_b, n_mem, MEM_HEADS, MEM_HEAD_DIM)
    mv_p4 = mv_p.reshape(n_b, n_mem, MEM_HEADS, MEM_HEAD_DIM)
    w_r = jnp.pad(w_router[l], ((0, 0), (0, LANES - N_EXPERTS)))
    b_r = jnp.pad(b_router[l], (0, LANES - N_EXPERTS)).reshape(1, -1)
    w_r_hi = w_r.astype(BF16)
    w_r_lo = (w_r - w_r_hi.astype(F32)).astype(BF16)
    mem_args = (row(g_mem_in), w_mq[l].astype(BF16), row(g_mq), w_mo[l].astype(BF16), row(g_moe), w_r_hi, w_r_lo, b_r)
    x2p, hnp, gatep, routep, cntp = _memory_router(
        x1p, mk_p4, mv_p4,
        seq, *mem_args, tile, "memory_router_prompt")
    x2s, hns, gates, routes, cnts = _memory_router(
        x1s, cache_mem_k[l], cache_mem_v[l],
        dec_seq, *mem_args, SAMPLE_TILE, "memory_router_sample")

    cnt_p = cntp[0, :N_EXPERTS]
    cnt_s = cnts[0, :N_EXPERTS]
    total = cnt_p + cnt_s
    nblk = (total + EXPERT_ROWS - 1) // EXPERT_ROWS
    blk_end = jnp.cumsum(nblk)
    row_start = (blk_end - nblk) * EXPERT_ROWS
    n_blocks = (tp + ts) * TOP_K // EXPERT_ROWS + N_EXPERTS
    bidx = jnp.arange(n_blocks, dtype=I32)
    used = bidx < blk_end[-1]
    last = jnp.maximum(blk_end[-1] - 1, 0)
    bsafe = jnp.minimum(bidx, last)
    block_expert = jnp.minimum(jnp.searchsorted(blk_end, bsafe, side="right"), N_EXPERTS - 1).astype(I32)
    within = bsafe - (blk_end - nblk)[block_expert]
    block_valid = jnp.where(used, jnp.clip(total[block_expert] - within * EXPERT_ROWS, 0, EXPERT_ROWS), 0).astype(I32)
    dest_p = (row_start[routep[:TOP_K]] + routep[TOP_K:]).T
    dest_s = ((row_start + cnt_p)[routes[:TOP_K]] + routes[TOP_K:]).T

    xb = _dispatch(hnp, dest_p, None, n_blocks * EXPERT_ROWS, min(ROW_TILE, tp))
    xb = _dispatch(hns, dest_s, xb, n_blocks * EXPERT_ROWS, min(ROW_TILE, ts))
    yb = _experts(xb, block_expert, block_valid, bsafe.astype(I32), w_gate_up[l], b_gate_up[l], w_down[l], b_down[l])
    yp = _combine(x2p, gatep, dest_p, yb, min(ROW_TILE, tp))
    ys = _combine(x2s, gates, dest_s, yb, min(ROW_TILE, ts))

    new = dict(
        swa_k_p=kp.reshape(n_b, seq, B_KV_HEADS, B_HEAD_DIM)[:, seq - WINDOW:],
        swa_v_p=vp.reshape(n_b, seq, B_KV_HEADS, B_HEAD_DIM)[:, seq - WINDOW:],
        mem_k_p=mk_p4,
        mem_v_p=mv_p4,
        swa_k_s=ks.reshape(n_db, dec_seq, B_KV_HEADS, B_HEAD_DIM),
        swa_v_s=vs.reshape(n_db, dec_seq, B_KV_HEADS, B_HEAD_DIM),
        sgu_v_s=vas.reshape(n_db, dec_seq, A_HEADS, A_HEAD_DIM))
    return yp.reshape(n_b, seq, D_MODEL), ys.reshape(n_db, dec_seq, D_MODEL), new


def kernel(x_prompt, x_sample, cache_swa_k, cache_swa_v, cache_mem_k, cache_mem_v, mem_prompt, g_mix, w_in, g_sgu, w_s, b_s, g_q, g_k, sinks, g_out_a, g_out_b, w_out, g_mem_in, g_mem_src, w_mq, w_mk, w_mv, g_mq, g_mk, w_mo, g_moe, w_router, b_router, w_gate_up, b_gate_up, w_down, b_down):
    xp, xs = x_prompt, x_sample
    news = []
    for l in range(g_mix.shape[0]):
        xp, xs, new = _layer(l, xp, xs, cache_swa_k, cache_swa_v, cache_mem_k, cache_mem_v, mem_prompt,
                             g_mix, w_in, g_sgu, w_s, b_s, g_q, g_k, sinks, g_out_a, g_out_b, w_out,
                             g_mem_in, g_mem_src, w_mq, w_mk, w_mv, g_mq, g_mk, w_mo,
                             g_moe, w_router, b_router, w_gate_up, b_gate_up, w_down, b_down)
        news.append(new)
    stack = lambda name: jnp.stack([n[name] for n in news], 0)
    return (xp, xs, stack("swa_k_p"), stack("swa_v_p"), stack("mem_k_p"), stack("mem_v_p"),
            stack("swa_k_s"), stack("swa_v_s"), stack("sgu_v_s"))
```

```python
import functools

import jax
import jax.numpy as jnp
from jax import lax
from jax.experimental import pallas as pl
from jax.experimental.pallas import tpu as pltpu

F32 = jnp.float32
BF16 = jnp.bfloat16
I32 = jnp.int32
U32 = jnp.uint32

D_MODEL = 2048
PAST_LEN = 2048
CHUNK = 64
EPS = 1e-6
A_HEADS = 16
A_HEAD_DIM = 64
A_WIDTH = A_HEADS * A_HEAD_DIM
MLP_CHUNK = 128
B_HEADS = 16
B_KV_HEADS = 2
B_HEAD_DIM = 64
B_WIDTH = B_HEADS * B_HEAD_DIM
KV_WIDTH = B_KV_HEADS * B_HEAD_DIM
WINDOW = 128
ROPE_THETA = 10000.0
ATTN_SCALE = B_HEAD_DIM ** -0.5
Q_OFF = 2 * A_WIDTH
K_OFF = Q_OFF + B_WIDTH
V_OFF = K_OFF + KV_WIDTH
IN_WIDTH = V_OFF + KV_WIDTH
QK_WIDTH = B_WIDTH + KV_WIDTH
MEM_HEADS = 4
MEM_HEAD_DIM = 128
MEM_WIDTH = MEM_HEADS * MEM_HEAD_DIM
MEM_SCALE = MEM_HEAD_DIM ** -0.5
N_EXPERTS = 32
TOP_K = 4
D_FF = D_MODEL
SWIGLU_LIMIT = 7.0
SWIGLU_ALPHA = 1.702
NEG_BIG = -1e30

LANES = 128
SUBLANES = 8
VMEM_LIMIT = 56 * 1024 * 1024

TOKEN_TILE = 512
SAMPLE_TILE = 128
EXPERT_ROWS = 1280
EXPERT_SUB = 320
FF_TILE = 512
OUT_TILE = 1024
ROW_TILE = 256


def _dot(a, b):
    return jnp.dot(a, b, preferred_element_type=F32)


def _dot_nt(a, b):
    return lax.dot_general(a, b, (((1,), (1,)), ((), ())), preferred_element_type=F32)


def _rms(x):
    return x * lax.rsqrt(jnp.mean(x * x, axis=-1, keepdims=True) + EPS)


def _gelu(x):
    return 0.5 * x * (1.0 + lax.erf(x * 0.7071067811865476))


def _pack_rows(x):
    c = x.shape[1] // 2
    bits = pltpu.bitcast(x.astype(BF16).astype(F32), U32)
    return (bits[:, :c] >> 16) | (bits[:, c:] & jnp.uint32(0xFFFF0000))


def _unpack_rows(w):
    return pltpu.bitcast(w << 16, F32), pltpu.bitcast(w & jnp.uint32(0xFFFF0000), F32)


def _params(n_axes=1):
    return pltpu.CompilerParams(dimension_semantics=("arbitrary",) * n_axes,
                                vmem_limit_bytes=VMEM_LIMIT)


def _resident(shape):
    nd = len(shape)
    return pl.BlockSpec(shape, lambda *_: (0,) * nd, pipeline_mode=pl.Buffered(1))


def _mem_kv_kernel(mem_ref, gsrc_ref, wk_ref, wv_ref, gk_ref, k_ref, v_ref):
    m = (_rms(mem_ref[...]) * gsrc_ref[...]).astype(BF16)
    kz = _dot(m, wk_ref[...])
    gk = gk_ref[...]
    for h in range(MEM_HEADS):
        sl = slice(h * MEM_HEAD_DIM, (h + 1) * MEM_HEAD_DIM)
        k_ref[:, sl] = _rms(kz[:, sl]) * gk
    v_ref[...] = _dot(m, wv_ref[...])


def _mem_kv(mem, g_src, w_mk, w_mv, g_mk):
    n = mem.shape[0]
    out = jax.ShapeDtypeStruct((n, MEM_WIDTH), F32)
    return pl.pallas_call(
        _mem_kv_kernel, out_shape=(out, out), name="mem_kv",
        compiler_params=pltpu.CompilerParams(vmem_limit_bytes=VMEM_LIMIT),
    )(mem, g_src, w_mk, w_mv, g_mk)


def _in_proj_kernel(x_ref, gmix_ref, w_ref, gsgu_ref, gqk_ref, seg_ref, segt_ref, cos_ref, sin_ref,
                    u_ref, va_ref, q_ref, k_ref, v_ref):
    xn = (_rms(x_ref[...]) * gmix_ref[...]).astype(BF16)
    u_ref[...] = _gelu(_dot(xn, w_ref[:, 0:A_WIDTH])).astype(BF16)
    va_ref[...] = _rms(_gelu(_dot(xn, w_ref[:, A_WIDTH:Q_OFF]))) * gsgu_ref[...]
    v_ref[...] = _dot(xn, w_ref[:, V_OFF:IN_WIDTH])

    qk = _dot(xn, w_ref[:, Q_OFF:V_OFF])
    ssq = _dot((qk * qk).astype(BF16), seg_ref[...])
    inv = lax.rsqrt(ssq * (1.0 / B_HEAD_DIM) + EPS)
    inv_hi = inv.astype(BF16)
    inv_lo = (inv - inv_hi.astype(F32)).astype(BF16)
    inv_b = _dot(inv_hi, segt_ref[...]) + _dot(inv_lo, segt_ref[...])
    qkn = (qk * inv_b) * gqk_ref[...]

    cos = cos_ref[...]
    sin = sin_ref[...]
    lane = lax.broadcasted_iota(I32, cos.shape, 1)
    first_half = (lane % B_HEAD_DIM) < (B_HEAD_DIM // 2)
    for g in range(QK_WIDTH // LANES):
        xg = qkn[:, g * LANES:(g + 1) * LANES]
        rot = jnp.where(first_half, pltpu.roll(xg, LANES - B_HEAD_DIM // 2, 1),
                        pltpu.roll(xg, B_HEAD_DIM // 2, 1))
        r = xg * cos + rot * sin
        if g < B_WIDTH // LANES:
            q_ref[:, g * LANES:(g + 1) * LANES] = (r * ATTN_SCALE).astype(BF16)
        else:
            k_ref[...] = r


def _in_proj(x, g_mix, w_in, g_sgu, g_qk, seg, segt, cos_tab, sin_tab, tile, name):
    t = x.shape[0]
    n = t // tile
    tab_tiles = cos_tab.shape[0] // tile
    row = lambda w: pl.BlockSpec((tile, w), lambda i: (i, 0))
    tab = pl.BlockSpec((tile, LANES), lambda i: (i % tab_tiles, 0))
    return pl.pallas_call(
        _in_proj_kernel, grid=(n,), name=name,
        in_specs=[row(D_MODEL), _resident(g_mix.shape), _resident(w_in.shape), _resident(g_sgu.shape),
                  _resident(g_qk.shape), _resident(seg.shape), _resident(segt.shape), tab, tab],
        out_specs=[row(A_WIDTH), row(A_WIDTH), row(B_WIDTH), row(KV_WIDTH), row(KV_WIDTH)],
        out_shape=(jax.ShapeDtypeStruct((t, A_WIDTH), BF16), jax.ShapeDtypeStruct((t, A_WIDTH), F32),
                   jax.ShapeDtypeStruct((t, B_WIDTH), BF16), jax.ShapeDtypeStruct((t, KV_WIDTH), F32),
                   jax.ShapeDtypeStruct((t, KV_WIDTH), F32)),
        compiler_params=_params(),
    )(x, g_mix, w_in, g_sgu, g_qk, seg, segt, cos_tab, sin_tab)


def _split_heads(ref_dst_lo, ref_dst_hi, rows, x, swap):
    lane = lax.broadcasted_iota(I32, x.shape, 1)
    low = lane < B_HEAD_DIM
    zero = jnp.zeros_like(x)
    ref_dst_lo[0, rows, :] = jnp.where(low, x, zero).astype(BF16)
    ref_dst_hi[0, rows, :] = jnp.where(low, zero, swap).astype(BF16)
    ref_dst_lo[1, rows, :] = jnp.where(low, swap, zero).astype(BF16)
    ref_dst_hi[1, rows, :] = jnp.where(low, zero, x).astype(BF16)


def _sgu_chunk(u, va, ws_ref, bias):
    lane = lax.broadcasted_iota(I32, (MLP_CHUNK, LANES), 1)
    low = lane < A_HEAD_DIM
    outs = []
    for p in range(A_WIDTH // LANES):
        sl = slice(p * LANES, (p + 1) * LANES)
        v2 = va[:, sl]
        zero = jnp.zeros_like(v2)
        mixed = (_dot(ws_ref[2 * p], jnp.where(low, v2, zero).astype(BF16))
                 + _dot(ws_ref[2 * p + 1], jnp.where(low, zero, v2).astype(BF16)))
        outs.append(u[:, sl].astype(F32) * (mixed + bias[:, sl]))
    return jnp.concatenate(outs, axis=1)


def _attend(q2, kl, kh, vl, vh, mask, sink_even, sink_odd):
    outs = []
    for kk, vv, sink in ((kl, vl, sink_even), (kh, vh, sink_odd)):
        s = _dot_nt(q2, kk)
        if mask is not None:
            s = jnp.where(mask, s, NEG_BIG)
        m = jnp.maximum(jnp.max(s, axis=-1, keepdims=True), sink)
        p = jnp.exp(s - m)
        den = jnp.sum(p, axis=-1, keepdims=True) + jnp.exp(sink - m)
        outs.append(_dot(p.astype(BF16), vv) * (1.0 / den))
    return outs[0] + outs[1]


def _out_proj(oa, ob, ga_ref, gb_ref, wout_ref, x):
    a = (_rms(oa) * ga_ref[...]).astype(BF16)
    b = (_rms(ob) * gb_ref[...]).astype(BF16)
    return x + _dot(a, wout_ref[0:A_WIDTH, :]) + _dot(b, wout_ref[A_WIDTH:, :])


def _mixer_prompt_kernel(tiles_per_seq, sinks_ref, x_ref, u_ref, va_ref, q_ref, kc_ref, vc_ref, kp_ref, vp_ref,
                         ws_ref, bias_ref, ga_ref, gb_ref, wout_ref, o_ref,
                         kl_s, kh_s, vl_s, vh_s, oa_s, ob_s):
    tile = x_ref.shape[0]
    n_sub = tile // MLP_CHUNK
    seq_start = (pl.program_id(0) % tiles_per_seq) == 0

    for src_p, src_c, dl, dh in ((kp_ref, kc_ref, kl_s, kh_s), (vp_ref, vc_ref, vl_s, vh_s)):
        prev = src_p[...]
        cur = src_c[...]
        _split_heads(dl, dh, slice(0, WINDOW), prev, pltpu.roll(prev, B_HEAD_DIM, 1))
        _split_heads(dl, dh, slice(WINDOW, WINDOW + tile), cur, pltpu.roll(cur, B_HEAD_DIM, 1))

    qc = lax.broadcasted_iota(I32, (MLP_CHUNK, 2 * MLP_CHUNK), 0) // CHUNK
    kc = lax.broadcasted_iota(I32, (MLP_CHUNK, 2 * MLP_CHUNK), 1) // CHUNK
    window_mask = (kc >= qc) & (kc <= qc + 2)
    bias = bias_ref[...]

    def sub(j, carry):
        r0 = pl.multiple_of(j * MLP_CHUNK, MLP_CHUNK)
        rows = pl.ds(r0, MLP_CHUNK)
        keys = pl.ds(r0, 2 * MLP_CHUNK)
        first_key_chunk = jnp.where(jnp.logical_and(seq_start, j == 0), 2, 0)
        mask = window_mask & (kc >= first_key_chunk)
        oa_s[rows, :] = _sgu_chunk(u_ref[rows, :], va_ref[rows, :], ws_ref, bias)
        for p in range(B_WIDTH // LANES):
            h = p // (B_WIDTH // LANES // B_KV_HEADS)
            sl = slice(p * LANES, (p + 1) * LANES)
            ob_s[rows, sl] = _attend(q_ref[rows, sl], kl_s[h, keys, :], kh_s[h, keys, :],
                                     vl_s[h, keys, :], vh_s[h, keys, :], mask,
                                     sinks_ref[2 * p], sinks_ref[2 * p + 1])
        return carry

    lax.fori_loop(0, n_sub, sub, 0)
    o_ref[...] = _out_proj(oa_s[...], ob_s[...], ga_ref, gb_ref, wout_ref, x_ref[...])


def _mixer_prompt(x, u, va, q, k, v, sinks, ws, bias, g_a, g_b, w_out, seq, tile):
    t = x.shape[0]
    n = t // tile
    tiles_per_seq = seq // tile
    per = tile // WINDOW
    row = lambda w: pl.BlockSpec((tile, w), lambda i, s: (i, 0))
    prev = pl.BlockSpec((WINDOW, KV_WIDTH), lambda i, s: (jnp.maximum(i * per - 1, 0), 0))
    res = lambda a: pl.BlockSpec(a.shape, lambda i, s: (0,) * a.ndim, pipeline_mode=pl.Buffered(1))
    grid_spec = pltpu.PrefetchScalarGridSpec(
        num_scalar_prefetch=1, grid=(n,),
        in_specs=[row(D_MODEL), row(A_WIDTH), row(A_WIDTH), row(B_WIDTH), row(KV_WIDTH), row(KV_WIDTH), prev, prev,
                  res(ws), res(bias), res(g_a), res(g_b), res(w_out)],
        out_specs=row(D_MODEL),
        scratch_shapes=[pltpu.VMEM((B_KV_HEADS, WINDOW + tile, LANES), BF16) for _ in range(4)]
        + [pltpu.VMEM((tile, A_WIDTH), F32), pltpu.VMEM((tile, B_WIDTH), F32)])
    return pl.pallas_call(
        functools.partial(_mixer_prompt_kernel, tiles_per_seq), grid_spec=grid_spec, name="mixer_prompt",
        out_shape=jax.ShapeDtypeStruct((t, D_MODEL), F32), compiler_params=_params(),
    )(sinks, x, u, va, q, k, v, k, v, ws, bias, g_a, g_b, w_out)


def _mixer_sample_kernel(dec_seq, sinks_ref, x_ref, u_ref, va_ref, q_ref, kc_ref, vc_ref, ck_ref, cv_ref,
                         ws_ref, bias_ref, ga_ref, gb_ref, wout_ref, o_ref,
                         kl_s, kh_s, vl_s, vh_s, ob_s):
    tile = x_ref.shape[0]
    n_seq = tile // dec_seq
    n_cache = ck_ref.shape[1]
    n_keys = n_cache + dec_seq
    oa = _sgu_chunk(u_ref[...], va_ref[...], ws_ref, bias_ref[...])
    for b in range(n_seq):
        rows = slice(b * dec_seq, (b + 1) * dec_seq)
        for src_c, src_n, dl, dh in ((ck_ref, kc_ref, kl_s, kh_s), (cv_ref, vc_ref, vl_s, vh_s)):
            heads = [src_c[b, :, h, :] for h in range(B_KV_HEADS)]
            new = src_n[rows, :]
            _split_heads(dl, dh, slice(b * n_keys, b * n_keys + n_cache), jnp.concatenate(heads, axis=1),
                         jnp.concatenate(heads[::-1], axis=1))
            _split_heads(dl, dh, slice(b * n_keys + n_cache, (b + 1) * n_keys), new, pltpu.roll(new, B_HEAD_DIM, 1))
    q_seq = lax.broadcasted_iota(I32, (tile, n_seq * n_keys), 0) // dec_seq
    k_seq = lax.broadcasted_iota(I32, (tile, n_seq * n_keys), 1) // n_keys
    mask = q_seq == k_seq
    for p in range(B_WIDTH // LANES):
        h = p // (B_WIDTH // LANES // B_KV_HEADS)
        sl = slice(p * LANES, (p + 1) * LANES)
        ob_s[:, sl] = _attend(q_ref[:, sl], kl_s[h], kh_s[h], vl_s[h], vh_s[h], mask,
                              sinks_ref[2 * p], sinks_ref[2 * p + 1])
    o_ref[...] = _out_proj(oa, ob_s[...], ga_ref, gb_ref, wout_ref, x_ref[...])


def _mixer_sample(x, u, va, q, k, v, cache_k, cache_v, sinks, ws, bias, g_a, g_b, w_out, dec_seq, tile):
    t = x.shape[0]
    n = t // tile
    n_seq = tile // dec_seq
    n_cache = cache_k.shape[1]
    row = lambda w: pl.BlockSpec((tile, w), lambda i, s: (i, 0))
    cache = pl.BlockSpec((n_seq, n_cache, B_KV_HEADS, B_HEAD_DIM), lambda i, s: (i, 0, 0, 0))
    res = lambda a: pl.BlockSpec(a.shape, lambda i, s: (0,) * a.ndim, pipeline_mode=pl.Buffered(1))
    grid_spec = pltpu.PrefetchScalarGridSpec(
        num_scalar_prefetch=1, grid=(n,),
        in_specs=[row(D_MODEL), row(A_WIDTH), row(A_WIDTH), row(B_WIDTH), row(KV_WIDTH), row(KV_WIDTH), cache, cache,
                  res(ws), res(bias), res(g_a), res(g_b), res(w_out)],
        out_specs=row(D_MODEL),
        scratch_shapes=[pltpu.VMEM((B_KV_HEADS, n_seq * (n_cache + dec_seq), LANES), BF16) for _ in range(4)]
        + [pltpu.VMEM((tile, B_WIDTH), F32)])
    return pl.pallas_call(
        functools.partial(_mixer_sample_kernel, dec_seq), grid_spec=grid_spec, name="mixer_sample",
        out_shape=jax.ShapeDtypeStruct((t, D_MODEL), F32), compiler_params=_params(),
    )(sinks, x, u, va, q, k, v, cache_k, cache_v, ws, bias, g_a, g_b, w_out)


def _memory_router_kernel(rows_per_mem, x_ref, mk_ref, mv_ref, gin_ref, wq_ref, gq_ref, wo_ref,
                          gmoe_ref, wrh_ref, wrl_ref, br_ref,
                          x2_ref, hn_ref, gate_ref, route_ref, cnt_ref, o_s, base_s):
    tile = x_ref.shape[0]
    x = x_ref[...]
    qz = _dot((_rms(x) * gin_ref[...]).astype(BF16), wq_ref[...])
    gq = gq_ref[...]

    def head(ref, r, h):
        if len(ref.shape) == 4:
            return ref[r, :, h, :]
        return ref[r, :, h * MEM_HEAD_DIM:(h + 1) * MEM_HEAD_DIM]
    for h in range(MEM_HEADS):
        sl = slice(h * MEM_HEAD_DIM, (h + 1) * MEM_HEAD_DIM)
        qh = (_rms(qz[:, sl]) * gq).astype(BF16)
        for r in range(tile // rows_per_mem):
            rows = slice(r * rows_per_mem, (r + 1) * rows_per_mem)
            s = _dot_nt(qh[rows], head(mk_ref, r, h).astype(BF16)) * MEM_SCALE
            p = jnp.exp(s - jnp.max(s, axis=-1, keepdims=True))
            den = jnp.sum(p, axis=-1, keepdims=True)
            o_s[rows, sl] = _dot(p.astype(BF16), head(mv_ref, r, h).astype(BF16)) * (1.0 / den)
    x2 = x + _dot(o_s[...].astype(BF16), wo_ref[...])
    x2_ref[...] = x2
    hn = _rms(x2) * gmoe_ref[...]
    hn_ref[...] = _pack_rows(hn)

    hn_hi = hn.astype(BF16)
    hn_lo = (hn - hn_hi.astype(F32)).astype(BF16)
    logits = ((_dot(hn_lo, wrh_ref[...]) + _dot(hn_hi, wrl_ref[...])) + _dot(hn_hi, wrh_ref[...])) + br_ref[...]
    lane = lax.broadcasted_iota(I32, (tile, LANES), 1)
    lane_f = lane.astype(F32)
    work = jnp.where(lane < N_EXPERTS, logits, -jnp.inf)
    idx_out = jnp.zeros((tile, LANES), F32)
    val_out = jnp.zeros((tile, LANES), F32)
    hot = jnp.zeros((tile, LANES), F32)
    top = None
    picks = []
    for k in range(TOP_K):
        m = jnp.max(work, axis=-1, keepdims=True)
        pick = jnp.min(jnp.where(work == m, lane_f, float(LANES)), axis=-1, keepdims=True)
        chosen = lane_f == pick
        if top is None:
            top = m
        idx_out = jnp.where(lane == k, pick, idx_out)
        val_out = jnp.where(lane == k, jnp.exp(m - top), val_out)
        hot = jnp.where(chosen, 1.0, hot)
        work = jnp.where(chosen, -jnp.inf, work)
        picks.append(chosen)
    gate_ref[...] = val_out * (1.0 / jnp.sum(val_out, axis=-1, keepdims=True))

    @pl.when(pl.program_id(0) == 0)
    def _():
        base_s[...] = jnp.zeros_like(base_s)

    earlier = (lax.broadcasted_iota(I32, (tile, tile), 1) < lax.broadcasted_iota(I32, (tile, tile), 0))
    pos = _dot(jnp.where(earlier, 1.0, 0.0).astype(BF16), hot.astype(BF16)) + base_s[0:1, :]
    route = idx_out
    for k in range(TOP_K):
        route = jnp.where(lane == TOP_K + k, jnp.sum(jnp.where(picks[k], pos, 0.0), axis=-1, keepdims=True), route)
    route_ref[...] = route.T[0:2 * TOP_K, :].astype(I32)
    total = base_s[0:1, :] + jnp.sum(hot, axis=0, keepdims=True)
    base_s[...] = jnp.broadcast_to(total, base_s.shape)
    cnt_ref[...] = jnp.broadcast_to(total, cnt_ref.shape).astype(I32)


def _memory_router(x, mk, mv, rows_per_mem, g_in, w_mq, g_mq, w_mo, g_moe, w_r_hi, w_r_lo, b_r, tile, name):
    t = x.shape[0]
    n = t // tile
    mems = tile // rows_per_mem if rows_per_mem <= tile else 1
    per_mem_tiles = max(rows_per_mem // tile, 1)
    rpm = min(rows_per_mem, tile)
    row = lambda w: pl.BlockSpec((tile, w), lambda i: (i, 0))
    mem = pl.BlockSpec((mems,) + mk.shape[1:], lambda i: (i // per_mem_tiles,) + (0,) * (mk.ndim - 1))
    res = lambda a: pl.BlockSpec(a.shape, lambda i: (0,) * a.ndim, pipeline_mode=pl.Buffered(1))
    return pl.pallas_call(
        functools.partial(_memory_router_kernel, rpm), grid=(n,), name=name,
        in_specs=[row(D_MODEL), mem, mem, res(g_in), res(w_mq), res(g_mq), res(w_mo), res(g_moe), res(w_r_hi), res(w_r_lo), res(b_r)],
        out_specs=[row(D_MODEL), row(D_MODEL // 2), row(LANES),
                   pl.BlockSpec((2 * TOP_K, tile), lambda i: (0, i)), pl.BlockSpec((8, LANES), lambda i: (0, 0))],
        out_shape=(jax.ShapeDtypeStruct((t, D_MODEL), F32), jax.ShapeDtypeStruct((t, D_MODEL // 2), U32),
                   jax.ShapeDtypeStruct((t, LANES), F32), jax.ShapeDtypeStruct((2 * TOP_K, t), I32),
                   jax.ShapeDtypeStruct((8, LANES), I32)),
        scratch_shapes=[pltpu.VMEM((tile, MEM_WIDTH), F32), pltpu.VMEM((8, LANES), F32)],
        compiler_params=_params(),
    )(x, mk, mv, g_in, w_mq, g_mq, w_mo, g_moe, w_r_hi, w_r_lo, b_r)


def _dispatch_kernel(dest_ref, hn_ref, *rest):
    xb_ref, sem = rest[-2:]
    groups = hn_ref.shape[0]

    def copy(j, u, k):
        return pltpu.make_async_copy(hn_ref.at[j, pl.ds(u, 1), :],
                                     xb_ref.at[pl.ds(dest_ref[0, j * (SUBLANES * TOP_K) + u * TOP_K + k], 1), :], sem)

    def start(j, carry):
        for u in range(SUBLANES):
            for k in range(TOP_K):
                copy(j, u, k).start()
        return carry

    def wait(j, carry):
        for u in range(SUBLANES):
            for k in range(TOP_K):
                copy(j, u, k).wait()
        return carry

    lax.fori_loop(0, groups, start, 0)
    lax.fori_loop(0, groups, wait, 0)


def _dispatch(hn, dest, xb, n_rows, tile):
    t = hn.shape[0]
    n = t // tile
    dest = dest.reshape(n, 1, tile * TOP_K)
    in_specs = [pl.BlockSpec((None, 1, tile * TOP_K), lambda i: (i, 0, 0), memory_space=pltpu.SMEM),
                pl.BlockSpec((tile // SUBLANES, SUBLANES, hn.shape[1]), lambda i: (i, 0, 0))]
    args = [dest, hn.reshape(t // SUBLANES, SUBLANES, hn.shape[1])]
    if xb is not None:
        in_specs.append(pl.BlockSpec(memory_space=pl.ANY))
        args.append(xb)
    return pl.pallas_call(
        _dispatch_kernel, grid=(n,), name="dispatch",
        in_specs=in_specs,
        out_specs=pl.BlockSpec(memory_space=pl.ANY),
        out_shape=jax.ShapeDtypeStruct((n_rows, hn.shape[1]), hn.dtype),
        scratch_shapes=[pltpu.SemaphoreType.DMA(())],
        input_output_aliases={} if xb is None else {2: 0},
        compiler_params=_params(),
    )(*args)


def _for_overlapped(n, body):
    def trip(j, carry):
        for g in range(4):
            body(4 * j + g)
        return carry
    lax.fori_loop(0, n // 4, trip, 0)
    base = (n // 4) * 4

    @pl.when(n % 4 >= 2)
    def _():
        body(base)
        body(base + 1)

    @pl.when(n % 2 == 1)
    def _():
        body(n - 1)


def _experts_kernel(be_ref, nv_ref, xi_ref, x_ref, wg_ref, wu_ref, bg_ref, bu_ref, wd_ref, bd_ref, y_ref, h_s):
    del be_ref, xi_ref
    b = pl.program_id(0)
    s = pl.program_id(1)
    n_ff = D_FF // FF_TILE
    valid = nv_ref[b]
    n_sub = (valid + EXPERT_SUB - 1) // EXPERT_SUB

    @pl.when(s < n_ff)
    def _():
        bg = bg_ref[0]
        bu = bu_ref[0]

        def up(i):
            rows = pl.ds(pl.multiple_of(i * EXPERT_SUB, EXPERT_SUB), EXPERT_SUB)
            keep = (i * EXPERT_SUB + lax.broadcasted_iota(I32, (EXPERT_SUB, 1), 0)) < valid
            lo, hi = _unpack_rows(x_ref[rows, :])
            xs = jnp.concatenate([jnp.where(keep, lo, 0.0), jnp.where(keep, hi, 0.0)], axis=1)
            gate = jnp.minimum(_dot(xs, wg_ref[0]) + bg, SWIGLU_LIMIT)
            lin = jnp.clip(_dot(xs, wu_ref[0]) + bu, -SWIGLU_LIMIT, SWIGLU_LIMIT)
            act = (lin + 1.0) * (gate * (1.0 / (1.0 + jnp.exp(-SWIGLU_ALPHA * gate))))
            h_s[s, rows, :] = act.astype(BF16)
        _for_overlapped(n_sub, up)

    @pl.when(s >= n_ff)
    def _():
        bd = bd_ref[0]

        def down(i):
            rows = pl.ds(pl.multiple_of(i * EXPERT_SUB, EXPERT_SUB), EXPERT_SUB)
            h = jnp.concatenate([h_s[f, rows, :] for f in range(n_ff)], axis=1)
            y_ref[rows, :] = _pack_rows(_dot(h.astype(F32), wd_ref[0]) + bd)
        _for_overlapped(n_sub, down)


def _experts(xb, block_expert, block_valid, block_index, w_gu, b_gu, w_d, b_d):
    n_blocks = xb.shape[0] // EXPERT_ROWS
    n_ff = D_FF // FF_TILE
    n_out = D_MODEL // OUT_TILE

    def up_window(b, s, be, nv, half):
        nxt = jnp.minimum(b + 1, n_blocks - 1)
        ahead = jnp.logical_and(s >= n_ff, nv[nxt] > 0)
        e = jnp.where(ahead, be[nxt], be[b])
        tile = jnp.where(ahead, 0, jnp.where(nv[b] > 0, jnp.minimum(s, n_ff - 1), n_ff - 1))
        return e, 0, tile + half * n_ff

    def oc(b, s, nv):
        return jnp.where(nv[b] > 0, jnp.maximum(s - n_ff, 0), n_out - 1)

    def x_block(b, s, xi):
        return xi[jnp.where(s < n_ff, b, jnp.minimum(b + 1, n_blocks - 1))]

    def down_window(b, s, be, nv):
        live = jnp.logical_and(nv[b] > 0, s >= n_ff - 1)
        return (jnp.where(live, be[b], be[jnp.maximum(b - 1, 0)]), 0,
                jnp.where(live, jnp.maximum(s - n_ff, 0), n_out - 1))

    b_gu3 = b_gu.reshape(N_EXPERTS, 1, 2 * D_FF)
    b_d3 = b_d.reshape(N_EXPERTS, 1, D_MODEL)
    grid_spec = pltpu.PrefetchScalarGridSpec(
        num_scalar_prefetch=3, grid=(n_blocks, n_ff + n_out),
        in_specs=[
            pl.BlockSpec((EXPERT_ROWS, D_MODEL // 2), lambda b, s, be, nv, xi: (x_block(b, s, xi), 0)),
            pl.BlockSpec((1, D_MODEL, FF_TILE), lambda b, s, be, nv, xi: up_window(b, s, be, nv, 0)),
            pl.BlockSpec((1, D_MODEL, FF_TILE), lambda b, s, be, nv, xi: up_window(b, s, be, nv, 1)),
            pl.BlockSpec((1, 1, FF_TILE), lambda b, s, be, nv, xi: up_window(b, s, be, nv, 0)),
            pl.BlockSpec((1, 1, FF_TILE), lambda b, s, be, nv, xi: up_window(b, s, be, nv, 1)),
            pl.BlockSpec((1, D_FF, OUT_TILE), lambda b, s, be, nv, xi: down_window(b, s, be, nv)),
            pl.BlockSpec((1, 1, OUT_TILE), lambda b, s, be, nv, xi: down_window(b, s, be, nv)),
        ],
        out_specs=pl.BlockSpec((EXPERT_ROWS, OUT_TILE // 2), lambda b, s, be, nv, xi: (xi[b], oc(b, s, nv))),
        scratch_shapes=[pltpu.VMEM((n_ff, EXPERT_ROWS, FF_TILE), BF16)])
    return pl.pallas_call(
        _experts_kernel, grid_spec=grid_spec, name="experts",
        out_shape=jax.ShapeDtypeStruct((n_blocks * EXPERT_ROWS, D_MODEL // 2), U32),
        compiler_params=_params(2),
    )(block_expert, block_valid, block_index, xb, w_gu, w_gu, b_gu3, b_gu3, w_d, b_d3)


def _combine_kernel(n_tiles, dest_ref, next_ref, x_ref, gate_ref, yb_ref, o_ref, buf, sem):
    tile = x_ref.shape[0]
    i = pl.program_id(0)
    slot = i % 2

    groups = tile // SUBLANES

    def copy(idx_ref, s, j, u, k):
        return pltpu.make_async_copy(yb_ref.at[pl.ds(idx_ref[0, j * (SUBLANES * TOP_K) + u * TOP_K + k], 1), :],
                                     buf.at[s, k, j, pl.ds(u, 1), :], sem.at[s])

    def start_tile(idx_ref, s):
        def start(j, carry):
            for u in range(SUBLANES):
                for k in range(TOP_K):
                    copy(idx_ref, s, j, u, k).start()
            return carry
        lax.fori_loop(0, groups, start, 0)

    pl.when(i == 0)(lambda: start_tile(dest_ref, 0))
    pl.when(i + 1 < n_tiles)(lambda: start_tile(next_ref, 1 - slot))

    def wait(j, carry):
        for u in range(SUBLANES):
            for k in range(TOP_K):
                copy(dest_ref, slot, j, u, k).wait()
        return carry
    lax.fori_loop(0, groups, wait, 0)

    gate = gate_ref[...]
    half = OUT_TILE // 2
    for g in range(D_MODEL // OUT_TILE):
        c0 = g * OUT_TILE
        acc_lo = x_ref[:, c0:c0 + half]
        acc_hi = x_ref[:, c0 + half:c0 + OUT_TILE]
        for k in range(TOP_K):
            lo, hi = _unpack_rows(buf[slot, k, :, :, g * half:(g + 1) * half].reshape(tile, half))
            acc_lo = acc_lo + gate[:, k:k + 1] * lo
            acc_hi = acc_hi + gate[:, k:k + 1] * hi
        o_ref[:, c0:c0 + half] = acc_lo
        o_ref[:, c0 + half:c0 + OUT_TILE] = acc_hi


def _combine(x, gate, dest, yb, tile):
    t = x.shape[0]
    n = t // tile
    dest = dest.reshape(n, 1, tile * TOP_K)
    return pl.pallas_call(
        functools.partial(_combine_kernel, n), grid=(n,), name="combine",
        in_specs=[pl.BlockSpec((None, 1, tile * TOP_K), lambda i: (i, 0, 0), memory_space=pltpu.SMEM),
                  pl.BlockSpec((None, 1, tile * TOP_K), lambda i: (jnp.minimum(i + 1, n - 1), 0, 0),
                               memory_space=pltpu.SMEM),
                  pl.BlockSpec((tile, D_MODEL), lambda i: (i, 0)),
                  pl.BlockSpec((tile, LANES), lambda i: (i, 0)),
                  pl.BlockSpec(memory_space=pl.ANY)],
        out_specs=pl.BlockSpec((tile, D_MODEL), lambda i: (i, 0)),
        out_shape=jax.ShapeDtypeStruct((t, D_MODEL), F32),
        scratch_shapes=[pltpu.VMEM((2, TOP_K, tile // SUBLANES, SUBLANES, D_MODEL // 2), U32),
                        pltpu.SemaphoreType.DMA((2,))],
        compiler_params=_params(),
    )(dest, dest, x, gate, yb)


def _rope_tables(pos):
    half = B_HEAD_DIM // 2
    inv_freq = ROPE_THETA ** (-jnp.arange(half, dtype=F32) / half)
    ang = pos.astype(F32)[:, None] * inv_freq[None, :]
    cos = jnp.cos(ang)
    sin = jnp.sin(ang)
    return jnp.tile(cos, (1, LANES // half)), jnp.tile(jnp.concatenate([-sin, sin], axis=1), (1, LANES // B_HEAD_DIM))


def _layer(l, xp, xs, cache_swa_k, cache_swa_v, cache_mem_k, cache_mem_v, mem_prompt,
           g_mix, w_in, g_sgu, w_s, b_s, g_q, g_k, sinks, g_out_a, g_out_b, w_out,
           g_mem_in, g_mem_src, w_mq, w_mk, w_mv, g_mq, g_mk, w_mo,
           g_moe, w_router, b_router, w_gate_up, b_gate_up, w_down, b_down):
    n_b, seq, _ = xp.shape
    n_db, dec_seq, _ = xs.shape
    tp = n_b * seq
    ts = n_db * dec_seq
    tile = min(TOKEN_TILE, seq)
    row = lambda a: a[l].reshape(1, -1)

    xp2 = xp.reshape(tp, D_MODEL)
    xs2 = xs.reshape(ts, D_MODEL)
    head = jnp.arange(QK_WIDTH, dtype=I32) // B_HEAD_DIM
    seg = (head[:, None] == jnp.arange(LANES, dtype=I32)[None, :]).astype(BF16)
    g_qk = jnp.concatenate([jnp.tile(g_q[l], B_HEADS), jnp.tile(g_k[l], B_KV_HEADS)]).reshape(1, -1)
    w_in_b = w_in[l].astype(BF16)
    cos_p, sin_p = _rope_tables(jnp.arange(seq, dtype=I32))
    cos_s, sin_s = _rope_tables(PAST_LEN + jnp.arange(SAMPLE_TILE, dtype=I32) % dec_seq)
    up, vap, qp, kp, vp = _in_proj(xp2, row(g_mix), w_in_b, row(g_sgu), g_qk, seg, seg.T, cos_p, sin_p,
                                   tile, "in_proj_prompt")
    us, vas, qs, ks, vs = _in_proj(xs2, row(g_mix), w_in_b, row(g_sgu), g_qk, seg, seg.T, cos_s, sin_s,
                                   SAMPLE_TILE, "in_proj_sample")

    w_out_b = w_out[l].astype(BF16)
    tri = jnp.tril(jnp.ones((MLP_CHUNK, MLP_CHUNK), bool))
    ws_p = jnp.where(tri[None], w_s[l], 0.0).astype(BF16)
    bias_p = jnp.repeat(b_s[l].T, A_HEAD_DIM, axis=1)
    x1p = _mixer_prompt(xp2, up, vap, qp, kp, vp, sinks[l], ws_p, bias_p, row(g_out_a), row(g_out_b), w_out_b, seq, tile)

    reps = SAMPLE_TILE // dec_seq
    tri_s = jnp.tril(jnp.ones((dec_seq, dec_seq), bool))
    ws_small = jnp.where(tri_s[None], w_s[l][:, :dec_seq, :dec_seq], 0.0)
    ws_s = jnp.einsum("ab,hts->hatbs", jnp.eye(reps, dtype=F32), ws_small).reshape(A_HEADS, SAMPLE_TILE, SAMPLE_TILE).astype(BF16)
    bias_s = jnp.tile(jnp.repeat(b_s[l][:, :dec_seq].T, A_HEAD_DIM, axis=1), (reps, 1))
    ck = cache_swa_k[l]
    cv = cache_swa_v[l]
    x1s = _mixer_sample(xs2, us, vas, qs, ks, vs, ck, cv, sinks[l], ws_s, bias_s,
                        row(g_out_a), row(g_out_b), w_out_b, dec_seq, SAMPLE_TILE)

    mk_p, mv_p = _mem_kv(mem_prompt.reshape(-1, D_MODEL), row(g_mem_src), w_mk[l].astype(BF16), w_mv[l].astype(BF16), row(g_mk))
    n_mem = mem_prompt.shape[1]
    mk_p4 = mk_p.reshape(n_b, n_mem, MEM_HEADS, MEM_HEAD_DIM)
    mv_p4 = mv_p.reshape(n_b, n_mem, MEM_HEADS, MEM_HEAD_DIM)
    w_r = jnp.pad(w_router[l], ((0, 0), (0, LANES - N_EXPERTS)))
    b_r = jnp.pad(b_router[l], (0, LANES - N_EXPERTS)).reshape(1, -1)
    w_r_hi = w_r.astype(BF16)
    w_r_lo = (w_r - w_r_hi.astype(F32)).astype(BF16)
    mem_args = (row(g_mem_in), w_mq[l].astype(BF16), row(g_mq), w_mo[l].astype(BF16), row(g_moe), w_r_hi, w_r_lo, b_r)
    x2p, hnp, gatep, routep, cntp = _memory_router(
        x1p, mk_p.reshape(n_b, n_mem, MEM_WIDTH), mv_p.reshape(n_b, n_mem, MEM_WIDTH),
        seq, *mem_args, tile, "memory_router_prompt")
    x2s, hns, gates, routes, cnts = _memory_router(
        x1s, cache_mem_k[l], cache_mem_v[l],
        dec_seq, *mem_args, SAMPLE_TILE, "memory_router_sample")

    cnt_p = cntp[0, :N_EXPERTS]
    cnt_s = cnts[0, :N_EXPERTS]
    total = cnt_p + cnt_s
    nblk = (total + EXPERT_ROWS - 1) // EXPERT_ROWS
    blk_end = jnp.cumsum(nblk)
    row_start = (blk_end - nblk) * EXPERT_ROWS
    n_blocks = (tp + ts) * TOP_K // EXPERT_ROWS + N_EXPERTS
    bidx = jnp.arange(n_blocks, dtype=I32)
    used = bidx < blk_end[-1]
    last = jnp.maximum(blk_end[-1] - 1, 0)
    bsafe = jnp.minimum(bidx, last)
    block_expert = jnp.minimum(jnp.searchsorted(blk_end, bsafe, side="right"), N_EXPERTS - 1).astype(I32)
    within = bsafe - (blk_end - nblk)[block_expert]
    block_valid = jnp.where(used, jnp.clip(total[block_expert] - within * EXPERT_ROWS, 0, EXPERT_ROWS), 0).astype(I32)
    def destinations(route, start):
        experts = jnp.arange(N_EXPERTS, dtype=I32)[:, None, None]
        first = jnp.sum(jnp.where(route[None, :TOP_K] == experts, start[:, None, None], 0), axis=0)
        return (first + route[TOP_K:]).T

    dest_p = destinations(routep, row_start)
    dest_s = destinations(routes, row_start + cnt_p)

    xb = _dispatch(hnp, dest_p, None, n_blocks * EXPERT_ROWS, min(ROW_TILE, tp))
    xb = _dispatch(hns, dest_s, xb, n_blocks * EXPERT_ROWS, min(ROW_TILE, ts))
    yb = _experts(xb, block_expert, block_valid, bsafe.astype(I32), w_gate_up[l], b_gate_up[l], w_down[l], b_down[l])
    yp = _combine(x2p, gatep, dest_p, yb, min(ROW_TILE, tp))
    ys = _combine(x2s, gates, dest_s, yb, min(ROW_TILE, ts))

    new = dict(
        swa_k_p=kp.reshape(n_b, seq, B_KV_HEADS, B_HEAD_DIM)[:, seq - WINDOW:],
        swa_v_p=vp.reshape(n_b, seq, B_KV_HEADS, B_HEAD_DIM)[:, seq - WINDOW:],
        mem_k_p=mk_p4,
        mem_v_p=mv_p4,
        swa_k_s=ks.reshape(n_db, dec_seq, B_KV_HEADS, B_HEAD_DIM),
        swa_v_s=vs.reshape(n_db, dec_seq, B_KV_HEADS, B_HEAD_DIM),
        sgu_v_s=vas.reshape(n_db, dec_seq, A_HEADS, A_HEAD_DIM))
    return yp.reshape(n_b, seq, D_MODEL), ys.reshape(n_db, dec_seq, D_MODEL), new


def kernel(x_prompt, x_sample, cache_swa_k, cache_swa_v, cache_mem_k, cache_mem_v, mem_prompt, g_mix, w_in, g_sgu, w_s, b_s, g_q, g_k, sinks, g_out_a, g_out_b, w_out, g_mem_in, g_mem_src, w_mq, w_mk, w_mv, g_mq, g_mk, w_mo, g_moe, w_router, b_router, w_gate_up, b_gate_up, w_down, b_down):
    xp, xs = x_prompt, x_sample
    news = []
    for l in range(g_mix.shape[0]):
        xp, xs, new = _layer(l, xp, xs, cache_swa_k, cache_swa_v, cache_mem_k, cache_mem_v, mem_prompt,
                             g_mix, w_in, g_sgu, w_s, b_s, g_q, g_k, sinks, g_out_a, g_out_b, w_out,
                             g_mem_in, g_mem_src, w_mq, w_mk, w_mv, g_mq, g_mk, w_mo,
                             g_moe, w_router, b_router, w_gate_up, b_gate_up, w_down, b_down)
        news.append(new)
    stack = lambda name: jnp.stack([n[name] for n in news], 0)
    return (xp, xs, stack("swa_k_p"), stack("swa_v_p"), stack("mem_k_p"), stack("mem_v_p"),
            stack("swa_k_s"), stack("swa_v_s"), stack("sgu_v_s"))
```

```python
import functools

import jax
import jax.numpy as jnp
from jax import lax
from jax.experimental import pallas as pl
from jax.experimental.pallas import tpu as pltpu

F32 = jnp.float32
BF16 = jnp.bfloat16
I32 = jnp.int32
U32 = jnp.uint32

D_MODEL = 2048
PAST_LEN = 2048
CHUNK = 64
EPS = 1e-6
A_HEADS = 16
A_HEAD_DIM = 64
A_WIDTH = A_HEADS * A_HEAD_DIM
MLP_CHUNK = 128
B_HEADS = 16
B_KV_HEADS = 2
B_HEAD_DIM = 64
B_WIDTH = B_HEADS * B_HEAD_DIM
KV_WIDTH = B_KV_HEADS * B_HEAD_DIM
WINDOW = 128
ROPE_THETA = 10000.0
ATTN_SCALE = B_HEAD_DIM ** -0.5
Q_OFF = 2 * A_WIDTH
K_OFF = Q_OFF + B_WIDTH
V_OFF = K_OFF + KV_WIDTH
IN_WIDTH = V_OFF + KV_WIDTH
QK_WIDTH = B_WIDTH + KV_WIDTH
MEM_HEADS = 4
MEM_HEAD_DIM = 128
MEM_WIDTH = MEM_HEADS * MEM_HEAD_DIM
MEM_SCALE = MEM_HEAD_DIM ** -0.5
N_EXPERTS = 32
TOP_K = 4
D_FF = D_MODEL
SWIGLU_LIMIT = 7.0
SWIGLU_ALPHA = 1.702
NEG_BIG = -1e30

LANES = 128
SUBLANES = 8
VMEM_LIMIT = 56 * 1024 * 1024

TOKEN_TILE = 512
SAMPLE_TILE = 128
EXPERT_ROWS = 1280
EXPERT_SUB = 320
FF_TILE = 512
OUT_TILE = 1024
ROW_TILE = 256


def _dot(a, b):
    return jnp.dot(a, b, preferred_element_type=F32)


def _dot_nt(a, b):
    return lax.dot_general(a, b, (((1,), (1,)), ((), ())), preferred_element_type=F32)


def _rms(x):
    return x * lax.rsqrt(jnp.mean(x * x, axis=-1, keepdims=True) + EPS)


def _gelu(x):
    return 0.5 * x * (1.0 + lax.erf(x * 0.7071067811865476))


def _pack_rows(x):
    c = x.shape[1] // 2
    bits = pltpu.bitcast(x.astype(BF16).astype(F32), U32)
    return (bits[:, :c] >> 16) | (bits[:, c:] & jnp.uint32(0xFFFF0000))


def _unpack_rows(w):
    return pltpu.bitcast(w << 16, F32), pltpu.bitcast(w & jnp.uint32(0xFFFF0000), F32)


def _params(n_axes=1):
    return pltpu.CompilerParams(dimension_semantics=("arbitrary",) * n_axes,
                                vmem_limit_bytes=VMEM_LIMIT)


def _resident(shape):
    nd = len(shape)
    return pl.BlockSpec(shape, lambda *_: (0,) * nd, pipeline_mode=pl.Buffered(1))


def _mem_kv_kernel(mem_ref, gsrc_ref, wk_ref, wv_ref, gk_ref, k_ref, v_ref):
    m = (_rms(mem_ref[...]) * gsrc_ref[...]).astype(BF16)
    kz = _dot(m, wk_ref[...])
    gk = gk_ref[...]
    for h in range(MEM_HEADS):
        sl = slice(h * MEM_HEAD_DIM, (h + 1) * MEM_HEAD_DIM)
        k_ref[:, sl] = _rms(kz[:, sl]) * gk
    v_ref[...] = _dot(m, wv_ref[...])


def _mem_kv(mem, g_src, w_mk, w_mv, g_mk):
    n = mem.shape[0]
    out = jax.ShapeDtypeStruct((n, MEM_WIDTH), F32)
    return pl.pallas_call(
        _mem_kv_kernel, out_shape=(out, out), name="mem_kv",
        compiler_params=pltpu.CompilerParams(vmem_limit_bytes=VMEM_LIMIT),
    )(mem, g_src, w_mk, w_mv, g_mk)


def _in_proj_kernel(x_ref, gmix_ref, w_ref, gsgu_ref, gqk_ref, seg_ref, segt_ref, cos_ref, sin_ref,
                    u_ref, va_ref, q_ref, k_ref, v_ref):
    xn = (_rms(x_ref[...]) * gmix_ref[...]).astype(BF16)
    u_ref[...] = _gelu(_dot(xn, w_ref[:, 0:A_WIDTH])).astype(BF16)
    va_ref[...] = _rms(_gelu(_dot(xn, w_ref[:, A_WIDTH:Q_OFF]))) * gsgu_ref[...]
    v_ref[...] = _dot(xn, w_ref[:, V_OFF:IN_WIDTH])

    qk = _dot(xn, w_ref[:, Q_OFF:V_OFF])
    ssq = _dot((qk * qk).astype(BF16), seg_ref[...])
    inv = lax.rsqrt(ssq * (1.0 / B_HEAD_DIM) + EPS)
    inv_hi = inv.astype(BF16)
    inv_lo = (inv - inv_hi.astype(F32)).astype(BF16)
    inv_b = _dot(inv_hi, segt_ref[...]) + _dot(inv_lo, segt_ref[...])
    qkn = (qk * inv_b) * gqk_ref[...]

    cos = cos_ref[...]
    sin = sin_ref[...]
    lane = lax.broadcasted_iota(I32, cos.shape, 1)
    first_half = (lane % B_HEAD_DIM) < (B_HEAD_DIM // 2)
    for g in range(QK_WIDTH // LANES):
        xg = qkn[:, g * LANES:(g + 1) * LANES]
        rot = jnp.where(first_half, pltpu.roll(xg, LANES - B_HEAD_DIM // 2, 1),
                        pltpu.roll(xg, B_HEAD_DIM // 2, 1))
        r = xg * cos + rot * sin
        if g < B_WIDTH // LANES:
            q_ref[:, g * LANES:(g + 1) * LANES] = (r * ATTN_SCALE).astype(BF16)
        else:
            k_ref[...] = r


def _in_proj(x, g_mix, w_in, g_sgu, g_qk, seg, segt, cos_tab, sin_tab, tile, name):
    t = x.shape[0]
    n = t // tile
    tab_tiles = cos_tab.shape[0] // tile
    row = lambda w: pl.BlockSpec((tile, w), lambda i: (i, 0))
    tab = pl.BlockSpec((tile, LANES), lambda i: (i % tab_tiles, 0))
    return pl.pallas_call(
        _in_proj_kernel, grid=(n,), name=name,
        in_specs=[row(D_MODEL), _resident(g_mix.shape), _resident(w_in.shape), _resident(g_sgu.shape),
                  _resident(g_qk.shape), _resident(seg.shape), _resident(segt.shape), tab, tab],
        out_specs=[row(A_WIDTH), row(A_WIDTH), row(B_WIDTH), row(KV_WIDTH), row(KV_WIDTH)],
        out_shape=(jax.ShapeDtypeStruct((t, A_WIDTH), BF16), jax.ShapeDtypeStruct((t, A_WIDTH), F32),
                   jax.ShapeDtypeStruct((t, B_WIDTH), BF16), jax.ShapeDtypeStruct((t, KV_WIDTH), F32),
                   jax.ShapeDtypeStruct((t, KV_WIDTH), F32)),
        compiler_params=_params(),
    )(x, g_mix, w_in, g_sgu, g_qk, seg, segt, cos_tab, sin_tab)


def _split_heads(ref_dst_lo, ref_dst_hi, rows, x, swap):
    lane = lax.broadcasted_iota(I32, x.shape, 1)
    low = lane < B_HEAD_DIM
    zero = jnp.zeros_like(x)
    ref_dst_lo[0, rows, :] = jnp.where(low, x, zero).astype(BF16)
    ref_dst_hi[0, rows, :] = jnp.where(low, zero, swap).astype(BF16)
    ref_dst_lo[1, rows, :] = jnp.where(low, swap, zero).astype(BF16)
    ref_dst_hi[1, rows, :] = jnp.where(low, zero, x).astype(BF16)


def _sgu_chunk(u, va, ws_ref, bias):
    lane = lax.broadcasted_iota(I32, (MLP_CHUNK, LANES), 1)
    low = lane < A_HEAD_DIM
    outs = []
    for p in range(A_WIDTH // LANES):
        sl = slice(p * LANES, (p + 1) * LANES)
        v2 = va[:, sl]
        zero = jnp.zeros_like(v2)
        mixed = (_dot(ws_ref[2 * p], jnp.where(low, v2, zero).astype(BF16))
                 + _dot(ws_ref[2 * p + 1], jnp.where(low, zero, v2).astype(BF16)))
        outs.append(u[:, sl].astype(F32) * (mixed + bias[:, sl]))
    return jnp.concatenate(outs, axis=1)


def _attend(q2, kl, kh, vl, vh, mask, sink_even, sink_odd):
    outs = []
    for kk, vv, sink in ((kl, vl, sink_even), (kh, vh, sink_odd)):
        s = _dot_nt(q2, kk)
        if mask is not None:
            s = jnp.where(mask, s, NEG_BIG)
        m = jnp.maximum(jnp.max(s, axis=-1, keepdims=True), sink)
        p = jnp.exp(s - m)
        den = jnp.sum(p, axis=-1, keepdims=True) + jnp.exp(sink - m)
        outs.append(_dot(p.astype(BF16), vv) * (1.0 / den))
    return outs[0] + outs[1]


def _out_proj(oa, ob, ga_ref, gb_ref, wout_ref, x):
    a = (_rms(oa) * ga_ref[...]).astype(BF16)
    b = (_rms(ob) * gb_ref[...]).astype(BF16)
    return x + _dot(a, wout_ref[0:A_WIDTH, :]) + _dot(b, wout_ref[A_WIDTH:, :])


def _mixer_prompt_kernel(tiles_per_seq, sinks_ref, x_ref, u_ref, va_ref, q_ref, kc_ref, vc_ref, kp_ref, vp_ref,
                         ws_ref, bias_ref, ga_ref, gb_ref, wout_ref, o_ref,
                         kl_s, kh_s, vl_s, vh_s, oa_s, ob_s):
    tile = x_ref.shape[0]
    n_sub = tile // MLP_CHUNK
    seq_start = (pl.program_id(0) % tiles_per_seq) == 0

    for src_p, src_c, dl, dh in ((kp_ref, kc_ref, kl_s, kh_s), (vp_ref, vc_ref, vl_s, vh_s)):
        prev = src_p[...]
        cur = src_c[...]
        _split_heads(dl, dh, slice(0, WINDOW), prev, pltpu.roll(prev, B_HEAD_DIM, 1))
        _split_heads(dl, dh, slice(WINDOW, WINDOW + tile), cur, pltpu.roll(cur, B_HEAD_DIM, 1))

    qc = lax.broadcasted_iota(I32, (MLP_CHUNK, 2 * MLP_CHUNK), 0) // CHUNK
    kc = lax.broadcasted_iota(I32, (MLP_CHUNK, 2 * MLP_CHUNK), 1) // CHUNK
    window_mask = (kc >= qc) & (kc <= qc + 2)
    bias = bias_ref[...]

    def sub(j, carry):
        r0 = pl.multiple_of(j * MLP_CHUNK, MLP_CHUNK)
        rows = pl.ds(r0, MLP_CHUNK)
        keys = pl.ds(r0, 2 * MLP_CHUNK)
        first_key_chunk = jnp.where(jnp.logical_and(seq_start, j == 0), 2, 0)
        mask = window_mask & (kc >= first_key_chunk)
        oa_s[rows, :] = _sgu_chunk(u_ref[rows, :], va_ref[rows, :], ws_ref, bias)
        for p in range(B_WIDTH // LANES):
            h = p // (B_WIDTH // LANES // B_KV_HEADS)
            sl = slice(p * LANES, (p + 1) * LANES)
            ob_s[rows, sl] = _attend(q_ref[rows, sl], kl_s[h, keys, :], kh_s[h, keys, :],
                                     vl_s[h, keys, :], vh_s[h, keys, :], mask,
                                     sinks_ref[2 * p], sinks_ref[2 * p + 1])
        return carry

    lax.fori_loop(0, n_sub, sub, 0)
    o_ref[...] = _out_proj(oa_s[...], ob_s[...], ga_ref, gb_ref, wout_ref, x_ref[...])


def _mixer_prompt(x, u, va, q, k, v, sinks, ws, bias, g_a, g_b, w_out, seq, tile):
    t = x.shape[0]
    n = t // tile
    tiles_per_seq = seq // tile
    per = tile // WINDOW
    row = lambda w: pl.BlockSpec((tile, w), lambda i, s: (i, 0))
    prev = pl.BlockSpec((WINDOW, KV_WIDTH), lambda i, s: (jnp.maximum(i * per - 1, 0), 0))
    res = lambda a: pl.BlockSpec(a.shape, lambda i, s: (0,) * a.ndim, pipeline_mode=pl.Buffered(1))
    grid_spec = pltpu.PrefetchScalarGridSpec(
        num_scalar_prefetch=1, grid=(n,),
        in_specs=[row(D_MODEL), row(A_WIDTH), row(A_WIDTH), row(B_WIDTH), row(KV_WIDTH), row(KV_WIDTH), prev, prev,
                  res(ws), res(bias), res(g_a), res(g_b), res(w_out)],
        out_specs=row(D_MODEL),
        scratch_shapes=[pltpu.VMEM((B_KV_HEADS, WINDOW + tile, LANES), BF16) for _ in range(4)]
        + [pltpu.VMEM((tile, A_WIDTH), F32), pltpu.VMEM((tile, B_WIDTH), F32)])
    return pl.pallas_call(
        functools.partial(_mixer_prompt_kernel, tiles_per_seq), grid_spec=grid_spec, name="mixer_prompt",
        out_shape=jax.ShapeDtypeStruct((t, D_MODEL), F32), compiler_params=_params(),
    )(sinks, x, u, va, q, k, v, k, v, ws, bias, g_a, g_b, w_out)


def _mixer_sample_kernel(dec_seq, sinks_ref, x_ref, u_ref, va_ref, q_ref, kc_ref, vc_ref, ck_ref, cv_ref,
                         ws_ref, bias_ref, ga_ref, gb_ref, wout_ref, o_ref,
                         kl_s, kh_s, vl_s, vh_s, ob_s):
    tile = x_ref.shape[0]
    n_seq = tile // dec_seq
    n_cache = ck_ref.shape[1]
    n_keys = n_cache + dec_seq
    oa = _sgu_chunk(u_ref[...], va_ref[...], ws_ref, bias_ref[...])
    for b in range(n_seq):
        rows = slice(b * dec_seq, (b + 1) * dec_seq)
        for src_c, src_n, dl, dh in ((ck_ref, kc_ref, kl_s, kh_s), (cv_ref, vc_ref, vl_s, vh_s)):
            heads = [src_c[b, :, h, :] for h in range(B_KV_HEADS)]
            new = src_n[rows, :]
            _split_heads(dl, dh, slice(b * n_keys, b * n_keys + n_cache), jnp.concatenate(heads, axis=1),
                         jnp.concatenate(heads[::-1], axis=1))
            _split_heads(dl, dh, slice(b * n_keys + n_cache, (b + 1) * n_keys), new, pltpu.roll(new, B_HEAD_DIM, 1))
    q_seq = lax.broadcasted_iota(I32, (tile, n_seq * n_keys), 0) // dec_seq
    k_seq = lax.broadcasted_iota(I32, (tile, n_seq * n_keys), 1) // n_keys
    mask = q_seq == k_seq
    for p in range(B_WIDTH // LANES):
        h = p // (B_WIDTH // LANES // B_KV_HEADS)
        sl = slice(p * LANES, (p + 1) * LANES)
        ob_s[:, sl] = _attend(q_ref[:, sl], kl_s[h], kh_s[h], vl_s[h], vh_s[h], mask,
                              sinks_ref[2 * p], sinks_ref[2 * p + 1])
    o_ref[...] = _out_proj(oa, ob_s[...], ga_ref, gb_ref, wout_ref, x_ref[...])


def _mixer_sample(x, u, va, q, k, v, cache_k, cache_v, sinks, ws, bias, g_a, g_b, w_out, dec_seq, tile):
    t = x.shape[0]
    n = t // tile
    n_seq = tile // dec_seq
    n_cache = cache_k.shape[1]
    row = lambda w: pl.BlockSpec((tile, w), lambda i, s: (i, 0))
    cache = pl.BlockSpec((n_seq, n_cache, B_KV_HEADS, B_HEAD_DIM), lambda i, s: (i, 0, 0, 0))
    res = lambda a: pl.BlockSpec(a.shape, lambda i, s: (0,) * a.ndim, pipeline_mode=pl.Buffered(1))
    grid_spec = pltpu.PrefetchScalarGridSpec(
        num_scalar_prefetch=1, grid=(n,),
        in_specs=[row(D_MODEL), row(A_WIDTH), row(A_WIDTH), row(B_WIDTH), row(KV_WIDTH), row(KV_WIDTH), cache, cache,
                  res(ws), res(bias), res(g_a), res(g_b), res(w_out)],
        out_specs=row(D_MODEL),
        scratch_shapes=[pltpu.VMEM((B_KV_HEADS, n_seq * (n_cache + dec_seq), LANES), BF16) for _ in range(4)]
        + [pltpu.VMEM((tile, B_WIDTH), F32)])
    return pl.pallas_call(
        functools.partial(_mixer_sample_kernel, dec_seq), grid_spec=grid_spec, name="mixer_sample",
        out_shape=jax.ShapeDtypeStruct((t, D_MODEL), F32), compiler_params=_params(),
    )(sinks, x, u, va, q, k, v, cache_k, cache_v, ws, bias, g_a, g_b, w_out)


def _memory_router_kernel(rows_per_mem, x_ref, mk_ref, mv_ref, gin_ref, wq_ref, gq_ref, wo_ref,
                          gmoe_ref, wrh_ref, wrl_ref, br_ref,
                          x2_ref, hn_ref, gate_ref, route_ref, cnt_ref, o_s, base_s):
    tile = x_ref.shape[0]
    x = x_ref[...]
    qz = _dot((_rms(x) * gin_ref[...]).astype(BF16), wq_ref[...])
    gq = gq_ref[...]

    def head(ref, r, h):
        if len(ref.shape) == 4:
            return ref[r, :, h, :]
        return ref[r, :, h * MEM_HEAD_DIM:(h + 1) * MEM_HEAD_DIM]
    for h in range(MEM_HEADS):
        sl = slice(h * MEM_HEAD_DIM, (h + 1) * MEM_HEAD_DIM)
        qh = (_rms(qz[:, sl]) * gq).astype(BF16)
        for r in range(tile // rows_per_mem):
            rows = slice(r * rows_per_mem, (r + 1) * rows_per_mem)
            s = _dot_nt(qh[rows], head(mk_ref, r, h).astype(BF16)) * MEM_SCALE
            p = jnp.exp(s - jnp.max(s, axis=-1, keepdims=True))
            den = jnp.sum(p, axis=-1, keepdims=True)
            o_s[rows, sl] = _dot(p.astype(BF16), head(mv_ref, r, h).astype(BF16)) * (1.0 / den)
    x2 = x + _dot(o_s[...].astype(BF16), wo_ref[...])
    x2_ref[...] = x2
    hn = _rms(x2) * gmoe_ref[...]
    hn_ref[...] = _pack_rows(hn)

    hn_hi = hn.astype(BF16)
    hn_lo = (hn - hn_hi.astype(F32)).astype(BF16)
    logits = ((_dot(hn_lo, wrh_ref[...]) + _dot(hn_hi, wrl_ref[...])) + _dot(hn_hi, wrh_ref[...])) + br_ref[...]
    lane = lax.broadcasted_iota(I32, (tile, LANES), 1)
    lane_f = lane.astype(F32)
    work = jnp.where(lane < N_EXPERTS, logits, -jnp.inf)
    idx_out = jnp.zeros((tile, LANES), F32)
    val_out = jnp.zeros((tile, LANES), F32)
    hot = jnp.zeros((tile, LANES), F32)
    top = None
    picks = []
    for k in range(TOP_K):
        m = jnp.max(work, axis=-1, keepdims=True)
        pick = jnp.min(jnp.where(work == m, lane_f, float(LANES)), axis=-1, keepdims=True)
        chosen = lane_f == pick
        if top is None:
            top = m
        idx_out = jnp.where(lane == k, pick, idx_out)
        val_out = jnp.where(lane == k, jnp.exp(m - top), val_out)
        hot = jnp.where(chosen, 1.0, hot)
        work = jnp.where(chosen, -jnp.inf, work)
        picks.append(chosen)
    gate_ref[...] = val_out * (1.0 / jnp.sum(val_out, axis=-1, keepdims=True))

    @pl.when(pl.program_id(0) == 0)
    def _():
        base_s[...] = jnp.zeros_like(base_s)

    earlier = (lax.broadcasted_iota(I32, (tile, tile), 1) < lax.broadcasted_iota(I32, (tile, tile), 0))
    pos = _dot(jnp.where(earlier, 1.0, 0.0).astype(BF16), hot.astype(BF16)) + base_s[0:1, :]
    route = idx_out
    for k in range(TOP_K):
        route = jnp.where(lane == TOP_K + k, jnp.sum(jnp.where(picks[k], pos, 0.0), axis=-1, keepdims=True), route)
    route_ref[...] = route.T[0:2 * TOP_K, :].astype(I32)
    total = base_s[0:1, :] + jnp.sum(hot, axis=0, keepdims=True)
    base_s[...] = jnp.broadcast_to(total, base_s.shape)
    cnt_ref[...] = jnp.broadcast_to(total, cnt_ref.shape).astype(I32)


def _memory_router(x, mk, mv, rows_per_mem, g_in, w_mq, g_mq, w_mo, g_moe, w_r_hi, w_r_lo, b_r, tile, name):
    t = x.shape[0]
    n = t // tile
    mems = tile // rows_per_mem if rows_per_mem <= tile else 1
    per_mem_tiles = max(rows_per_mem // tile, 1)
    rpm = min(rows_per_mem, tile)
    row = lambda w: pl.BlockSpec((tile, w), lambda i: (i, 0))
    mem = pl.BlockSpec((mems,) + mk.shape[1:], lambda i: (i // per_mem_tiles,) + (0,) * (mk.ndim - 1))
    res = lambda a: pl.BlockSpec(a.shape, lambda i: (0,) * a.ndim, pipeline_mode=pl.Buffered(1))
    return pl.pallas_call(
        functools.partial(_memory_router_kernel, rpm), grid=(n,), name=name,
        in_specs=[row(D_MODEL), mem, mem, res(g_in), res(w_mq), res(g_mq), res(w_mo), res(g_moe), res(w_r_hi), res(w_r_lo), res(b_r)],
        out_specs=[row(D_MODEL), row(D_MODEL // 2), row(LANES),
                   pl.BlockSpec((2 * TOP_K, tile), lambda i: (0, i)), pl.BlockSpec((8, LANES), lambda i: (0, 0))],
        out_shape=(jax.ShapeDtypeStruct((t, D_MODEL), F32), jax.ShapeDtypeStruct((t, D_MODEL // 2), U32),
                   jax.ShapeDtypeStruct((t, LANES), F32), jax.ShapeDtypeStruct((2 * TOP_K, t), I32),
                   jax.ShapeDtypeStruct((8, LANES), I32)),
        scratch_shapes=[pltpu.VMEM((tile, MEM_WIDTH), F32), pltpu.VMEM((8, LANES), F32)],
        compiler_params=_params(),
    )(x, mk, mv, g_in, w_mq, g_mq, w_mo, g_moe, w_r_hi, w_r_lo, b_r)


def _dispatch_kernel(dest_ref, hn_ref, *rest):
    xb_ref, sem = rest[-2:]
    groups = hn_ref.shape[0]

    def copy(j, u, k):
        return pltpu.make_async_copy(hn_ref.at[j, pl.ds(u, 1), :],
                                     xb_ref.at[pl.ds(dest_ref[0, j * (SUBLANES * TOP_K) + u * TOP_K + k], 1), :], sem)

    def start(j, carry):
        for u in range(SUBLANES):
            for k in range(TOP_K):
                copy(j, u, k).start()
        return carry

    def wait(j, carry):
        for u in range(SUBLANES):
            for k in range(TOP_K):
                copy(j, u, k).wait()
        return carry

    lax.fori_loop(0, groups, start, 0)
    lax.fori_loop(0, groups, wait, 0)


def _dispatch(hn, dest, xb, n_rows, tile):
    t = hn.shape[0]
    n = t // tile
    dest = dest.reshape(n, 1, tile * TOP_K)
    in_specs = [pl.BlockSpec((None, 1, tile * TOP_K), lambda i: (i, 0, 0), memory_space=pltpu.SMEM),
                pl.BlockSpec((tile // SUBLANES, SUBLANES, hn.shape[1]), lambda i: (i, 0, 0))]
    args = [dest, hn.reshape(t // SUBLANES, SUBLANES, hn.shape[1])]
    if xb is not None:
        in_specs.append(pl.BlockSpec(memory_space=pl.ANY))
        args.append(xb)
    return pl.pallas_call(
        _dispatch_kernel, grid=(n,), name="dispatch",
        in_specs=in_specs,
        out_specs=pl.BlockSpec(memory_space=pl.ANY),
        out_shape=jax.ShapeDtypeStruct((n_rows, hn.shape[1]), hn.dtype),
        scratch_shapes=[pltpu.SemaphoreType.DMA(())],
        input_output_aliases={} if xb is None else {2: 0},
        compiler_params=_params(),
    )(*args)


def _for_overlapped(n, body):
    def trip(j, carry):
        for g in range(4):
            body(4 * j + g)
        return carry
    lax.fori_loop(0, n // 4, trip, 0)
    base = (n // 4) * 4

    @pl.when(n % 4 >= 2)
    def _():
        body(base)
        body(base + 1)

    @pl.when(n % 2 == 1)
    def _():
        body(n - 1)


def _experts_kernel(be_ref, nv_ref, x_ref, wg_ref, wu_ref, bg_ref, bu_ref, wd_ref, bd_ref, y_ref, h_s):
    del be_ref
    b = pl.program_id(0)
    s = pl.program_id(1)
    n_ff = D_FF // FF_TILE
    valid = nv_ref[b]
    n_sub = (valid + EXPERT_SUB - 1) // EXPERT_SUB

    @pl.when(s < n_ff)
    def _():
        bg = bg_ref[0]
        bu = bu_ref[0]

        def up(i):
            rows = pl.ds(pl.multiple_of(i * EXPERT_SUB, EXPERT_SUB), EXPERT_SUB)
            keep = (i * EXPERT_SUB + lax.broadcasted_iota(I32, (EXPERT_SUB, 1), 0)) < valid
            lo, hi = _unpack_rows(x_ref[rows, :])
            xs = jnp.concatenate([jnp.where(keep, lo, 0.0), jnp.where(keep, hi, 0.0)], axis=1)
            gate = jnp.minimum(_dot(xs, wg_ref[0]) + bg, SWIGLU_LIMIT)
            lin = jnp.clip(_dot(xs, wu_ref[0]) + bu, -SWIGLU_LIMIT, SWIGLU_LIMIT)
            act = (lin + 1.0) * (gate * (1.0 / (1.0 + jnp.exp(-SWIGLU_ALPHA * gate))))
            h_s[s, rows, :] = act.astype(BF16)
        _for_overlapped(n_sub, up)

    @pl.when(s >= n_ff)
    def _():
        bd = bd_ref[0]

        def down(i):
            rows = pl.ds(pl.multiple_of(i * EXPERT_SUB, EXPERT_SUB), EXPERT_SUB)
            h = jnp.concatenate([h_s[f, rows, :] for f in range(n_ff)], axis=1)
            y_ref[rows, :] = _pack_rows(_dot(h.astype(F32), wd_ref[0]) + bd)
        _for_overlapped(n_sub, down)


def _experts(xb, block_expert, block_valid, w_gu, b_gu, w_d, b_d):
    n_blocks = xb.shape[0] // EXPERT_ROWS
    n_ff = D_FF // FF_TILE
    n_out = D_MODEL // OUT_TILE

    def ahead(b, s, nv):
        nxt = jnp.minimum(b + 1, n_blocks - 1)
        return jnp.where(jnp.logical_and(s >= n_ff, nv[nxt] > 0), nxt, b)

    def up_window(b, s, be, nv, half):
        a = ahead(b, s, nv)
        return be[a], 0, jnp.where(a == b, jnp.minimum(s, n_ff - 1), 0) + half * n_ff

    def down_window(b, s, be):
        live = s >= n_ff - 1
        return (jnp.where(live, be[b], be[jnp.maximum(b - 1, 0)]), 0,
                jnp.where(live, jnp.maximum(s - n_ff, 0), n_out - 1))

    b_gu3 = b_gu.reshape(N_EXPERTS, 1, 2 * D_FF)
    b_d3 = b_d.reshape(N_EXPERTS, 1, D_MODEL)
    grid_spec = pltpu.PrefetchScalarGridSpec(
        num_scalar_prefetch=2, grid=(jnp.sum((block_valid > 0).astype(I32)), n_ff + n_out),
        in_specs=[
            pl.BlockSpec((EXPERT_ROWS, D_MODEL // 2), lambda b, s, be, nv: (ahead(b, s, nv), 0)),
            pl.BlockSpec((1, D_MODEL, FF_TILE), lambda b, s, be, nv: up_window(b, s, be, nv, 0)),
            pl.BlockSpec((1, D_MODEL, FF_TILE), lambda b, s, be, nv: up_window(b, s, be, nv, 1)),
            pl.BlockSpec((1, 1, FF_TILE), lambda b, s, be, nv: up_window(b, s, be, nv, 0)),
            pl.BlockSpec((1, 1, FF_TILE), lambda b, s, be, nv: up_window(b, s, be, nv, 1)),
            pl.BlockSpec((1, D_FF, OUT_TILE), lambda b, s, be, nv: down_window(b, s, be)),
            pl.BlockSpec((1, 1, OUT_TILE), lambda b, s, be, nv: down_window(b, s, be)),
        ],
        out_specs=pl.BlockSpec((EXPERT_ROWS, OUT_TILE // 2), lambda b, s, be, nv: (b, jnp.maximum(s - n_ff, 0))),
        scratch_shapes=[pltpu.VMEM((n_ff, EXPERT_ROWS, FF_TILE), BF16)])
    return pl.pallas_call(
        _experts_kernel, grid_spec=grid_spec, name="experts",
        out_shape=jax.ShapeDtypeStruct((n_blocks * EXPERT_ROWS, D_MODEL // 2), U32),
        compiler_params=_params(2),
    )(block_expert, block_valid, xb, w_gu, w_gu, b_gu3, b_gu3, w_d, b_d3)


def _combine_kernel(n_tiles, dest_ref, next_ref, x_ref, gate_ref, yb_ref, o_ref, buf, sem):
    tile = x_ref.shape[0]
    i = pl.program_id(0)
    slot = i % 2

    groups = tile // SUBLANES

    def copy(idx_ref, s, j, u, k):
        return pltpu.make_async_copy(yb_ref.at[pl.ds(idx_ref[0, j * (SUBLANES * TOP_K) + u * TOP_K + k], 1), :],
                                     buf.at[s, k, j, pl.ds(u, 1), :], sem.at[s])

    def start_tile(idx_ref, s):
        def start(j, carry):
            for u in range(SUBLANES):
                for k in range(TOP_K):
                    copy(idx_ref, s, j, u, k).start()
            return carry
        lax.fori_loop(0, groups, start, 0)

    pl.when(i == 0)(lambda: start_tile(dest_ref, 0))
    pl.when(i + 1 < n_tiles)(lambda: start_tile(next_ref, 1 - slot))

    def wait(j, carry):
        for u in range(SUBLANES):
            for k in range(TOP_K):
                copy(dest_ref, slot, j, u, k).wait()
        return carry
    lax.fori_loop(0, groups, wait, 0)

    gate = gate_ref[...]
    half = OUT_TILE // 2
    for g in range(D_MODEL // OUT_TILE):
        c0 = g * OUT_TILE
        acc_lo = x_ref[:, c0:c0 + half]
        acc_hi = x_ref[:, c0 + half:c0 + OUT_TILE]
        for k in range(TOP_K):
            lo, hi = _unpack_rows(buf[slot, k, :, :, g * half:(g + 1) * half].reshape(tile, half))
            acc_lo = acc_lo + gate[:, k:k + 1] * lo
            acc_hi = acc_hi + gate[:, k:k + 1] * hi
        o_ref[:, c0:c0 + half] = acc_lo
        o_ref[:, c0 + half:c0 + OUT_TILE] = acc_hi


def _combine(x, gate, dest, yb, tile):
    t = x.shape[0]
    n = t // tile
    dest = dest.reshape(n, 1, tile * TOP_K)
    return pl.pallas_call(
        functools.partial(_combine_kernel, n), grid=(n,), name="combine",
        in_specs=[pl.BlockSpec((None, 1, tile * TOP_K), lambda i: (i, 0, 0), memory_space=pltpu.SMEM),
                  pl.BlockSpec((None, 1, tile * TOP_K), lambda i: (jnp.minimum(i + 1, n - 1), 0, 0),
                               memory_space=pltpu.SMEM),
                  pl.BlockSpec((tile, D_MODEL), lambda i: (i, 0)),
                  pl.BlockSpec((tile, LANES), lambda i: (i, 0)),
                  pl.BlockSpec(memory_space=pl.ANY)],
        out_specs=pl.BlockSpec((tile, D_MODEL), lambda i: (i, 0)),
        out_shape=jax.ShapeDtypeStruct((t, D_MODEL), F32),
        scratch_shapes=[pltpu.VMEM((2, TOP_K, tile // SUBLANES, SUBLANES, D_MODEL // 2), U32),
                        pltpu.SemaphoreType.DMA((2,))],
        compiler_params=_params(),
    )(dest, dest, x, gate, yb)


def _rope_tables(pos):
    half = B_HEAD_DIM // 2
    inv_freq = ROPE_THETA ** (-jnp.arange(half, dtype=F32) / half)
    ang = pos.astype(F32)[:, None] * inv_freq[None, :]
    cos = jnp.cos(ang)
    sin = jnp.sin(ang)
    return jnp.tile(cos, (1, LANES // half)), jnp.tile(jnp.concatenate([-sin, sin], axis=1), (1, LANES // B_HEAD_DIM))


def _layer(l, xp, xs, cache_swa_k, cache_swa_v, cache_mem_k, cache_mem_v, mem_prompt,
           g_mix, w_in, g_sgu, w_s, b_s, g_q, g_k, sinks, g_out_a, g_out_b, w_out,
           g_mem_in, g_mem_src, w_mq, w_mk, w_mv, g_mq, g_mk, w_mo,
           g_moe, w_router, b_router, w_gate_up, b_gate_up, w_down, b_down):
    n_b, seq, _ = xp.shape
    n_db, dec_seq, _ = xs.shape
    tp = n_b * seq
    ts = n_db * dec_seq
    tile = min(TOKEN_TILE, seq)
    row = lambda a: a[l].reshape(1, -1)

    xp2 = xp.reshape(tp, D_MODEL)
    xs2 = xs.reshape(ts, D_MODEL)
    head = jnp.arange(QK_WIDTH, dtype=I32) // B_HEAD_DIM
    seg = (head[:, None] == jnp.arange(LANES, dtype=I32)[None, :]).astype(BF16)
    g_qk = jnp.concatenate([jnp.tile(g_q[l], B_HEADS), jnp.tile(g_k[l], B_KV_HEADS)]).reshape(1, -1)
    w_in_b = w_in[l].astype(BF16)
    cos_p, sin_p = _rope_tables(jnp.arange(seq, dtype=I32))
    cos_s, sin_s = _rope_tables(PAST_LEN + jnp.arange(SAMPLE_TILE, dtype=I32) % dec_seq)
    up, vap, qp, kp, vp = _in_proj(xp2, row(g_mix), w_in_b, row(g_sgu), g_qk, seg, seg.T, cos_p, sin_p,
                                   tile, "in_proj_prompt")
    us, vas, qs, ks, vs = _in_proj(xs2, row(g_mix), w_in_b, row(g_sgu), g_qk, seg, seg.T, cos_s, sin_s,
                                   SAMPLE_TILE, "in_proj_sample")

    w_out_b = w_out[l].astype(BF16)
    tri = jnp.tril(jnp.ones((MLP_CHUNK, MLP_CHUNK), bool))
    ws_p = jnp.where(tri[None], w_s[l], 0.0).astype(BF16)
    bias_p = jnp.repeat(b_s[l].T, A_HEAD_DIM, axis=1)
    x1p = _mixer_prompt(xp2, up, vap, qp, kp, vp, sinks[l], ws_p, bias_p, row(g_out_a), row(g_out_b), w_out_b, seq, tile)

    reps = SAMPLE_TILE // dec_seq
    tri_s = jnp.tril(jnp.ones((dec_seq, dec_seq), bool))
    ws_small = jnp.where(tri_s[None], w_s[l][:, :dec_seq, :dec_seq], 0.0)
    ws_s = jnp.einsum("ab,hts->hatbs", jnp.eye(reps, dtype=F32), ws_small).reshape(A_HEADS, SAMPLE_TILE, SAMPLE_TILE).astype(BF16)
    bias_s = jnp.tile(jnp.repeat(b_s[l][:, :dec_seq].T, A_HEAD_DIM, axis=1), (reps, 1))
    ck = cache_swa_k[l]
    cv = cache_swa_v[l]
    x1s = _mixer_sample(xs2, us, vas, qs, ks, vs, ck, cv, sinks[l], ws_s, bias_s,
                        row(g_out_a), row(g_out_b), w_out_b, dec_seq, SAMPLE_TILE)

    mk_p, mv_p = _mem_kv(mem_prompt.reshape(-1, D_MODEL), row(g_mem_src), w_mk[l].astype(BF16), w_mv[l].astype(BF16), row(g_mk))
    n_mem = mem_prompt.shape[1]
    mk_p4 = mk_p.reshape(n_b, n_mem, MEM_HEADS, MEM_HEAD_DIM)
    mv_p4 = mv_p.reshape(n_b, n_mem, MEM_HEADS, MEM_HEAD_DIM)
    w_r = jnp.pad(w_router[l], ((0, 0), (0, LANES - N_EXPERTS)))
    b_r = jnp.pad(b_router[l], (0, LANES - N_EXPERTS)).reshape(1, -1)
    w_r_hi = w_r.astype(BF16)
    w_r_lo = (w_r - w_r_hi.astype(F32)).astype(BF16)
    mem_args = (row(g_mem_in), w_mq[l].astype(BF16), row(g_mq), w_mo[l].astype(BF16), row(g_moe), w_r_hi, w_r_lo, b_r)
    x2p, hnp, gatep, routep, cntp = _memory_router(
        x1p, mk_p.reshape(n_b, n_mem, MEM_WIDTH), mv_p.reshape(n_b, n_mem, MEM_WIDTH),
        seq, *mem_args, tile, "memory_router_prompt")
    x2s, hns, gates, routes, cnts = _memory_router(
        x1s, cache_mem_k[l], cache_mem_v[l],
        dec_seq, *mem_args, SAMPLE_TILE, "memory_router_sample")

    cnt_p = cntp[0, :N_EXPERTS]
    cnt_s = cnts[0, :N_EXPERTS]
    total = cnt_p + cnt_s
    nblk = (total + EXPERT_ROWS - 1) // EXPERT_ROWS
    blk_end = jnp.cumsum(nblk)
    row_start = (blk_end - nblk) * EXPERT_ROWS
    n_blocks = (tp + ts) * TOP_K // EXPERT_ROWS + N_EXPERTS
    bidx = jnp.arange(n_blocks, dtype=I32)
    used = bidx < blk_end[-1]
    last = jnp.maximum(blk_end[-1] - 1, 0)
    bsafe = jnp.minimum(bidx, last)
    block_expert = jnp.minimum(jnp.searchsorted(blk_end, bsafe, side="right"), N_EXPERTS - 1).astype(I32)
    within = bsafe - (blk_end - nblk)[block_expert]
    block_valid = jnp.where(used, jnp.clip(total[block_expert] - within * EXPERT_ROWS, 0, EXPERT_ROWS), 0).astype(I32)
    def destinations(route, start):
        experts = jnp.arange(N_EXPERTS, dtype=I32)[:, None, None]
        first = jnp.sum(jnp.where(route[None, :TOP_K] == experts, start[:, None, None], 0), axis=0)
        return (first + route[TOP_K:]).T

    dest_p = destinations(routep, row_start)
    dest_s = destinations(routes, row_start + cnt_p)

    xb = _dispatch(hnp, dest_p, None, n_blocks * EXPERT_ROWS, min(ROW_TILE, tp))
    xb = _dispatch(hns, dest_s, xb, n_blocks * EXPERT_ROWS, min(ROW_TILE, ts))
    yb = _experts(xb, block_expert, block_valid, w_gate_up[l], b_gate_up[l], w_down[l], b_down[l])
    yp = _combine(x2p, gatep, dest_p, yb, min(ROW_TILE, tp))
    ys = _combine(x2s, gates, dest_s, yb, min(ROW_TILE, ts))

    new = dict(
        swa_k_p=kp.reshape(n_b, seq, KV_WIDTH)[:, seq - WINDOW:].reshape(n_b, WINDOW, B_KV_HEADS, B_HEAD_DIM),
        swa_v_p=vp.reshape(n_b, seq, KV_WIDTH)[:, seq - WINDOW:].reshape(n_b, WINDOW, B_KV_HEADS, B_HEAD_DIM),
        mem_k_p=mk_p4,
        mem_v_p=mv_p4,
        swa_k_s=ks.reshape(n_db, dec_seq, B_KV_HEADS, B_HEAD_DIM),
        swa_v_s=vs.reshape(n_db, dec_seq, B_KV_HEADS, B_HEAD_DIM),
        sgu_v_s=vas.reshape(n_db, dec_seq, A_HEADS, A_HEAD_DIM))
    return yp.reshape(n_b, seq, D_MODEL), ys.reshape(n_db, dec_seq, D_MODEL), new


def kernel(x_prompt, x_sample, cache_swa_k, cache_swa_v, cache_mem_k, cache_mem_v, mem_prompt, g_mix, w_in, g_sgu, w_s, b_s, g_q, g_k, sinks, g_out_a, g_out_b, w_out, g_mem_in, g_mem_src, w_mq, w_mk, w_mv, g_mq, g_mk, w_mo, g_moe, w_router, b_router, w_gate_up, b_gate_up, w_down, b_down):
    xp, xs = x_prompt, x_sample
    news = []
    for l in range(g_mix.shape[0]):
        xp, xs, new = _layer(l, xp, xs, cache_swa_k, cache_swa_v, cache_mem_k, cache_mem_v, mem_prompt,
                             g_mix, w_in, g_sgu, w_s, b_s, g_q, g_k, sinks, g_out_a, g_out_b, w_out,
                             g_mem_in, g_mem_src, w_mq, w_mk, w_mv, g_mq, g_mk, w_mo,
                             g_moe, w_router, b_router, w_gate_up, b_gate_up, w_down, b_down)
        news.append(new)
    stack = lambda name: jnp.stack([n[name] for n in news], 0)
    return (xp, xs, stack("swa_k_p"), stack("swa_v_p"), stack("mem_k_p"), stack("mem_v_p"),
            stack("swa_k_s"), stack("swa_v_s"), stack("sgu_v_s"))
```

```python
import functools

import jax
import jax.numpy as jnp
from jax import lax
from jax.experimental import pallas as pl
from jax.experimental.pallas import tpu as pltpu

F32 = jnp.float32
BF16 = jnp.bfloat16
I32 = jnp.int32
U32 = jnp.uint32

D_MODEL = 2048
PAST_LEN = 2048
CHUNK = 64
EPS = 1e-6
A_HEADS = 16
A_HEAD_DIM = 64
A_WIDTH = A_HEADS * A_HEAD_DIM
MLP_CHUNK = 128
B_HEADS = 16
B_KV_HEADS = 2
B_HEAD_DIM = 64
B_WIDTH = B_HEADS * B_HEAD_DIM
KV_WIDTH = B_KV_HEADS * B_HEAD_DIM
WINDOW = 128
ROPE_THETA = 10000.0
ATTN_SCALE = B_HEAD_DIM ** -0.5
Q_OFF = 2 * A_WIDTH
K_OFF = Q_OFF + B_WIDTH
V_OFF = K_OFF + KV_WIDTH
IN_WIDTH = V_OFF + KV_WIDTH
QK_WIDTH = B_WIDTH + KV_WIDTH
MEM_HEADS = 4
MEM_HEAD_DIM = 128
MEM_WIDTH = MEM_HEADS * MEM_HEAD_DIM
MEM_SCALE = MEM_HEAD_DIM ** -0.5
N_EXPERTS = 32
TOP_K = 4
D_FF = D_MODEL
SWIGLU_LIMIT = 7.0
SWIGLU_ALPHA = 1.702
NEG_BIG = -1e30

LANES = 128
SUBLANES = 8
VMEM_LIMIT = 56 * 1024 * 1024

TOKEN_TILE = 512
SAMPLE_TILE = 128
EXPERT_ROWS = 1280
EXPERT_SUB = 320
FF_TILE = 512
OUT_TILE = 1024
ROW_TILE = 512


def _dot(a, b):
    return jnp.dot(a, b, preferred_element_type=F32)


def _dot_nt(a, b):
    return lax.dot_general(a, b, (((1,), (1,)), ((), ())), preferred_element_type=F32)


def _rms(x):
    return x * lax.rsqrt(jnp.mean(x * x, axis=-1, keepdims=True) + EPS)


def _gelu(x):
    return 0.5 * x * (1.0 + lax.erf(x * 0.7071067811865476))


def _pack_rows(x):
    c = x.shape[1] // 2
    bits = pltpu.bitcast(x.astype(BF16).astype(F32), U32)
    return (bits[:, :c] >> 16) | (bits[:, c:] & jnp.uint32(0xFFFF0000))


def _unpack_rows(w):
    return pltpu.bitcast(w << 16, F32), pltpu.bitcast(w & jnp.uint32(0xFFFF0000), F32)


def _params(n_axes=1):
    return pltpu.CompilerParams(dimension_semantics=("arbitrary",) * n_axes,
                                vmem_limit_bytes=VMEM_LIMIT)


def _resident(shape):
    nd = len(shape)
    return pl.BlockSpec(shape, lambda *_: (0,) * nd, pipeline_mode=pl.Buffered(1))


def _mem_kv_kernel(mem_ref, gsrc_ref, wk_ref, wv_ref, gk_ref, k_ref, v_ref):
    m = (_rms(mem_ref[...]) * gsrc_ref[...]).astype(BF16)
    kz = _dot(m, wk_ref[...])
    gk = gk_ref[...]
    for h in range(MEM_HEADS):
        sl = slice(h * MEM_HEAD_DIM, (h + 1) * MEM_HEAD_DIM)
        k_ref[:, sl] = _rms(kz[:, sl]) * gk
    v_ref[...] = _dot(m, wv_ref[...])


def _mem_kv(mem, g_src, w_mk, w_mv, g_mk):
    n = mem.shape[0]
    out = jax.ShapeDtypeStruct((n, MEM_WIDTH), F32)
    return pl.pallas_call(
        _mem_kv_kernel, out_shape=(out, out), name="mem_kv",
        compiler_params=pltpu.CompilerParams(vmem_limit_bytes=VMEM_LIMIT),
    )(mem, g_src, w_mk, w_mv, g_mk)


def _in_proj_kernel(x_ref, gmix_ref, w_ref, gsgu_ref, gqk_ref, seg_ref, segt_ref, cos_ref, sin_ref,
                    u_ref, va_ref, q_ref, k_ref, v_ref):
    xn = (_rms(x_ref[...]) * gmix_ref[...]).astype(BF16)
    u_ref[...] = _gelu(_dot(xn, w_ref[:, 0:A_WIDTH])).astype(BF16)
    va_ref[...] = _rms(_gelu(_dot(xn, w_ref[:, A_WIDTH:Q_OFF]))) * gsgu_ref[...]
    v_ref[...] = _dot(xn, w_ref[:, V_OFF:IN_WIDTH])

    qk = _dot(xn, w_ref[:, Q_OFF:V_OFF])
    ssq = _dot((qk * qk).astype(BF16), seg_ref[...])
    inv = lax.rsqrt(ssq * (1.0 / B_HEAD_DIM) + EPS)
    inv_hi = inv.astype(BF16)
    inv_lo = (inv - inv_hi.astype(F32)).astype(BF16)
    inv_b = _dot(inv_hi, segt_ref[...]) + _dot(inv_lo, segt_ref[...])
    qkn = (qk * inv_b) * gqk_ref[...]

    cos = cos_ref[...]
    sin = sin_ref[...]
    lane = lax.broadcasted_iota(I32, cos.shape, 1)
    first_half = (lane % B_HEAD_DIM) < (B_HEAD_DIM // 2)
    for g in range(QK_WIDTH // LANES):
        xg = qkn[:, g * LANES:(g + 1) * LANES]
        rot = jnp.where(first_half, pltpu.roll(xg, LANES - B_HEAD_DIM // 2, 1),
                        pltpu.roll(xg, B_HEAD_DIM // 2, 1))
        r = xg * cos + rot * sin
        if g < B_WIDTH // LANES:
            q_ref[:, g * LANES:(g + 1) * LANES] = (r * ATTN_SCALE).astype(BF16)
        else:
            k_ref[...] = r


def _in_proj(x, g_mix, w_in, g_sgu, g_qk, seg, segt, cos_tab, sin_tab, tile, name):
    t = x.shape[0]
    n = t // tile
    tab_tiles = cos_tab.shape[0] // tile
    row = lambda w: pl.BlockSpec((tile, w), lambda i: (i, 0))
    tab = pl.BlockSpec((tile, LANES), lambda i: (i % tab_tiles, 0))
    return pl.pallas_call(
        _in_proj_kernel, grid=(n,), name=name,
        in_specs=[row(D_MODEL), _resident(g_mix.shape), _resident(w_in.shape), _resident(g_sgu.shape),
                  _resident(g_qk.shape), _resident(seg.shape), _resident(segt.shape), tab, tab],
        out_specs=[row(A_WIDTH), row(A_WIDTH), row(B_WIDTH), row(KV_WIDTH), row(KV_WIDTH)],
        out_shape=(jax.ShapeDtypeStruct((t, A_WIDTH), BF16), jax.ShapeDtypeStruct((t, A_WIDTH), F32),
                   jax.ShapeDtypeStruct((t, B_WIDTH), BF16), jax.ShapeDtypeStruct((t, KV_WIDTH), F32),
                   jax.ShapeDtypeStruct((t, KV_WIDTH), F32)),
        compiler_params=_params(),
    )(x, g_mix, w_in, g_sgu, g_qk, seg, segt, cos_tab, sin_tab)


def _split_heads(ref_dst_lo, ref_dst_hi, rows, x, swap):
    lane = lax.broadcasted_iota(I32, x.shape, 1)
    low = lane < B_HEAD_DIM
    zero = jnp.zeros_like(x)
    ref_dst_lo[0, rows, :] = jnp.where(low, x, zero).astype(BF16)
    ref_dst_hi[0, rows, :] = jnp.where(low, zero, swap).astype(BF16)
    ref_dst_lo[1, rows, :] = jnp.where(low, swap, zero).astype(BF16)
    ref_dst_hi[1, rows, :] = jnp.where(low, zero, x).astype(BF16)


def _sgu_chunk(u, va, ws_ref, bias):
    lane = lax.broadcasted_iota(I32, (MLP_CHUNK, LANES), 1)
    low = lane < A_HEAD_DIM
    outs = []
    for p in range(A_WIDTH // LANES):
        sl = slice(p * LANES, (p + 1) * LANES)
        v2 = va[:, sl]
        zero = jnp.zeros_like(v2)
        mixed = (_dot(ws_ref[2 * p], jnp.where(low, v2, zero).astype(BF16))
                 + _dot(ws_ref[2 * p + 1], jnp.where(low, zero, v2).astype(BF16)))
        outs.append(u[:, sl].astype(F32) * (mixed + bias[:, sl]))
    return jnp.concatenate(outs, axis=1)


def _attend(q2, kl, kh, vl, vh, mask, sink_even, sink_odd):
    outs = []
    for kk, vv, sink in ((kl, vl, sink_even), (kh, vh, sink_odd)):
        s = _dot_nt(q2, kk)
        if mask is not None:
            s = jnp.where(mask, s, NEG_BIG)
        m = jnp.maximum(jnp.max(s, axis=-1, keepdims=True), sink)
        p = jnp.exp(s - m)
        den = jnp.sum(p, axis=-1, keepdims=True) + jnp.exp(sink - m)
        outs.append(_dot(p.astype(BF16), vv) * (1.0 / den))
    return outs[0] + outs[1]


def _out_proj(oa, ob, ga_ref, gb_ref, wout_ref, x):
    a = (_rms(oa) * ga_ref[...]).astype(BF16)
    b = (_rms(ob) * gb_ref[...]).astype(BF16)
    return x + _dot(a, wout_ref[0:A_WIDTH, :]) + _dot(b, wout_ref[A_WIDTH:, :])


def _mixer_prompt_kernel(tiles_per_seq, sinks_ref, x_ref, u_ref, va_ref, q_ref, kc_ref, vc_ref, kp_ref, vp_ref,
                         ws_ref, bias_ref, ga_ref, gb_ref, wout_ref, o_ref,
                         kl_s, kh_s, vl_s, vh_s, oa_s, ob_s):
    tile = x_ref.shape[0]
    n_sub = tile // MLP_CHUNK
    seq_start = (pl.program_id(0) % tiles_per_seq) == 0

    for src_p, src_c, dl, dh in ((kp_ref, kc_ref, kl_s, kh_s), (vp_ref, vc_ref, vl_s, vh_s)):
        prev = src_p[...]
        cur = src_c[...]
        _split_heads(dl, dh, slice(0, WINDOW), prev, pltpu.roll(prev, B_HEAD_DIM, 1))
        _split_heads(dl, dh, slice(WINDOW, WINDOW + tile), cur, pltpu.roll(cur, B_HEAD_DIM, 1))

    qc = lax.broadcasted_iota(I32, (MLP_CHUNK, 2 * MLP_CHUNK), 0) // CHUNK
    kc = lax.broadcasted_iota(I32, (MLP_CHUNK, 2 * MLP_CHUNK), 1) // CHUNK
    window_mask = (kc >= qc) & (kc <= qc + 2)
    bias = bias_ref[...]

    def sub(j, carry):
        r0 = pl.multiple_of(j * MLP_CHUNK, MLP_CHUNK)
        rows = pl.ds(r0, MLP_CHUNK)
        keys = pl.ds(r0, 2 * MLP_CHUNK)
        first_key_chunk = jnp.where(jnp.logical_and(seq_start, j == 0), 2, 0)
        mask = window_mask & (kc >= first_key_chunk)
        oa_s[rows, :] = _sgu_chunk(u_ref[rows, :], va_ref[rows, :], ws_ref, bias)
        for p in range(B_WIDTH // LANES):
            h = p // (B_WIDTH // LANES // B_KV_HEADS)
            sl = slice(p * LANES, (p + 1) * LANES)
            ob_s[rows, sl] = _attend(q_ref[rows, sl], kl_s[h, keys, :], kh_s[h, keys, :],
                                     vl_s[h, keys, :], vh_s[h, keys, :], mask,
                                     sinks_ref[2 * p], sinks_ref[2 * p + 1])
        return carry

    lax.fori_loop(0, n_sub, sub, 0)
    o_ref[...] = _out_proj(oa_s[...], ob_s[...], ga_ref, gb_ref, wout_ref, x_ref[...])


def _mixer_prompt(x, u, va, q, k, v, sinks, ws, bias, g_a, g_b, w_out, seq, tile):
    t = x.shape[0]
    n = t // tile
    tiles_per_seq = seq // tile
    per = tile // WINDOW
    row = lambda w: pl.BlockSpec((tile, w), lambda i, s: (i, 0))
    prev = pl.BlockSpec((WINDOW, KV_WIDTH), lambda i, s: (jnp.maximum(i * per - 1, 0), 0))
    res = lambda a: pl.BlockSpec(a.shape, lambda i, s: (0,) * a.ndim, pipeline_mode=pl.Buffered(1))
    grid_spec = pltpu.PrefetchScalarGridSpec(
        num_scalar_prefetch=1, grid=(n,),
        in_specs=[row(D_MODEL), row(A_WIDTH), row(A_WIDTH), row(B_WIDTH), row(KV_WIDTH), row(KV_WIDTH), prev, prev,
                  res(ws), res(bias), res(g_a), res(g_b), res(w_out)],
        out_specs=row(D_MODEL),
        scratch_shapes=[pltpu.VMEM((B_KV_HEADS, WINDOW + tile, LANES), BF16) for _ in range(4)]
        + [pltpu.VMEM((tile, A_WIDTH), F32), pltpu.VMEM((tile, B_WIDTH), F32)])
    return pl.pallas_call(
        functools.partial(_mixer_prompt_kernel, tiles_per_seq), grid_spec=grid_spec, name="mixer_prompt",
        out_shape=jax.ShapeDtypeStruct((t, D_MODEL), F32), compiler_params=_params(),
    )(sinks, x, u, va, q, k, v, k, v, ws, bias, g_a, g_b, w_out)


def _mixer_sample_kernel(dec_seq, sinks_ref, x_ref, u_ref, va_ref, q_ref, kc_ref, vc_ref, ck_ref, cv_ref,
                         ws_ref, bias_ref, ga_ref, gb_ref, wout_ref, o_ref,
                         kl_s, kh_s, vl_s, vh_s, ob_s):
    tile = x_ref.shape[0]
    n_seq = tile // dec_seq
    n_cache = ck_ref.shape[1]
    n_keys = n_cache + dec_seq
    oa = _sgu_chunk(u_ref[...], va_ref[...], ws_ref, bias_ref[...])
    for b in range(n_seq):
        rows = slice(b * dec_seq, (b + 1) * dec_seq)
        for src_c, src_n, dl, dh in ((ck_ref, kc_ref, kl_s, kh_s), (cv_ref, vc_ref, vl_s, vh_s)):
            heads = [src_c[b, :, h, :] for h in range(B_KV_HEADS)]
            new = src_n[rows, :]
            _split_heads(dl, dh, slice(b * n_keys, b * n_keys + n_cache), jnp.concatenate(heads, axis=1),
                         jnp.concatenate(heads[::-1], axis=1))
            _split_heads(dl, dh, slice(b * n_keys + n_cache, (b + 1) * n_keys), new, pltpu.roll(new, B_HEAD_DIM, 1))
    q_seq = lax.broadcasted_iota(I32, (tile, n_seq * n_keys), 0) // dec_seq
    k_seq = lax.broadcasted_iota(I32, (tile, n_seq * n_keys), 1) // n_keys
    mask = q_seq == k_seq
    for p in range(B_WIDTH // LANES):
        h = p // (B_WIDTH // LANES // B_KV_HEADS)
        sl = slice(p * LANES, (p + 1) * LANES)
        ob_s[:, sl] = _attend(q_ref[:, sl], kl_s[h], kh_s[h], vl_s[h], vh_s[h], mask,
                              sinks_ref[2 * p], sinks_ref[2 * p + 1])
    o_ref[...] = _out_proj(oa, ob_s[...], ga_ref, gb_ref, wout_ref, x_ref[...])


def _mixer_sample(x, u, va, q, k, v, cache_k, cache_v, sinks, ws, bias, g_a, g_b, w_out, dec_seq, tile):
    t = x.shape[0]
    n = t // tile
    n_seq = tile // dec_seq
    n_cache = cache_k.shape[1]
    row = lambda w: pl.BlockSpec((tile, w), lambda i, s: (i, 0))
    cache = pl.BlockSpec((n_seq, n_cache, B_KV_HEADS, B_HEAD_DIM), lambda i, s: (i, 0, 0, 0))
    res = lambda a: pl.BlockSpec(a.shape, lambda i, s: (0,) * a.ndim, pipeline_mode=pl.Buffered(1))
    grid_spec = pltpu.PrefetchScalarGridSpec(
        num_scalar_prefetch=1, grid=(n,),
        in_specs=[row(D_MODEL), row(A_WIDTH), row(A_WIDTH), row(B_WIDTH), row(KV_WIDTH), row(KV_WIDTH), cache, cache,
                  res(ws), res(bias), res(g_a), res(g_b), res(w_out)],
        out_specs=row(D_MODEL),
        scratch_shapes=[pltpu.VMEM((B_KV_HEADS, n_seq * (n_cache + dec_seq), LANES), BF16) for _ in range(4)]
        + [pltpu.VMEM((tile, B_WIDTH), F32)])
    return pl.pallas_call(
        functools.partial(_mixer_sample_kernel, dec_seq), grid_spec=grid_spec, name="mixer_sample",
        out_shape=jax.ShapeDtypeStruct((t, D_MODEL), F32), compiler_params=_params(),
    )(sinks, x, u, va, q, k, v, cache_k, cache_v, ws, bias, g_a, g_b, w_out)


def _memory_router_kernel(rows_per_mem, x_ref, mk_ref, mv_ref, gin_ref, wq_ref, gq_ref, wo_ref,
                          gmoe_ref, wrh_ref, wrl_ref, br_ref,
                          x2_ref, hn_ref, gate_ref, route_ref, cnt_ref, o_s, base_s):
    tile = x_ref.shape[0]
    x = x_ref[...]
    qz = _dot((_rms(x) * gin_ref[...]).astype(BF16), wq_ref[...])
    gq = gq_ref[...]

    def head(ref, r, h):
        if len(ref.shape) == 4:
            return ref[r, :, h, :]
        return ref[r, :, h * MEM_HEAD_DIM:(h + 1) * MEM_HEAD_DIM]
    for h in range(MEM_HEADS):
        sl = slice(h * MEM_HEAD_DIM, (h + 1) * MEM_HEAD_DIM)
        qh = (_rms(qz[:, sl]) * gq).astype(BF16)
        for r in range(tile // rows_per_mem):
            rows = slice(r * rows_per_mem, (r + 1) * rows_per_mem)
            s = _dot_nt(qh[rows], head(mk_ref, r, h).astype(BF16)) * MEM_SCALE
            p = jnp.exp(s - jnp.max(s, axis=-1, keepdims=True))
            den = jnp.sum(p, axis=-1, keepdims=True)
            o_s[rows, sl] = _dot(p.astype(BF16), head(mv_ref, r, h).astype(BF16)) * (1.0 / den)
    x2 = x + _dot(o_s[...].astype(BF16), wo_ref[...])
    x2_ref[...] = x2
    hn = _rms(x2) * gmoe_ref[...]
    hn_ref[...] = _pack_rows(hn)

    hn_hi = hn.astype(BF16)
    hn_lo = (hn - hn_hi.astype(F32)).astype(BF16)
    logits = ((_dot(hn_lo, wrh_ref[...]) + _dot(hn_hi, wrl_ref[...])) + _dot(hn_hi, wrh_ref[...])) + br_ref[...]
    lane = lax.broadcasted_iota(I32, (tile, LANES), 1)
    lane_f = lane.astype(F32)
    work = jnp.where(lane < N_EXPERTS, logits, -jnp.inf)
    idx_out = jnp.zeros((tile, LANES), F32)
    val_out = jnp.zeros((tile, LANES), F32)
    hot = jnp.zeros((tile, LANES), F32)
    top = None
    picks = []
    for k in range(TOP_K):
        m = jnp.max(work, axis=-1, keepdims=True)
        pick = jnp.min(jnp.where(work == m, lane_f, float(LANES)), axis=-1, keepdims=True)
        chosen = lane_f == pick
        if top is None:
            top = m
        idx_out = jnp.where(lane == k, pick, idx_out)
        val_out = jnp.where(lane == k, jnp.exp(m - top), val_out)
        hot = jnp.where(chosen, 1.0, hot)
        work = jnp.where(chosen, -jnp.inf, work)
        picks.append(chosen)
    gate_ref[...] = val_out * (1.0 / jnp.sum(val_out, axis=-1, keepdims=True))

    @pl.when(pl.program_id(0) == 0)
    def _():
        base_s[...] = jnp.zeros_like(base_s)

    earlier = (lax.broadcasted_iota(I32, (tile, tile), 1) < lax.broadcasted_iota(I32, (tile, tile), 0))
    pos = _dot(jnp.where(earlier, 1.0, 0.0).astype(BF16), hot.astype(BF16)) + base_s[0:1, :]
    route = idx_out
    for k in range(TOP_K):
        route = jnp.where(lane == TOP_K + k, jnp.sum(jnp.where(picks[k], pos, 0.0), axis=-1, keepdims=True), route)
    route_ref[...] = route.T[0:2 * TOP_K, :].astype(I32)
    total = base_s[0:1, :] + jnp.sum(hot, axis=0, keepdims=True)
    base_s[...] = jnp.broadcast_to(total, base_s.shape)
    cnt_ref[...] = jnp.broadcast_to(total, cnt_ref.shape).astype(I32)


def _memory_router(x, mk, mv, rows_per_mem, g_in, w_mq, g_mq, w_mo, g_moe, w_r_hi, w_r_lo, b_r, tile, name):
    t = x.shape[0]
    n = t // tile
    mems = tile // rows_per_mem if rows_per_mem <= tile else 1
    per_mem_tiles = max(rows_per_mem // tile, 1)
    rpm = min(rows_per_mem, tile)
    row = lambda w: pl.BlockSpec((tile, w), lambda i: (i, 0))
    mem = pl.BlockSpec((mems,) + mk.shape[1:], lambda i: (i // per_mem_tiles,) + (0,) * (mk.ndim - 1))
    res = lambda a: pl.BlockSpec(a.shape, lambda i: (0,) * a.ndim, pipeline_mode=pl.Buffered(1))
    return pl.pallas_call(
        functools.partial(_memory_router_kernel, rpm), grid=(n,), name=name,
        in_specs=[row(D_MODEL), mem, mem, res(g_in), res(w_mq), res(g_mq), res(w_mo), res(g_moe), res(w_r_hi), res(w_r_lo), res(b_r)],
        out_specs=[row(D_MODEL), row(D_MODEL // 2), row(LANES),
                   pl.BlockSpec((2 * TOP_K, tile), lambda i: (0, i)), pl.BlockSpec((8, LANES), lambda i: (0, 0))],
        out_shape=(jax.ShapeDtypeStruct((t, D_MODEL), F32), jax.ShapeDtypeStruct((t, D_MODEL // 2), U32),
                   jax.ShapeDtypeStruct((t, LANES), F32), jax.ShapeDtypeStruct((2 * TOP_K, t), I32),
                   jax.ShapeDtypeStruct((8, LANES), I32)),
        scratch_shapes=[pltpu.VMEM((tile, MEM_WIDTH), F32), pltpu.VMEM((8, LANES), F32)],
        compiler_params=_params(),
    )(x, mk, mv, g_in, w_mq, g_mq, w_mo, g_moe, w_r_hi, w_r_lo, b_r)


def _dispatch_kernel(dest_ref, hn_ref, *rest):
    xb_ref, sem = rest[-2:]
    groups = hn_ref.shape[0]

    def copy(j, u, k):
        return pltpu.make_async_copy(hn_ref.at[j, pl.ds(u, 1), :],
                                     xb_ref.at[pl.ds(dest_ref[0, j * (SUBLANES * TOP_K) + u * TOP_K + k], 1), :], sem)

    def start(j, carry):
        for u in range(SUBLANES):
            for k in range(TOP_K):
                copy(j, u, k).start()
        return carry

    def wait(j, carry):
        for u in range(SUBLANES):
            for k in range(TOP_K):
                copy(j, u, k).wait()
        return carry

    lax.fori_loop(0, groups, start, 0)
    lax.fori_loop(0, groups, wait, 0)


def _dispatch(hn, dest, xb, n_rows, tile):
    t = hn.shape[0]
    n = t // tile
    dest = dest.reshape(n, 1, tile * TOP_K)
    in_specs = [pl.BlockSpec((None, 1, tile * TOP_K), lambda i: (i, 0, 0), memory_space=pltpu.SMEM),
                pl.BlockSpec((tile // SUBLANES, SUBLANES, hn.shape[1]), lambda i: (i, 0, 0))]
    args = [dest, hn.reshape(t // SUBLANES, SUBLANES, hn.shape[1])]
    if xb is not None:
        in_specs.append(pl.BlockSpec(memory_space=pl.ANY))
        args.append(xb)
    return pl.pallas_call(
        _dispatch_kernel, grid=(n,), name="dispatch",
        in_specs=in_specs,
        out_specs=pl.BlockSpec(memory_space=pl.ANY),
        out_shape=jax.ShapeDtypeStruct((n_rows, hn.shape[1]), hn.dtype),
        scratch_shapes=[pltpu.SemaphoreType.DMA(())],
        input_output_aliases={} if xb is None else {2: 0},
        compiler_params=_params(),
    )(*args)


def _for_overlapped(n, body):
    def trip(j, carry):
        for g in range(4):
            body(4 * j + g)
        return carry
    lax.fori_loop(0, n // 4, trip, 0)
    base = (n // 4) * 4

    @pl.when(n % 4 >= 2)
    def _():
        body(base)
        body(base + 1)

    @pl.when(n % 2 == 1)
    def _():
        body(n - 1)


def _experts_kernel(be_ref, nv_ref, x_ref, wg_ref, wu_ref, bg_ref, bu_ref, wd_ref, bd_ref, y_ref, h_s):
    del be_ref
    b = pl.program_id(0)
    s = pl.program_id(1)
    n_ff = D_FF // FF_TILE
    valid = nv_ref[b]
    n_sub = (valid + EXPERT_SUB - 1) // EXPERT_SUB

    @pl.when(s < n_ff)
    def _():
        bg = bg_ref[0]
        bu = bu_ref[0]

        def up(i):
            rows = pl.ds(pl.multiple_of(i * EXPERT_SUB, EXPERT_SUB), EXPERT_SUB)
            keep = (i * EXPERT_SUB + lax.broadcasted_iota(I32, (EXPERT_SUB, 1), 0)) < valid
            lo, hi = _unpack_rows(x_ref[rows, :])
            xs = jnp.concatenate([jnp.where(keep, lo, 0.0), jnp.where(keep, hi, 0.0)], axis=1)
            gate = jnp.minimum(_dot(xs, wg_ref[0]) + bg, SWIGLU_LIMIT)
            lin = jnp.clip(_dot(xs, wu_ref[0]) + bu, -SWIGLU_LIMIT, SWIGLU_LIMIT)
            act = (lin + 1.0) * (gate * (1.0 / (1.0 + jnp.exp(-SWIGLU_ALPHA * gate))))
            h_s[s, rows, :] = act.astype(BF16)
        _for_overlapped(n_sub, up)

    @pl.when(s >= n_ff)
    def _():
        bd = bd_ref[0]

        def down(i):
            rows = pl.ds(pl.multiple_of(i * EXPERT_SUB, EXPERT_SUB), EXPERT_SUB)
            h = jnp.concatenate([h_s[f, rows, :] for f in range(n_ff)], axis=1)
            y_ref[rows, :] = _pack_rows(_dot(h.astype(F32), wd_ref[0]) + bd)
        _for_overlapped(n_sub, down)


def _experts(xb, block_expert, block_valid, w_gu, b_gu, w_d, b_d):
    n_blocks = xb.shape[0] // EXPERT_ROWS
    n_ff = D_FF // FF_TILE
    n_out = D_MODEL // OUT_TILE

    def ahead(b, s, nv):
        nxt = jnp.minimum(b + 1, n_blocks - 1)
        return jnp.where(jnp.logical_and(s >= n_ff, nv[nxt] > 0), nxt, b)

    def up_window(b, s, be, nv, half):
        a = ahead(b, s, nv)
        return be[a], 0, jnp.where(a == b, jnp.minimum(s, n_ff - 1), 0) + half * n_ff

    def down_window(b, s, be):
        live = s >= n_ff - 1
        return (jnp.where(live, be[b], be[jnp.maximum(b - 1, 0)]), 0,
                jnp.where(live, jnp.maximum(s - n_ff, 0), n_out - 1))

    b_gu3 = b_gu.reshape(N_EXPERTS, 1, 2 * D_FF)
    b_d3 = b_d.reshape(N_EXPERTS, 1, D_MODEL)
    grid_spec = pltpu.PrefetchScalarGridSpec(
        num_scalar_prefetch=2, grid=(jnp.sum((block_valid > 0).astype(I32)), n_ff + n_out),
        in_specs=[
            pl.BlockSpec((EXPERT_ROWS, D_MODEL // 2), lambda b, s, be, nv: (ahead(b, s, nv), 0)),
            pl.BlockSpec((1, D_MODEL, FF_TILE), lambda b, s, be, nv: up_window(b, s, be, nv, 0)),
            pl.BlockSpec((1, D_MODEL, FF_TILE), lambda b, s, be, nv: up_window(b, s, be, nv, 1)),
            pl.BlockSpec((1, 1, FF_TILE), lambda b, s, be, nv: up_window(b, s, be, nv, 0)),
            pl.BlockSpec((1, 1, FF_TILE), lambda b, s, be, nv: up_window(b, s, be, nv, 1)),
            pl.BlockSpec((1, D_FF, OUT_TILE), lambda b, s, be, nv: down_window(b, s, be)),
            pl.BlockSpec((1, 1, OUT_TILE), lambda b, s, be, nv: down_window(b, s, be)),
        ],
        out_specs=pl.BlockSpec((EXPERT_ROWS, OUT_TILE // 2), lambda b, s, be, nv: (b, jnp.maximum(s - n_ff, 0))),
        scratch_shapes=[pltpu.VMEM((n_ff, EXPERT_ROWS, FF_TILE), BF16)])
    return pl.pallas_call(
        _experts_kernel, grid_spec=grid_spec, name="experts",
        out_shape=jax.ShapeDtypeStruct((n_blocks * EXPERT_ROWS, D_MODEL // 2), U32),
        compiler_params=_params(2),
    )(block_expert, block_valid, xb, w_gu, w_gu, b_gu3, b_gu3, w_d, b_d3)


def _combine_kernel(n_tiles, dest_ref, next_ref, x_ref, gate_ref, yb_ref, o_ref, buf, sem):
    tile = x_ref.shape[0]
    i = pl.program_id(0)
    slot = i % 2

    groups = tile // SUBLANES

    def copy(idx_ref, s, j, u, k):
        return pltpu.make_async_copy(yb_ref.at[pl.ds(idx_ref[0, j * (SUBLANES * TOP_K) + u * TOP_K + k], 1), :],
                                     buf.at[s, k, j, pl.ds(u, 1), :], sem.at[s])

    def start_tile(idx_ref, s):
        def start(j, carry):
            for u in range(SUBLANES):
                for k in range(TOP_K):
                    copy(idx_ref, s, j, u, k).start()
            return carry
        lax.fori_loop(0, groups, start, 0)

    pl.when(i == 0)(lambda: start_tile(dest_ref, 0))
    pl.when(i + 1 < n_tiles)(lambda: start_tile(next_ref, 1 - slot))

    def wait(j, carry):
        for u in range(SUBLANES):
            for k in range(TOP_K):
                copy(dest_ref, slot, j, u, k).wait()
        return carry
    lax.fori_loop(0, groups, wait, 0)

    gate = gate_ref[...]
    half = OUT_TILE // 2
    for g in range(D_MODEL // OUT_TILE):
        c0 = g * OUT_TILE
        acc_lo = x_ref[:, c0:c0 + half]
        acc_hi = x_ref[:, c0 + half:c0 + OUT_TILE]
        for k in range(TOP_K):
            lo, hi = _unpack_rows(buf[slot, k, :, :, g * half:(g + 1) * half].reshape(tile, half))
            acc_lo = acc_lo + gate[:, k:k + 1] * lo
            acc_hi = acc_hi + gate[:, k:k + 1] * hi
        o_ref[:, c0:c0 + half] = acc_lo
        o_ref[:, c0 + half:c0 + OUT_TILE] = acc_hi


def _combine(x, gate, dest, yb, tile):
    t = x.shape[0]
    n = t // tile
    dest = dest.reshape(n, 1, tile * TOP_K)
    return pl.pallas_call(
        functools.partial(_combine_kernel, n), grid=(n,), name="combine",
        in_specs=[pl.BlockSpec((None, 1, tile * TOP_K), lambda i: (i, 0, 0), memory_space=pltpu.SMEM),
                  pl.BlockSpec((None, 1, tile * TOP_K), lambda i: (jnp.minimum(i + 1, n - 1), 0, 0),
                               memory_space=pltpu.SMEM),
                  pl.BlockSpec((tile, D_MODEL), lambda i: (i, 0)),
                  pl.BlockSpec((tile, LANES), lambda i: (i, 0)),
                  pl.BlockSpec(memory_space=pl.ANY)],
        out_specs=pl.BlockSpec((tile, D_MODEL), lambda i: (i, 0)),
        out_shape=jax.ShapeDtypeStruct((t, D_MODEL), F32),
        scratch_shapes=[pltpu.VMEM((2, TOP_K, tile // SUBLANES, SUBLANES, D_MODEL // 2), U32),
                        pltpu.SemaphoreType.DMA((2,))],
        compiler_params=_params(),
    )(dest, dest, x, gate, yb)


def _rope_tables(pos):
    half = B_HEAD_DIM // 2
    inv_freq = ROPE_THETA ** (-jnp.arange(half, dtype=F32) / half)
    ang = pos.astype(F32)[:, None] * inv_freq[None, :]
    cos = jnp.cos(ang)
    sin = jnp.sin(ang)
    return jnp.tile(cos, (1, LANES // half)), jnp.tile(jnp.concatenate([-sin, sin], axis=1), (1, LANES // B_HEAD_DIM))


def _layer(l, xp, xs, cache_swa_k, cache_swa_v, cache_mem_k, cache_mem_v, mem_prompt,
           g_mix, w_in, g_sgu, w_s, b_s, g_q, g_k, sinks, g_out_a, g_out_b, w_out,
           g_mem_in, g_mem_src, w_mq, w_mk, w_mv, g_mq, g_mk, w_mo,
           g_moe, w_router, b_router, w_gate_up, b_gate_up, w_down, b_down):
    n_b, seq, _ = xp.shape
    n_db, dec_seq, _ = xs.shape
    tp = n_b * seq
    ts = n_db * dec_seq
    tile = min(TOKEN_TILE, seq)
    row = lambda a: a[l].reshape(1, -1)

    xp2 = xp.reshape(tp, D_MODEL)
    xs2 = xs.reshape(ts, D_MODEL)
    head = jnp.arange(QK_WIDTH, dtype=I32) // B_HEAD_DIM
    seg = (head[:, None] == jnp.arange(LANES, dtype=I32)[None, :]).astype(BF16)
    g_qk = jnp.concatenate([jnp.tile(g_q[l], B_HEADS), jnp.tile(g_k[l], B_KV_HEADS)]).reshape(1, -1)
    w_in_b = w_in[l].astype(BF16)
    cos_p, sin_p = _rope_tables(jnp.arange(seq, dtype=I32))
    tile_s = min(TOKEN_TILE, ts)
    cos_s, sin_s = _rope_tables(PAST_LEN + jnp.arange(tile_s, dtype=I32) % dec_seq)
    up, vap, qp, kp, vp = _in_proj(xp2, row(g_mix), w_in_b, row(g_sgu), g_qk, seg, seg.T, cos_p, sin_p,
                                   tile, "in_proj_prompt")
    us, vas, qs, ks, vs = _in_proj(xs2, row(g_mix), w_in_b, row(g_sgu), g_qk, seg, seg.T, cos_s, sin_s,
                                   tile_s, "in_proj_sample")

    w_out_b = w_out[l].astype(BF16)
    tri = jnp.tril(jnp.ones((MLP_CHUNK, MLP_CHUNK), bool))
    ws_p = jnp.where(tri[None], w_s[l], 0.0).astype(BF16)
    bias_p = jnp.repeat(b_s[l].T, A_HEAD_DIM, axis=1)
    x1p = _mixer_prompt(xp2, up, vap, qp, kp, vp, sinks[l], ws_p, bias_p, row(g_out_a), row(g_out_b), w_out_b, seq, tile)

    reps = SAMPLE_TILE // dec_seq
    tri_s = jnp.tril(jnp.ones((dec_seq, dec_seq), bool))
    ws_small = jnp.where(tri_s[None], w_s[l][:, :dec_seq, :dec_seq], 0.0)
    ws_s = jnp.einsum("ab,hts->hatbs", jnp.eye(reps, dtype=F32), ws_small).reshape(A_HEADS, SAMPLE_TILE, SAMPLE_TILE).astype(BF16)
    bias_s = jnp.tile(jnp.repeat(b_s[l][:, :dec_seq].T, A_HEAD_DIM, axis=1), (reps, 1))
    ck = cache_swa_k[l]
    cv = cache_swa_v[l]
    x1s = _mixer_sample(xs2, us, vas, qs, ks, vs, ck, cv, sinks[l], ws_s, bias_s,
                        row(g_out_a), row(g_out_b), w_out_b, dec_seq, SAMPLE_TILE)

    mk_p, mv_p = _mem_kv(mem_prompt.reshape(-1, D_MODEL), row(g_mem_src), w_mk[l].astype(BF16), w_mv[l].astype(BF16), row(g_mk))
    n_mem = mem_prompt.shape[1]
    mk_p4 = mk_p.reshape(n_b, n_mem, MEM_HEADS, MEM_HEAD_DIM)
    mv_p4 = mv_p.reshape(n_b, n_mem, MEM_HEADS, MEM_HEAD_DIM)
    w_r = jnp.pad(w_router[l], ((0, 0), (0, LANES - N_EXPERTS)))
    b_r = jnp.pad(b_router[l], (0, LANES - N_EXPERTS)).reshape(1, -1)
    w_r_hi = w_r.astype(BF16)
    w_r_lo = (w_r - w_r_hi.astype(F32)).astype(BF16)
    mem_args = (row(g_mem_in), w_mq[l].astype(BF16), row(g_mq), w_mo[l].astype(BF16), row(g_moe), w_r_hi, w_r_lo, b_r)
    x2p, hnp, gatep, routep, cntp = _memory_router(
        x1p, mk_p.reshape(n_b, n_mem, MEM_WIDTH), mv_p.reshape(n_b, n_mem, MEM_WIDTH),
        seq, *mem_args, tile, "memory_router_prompt")
    x2s, hns, gates, routes, cnts = _memory_router(
        x1s, cache_mem_k[l], cache_mem_v[l],
        dec_seq, *mem_args, SAMPLE_TILE, "memory_router_sample")

    cnt_p = cntp[0, :N_EXPERTS]
    cnt_s = cnts[0, :N_EXPERTS]
    total = cnt_p + cnt_s
    nblk = (total + EXPERT_ROWS - 1) // EXPERT_ROWS
    blk_end = jnp.cumsum(nblk)
    row_start = (blk_end - nblk) * EXPERT_ROWS
    n_blocks = (tp + ts) * TOP_K // EXPERT_ROWS + N_EXPERTS
    bidx = jnp.arange(n_blocks, dtype=I32)
    used = bidx < blk_end[-1]
    last = jnp.maximum(blk_end[-1] - 1, 0)
    bsafe = jnp.minimum(bidx, last)
    block_expert = jnp.minimum(jnp.searchsorted(blk_end, bsafe, side="right"), N_EXPERTS - 1).astype(I32)
    within = bsafe - (blk_end - nblk)[block_expert]
    block_valid = jnp.where(used, jnp.clip(total[block_expert] - within * EXPERT_ROWS, 0, EXPERT_ROWS), 0).astype(I32)
    def destinations(route, start):
        experts = jnp.arange(N_EXPERTS, dtype=I32)[:, None, None]
        first = jnp.sum(jnp.where(route[None, :TOP_K] == experts, start[:, None, None], 0), axis=0)
        return (first + route[TOP_K:]).T

    dest_p = destinations(routep, row_start)
    dest_s = destinations(routes, row_start + cnt_p)

    xb = _dispatch(hnp, dest_p, None, n_blocks * EXPERT_ROWS, min(ROW_TILE, tp))
    xb = _dispatch(hns, dest_s, xb, n_blocks * EXPERT_ROWS, min(ROW_TILE, ts))
    yb = _experts(xb, block_expert, block_valid, w_gate_up[l], b_gate_up[l], w_down[l], b_down[l])
    yp = _combine(x2p, gatep, dest_p, yb, min(ROW_TILE, tp))
    ys = _combine(x2s, gates, dest_s, yb, min(ROW_TILE, ts))

    new = dict(
        swa_k_p=kp.reshape(n_b, seq, KV_WIDTH)[:, seq - WINDOW:].reshape(n_b, WINDOW, B_KV_HEADS, B_HEAD_DIM),
        swa_v_p=vp.reshape(n_b, seq, KV_WIDTH)[:, seq - WINDOW:].reshape(n_b, WINDOW, B_KV_HEADS, B_HEAD_DIM),
        mem_k_p=mk_p4,
        mem_v_p=mv_p4,
        swa_k_s=ks.reshape(n_db, dec_seq, B_KV_HEADS, B_HEAD_DIM),
        swa_v_s=vs.reshape(n_db, dec_seq, B_KV_HEADS, B_HEAD_DIM),
        sgu_v_s=vas.reshape(n_db, dec_seq, A_HEADS, A_HEAD_DIM))
    return yp.reshape(n_b, seq, D_MODEL), ys.reshape(n_db, dec_seq, D_MODEL), new


def kernel(x_prompt, x_sample, cache_swa_k, cache_swa_v, cache_mem_k, cache_mem_v, mem_prompt, g_mix, w_in, g_sgu, w_s, b_s, g_q, g_k, sinks, g_out_a, g_out_b, w_out, g_mem_in, g_mem_src, w_mq, w_mk, w_mv, g_mq, g_mk, w_mo, g_moe, w_router, b_router, w_gate_up, b_gate_up, w_down, b_down):
    xp, xs = x_prompt, x_sample
    news = []
    for l in range(g_mix.shape[0]):
        xp, xs, new = _layer(l, xp, xs, cache_swa_k, cache_swa_v, cache_mem_k, cache_mem_v, mem_prompt,
                             g_mix, w_in, g_sgu, w_s, b_s, g_q, g_k, sinks, g_out_a, g_out_b, w_out,
                             g_mem_in, g_mem_src, w_mq, w_mk, w_mv, g_mq, g_mk, w_mo,
                             g_moe, w_router, b_router, w_gate_up, b_gate_up, w_down, b_down)
        news.append(new)
    stack = lambda name: jnp.stack([n[name] for n in news], 0)
    return (xp, xs, stack("swa_k_p"), stack("swa_v_p"), stack("mem_k_p"), stack("mem_v_p"),
            stack("swa_k_s"), stack("swa_v_s"), stack("sgu_v_s"))
```

```python
import functools

import jax
import jax.numpy as jnp
from jax import lax
from jax.experimental import pallas as pl
from jax.experimental.pallas import tpu as pltpu

F32 = jnp.float32
BF16 = jnp.bfloat16
I32 = jnp.int32
U32 = jnp.uint32

D_MODEL = 2048
PAST_LEN = 2048
CHUNK = 64
EPS = 1e-6
A_HEADS = 16
A_HEAD_DIM = 64
A_WIDTH = A_HEADS * A_HEAD_DIM
MLP_CHUNK = 128
B_HEADS = 16
B_KV_HEADS = 2
B_HEAD_DIM = 64
B_WIDTH = B_HEADS * B_HEAD_DIM
KV_WIDTH = B_KV_HEADS * B_HEAD_DIM
WINDOW = 128
ROPE_THETA = 10000.0
ATTN_SCALE = B_HEAD_DIM ** -0.5
Q_OFF = 2 * A_WIDTH
K_OFF = Q_OFF + B_WIDTH
V_OFF = K_OFF + KV_WIDTH
IN_WIDTH = V_OFF + KV_WIDTH
QK_WIDTH = B_WIDTH + KV_WIDTH
MEM_HEADS = 4
MEM_HEAD_DIM = 128
MEM_WIDTH = MEM_HEADS * MEM_HEAD_DIM
MEM_SCALE = MEM_HEAD_DIM ** -0.5
N_EXPERTS = 32
TOP_K = 4
D_FF = D_MODEL
SWIGLU_LIMIT = 7.0
SWIGLU_ALPHA = 1.702
NEG_BIG = -1e30

LANES = 128
SUBLANES = 8
VMEM_LIMIT = 56 * 1024 * 1024

TOKEN_TILE = 512
SAMPLE_TILE = 128
EXPERT_ROWS = 1280
EXPERT_SUB = 320
FF_TILE = 512
OUT_TILE = 1024
DISPATCH_TILE = 512
COMBINE_TILE = 256


def _dot(a, b):
    return jnp.dot(a, b, preferred_element_type=F32)


def _dot_nt(a, b):
    return lax.dot_general(a, b, (((1,), (1,)), ((), ())), preferred_element_type=F32)


def _rms(x):
    return x * lax.rsqrt(jnp.mean(x * x, axis=-1, keepdims=True) + EPS)


def _gelu(x):
    return 0.5 * x * (1.0 + lax.erf(x * 0.7071067811865476))


def _pack_rows(x):
    c = x.shape[1] // 2
    bits = pltpu.bitcast(x.astype(BF16).astype(F32), U32)
    return (bits[:, :c] >> 16) | (bits[:, c:] & jnp.uint32(0xFFFF0000))


def _unpack_rows(w):
    return pltpu.bitcast(w << 16, F32), pltpu.bitcast(w & jnp.uint32(0xFFFF0000), F32)


def _params(n_axes=1):
    return pltpu.CompilerParams(dimension_semantics=("arbitrary",) * n_axes,
                                vmem_limit_bytes=VMEM_LIMIT)


def _resident(shape):
    nd = len(shape)
    return pl.BlockSpec(shape, lambda *_: (0,) * nd, pipeline_mode=pl.Buffered(1))


def _mem_kv_kernel(mem_ref, gsrc_ref, wk_ref, wv_ref, gk_ref, k_ref, v_ref):
    m = (_rms(mem_ref[...]) * gsrc_ref[...]).astype(BF16)
    kz = _dot(m, wk_ref[...])
    gk = gk_ref[...]
    for h in range(MEM_HEADS):
        sl = slice(h * MEM_HEAD_DIM, (h + 1) * MEM_HEAD_DIM)
        k_ref[:, sl] = _rms(kz[:, sl]) * gk
    v_ref[...] = _dot(m, wv_ref[...])


def _mem_kv(mem, g_src, w_mk, w_mv, g_mk):
    n = mem.shape[0]
    out = jax.ShapeDtypeStruct((n, MEM_WIDTH), F32)
    return pl.pallas_call(
        _mem_kv_kernel, out_shape=(out, out), name="mem_kv",
        compiler_params=pltpu.CompilerParams(vmem_limit_bytes=VMEM_LIMIT),
    )(mem, g_src, w_mk, w_mv, g_mk)


def _in_proj_kernel(x_ref, gmix_ref, w_ref, gsgu_ref, gqk_ref, seg_ref, segt_ref, cos_ref, sin_ref,
                    u_ref, va_ref, q_ref, k_ref, v_ref):
    xn = (_rms(x_ref[...]) * gmix_ref[...]).astype(BF16)
    u_ref[...] = _gelu(_dot(xn, w_ref[:, 0:A_WIDTH])).astype(BF16)
    va_ref[...] = _rms(_gelu(_dot(xn, w_ref[:, A_WIDTH:Q_OFF]))) * gsgu_ref[...]
    v_ref[...] = _dot(xn, w_ref[:, V_OFF:IN_WIDTH])

    qk = _dot(xn, w_ref[:, Q_OFF:V_OFF])
    ssq = _dot((qk * qk).astype(BF16), seg_ref[...])
    inv = lax.rsqrt(ssq * (1.0 / B_HEAD_DIM) + EPS)
    inv_hi = inv.astype(BF16)
    inv_lo = (inv - inv_hi.astype(F32)).astype(BF16)
    inv_b = _dot(inv_hi, segt_ref[...]) + _dot(inv_lo, segt_ref[...])
    qkn = (qk * inv_b) * gqk_ref[...]

    cos = cos_ref[...]
    sin = sin_ref[...]
    lane = lax.broadcasted_iota(I32, cos.shape, 1)
    first_half = (lane % B_HEAD_DIM) < (B_HEAD_DIM // 2)
    for g in range(QK_WIDTH // LANES):
        xg = qkn[:, g * LANES:(g + 1) * LANES]
        rot = jnp.where(first_half, pltpu.roll(xg, LANES - B_HEAD_DIM // 2, 1),
                        pltpu.roll(xg, B_HEAD_DIM // 2, 1))
        r = xg * cos + rot * sin
        if g < B_WIDTH // LANES:
            q_ref[:, g * LANES:(g + 1) * LANES] = (r * ATTN_SCALE).astype(BF16)
        else:
            k_ref[...] = r


def _in_proj(x, g_mix, w_in, g_sgu, g_qk, seg, segt, cos_tab, sin_tab, tile, name):
    t = x.shape[0]
    n = t // tile
    tab_tiles = cos_tab.shape[0] // tile
    row = lambda w: pl.BlockSpec((tile, w), lambda i: (i, 0))
    tab = pl.BlockSpec((tile, LANES), lambda i: (i % tab_tiles, 0))
    return pl.pallas_call(
        _in_proj_kernel, grid=(n,), name=name,
        in_specs=[row(D_MODEL), _resident(g_mix.shape), _resident(w_in.shape), _resident(g_sgu.shape),
                  _resident(g_qk.shape), _resident(seg.shape), _resident(segt.shape), tab, tab],
        out_specs=[row(A_WIDTH), row(A_WIDTH), row(B_WIDTH), row(KV_WIDTH), row(KV_WIDTH)],
        out_shape=(jax.ShapeDtypeStruct((t, A_WIDTH), BF16), jax.ShapeDtypeStruct((t, A_WIDTH), F32),
                   jax.ShapeDtypeStruct((t, B_WIDTH), BF16), jax.ShapeDtypeStruct((t, KV_WIDTH), F32),
                   jax.ShapeDtypeStruct((t, KV_WIDTH), F32)),
        compiler_params=_params(),
    )(x, g_mix, w_in, g_sgu, g_qk, seg, segt, cos_tab, sin_tab)


def _split_heads(ref_dst_lo, ref_dst_hi, rows, x, swap):
    lane = lax.broadcasted_iota(I32, x.shape, 1)
    low = lane < B_HEAD_DIM
    zero = jnp.zeros_like(x)
    ref_dst_lo[0, rows, :] = jnp.where(low, x, zero).astype(BF16)
    ref_dst_hi[0, rows, :] = jnp.where(low, zero, swap).astype(BF16)
    ref_dst_lo[1, rows, :] = jnp.where(low, swap, zero).astype(BF16)
    ref_dst_hi[1, rows, :] = jnp.where(low, zero, x).astype(BF16)


def _sgu_chunk(u, va, ws_ref, bias):
    lane = lax.broadcasted_iota(I32, (MLP_CHUNK, LANES), 1)
    low = lane < A_HEAD_DIM
    outs = []
    for p in range(A_WIDTH // LANES):
        sl = slice(p * LANES, (p + 1) * LANES)
        v2 = va[:, sl]
        zero = jnp.zeros_like(v2)
        mixed = (_dot(ws_ref[2 * p], jnp.where(low, v2, zero).astype(BF16))
                 + _dot(ws_ref[2 * p + 1], jnp.where(low, zero, v2).astype(BF16)))
        outs.append(u[:, sl].astype(F32) * (mixed + bias[:, sl]))
    return jnp.concatenate(outs, axis=1)


def _attend(q2, kl, kh, vl, vh, mask, sink_even, sink_odd):
    outs = []
    for kk, vv, sink in ((kl, vl, sink_even), (kh, vh, sink_odd)):
        s = _dot_nt(q2, kk)
        if mask is not None:
            s = jnp.where(mask, s, NEG_BIG)
        m = jnp.maximum(jnp.max(s, axis=-1, keepdims=True), sink)
        p = jnp.exp(s - m)
        den = jnp.sum(p, axis=-1, keepdims=True) + jnp.exp(sink - m)
        outs.append(_dot(p.astype(BF16), vv) * (1.0 / den))
    return outs[0] + outs[1]


def _out_proj(oa, ob, ga_ref, gb_ref, wout_ref, x):
    a = (_rms(oa) * ga_ref[...]).astype(BF16)
    b = (_rms(ob) * gb_ref[...]).astype(BF16)
    return x + _dot(a, wout_ref[0:A_WIDTH, :]) + _dot(b, wout_ref[A_WIDTH:, :])


def _mixer_prompt_kernel(tiles_per_seq, sinks_ref, x_ref, u_ref, va_ref, q_ref, kc_ref, vc_ref, kp_ref, vp_ref,
                         ws_ref, bias_ref, ga_ref, gb_ref, wout_ref, o_ref,
                         kl_s, kh_s, vl_s, vh_s, oa_s, ob_s):
    tile = x_ref.shape[0]
    n_sub = tile // MLP_CHUNK
    seq_start = (pl.program_id(0) % tiles_per_seq) == 0

    for src_p, src_c, dl, dh in ((kp_ref, kc_ref, kl_s, kh_s), (vp_ref, vc_ref, vl_s, vh_s)):
        prev = src_p[...]
        cur = src_c[...]
        _split_heads(dl, dh, slice(0, WINDOW), prev, pltpu.roll(prev, B_HEAD_DIM, 1))
        _split_heads(dl, dh, slice(WINDOW, WINDOW + tile), cur, pltpu.roll(cur, B_HEAD_DIM, 1))

    qc = lax.broadcasted_iota(I32, (MLP_CHUNK, 2 * MLP_CHUNK), 0) // CHUNK
    kc = lax.broadcasted_iota(I32, (MLP_CHUNK, 2 * MLP_CHUNK), 1) // CHUNK
    window_mask = (kc >= qc) & (kc <= qc + 2)
    bias = bias_ref[...]

    def sub(j, carry):
        r0 = pl.multiple_of(j * MLP_CHUNK, MLP_CHUNK)
        rows = pl.ds(r0, MLP_CHUNK)
        keys = pl.ds(r0, 2 * MLP_CHUNK)
        first_key_chunk = jnp.where(jnp.logical_and(seq_start, j == 0), 2, 0)
        mask = window_mask & (kc >= first_key_chunk)
        oa_s[rows, :] = _sgu_chunk(u_ref[rows, :], va_ref[rows, :], ws_ref, bias)
        for p in range(B_WIDTH // LANES):
            h = p // (B_WIDTH // LANES // B_KV_HEADS)
            sl = slice(p * LANES, (p + 1) * LANES)
            ob_s[rows, sl] = _attend(q_ref[rows, sl], kl_s[h, keys, :], kh_s[h, keys, :],
                                     vl_s[h, keys, :], vh_s[h, keys, :], mask,
                                     sinks_ref[2 * p], sinks_ref[2 * p + 1])
        return carry

    lax.fori_loop(0, n_sub, sub, 0)
    o_ref[...] = _out_proj(oa_s[...], ob_s[...], ga_ref, gb_ref, wout_ref, x_ref[...])


def _mixer_prompt(x, u, va, q, k, v, sinks, ws, bias, g_a, g_b, w_out, seq, tile):
    t = x.shape[0]
    n = t // tile
    tiles_per_seq = seq // tile
    per = tile // WINDOW
    row = lambda w: pl.BlockSpec((tile, w), lambda i, s: (i, 0))
    prev = pl.BlockSpec((WINDOW, KV_WIDTH), lambda i, s: (jnp.maximum(i * per - 1, 0), 0))
    res = lambda a: pl.BlockSpec(a.shape, lambda i, s: (0,) * a.ndim, pipeline_mode=pl.Buffered(1))
    grid_spec = pltpu.PrefetchScalarGridSpec(
        num_scalar_prefetch=1, grid=(n,),
        in_specs=[row(D_MODEL), row(A_WIDTH), row(A_WIDTH), row(B_WIDTH), row(KV_WIDTH), row(KV_WIDTH), prev, prev,
                  res(ws), res(bias), res(g_a), res(g_b), res(w_out)],
        out_specs=row(D_MODEL),
        scratch_shapes=[pltpu.VMEM((B_KV_HEADS, WINDOW + tile, LANES), BF16) for _ in range(4)]
        + [pltpu.VMEM((tile, A_WIDTH), F32), pltpu.VMEM((tile, B_WIDTH), F32)])
    return pl.pallas_call(
        functools.partial(_mixer_prompt_kernel, tiles_per_seq), grid_spec=grid_spec, name="mixer_prompt",
        out_shape=jax.ShapeDtypeStruct((t, D_MODEL), F32), compiler_params=_params(),
    )(sinks, x, u, va, q, k, v, k, v, ws, bias, g_a, g_b, w_out)


def _mixer_sample_kernel(dec_seq, sinks_ref, x_ref, u_ref, va_ref, q_ref, kc_ref, vc_ref, ck_ref, cv_ref,
                         ws_ref, bias_ref, ga_ref, gb_ref, wout_ref, o_ref,
                         kl_s, kh_s, vl_s, vh_s, ob_s):
    tile = x_ref.shape[0]
    n_seq = tile // dec_seq
    n_cache = ck_ref.shape[1]
    n_keys = n_cache + dec_seq
    oa = _sgu_chunk(u_ref[...], va_ref[...], ws_ref, bias_ref[...])
    for b in range(n_seq):
        rows = slice(b * dec_seq, (b + 1) * dec_seq)
        for src_c, src_n, dl, dh in ((ck_ref, kc_ref, kl_s, kh_s), (cv_ref, vc_ref, vl_s, vh_s)):
            heads = [src_c[b, :, h, :] for h in range(B_KV_HEADS)]
            new = src_n[rows, :]
            _split_heads(dl, dh, slice(b * n_keys, b * n_keys + n_cache), jnp.concatenate(heads, axis=1),
                         jnp.concatenate(heads[::-1], axis=1))
            _split_heads(dl, dh, slice(b * n_keys + n_cache, (b + 1) * n_keys), new, pltpu.roll(new, B_HEAD_DIM, 1))
    q_seq = lax.broadcasted_iota(I32, (tile, n_seq * n_keys), 0) // dec_seq
    k_seq = lax.broadcasted_iota(I32, (tile, n_seq * n_keys), 1) // n_keys
    mask = q_seq == k_seq
    for p in range(B_WIDTH // LANES):
        h = p // (B_WIDTH // LANES // B_KV_HEADS)
        sl = slice(p * LANES, (p + 1) * LANES)
        ob_s[:, sl] = _attend(q_ref[:, sl], kl_s[h], kh_s[h], vl_s[h], vh_s[h], mask,
                              sinks_ref[2 * p], sinks_ref[2 * p + 1])
    o_ref[...] = _out_proj(oa, ob_s[...], ga_ref, gb_ref, wout_ref, x_ref[...])


def _mixer_sample(x, u, va, q, k, v, cache_k, cache_v, sinks, ws, bias, g_a, g_b, w_out, dec_seq, tile):
    t = x.shape[0]
    n = t // tile
    n_seq = tile // dec_seq
    n_cache = cache_k.shape[1]
    row = lambda w: pl.BlockSpec((tile, w), lambda i, s: (i, 0))
    cache = pl.BlockSpec((n_seq, n_cache, B_KV_HEADS, B_HEAD_DIM), lambda i, s: (i, 0, 0, 0))
    res = lambda a: pl.BlockSpec(a.shape, lambda i, s: (0,) * a.ndim, pipeline_mode=pl.Buffered(1))
    grid_spec = pltpu.PrefetchScalarGridSpec(
        num_scalar_prefetch=1, grid=(n,),
        in_specs=[row(D_MODEL), row(A_WIDTH), row(A_WIDTH), row(B_WIDTH), row(KV_WIDTH), row(KV_WIDTH), cache, cache,
                  res(ws), res(bias), res(g_a), res(g_b), res(w_out)],
        out_specs=row(D_MODEL),
        scratch_shapes=[pltpu.VMEM((B_KV_HEADS, n_seq * (n_cache + dec_seq), LANES), BF16) for _ in range(4)]
        + [pltpu.VMEM((tile, B_WIDTH), F32)])
    return pl.pallas_call(
        functools.partial(_mixer_sample_kernel, dec_seq), grid_spec=grid_spec, name="mixer_sample",
        out_shape=jax.ShapeDtypeStruct((t, D_MODEL), F32), compiler_params=_params(),
    )(sinks, x, u, va, q, k, v, cache_k, cache_v, ws, bias, g_a, g_b, w_out)


def _memory_router_kernel(rows_per_mem, x_ref, mk_ref, mv_ref, gin_ref, wq_ref, gq_ref, wo_ref,
                          gmoe_ref, wrh_ref, wrl_ref, br_ref,
                          x2_ref, hn_ref, gate_ref, route_ref, cnt_ref, o_s, base_s):
    tile = x_ref.shape[0]
    x = x_ref[...]
    qz = _dot((_rms(x) * gin_ref[...]).astype(BF16), wq_ref[...])
    gq = gq_ref[...]

    def head(ref, r, h):
        if len(ref.shape) == 4:
            return ref[r, :, h, :]
        return ref[r, :, h * MEM_HEAD_DIM:(h + 1) * MEM_HEAD_DIM]
    for h in range(MEM_HEADS):
        sl = slice(h * MEM_HEAD_DIM, (h + 1) * MEM_HEAD_DIM)
        qh = (_rms(qz[:, sl]) * gq).astype(BF16)
        for r in range(tile // rows_per_mem):
            rows = slice(r * rows_per_mem, (r + 1) * rows_per_mem)
            s = _dot_nt(qh[rows], head(mk_ref, r, h).astype(BF16)) * MEM_SCALE
            p = jnp.exp(s - jnp.max(s, axis=-1, keepdims=True))
            den = jnp.sum(p, axis=-1, keepdims=True)
            o_s[rows, sl] = _dot(p.astype(BF16), head(mv_ref, r, h).astype(BF16)) * (1.0 / den)
    x2 = x + _dot(o_s[...].astype(BF16), wo_ref[...])
    x2_ref[...] = x2
    hn = _rms(x2) * gmoe_ref[...]
    hn_ref[...] = _pack_rows(hn)

    hn_hi = hn.astype(BF16)
    hn_lo = (hn - hn_hi.astype(F32)).astype(BF16)
    logits = ((_dot(hn_lo, wrh_ref[...]) + _dot(hn_hi, wrl_ref[...])) + _dot(hn_hi, wrh_ref[...])) + br_ref[...]
    lane = lax.broadcasted_iota(I32, (tile, LANES), 1)
    lane_f = lane.astype(F32)
    work = jnp.where(lane < N_EXPERTS, logits, -jnp.inf)
    idx_out = jnp.zeros((tile, LANES), F32)
    val_out = jnp.zeros((tile, LANES), F32)
    hot = jnp.zeros((tile, LANES), F32)
    top = None
    picks = []
    for k in range(TOP_K):
        m = jnp.max(work, axis=-1, keepdims=True)
        pick = jnp.min(jnp.where(work == m, lane_f, float(LANES)), axis=-1, keepdims=True)
        chosen = lane_f == pick
        if top is None:
            top = m
        idx_out = jnp.where(lane == k, pick, idx_out)
        val_out = jnp.where(lane == k, jnp.exp(m - top), val_out)
        hot = jnp.where(chosen, 1.0, hot)
        work = jnp.where(chosen, -jnp.inf, work)
        picks.append(chosen)
    gate_ref[...] = val_out * (1.0 / jnp.sum(val_out, axis=-1, keepdims=True))

    @pl.when(pl.program_id(0) == 0)
    def _():
        base_s[...] = jnp.zeros_like(base_s)

    earlier = (lax.broadcasted_iota(I32, (tile, tile), 1) < lax.broadcasted_iota(I32, (tile, tile), 0))
    pos = _dot(jnp.where(earlier, 1.0, 0.0).astype(BF16), hot.astype(BF16)) + base_s[0:1, :]
    route = idx_out
    for k in range(TOP_K):
        route = jnp.where(lane == TOP_K + k, jnp.sum(jnp.where(picks[k], pos, 0.0), axis=-1, keepdims=True), route)
    route_ref[...] = route.T[0:2 * TOP_K, :].astype(I32)
    total = base_s[0:1, :] + jnp.sum(hot, axis=0, keepdims=True)
    base_s[...] = jnp.broadcast_to(total, base_s.shape)
    cnt_ref[...] = jnp.broadcast_to(total, cnt_ref.shape).astype(I32)


def _memory_router(x, mk, mv, rows_per_mem, g_in, w_mq, g_mq, w_mo, g_moe, w_r_hi, w_r_lo, b_r, tile, name):
    t = x.shape[0]
    n = t // tile
    mems = tile // rows_per_mem if rows_per_mem <= tile else 1
    per_mem_tiles = max(rows_per_mem // tile, 1)
    rpm = min(rows_per_mem, tile)
    row = lambda w: pl.BlockSpec((tile, w), lambda i: (i, 0))
    mem = pl.BlockSpec((mems,) + mk.shape[1:], lambda i: (i // per_mem_tiles,) + (0,) * (mk.ndim - 1))
    res = lambda a: pl.BlockSpec(a.shape, lambda i: (0,) * a.ndim, pipeline_mode=pl.Buffered(1))
    return pl.pallas_call(
        functools.partial(_memory_router_kernel, rpm), grid=(n,), name=name,
        in_specs=[row(D_MODEL), mem, mem, res(g_in), res(w_mq), res(g_mq), res(w_mo), res(g_moe), res(w_r_hi), res(w_r_lo), res(b_r)],
        out_specs=[row(D_MODEL), row(D_MODEL // 2), row(LANES),
                   pl.BlockSpec((2 * TOP_K, tile), lambda i: (0, i)), pl.BlockSpec((8, LANES), lambda i: (0, 0))],
        out_shape=(jax.ShapeDtypeStruct((t, D_MODEL), F32), jax.ShapeDtypeStruct((t, D_MODEL // 2), U32),
                   jax.ShapeDtypeStruct((t, LANES), F32), jax.ShapeDtypeStruct((2 * TOP_K, t), I32),
                   jax.ShapeDtypeStruct((8, LANES), I32)),
        scratch_shapes=[pltpu.VMEM((tile, MEM_WIDTH), F32), pltpu.VMEM((8, LANES), F32)],
        compiler_params=_params(),
    )(x, mk, mv, g_in, w_mq, g_mq, w_mo, g_moe, w_r_hi, w_r_lo, b_r)


def _dispatch_kernel(dest_ref, hn_ref, *rest):
    xb_ref, sem = rest[-2:]
    groups = hn_ref.shape[0]

    def copy(j, u, k):
        return pltpu.make_async_copy(hn_ref.at[j, pl.ds(u, 1), :],
                                     xb_ref.at[pl.ds(dest_ref[0, j * (SUBLANES * TOP_K) + u * TOP_K + k], 1), :], sem)

    def start(j, carry):
        for u in range(SUBLANES):
            for k in range(TOP_K):
                copy(j, u, k).start()
        return carry

    def wait(j, carry):
        for u in range(SUBLANES):
            for k in range(TOP_K):
                copy(j, u, k).wait()
        return carry

    lax.fori_loop(0, groups, start, 0)
    lax.fori_loop(0, groups, wait, 0)


def _dispatch(hn, dest, xb, n_rows, tile):
    t = hn.shape[0]
    n = t // tile
    dest = dest.reshape(n, 1, tile * TOP_K)
    in_specs = [pl.BlockSpec((None, 1, tile * TOP_K), lambda i: (i, 0, 0), memory_space=pltpu.SMEM),
                pl.BlockSpec((tile // SUBLANES, SUBLANES, hn.shape[1]), lambda i: (i, 0, 0))]
    args = [dest, hn.reshape(t // SUBLANES, SUBLANES, hn.shape[1])]
    if xb is not None:
        in_specs.append(pl.BlockSpec(memory_space=pl.ANY))
        args.append(xb)
    return pl.pallas_call(
        _dispatch_kernel, grid=(n,), name="dispatch",
        in_specs=in_specs,
        out_specs=pl.BlockSpec(memory_space=pl.ANY),
        out_shape=jax.ShapeDtypeStruct((n_rows, hn.shape[1]), hn.dtype),
        scratch_shapes=[pltpu.SemaphoreType.DMA(())],
        input_output_aliases={} if xb is None else {2: 0},
        compiler_params=_params(),
    )(*args)


def _for_overlapped(n, body):
    def trip(j, carry):
        for g in range(4):
            body(4 * j + g)
        return carry
    lax.fori_loop(0, n // 4, trip, 0)
    base = (n // 4) * 4

    @pl.when(n % 4 >= 2)
    def _():
        body(base)
        body(base + 1)

    @pl.when(n % 2 == 1)
    def _():
        body(n - 1)


def _experts_kernel(be_ref, nv_ref, x_ref, wg_ref, wu_ref, bg_ref, bu_ref, wd_ref, bd_ref, y_ref, h_s):
    del be_ref
    b = pl.program_id(0)
    s = pl.program_id(1)
    n_ff = D_FF // FF_TILE
    valid = nv_ref[b]
    n_sub = (valid + EXPERT_SUB - 1) // EXPERT_SUB

    @pl.when(s < n_ff)
    def _():
        bg = bg_ref[0]
        bu = bu_ref[0]

        def up(i):
            rows = pl.ds(pl.multiple_of(i * EXPERT_SUB, EXPERT_SUB), EXPERT_SUB)
            keep = (i * EXPERT_SUB + lax.broadcasted_iota(I32, (EXPERT_SUB, 1), 0)) < valid
            lo, hi = _unpack_rows(x_ref[rows, :])
            xs = jnp.concatenate([jnp.where(keep, lo, 0.0), jnp.where(keep, hi, 0.0)], axis=1)
            gate = jnp.minimum(_dot(xs, wg_ref[0]) + bg, SWIGLU_LIMIT)
            lin = jnp.clip(_dot(xs, wu_ref[0]) + bu, -SWIGLU_LIMIT, SWIGLU_LIMIT)
            act = (lin + 1.0) * (gate * (1.0 / (1.0 + jnp.exp(-SWIGLU_ALPHA * gate))))
            h_s[s, rows, :] = act.astype(BF16)
        _for_overlapped(n_sub, up)

    @pl.when(s >= n_ff)
    def _():
        bd = bd_ref[0]

        def down(i):
            rows = pl.ds(pl.multiple_of(i * EXPERT_SUB, EXPERT_SUB), EXPERT_SUB)
            h = jnp.concatenate([h_s[f, rows, :] for f in range(n_ff)], axis=1)
            y_ref[rows, :] = _pack_rows(_dot(h.astype(F32), wd_ref[0]) + bd)
        _for_overlapped(n_sub, down)


def _experts(xb, block_expert, block_valid, w_gu, b_gu, w_d, b_d):
    n_blocks = xb.shape[0] // EXPERT_ROWS
    n_ff = D_FF // FF_TILE
    n_out = D_MODEL // OUT_TILE

    def ahead(b, s, nv):
        nxt = jnp.minimum(b + 1, n_blocks - 1)
        return jnp.where(jnp.logical_and(s >= n_ff, nv[nxt] > 0), nxt, b)

    def up_window(b, s, be, nv, half):
        a = ahead(b, s, nv)
        return be[a], 0, jnp.where(a == b, jnp.minimum(s, n_ff - 1), 0) + half * n_ff

    def down_window(b, s, be):
        live = s >= n_ff - 1
        return (jnp.where(live, be[b], be[jnp.maximum(b - 1, 0)]), 0,
                jnp.where(live, jnp.maximum(s - n_ff, 0), n_out - 1))

    b_gu3 = b_gu.reshape(N_EXPERTS, 1, 2 * D_FF)
    b_d3 = b_d.reshape(N_EXPERTS, 1, D_MODEL)
    grid_spec = pltpu.PrefetchScalarGridSpec(
        num_scalar_prefetch=2, grid=(jnp.sum((block_valid > 0).astype(I32)), n_ff + n_out),
        in_specs=[
            pl.BlockSpec((EXPERT_ROWS, D_MODEL // 2), lambda b, s, be, nv: (ahead(b, s, nv), 0)),
            pl.BlockSpec((1, D_MODEL, FF_TILE), lambda b, s, be, nv: up_window(b, s, be, nv, 0)),
            pl.BlockSpec((1, D_MODEL, FF_TILE), lambda b, s, be, nv: up_window(b, s, be, nv, 1)),
            pl.BlockSpec((1, 1, FF_TILE), lambda b, s, be, nv: up_window(b, s, be, nv, 0)),
            pl.BlockSpec((1, 1, FF_TILE), lambda b, s, be, nv: up_window(b, s, be, nv, 1)),
            pl.BlockSpec((1, D_FF, OUT_TILE), lambda b, s, be, nv: down_window(b, s, be)),
            pl.BlockSpec((1, 1, OUT_TILE), lambda b, s, be, nv: down_window(b, s, be)),
        ],
        out_specs=pl.BlockSpec((EXPERT_ROWS, OUT_TILE // 2), lambda b, s, be, nv: (b, jnp.maximum(s - n_ff, 0))),
        scratch_shapes=[pltpu.VMEM((n_ff, EXPERT_ROWS, FF_TILE), BF16)])
    return pl.pallas_call(
        _experts_kernel, grid_spec=grid_spec, name="experts",
        out_shape=jax.ShapeDtypeStruct((n_blocks * EXPERT_ROWS, D_MODEL // 2), U32),
        compiler_params=_params(2),
    )(block_expert, block_valid, xb, w_gu, w_gu, b_gu3, b_gu3, w_d, b_d3)


def _combine_kernel(n_tiles, dest_ref, next_ref, x_ref, gate_ref, yb_ref, o_ref, buf, sem):
    tile = x_ref.shape[0]
    i = pl.program_id(0)
    slot = i % 2

    groups = tile // SUBLANES

    def copy(idx_ref, s, j, u, k):
        return pltpu.make_async_copy(yb_ref.at[pl.ds(idx_ref[0, j * (SUBLANES * TOP_K) + u * TOP_K + k], 1), :],
                                     buf.at[s, k, j, pl.ds(u, 1), :], sem.at[s])

    def start_tile(idx_ref, s):
        def start(j, carry):
            for u in range(SUBLANES):
                for k in range(TOP_K):
                    copy(idx_ref, s, j, u, k).start()
            return carry
        lax.fori_loop(0, groups, start, 0)

    pl.when(i == 0)(lambda: start_tile(dest_ref, 0))
    pl.when(i + 1 < n_tiles)(lambda: start_tile(next_ref, 1 - slot))

    def wait(j, carry):
        for u in range(SUBLANES):
            for k in range(TOP_K):
                copy(dest_ref, slot, j, u, k).wait()
        return carry
    lax.fori_loop(0, groups, wait, 0)

    gate = gate_ref[...]
    half = OUT_TILE // 2
    for g in range(D_MODEL // OUT_TILE):
        c0 = g * OUT_TILE
        acc_lo = x_ref[:, c0:c0 + half]
        acc_hi = x_ref[:, c0 + half:c0 + OUT_TILE]
        for k in range(TOP_K):
            lo, hi = _unpack_rows(buf[slot, k, :, :, g * half:(g + 1) * half].reshape(tile, half))
            acc_lo = acc_lo + gate[:, k:k + 1] * lo
            acc_hi = acc_hi + gate[:, k:k + 1] * hi
        o_ref[:, c0:c0 + half] = acc_lo
        o_ref[:, c0 + half:c0 + OUT_TILE] = acc_hi


def _combine(x, gate, dest, yb, tile):
    t = x.shape[0]
    n = t // tile
    dest = dest.reshape(n, 1, tile * TOP_K)
    return pl.pallas_call(
        functools.partial(_combine_kernel, n), grid=(n,), name="combine",
        in_specs=[pl.BlockSpec((None, 1, tile * TOP_K), lambda i: (i, 0, 0), memory_space=pltpu.SMEM),
                  pl.BlockSpec((None, 1, tile * TOP_K), lambda i: (jnp.minimum(i + 1, n - 1), 0, 0),
                               memory_space=pltpu.SMEM),
                  pl.BlockSpec((tile, D_MODEL), lambda i: (i, 0)),
                  pl.BlockSpec((tile, LANES), lambda i: (i, 0)),
                  pl.BlockSpec(memory_space=pl.ANY)],
        out_specs=pl.BlockSpec((tile, D_MODEL), lambda i: (i, 0)),
        out_shape=jax.ShapeDtypeStruct((t, D_MODEL), F32),
        scratch_shapes=[pltpu.VMEM((2, TOP_K, tile // SUBLANES, SUBLANES, D_MODEL // 2), U32),
                        pltpu.SemaphoreType.DMA((2,))],
        compiler_params=_params(),
    )(dest, dest, x, gate, yb)


def _rope_tables(pos):
    half = B_HEAD_DIM // 2
    inv_freq = ROPE_THETA ** (-jnp.arange(half, dtype=F32) / half)
    ang = pos.astype(F32)[:, None] * inv_freq[None, :]
    cos = jnp.cos(ang)
    sin = jnp.sin(ang)
    return jnp.tile(cos, (1, LANES // half)), jnp.tile(jnp.concatenate([-sin, sin], axis=1), (1, LANES // B_HEAD_DIM))


def _layer(l, xp, xs, cache_swa_k, cache_swa_v, cache_mem_k, cache_mem_v, mem_prompt,
           g_mix, w_in, g_sgu, w_s, b_s, g_q, g_k, sinks, g_out_a, g_out_b, w_out,
           g_mem_in, g_mem_src, w_mq, w_mk, w_mv, g_mq, g_mk, w_mo,
           g_moe, w_router, b_router, w_gate_up, b_gate_up, w_down, b_down):
    n_b, seq, _ = xp.shape
    n_db, dec_seq, _ = xs.shape
    tp = n_b * seq
    ts = n_db * dec_seq
    tile = min(TOKEN_TILE, seq)
    row = lambda a: a[l].reshape(1, -1)

    xp2 = xp.reshape(tp, D_MODEL)
    xs2 = xs.reshape(ts, D_MODEL)
    head = jnp.arange(QK_WIDTH, dtype=I32) // B_HEAD_DIM
    seg = (head[:, None] == jnp.arange(LANES, dtype=I32)[None, :]).astype(BF16)
    g_qk = jnp.concatenate([jnp.tile(g_q[l], B_HEADS), jnp.tile(g_k[l], B_KV_HEADS)]).reshape(1, -1)
    w_in_b = w_in[l].astype(BF16)
    cos_p, sin_p = _rope_tables(jnp.arange(seq, dtype=I32))
    cos_s, sin_s = _rope_tables(PAST_LEN + jnp.arange(SAMPLE_TILE, dtype=I32) % dec_seq)
    up, vap, qp, kp, vp = _in_proj(xp2, row(g_mix), w_in_b, row(g_sgu), g_qk, seg, seg.T, cos_p, sin_p,
                                   tile, "in_proj_prompt")
    us, vas, qs, ks, vs = _in_proj(xs2, row(g_mix), w_in_b, row(g_sgu), g_qk, seg, seg.T, cos_s, sin_s,
                                   SAMPLE_TILE, "in_proj_sample")

    w_out_b = w_out[l].astype(BF16)
    tri = jnp.tril(jnp.ones((MLP_CHUNK, MLP_CHUNK), bool))
    ws_p = jnp.where(tri[None], w_s[l], 0.0).astype(BF16)
    bias_p = jnp.repeat(b_s[l].T, A_HEAD_DIM, axis=1)
    x1p = _mixer_prompt(xp2, up, vap, qp, kp, vp, sinks[l], ws_p, bias_p, row(g_out_a), row(g_out_b), w_out_b, seq, tile)

    reps = SAMPLE_TILE // dec_seq
    tri_s = jnp.tril(jnp.ones((dec_seq, dec_seq), bool))
    ws_small = jnp.where(tri_s[None], w_s[l][:, :dec_seq, :dec_seq], 0.0)
    ws_s = jnp.einsum("ab,hts->hatbs", jnp.eye(reps, dtype=F32), ws_small).reshape(A_HEADS, SAMPLE_TILE, SAMPLE_TILE).astype(BF16)
    bias_s = jnp.tile(jnp.repeat(b_s[l][:, :dec_seq].T, A_HEAD_DIM, axis=1), (reps, 1))
    ck = cache_swa_k[l]
    cv = cache_swa_v[l]
    x1s = _mixer_sample(xs2, us, vas, qs, ks, vs, ck, cv, sinks[l], ws_s, bias_s,
                        row(g_out_a), row(g_out_b), w_out_b, dec_seq, SAMPLE_TILE)

    mk_p, mv_p = _mem_kv(mem_prompt.reshape(-1, D_MODEL), row(g_mem_src), w_mk[l].astype(BF16), w_mv[l].astype(BF16), row(g_mk))
    n_mem = mem_prompt.shape[1]
    mk_p4 = mk_p.reshape(n_b, n_mem, MEM_HEADS, MEM_HEAD_DIM)
    mv_p4 = mv_p.reshape(n_b, n_mem, MEM_HEADS, MEM_HEAD_DIM)
    w_r = jnp.pad(w_router[l], ((0, 0), (0, LANES - N_EXPERTS)))
    b_r = jnp.pad(b_router[l], (0, LANES - N_EXPERTS)).reshape(1, -1)
    w_r_hi = w_r.astype(BF16)
    w_r_lo = (w_r - w_r_hi.astype(F32)).astype(BF16)
    mem_args = (row(g_mem_in), w_mq[l].astype(BF16), row(g_mq), w_mo[l].astype(BF16), row(g_moe), w_r_hi, w_r_lo, b_r)
    x2p, hnp, gatep, routep, cntp = _memory_router(
        x1p, mk_p.reshape(n_b, n_mem, MEM_WIDTH), mv_p.reshape(n_b, n_mem, MEM_WIDTH),
        seq, *mem_args, tile, "memory_router_prompt")
    x2s, hns, gates, routes, cnts = _memory_router(
        x1s, cache_mem_k[l], cache_mem_v[l],
        dec_seq, *mem_args, SAMPLE_TILE, "memory_router_sample")

    cnt_p = cntp[0, :N_EXPERTS]
    cnt_s = cnts[0, :N_EXPERTS]
    total = cnt_p + cnt_s
    nblk = (total + EXPERT_ROWS - 1) // EXPERT_ROWS
    blk_end = jnp.cumsum(nblk)
    row_start = (blk_end - nblk) * EXPERT_ROWS
    n_blocks = (tp + ts) * TOP_K // EXPERT_ROWS + N_EXPERTS
    bidx = jnp.arange(n_blocks, dtype=I32)
    used = bidx < blk_end[-1]
    last = jnp.maximum(blk_end[-1] - 1, 0)
    bsafe = jnp.minimum(bidx, last)
    block_expert = jnp.minimum(jnp.searchsorted(blk_end, bsafe, side="right"), N_EXPERTS - 1).astype(I32)
    within = bsafe - (blk_end - nblk)[block_expert]
    block_valid = jnp.where(used, jnp.clip(total[block_expert] - within * EXPERT_ROWS, 0, EXPERT_ROWS), 0).astype(I32)
    def destinations(route, start):
        experts = jnp.arange(N_EXPERTS, dtype=I32)[:, None, None]
        first = jnp.sum(jnp.where(route[None, :TOP_K] == experts, start[:, None, None], 0), axis=0)
        return (first + route[TOP_K:]).T

    dest_p = destinations(routep, row_start)
    dest_s = destinations(routes, row_start + cnt_p)

    xb = _dispatch(hnp, dest_p, None, n_blocks * EXPERT_ROWS, min(DISPATCH_TILE, tp))
    xb = _dispatch(hns, dest_s, xb, n_blocks * EXPERT_ROWS, min(DISPATCH_TILE, ts))
    yb = _experts(xb, block_expert, block_valid, w_gate_up[l], b_gate_up[l], w_down[l], b_down[l])
    yp = _combine(x2p, gatep, dest_p, yb, min(COMBINE_TILE, tp))
    ys = _combine(x2s, gates, dest_s, yb, min(COMBINE_TILE, ts))

    new = dict(
        swa_k_p=kp.reshape(n_b, seq, KV_WIDTH)[:, seq - WINDOW:].reshape(n_b, WINDOW, B_KV_HEADS, B_HEAD_DIM),
        swa_v_p=vp.reshape(n_b, seq, KV_WIDTH)[:, seq - WINDOW:].reshape(n_b, WINDOW, B_KV_HEADS, B_HEAD_DIM),
        mem_k_p=mk_p4,
        mem_v_p=mv_p4,
        swa_k_s=ks.reshape(n_db, dec_seq, B_KV_HEADS, B_HEAD_DIM),
        swa_v_s=vs.reshape(n_db, dec_seq, B_KV_HEADS, B_HEAD_DIM),
        sgu_v_s=vas.reshape(n_db, dec_seq, A_HEADS, A_HEAD_DIM))
    return yp.reshape(n_b, seq, D_MODEL), ys.reshape(n_db, dec_seq, D_MODEL), new


def kernel(x_prompt, x_sample, cache_swa_k, cache_swa_v, cache_mem_k, cache_mem_v, mem_prompt, g_mix, w_in, g_sgu, w_s, b_s, g_q, g_k, sinks, g_out_a, g_out_b, w_out, g_mem_in, g_mem_src, w_mq, w_mk, w_mv, g_mq, g_mk, w_mo, g_moe, w_router, b_router, w_gate_up, b_gate_up, w_down, b_down):
    xp, xs = x_prompt, x_sample
    news = []
    for l in range(g_mix.shape[0]):
        xp, xs, new = _layer(l, xp, xs, cache_swa_k, cache_swa_v, cache_mem_k, cache_mem_v, mem_prompt,
                             g_mix, w_in, g_sgu, w_s, b_s, g_q, g_k, sinks, g_out_a, g_out_b, w_out,
                             g_mem_in, g_mem_src, w_mq, w_mk, w_mv, g_mq, g_mk, w_mo,
                             g_moe, w_router, b_router, w_gate_up, b_gate_up, w_down, b_down)
        news.append(new)
    stack = lambda name: jnp.stack([n[name] for n in news], 0)
    return (xp, xs, stack("swa_k_p"), stack("swa_v_p"), stack("mem_k_p"), stack("mem_v_p"),
            stack("swa_k_s"), stack("swa_v_s"), stack("sgu_v_s"))
```
